```python
import math
import jax, jax.numpy as jnp
from jax import lax
import numpy as np

D_MODEL = 1024
BATCH = 8
SEQ = 2048
DEPTH = 2

D_MIX = D_MODEL
GROUP_W = D_MIX // 4
MLA_HEADS = 4
MLA_NOPE = 64
MLA_ROPE = 32
MLA_V = GROUP_W // MLA_HEADS
MLA_Q_LORA = 192
MLA_KV_LORA = 128
ML_HEADS = 4
ML_DH = GROUP_W // ML_HEADS
ML_CHUNK = 64
SC_CH = GROUP_W
SC_WIDTH = 3
DF_HEADS = 4
DF_V = GROUP_W // DF_HEADS
DF_DK = DF_V // 2
Q_BLOCK = 128
ROPE_THETA = 10000.0
N_EXPERTS = 16
N_GROUPS = 4
E_PER_GROUP = N_EXPERTS // N_GROUPS
TOPK_GROUPS = 1
TOP_K = 2
D_EXPERT = 256
DN_ALPHA = (2 * DEPTH) ** 0.25
DN_BETA = (8 * DEPTH) ** -0.25
LN_EPS = 1e-5
RMS_EPS = 1e-6
IN_SIZES = (
    MLA_Q_LORA, MLA_KV_LORA, MLA_ROPE,
    GROUP_W, GROUP_W, GROUP_W, ML_HEADS, ML_HEADS, GROUP_W,
    SC_CH, SC_CH, SC_CH,
    2 * DF_HEADS * DF_DK, 2 * DF_HEADS * DF_DK, DF_HEADS * DF_V,
)
IN_COLS = sum(IN_SIZES)

kernel_name = "hybrid_mla_mlstm_conv_diffattn_grouped_moe"


def layer_norm(x):
    xf = x.astype(jnp.float32)
    mu = jnp.mean(xf, -1, keepdims=True)
    var = jnp.mean(jnp.square(xf - mu), -1, keepdims=True)
    return ((xf - mu) * lax.rsqrt(var + LN_EPS)).astype(x.dtype)


def rms_norm(x, g):
    xf = x.astype(jnp.float32)
    y = xf * lax.rsqrt(jnp.mean(xf * xf, -1, keepdims=True) + RMS_EPS)
    return y.astype(x.dtype) * g


def rope_tables(seq, dim):
    inv = 1.0 / (ROPE_THETA ** (jnp.arange(0, dim, 2, dtype=jnp.float32) / dim))
    ang = jnp.arange(seq, dtype=jnp.float32)[:, None] * inv[None, :]
    return jnp.cos(ang), jnp.sin(ang)


def apply_rope(x, cos, sin):
    x1, x2 = jnp.split(x, 2, axis=-1)
    cos = cos.astype(x.dtype)
    sin = sin.astype(x.dtype)
    return jnp.concatenate([x1 * cos - x2 * sin, x1 * sin + x2 * cos], axis=-1)


def split_cols(z):
    idx, acc = [], 0
    for sz in IN_SIZES[:-1]:
        acc += sz
        idx.append(acc)
    return jnp.split(z, idx, axis=-1)


def causal_block_attention(q, k, v, scale):
    b, g, s, dk = q.shape
    nb = s // Q_BLOCK
    qb = q.reshape(b, g, nb, Q_BLOCK, dk).transpose(2, 0, 1, 3, 4)
    key_pos = jnp.arange(s)

    def one_block(args):
        q_blk, start = args
        sc = jnp.einsum('bgqd,bgkd->bgqk', q_blk, k).astype(jnp.float32) * scale
        q_pos = start + jnp.arange(Q_BLOCK)
        sc = jnp.where(key_pos[None, :] <= q_pos[:, None], sc, -jnp.inf)
        p = jax.nn.softmax(sc, axis=-1).astype(v.dtype)
        return jnp.einsum('bgqk,bgkd->bgqd', p, v)

    out = lax.map(one_block, (qb, jnp.arange(nb) * Q_BLOCK))
    return out.transpose(1, 2, 0, 3, 4).reshape(b, g, s, v.shape[-1])


def mla_mixer(p_q, p_kv, k_rope, g_q, g_kv, w_uq, w_ukv, cos, sin):
    b, s, _ = p_q.shape
    q = (rms_norm(p_q, g_q) @ w_uq).reshape(b, s, MLA_HEADS, MLA_NOPE + MLA_ROPE).transpose(0, 2, 1, 3)
    kv = (rms_norm(p_kv, g_kv) @ w_ukv).reshape(b, s, MLA_HEADS, MLA_NOPE + MLA_V).transpose(0, 2, 1, 3)
    q_nope, q_pe = q[..., :MLA_NOPE], q[..., MLA_NOPE:]
    k_nope, v = kv[..., :MLA_NOPE], kv[..., MLA_NOPE:]
    q_pe = apply_rope(q_pe, cos, sin)
    k_pe = apply_rope(k_rope[:, None], cos, sin)
    qf = jnp.concatenate([q_nope, q_pe], -1)
    kf = jnp.concatenate([k_nope, jnp.broadcast_to(k_pe, k_nope.shape[:-1] + (MLA_ROPE,))], -1)
    o = causal_block_attention(qf, kf, v, (MLA_NOPE + MLA_ROPE) ** -0.5)
    return o.transpose(0, 2, 1, 3).reshape(b, s, MLA_HEADS * MLA_V)


def mlstm_mixer(q, k, v, i_pre, f_pre, o_pre):
    f32 = jnp.float32
    b, s, _ = q.shape
    L = ML_CHUNK
    nc = s // L

    def heads(t):
        return t.reshape(b, nc, L, ML_HEADS, ML_DH).transpose(0, 3, 1, 2, 4).astype(f32)

    def gates(t):
        return t.reshape(b, nc, L, ML_HEADS).transpose(0, 3, 1, 2).astype(f32)

    qh = heads(q) * (ML_DH ** -0.5)
    kh = heads(k)
    vh = heads(v)
    ig = gates(i_pre)
    lf = jax.nn.log_sigmoid(gates(f_pre))
    bcum = jnp.cumsum(lf, axis=-1)

    causal = jnp.tril(jnp.ones((L, L), dtype=bool))
    dmat = bcum[..., :, None] - bcum[..., None, :] + ig[..., None, :]
    dmat = jnp.where(causal, dmat, -jnp.inf)

    g_tot = bcum[..., -1]
    w_end = g_tot[..., None] - bcum + ig
    m_loc = jnp.max(w_end, axis=-1)
    a_end = jnp.exp(w_end - m_loc[..., None])
    c_loc = jnp.einsum('bhcl,bhcld,bhcle->bhcde', a_end, kh, vh)
    n_loc = jnp.einsum('bhcl,bhcld->bhcd', a_end, kh)

    def step(carry, inp):
        c_st, n_st, m_st = carry
        g, ml, cl, nl = inp
        m_new = jnp.maximum(g + m_st, ml)
        a = jnp.exp(g + m_st - m_new)
        bb = jnp.exp(ml - m_new)
        c_new = a[..., None, None] * c_st + bb[..., None, None] * cl
        n_new = a[..., None] * n_st + bb[..., None] * nl
        return (c_new, n_new, m_new), (c_st, n_st, m_st)

    init = (jnp.zeros((b, ML_HEADS, ML_DH, ML_DH), f32),
            jnp.zeros((b, ML_HEADS, ML_DH), f32),
            jnp.zeros((b, ML_HEADS), f32))
    xs = (jnp.moveaxis(g_tot, 2, 0), jnp.moveaxis(m_loc, 2, 0),
          jnp.moveaxis(c_loc, 2, 0), jnp.moveaxis(n_loc, 2, 0))
    _, (c_prev, n_prev, m_prev) = lax.scan(step, init, xs)
    c_prev = jnp.moveaxis(c_prev, 0, 2)
    n_prev = jnp.moveaxis(n_prev, 0, 2)
    m_prev = jnp.moveaxis(m_prev, 0, 2)

    inter_log = bcum + m_prev[..., None]
    m_j = jnp.maximum(inter_log, jnp.max(dmat, axis=-1))
    a_inter = jnp.exp(inter_log - m_j)
    a_intra = jnp.exp(dmat - m_j[..., None])
    s_qk = jnp.einsum('bhcjd,bhcsd->bhcjs', qh, kh) * a_intra
    num = (a_inter[..., None] * jnp.einsum('bhcjd,bhcde->bhcje', qh, c_prev)
           + jnp.einsum('bhcjs,bhcse->bhcje', s_qk, vh))
    den = a_inter * jnp.einsum('bhcjd,bhcd->bhcj', qh, n_prev) + jnp.sum(s_qk, axis=-1)
    h = num / jnp.maximum(jnp.abs(den), jnp.exp(-m_j))[..., None]
    h = h.transpose(0, 2, 3, 1, 4).reshape(b, s, ML_HEADS * ML_DH)
    return (jax.nn.sigmoid(o_pre.astype(f32)) * h).astype(q.dtype)


def short_conv_mixer(b_gate, c_gate, h, w_conv):
    u = c_gate * h
    y = lax.conv_general_dilated(u, w_conv[:, None, :], window_strides=(1,),
                                 padding=[(SC_WIDTH - 1, 0)],
                                 dimension_numbers=('NWC', 'WIO', 'NWC'),
                                 feature_group_count=SC_CH)
    return b_gate * y


def diff_attn_mixer(q, k, v, lq1, lk1, lq2, lk2, g_sub, lambda_init, cos, sin):
    b, s, _ = q.shape
    qh = q.reshape(b, s, 2 * DF_HEADS, DF_DK).transpose(0, 2, 1, 3)
    kh = k.reshape(b, s, 2 * DF_HEADS, DF_DK).transpose(0, 2, 1, 3)
    vh = v.reshape(b, s, DF_HEADS, DF_V).transpose(0, 2, 1, 3)
    qh = apply_rope(qh, cos, sin)
    kh = apply_rope(kh, cos, sin)
    v2 = jnp.repeat(vh, 2, axis=1)
    o = causal_block_attention(qh, kh, v2, DF_DK ** -0.5).reshape(b, DF_HEADS, 2, s, DF_V)
    lam = (jnp.exp(jnp.sum(lq1 * lk1).astype(jnp.float32))
           - jnp.exp(jnp.sum(lq2 * lk2).astype(jnp.float32)) + lambda_init)
    o = o[:, :, 0] - lam.astype(o.dtype) * o[:, :, 1]
    o = rms_norm(o, g_sub) * (1.0 - lambda_init)
    return o.transpose(0, 2, 1, 3).reshape(b, s, DF_HEADS * DF_V)


def grouped_moe(u, w_router, b_router, w1, w3, w2):
    b, s, d = u.shape
    t = u.reshape(b * s, d)
    scores = jax.nn.sigmoid((t @ w_router).astype(jnp.float32))
    sel = scores + b_router.astype(jnp.float32)
    grp_score = jnp.sum(lax.top_k(sel.reshape(-1, N_GROUPS, E_PER_GROUP), 2)[0], axis=-1)
    _, gidx = lax.top_k(grp_score, TOPK_GROUPS)
    gmask = jnp.any(gidx[..., None] == jnp.arange(N_GROUPS), axis=1)
    emask = jnp.repeat(gmask, E_PER_GROUP, axis=1)
    _, eidx = lax.top_k(jnp.where(emask, sel, -jnp.inf), TOP_K)
    w = jnp.take_along_axis(scores, eidx, axis=1)
    w = w / jnp.sum(w, axis=-1, keepdims=True)
    gates = jnp.sum(jax.nn.one_hot(eidx, N_EXPERTS, dtype=jnp.float32) * w[..., None], axis=1)
    h = jax.nn.silu(jnp.einsum('td,edf->tef', t, w1)) * jnp.einsum('td,edf->tef', t, w3)
    y = jnp.einsum('tef,efd->td', h * gates.astype(h.dtype)[..., None], w2)
    return y.reshape(b, s, d)


def setup_inputs(seed: int = 0) -> dict:
    key = jax.random.key(seed)
    ks = iter(jax.random.split(key, 32))
    f32 = jnp.float32
    L = DEPTH

    def nrm(shape, scale):
        return jax.random.normal(next(ks), shape, f32) * scale

    def gain(shape):
        return 1.0 + nrm(shape, 0.02)

    return {
        "x": nrm((BATCH, SEQ, D_MODEL), 1.0),
        "c": nrm((BATCH, D_MODEL), 1.0),
        "w_ada": nrm((L, D_MODEL, 6 * D_MODEL), 0.5 * D_MODEL ** -0.5),
        "b_ada": nrm((L, 6 * D_MODEL), 0.02),
        "w_in": nrm((L, D_MODEL, IN_COLS), D_MODEL ** -0.5),
        "mla_g_q": gain((L, MLA_Q_LORA)),
        "mla_g_kv": gain((L, MLA_KV_LORA)),
        "mla_w_uq": nrm((L, MLA_Q_LORA, MLA_HEADS * (MLA_NOPE + MLA_ROPE)), MLA_Q_LORA ** -0.5),
        "mla_w_ukv": nrm((L, MLA_KV_LORA, MLA_HEADS * (MLA_NOPE + MLA_V)), MLA_KV_LORA ** -0.5),
        "ml_b_i": nrm((L, ML_HEADS), 0.1),
        "ml_b_f": jnp.linspace(3.0, 6.0, ML_HEADS, dtype=f32)[None, :] + nrm((L, ML_HEADS), 0.1),
        "sc_w": nrm((L, SC_WIDTH, SC_CH), SC_WIDTH ** -0.5),
        "df_lq1": nrm((L, DF_DK), 0.1),
        "df_lk1": nrm((L, DF_DK), 0.1),
        "df_lq2": nrm((L, DF_DK), 0.1),
        "df_lk2": nrm((L, DF_DK), 0.1),
        "df_g": gain((L, DF_V)),
        "w_out": nrm((L, D_MIX, D_MODEL), DN_BETA * D_MIX ** -0.5),
        "ln1_g": gain((L, D_MODEL)),
        "ln1_b": nrm((L, D_MODEL), 0.02),
        "w_router": nrm((D_MODEL, N_EXPERTS), D_MODEL ** -0.5),
        "b_router": nrm((N_EXPERTS,), 0.01),
        "w1": nrm((L, N_EXPERTS, D_MODEL, D_EXPERT), D_MODEL ** -0.5),
        "w3": nrm((L, N_EXPERTS, D_MODEL, D_EXPERT), D_MODEL ** -0.5),
        "w2": nrm((L, N_EXPERTS, D_EXPERT, D_MODEL), DN_BETA * D_EXPERT ** -0.5),
        "ln2_g": gain((L, D_MODEL)),
        "ln2_b": nrm((L, D_MODEL), 0.02),
    }


def reference(x, c, w_ada, b_ada, w_in, mla_g_q, mla_g_kv, mla_w_uq, mla_w_ukv, ml_b_i, ml_b_f,
              sc_w, df_lq1, df_lk1, df_lq2, df_lk2, df_g, w_out, ln1_g, ln1_b,
              w_router, b_router, w1, w3, w2, ln2_g, ln2_b):
    s = x.shape[1]
    cos_a, sin_a = rope_tables(s, MLA_ROPE)
    cos_d, sin_d = rope_tables(s, DF_DK)
    c_act = jax.nn.silu(c)
    for l in range(DEPTH):
        mod = (c_act @ w_ada[l] + b_ada[l])[:, None, :]
        sh1, sc1, g1, sh2, sc2, g2 = jnp.split(mod, 6, axis=-1)

        u = layer_norm(x) * (1.0 + sc1) + sh1
        (a_q, a_kv, a_kr, m_q, m_k, m_v, m_i, m_f, m_o,
         c_b, c_c, c_h, d_q, d_k, d_v) = split_cols(u @ w_in[l])
        y_a = mla_mixer(a_q, a_kv, a_kr, mla_g_q[l], mla_g_kv[l], mla_w_uq[l], mla_w_ukv[l], cos_a, sin_a)
        y_b = mlstm_mixer(m_q, m_k, m_v, m_i + ml_b_i[l], m_f + ml_b_f[l], m_o)
        y_c = short_conv_mixer(c_b, c_c, c_h, sc_w[l])
        lambda_init = 0.8 - 0.6 * math.exp(-0.3 * l)
        y_d = diff_attn_mixer(d_q, d_k, d_v, df_lq1[l], df_lk1[l], df_lq2[l], df_lk2[l], df_g[l],
                              lambda_init, cos_d, sin_d)
        mix = jnp.concatenate([y_a, y_b, y_c, y_d], axis=-1) @ w_out[l]
        x = layer_norm(DN_ALPHA * x + g1 * mix) * ln1_g[l] + ln1_b[l]

        u2 = layer_norm(x) * (1.0 + sc2) + sh2
        ffn = grouped_moe(u2, w_router, b_router, w1[l], w3[l], w2[l])
        x = layer_norm(DN_ALPHA * x + g2 * ffn) * ln2_g[l] + ln2_b[l]
    return x
```

```python
import functools
import math

import jax
import jax.numpy as jnp
from jax import lax
from jax.experimental import pallas as pl
from jax.experimental.pallas import tpu as pltpu

f32 = jnp.float32
bf16 = jnp.bfloat16
HIGHEST = lax.Precision.HIGHEST

GROUP_W = 256
MLA_HEADS, MLA_NOPE, MLA_ROPE, MLA_V = 4, 64, 32, 64
MLA_Q_LORA, MLA_KV_LORA = 192, 128
ML_HEADS, ML_DH, ML_CHUNK = 4, 64, 64
DF_HEADS, DF_DK, DF_V = 4, 32, 64
ROPE_THETA = 10000.0
N_EXPERTS, N_GROUPS, E_PER_GROUP, D_EXPERT = 16, 4, 4, 256
DEPTH = 2
DN_ALPHA = (2 * DEPTH) ** 0.25
LN_EPS = 1e-5
RMS_EPS = 1e-6
IN_SIZES = (192, 128, 32, 256, 256, 256, 4, 4, 256, 256, 256, 256, 256, 256, 256)

OFF_PQ, OFF_PKV, OFF_KR = 0, 256, 384
OFF_MQ, OFF_MK, OFF_MV, OFF_MG, OFF_MO = 512, 768, 1024, 1280, 1408
OFF_CB, OFF_CC, OFF_CH = 1664, 1920, 2176
OFF_DQ, OFF_DK, OFF_DV = 2432, 2688, 2944
IN_PACKED = 3200

LANE = 128
VMEM_LIMIT = 48 * 1024 * 1024
NEG = -1e30


def _cparams(sem):
    return pltpu.CompilerParams(dimension_semantics=sem, vmem_limit_bytes=VMEM_LIMIT)


def _sigmoid(x):
    return 1.0 / (1.0 + jnp.exp(-x))


def _layer_norm(x):
    mu = jnp.mean(x, axis=-1, keepdims=True)
    xc = x - mu
    var = jnp.mean(xc * xc, axis=-1, keepdims=True)
    return xc * lax.rsqrt(var + LN_EPS)


def _dot(a, b, **kw):
    return jnp.dot(a, b, preferred_element_type=f32, **kw)


def _dot_nt(a, b, **kw):
    return lax.dot_general(a, b, (((1,), (1,)), ((), ())), preferred_element_type=f32, **kw)


def _dot_tn(a, b, **kw):
    return lax.dot_general(a, b, (((0,), (0,)), ((), ())), preferred_element_type=f32, **kw)


def _expand_groups(cols, rows, width, group):
    lane = lax.broadcasted_iota(jnp.int32, (rows, width), 1)
    out = jnp.broadcast_to(cols[-1], (rows, width))
    for h in range(len(cols) - 2, -1, -1):
        out = jnp.where(lane < group * (h + 1), cols[h], out)
    return out


def _lane_group_mask(shape, lo, hi):
    lane = lax.broadcasted_iota(jnp.int32, shape, 1)
    return (lane >= lo) & (lane < hi)


def _ada_body(c_ref, w_ref, b_ref, o_ref):
    c = c_ref[...]
    ca = (c * _sigmoid(c)).astype(bf16)
    o_ref[0] = _dot(ca, w_ref[0].astype(bf16)) + b_ref[0]


def _ada_mod(c, w_ada, b_ada):
    depth, d, n = w_ada.shape
    b = c.shape[0]
    tn = 1536
    return pl.pallas_call(
        _ada_body,
        grid=(depth, n // tn),
        in_specs=[
            pl.BlockSpec((b, d), lambda l, j: (0, 0)),
            pl.BlockSpec((1, d, tn), lambda l, j: (l, 0, j)),
            pl.BlockSpec((1, 1, tn), lambda l, j: (l, 0, j)),
        ],
        out_specs=pl.BlockSpec((1, b, tn), lambda l, j: (l, 0, j)),
        out_shape=jax.ShapeDtypeStruct((depth, b, n), f32),
        compiler_params=_cparams(("arbitrary", "arbitrary")),
        name="ada_mod",
    )(c, w_ada, b_ada.reshape(depth, 1, n))


def _rope_lanes(x, cos, sin_signed):
    w = x.shape[1]
    lane = lax.broadcasted_iota(jnp.int32, x.shape, 1)
    rot = jnp.where(lane % 32 < 16, pltpu.roll(x, w - 16, axis=1), pltpu.roll(x, 16, axis=1))
    return x * cos + rot * sin_signed


def _inproj_body(x_ref, mod_ref, w_ref, cos_ref, sin_ref, gq_ref, gkv_ref, gb_ref, cw_ref,
                 qn_ref, kvn_ref, kpe_ref, mq_ref, mk_ref, mv_ref, mo_ref, gcol_ref, yc_ref,
                 dq_ref, dk_ref, dv_ref, carry_ref, *, tiles_per_seq):
    i = pl.program_id(0)
    tm = x_ref.shape[0]
    u = _layer_norm(x_ref[...]) * (1.0 + mod_ref[0, 1:2, :]) + mod_ref[0, 0:1, :]
    ub = u.astype(bf16)

    def seg(off, n):
        return _dot(ub, w_ref[:, off:off + n])

    cos = cos_ref[...]
    sin = sin_ref[...]

    pq = seg(OFF_PQ, 256)
    ms = jnp.sum(pq * pq, axis=-1, keepdims=True) * (1.0 / MLA_Q_LORA)
    qn_ref[...] = (pq * lax.rsqrt(ms + RMS_EPS) * gq_ref[...]).astype(bf16)
    pkv = seg(OFF_PKV, 128)
    ms = jnp.sum(pkv * pkv, axis=-1, keepdims=True) * (1.0 / MLA_KV_LORA)
    kvn_ref[...] = (pkv * lax.rsqrt(ms + RMS_EPS) * gkv_ref[...]).astype(bf16)
    kpe_ref[...] = _rope_lanes(seg(OFF_KR, 128), cos[:, :LANE], sin[:, :LANE]).astype(bf16)

    mq_ref[...] = (seg(OFF_MQ, 256) * (ML_DH ** -0.5)).astype(bf16)
    mk_ref[...] = seg(OFF_MK, 256).astype(bf16)
    mv_ref[...] = seg(OFF_MV, 256).astype(bf16)
    gcol_ref[...] = seg(OFF_MG, 128) + gb_ref[...]
    mo_ref[...] = seg(OFF_MO, 256).astype(bf16)

    uc = seg(OFF_CC, 256) * seg(OFF_CH, 256)

    @pl.when(i % tiles_per_seq == 0)
    def _():
        carry_ref[...] = jnp.zeros_like(carry_ref)

    prev = carry_ref[...]
    row =lax.broadcasted_iota(jnp.int32, uc.shape, 0)
    u1 = jnp.where(row == 0, prev[7:8, :], pltpu.roll(uc, 1, axis=0))
    u2 = jnp.where(row == 0, prev[6:7, :], jnp.where(row == 1, prev[7:8, :], pltpu.roll(uc, 2, axis=0)))
    carry_ref[...] = uc[tm - 8:, :]
    conv = cw_ref[0:1, :] * u2 + cw_ref[1:2, :] * u1 + cw_ref[2:3, :] * uc
    yc_ref[...] = (seg(OFF_CB, 256) * conv).astype(bf16)

    dq_ref[...] = (_rope_lanes(seg(OFF_DQ, 256), cos, sin) * (DF_DK ** -0.5)).astype(bf16)
    dk_ref[...] = _rope_lanes(seg(OFF_DK, 256), cos, sin).astype(bf16)
    dv_ref[...] = seg(OFF_DV, 256).astype(bf16)


def _inproj(x2d, mod, w_packed, cos_t, sin_t, gq, gkv, gbias, conv_w, seq):
    t, d = x2d.shape
    tm = 512
    tps = seq // tm
    row = lambda w: pl.BlockSpec((tm, w), lambda i: (i, 0))
    const = lambda a: pl.BlockSpec(a.shape, lambda i: (0,) * a.ndim)
    widths = (256, 128, 128, 256, 256, 256, 256, 128, 256, 256, 256, 256)
    dtypes = (bf16, bf16, bf16, bf16, bf16, bf16, bf16, f32, bf16, bf16, bf16, bf16)
    return pl.pallas_call(
        functools.partial(_inproj_body, tiles_per_seq=tps),
        grid=(t // tm,),
        in_specs=[
            row(d),
            pl.BlockSpec((1, 6, d), lambda i: (i // tps, 0, 0)),
            const(w_packed),
            pl.BlockSpec((tm, 256), lambda i: (i % tps, 0)),
            pl.BlockSpec((tm, 256), lambda i: (i % tps, 0)),
            const(gq), const(gkv), const(gbias), const(conv_w),
        ],
        out_specs=[row(w) for w in widths],
        out_shape=[jax.ShapeDtypeStruct((t, w), dt) for w, dt in zip(widths, dtypes)],
        scratch_shapes=[pltpu.VMEM((8, 256), f32)],
        compiler_params=_cparams(("arbitrary",)),
        name="inproj",
    )(x2d, mod, w_packed, cos_t, sin_t, gq, gkv, gbias, conv_w)


def _flash_causal(q_heads, k_at, v_at, acc_of, grp_of, n_acc, qi, tq, width):
    n_heads = len(q_heads)
    lane = lax.broadcasted_iota(jnp.int32, (tq, width), 1)
    rowi = lax.broadcasted_iota(jnp.int32, (tq, tq), 0)
    coli = lax.broadcasted_iota(jnp.int32, (tq, tq), 1)

    def step(start, carry, masked):
        ms, ls, accs = carry
        new_ms, new_ls = [], []
        pv = [None] * n_acc
        alpha_w = [None] * n_acc
        for g in range(n_heads):
            s = _dot_nt(q_heads[g], k_at(g, start))
            if masked:
                s = jnp.where(coli <= rowi, s, NEG)
            m_new = jnp.maximum(ms[g], jnp.max(s, axis=1, keepdims=True))
            p = jnp.exp(s - m_new)
            alpha = jnp.exp(ms[g] - m_new)
            new_ls.append(alpha * ls[g] + jnp.sum(p, axis=1, keepdims=True))
            new_ms.append(m_new)
            a = acc_of(g)
            contrib = _dot(p.astype(bf16), v_at(g, start))
            pv[a] = contrib if pv[a] is None else pv[a] + contrib
            grp = grp_of(g)
            sel = (lane >= 64 * grp) & (lane < 64 * (grp + 1))
            alpha_b = jnp.broadcast_to(alpha, (tq, width))
            alpha_w[a] = jnp.where(sel, alpha_b, 0.0 if alpha_w[a] is None else alpha_w[a])
        new_accs = tuple(accs[a] * alpha_w[a] + pv[a] for a in range(n_acc))
        return tuple(new_ms), tuple(new_ls), new_accs

    init = (tuple(jnp.full((tq, 1), NEG, f32) for _ in range(n_heads)),
            tuple(jnp.zeros((tq, 1), f32) for _ in range(n_heads)),
            tuple(jnp.zeros((tq, width), f32) for _ in range(n_acc)))
    carry = lax.fori_loop(0, qi, lambda j, c: step(pl.multiple_of(j * tq, tq), c, False), init)
    _, ls, accs = step(pl.multiple_of(qi * tq, tq), carry, True)
    return ls, accs


def _mla_body(qn_ref, kvn_ref, kpe_ref, wq_ref, wqr_ref, wk_ref, wv_ref, place_ref, cq_ref, sq_ref,
              o_ref, k_scr, v_scr):
    qi = pl.program_id(1)
    tq = qn_ref.shape[0]

    @pl.when(qi == 0)
    def _():
        kvn = kvn_ref[...]
        kpe_placed = _dot(kpe_ref[...], place_ref[...])
        for h in range(MLA_HEADS):
            k_scr[h] = (_dot(kvn, wk_ref[h]) + kpe_placed).astype(bf16)
            v_scr[h] = _dot(kvn, wv_ref[h]).astype(bf16)

    qn = qn_ref[...]
    cq = cq_ref[...]
    sq = sq_ref[...]
    q_heads = [(_dot(qn, wq_ref[h]) * cq + _dot(qn, wqr_ref[h]) * sq).astype(bf16)
               for h in range(MLA_HEADS)]
    ls, accs = _flash_causal(
        q_heads,
        lambda g, start: k_scr[g, pl.ds(start, tq), :],
        lambda g, start: v_scr[g, pl.ds(start, tq), :],
        lambda g: 0, lambda g: g, 1, qi, tq, GROUP_W)
    l_w = _expand_groups(list(ls), tq, GROUP_W, 64)
    o_ref[...] = (accs[0] / l_w).astype(bf16)


def _mla_attention(qn, kvn, kpe, wq, wqr, wk, wv, place, cq, sq, batch, seq):
    tq = 256
    nq = seq // tq
    const = lambda a: pl.BlockSpec(a.shape, lambda b, i: (0,) * a.ndim)
    return pl.pallas_call(
        _mla_body,
        grid=(batch, nq),
        in_specs=[
            pl.BlockSpec((tq, 256), lambda b, i: (b * nq + i, 0)),
            pl.BlockSpec((seq, 128), lambda b, i: (b, 0)),
            pl.BlockSpec((seq, 128), lambda b, i: (b, 0)),
            const(wq), const(wqr), const(wk), const(wv), const(place),
            pl.BlockSpec((tq, 128), lambda b, i: (i, 0)),
            pl.BlockSpec((tq, 128), lambda b, i: (i, 0)),
        ],
        out_specs=pl.BlockSpec((tq, GROUP_W), lambda b, i: (b * nq + i, 0)),
        out_shape=jax.ShapeDtypeStruct((batch * seq, GROUP_W), bf16),
        scratch_shapes=[pltpu.VMEM((MLA_HEADS, seq, 128), bf16),
                        pltpu.VMEM((MLA_HEADS, seq, GROUP_W), bf16)],
        compiler_params=_cparams(("arbitrary", "arbitrary")),
        name="mla_attn",
    )(qn, kvn, kpe, wq, wqr, wk, wv, place, cq, sq)


def _diff_body(q_ref, k_ref, v_ref, lq1_ref, lk1_ref, lq2_ref, lk2_ref, g_ref, bd_ref,
               o_ref, v_scr, *, lambda_init):
    qi = pl.program_id(1)
    tq = q_ref.shape[0]

    @pl.when(qi == 0)
    def _():
        v = v_ref[...]
        for h in range(DF_HEADS):
            v_scr[h] = jnp.where(_lane_group_mask(v.shape, DF_V * h, DF_V * (h + 1)), v, jnp.zeros_like(v))

    q = q_ref[...]
    q_heads = [jnp.where(_lane_group_mask(q.shape, DF_DK * g, DF_DK * (g + 1)), q, jnp.zeros_like(q))
               for g in range(2 * DF_HEADS)]
    ls, accs = _flash_causal(
        q_heads,
        lambda g, start: k_ref[pl.ds(start, tq), :],
        lambda g, start: v_scr[g // 2, pl.ds(start, tq), :],
        lambda g: g % 2, lambda g: g // 2, 2, qi, tq, GROUP_W)
    l1 = _expand_groups([ls[2 * h] for h in range(DF_HEADS)], tq, GROUP_W, DF_V)
    l2 = _expand_groups([ls[2 * h + 1] for h in range(DF_HEADS)], tq, GROUP_W, DF_V)
    lam = (jnp.exp(jnp.sum(lq1_ref[...] * lk1_ref[...], axis=1, keepdims=True))
           - jnp.exp(jnp.sum(lq2_ref[...] * lk2_ref[...], axis=1, keepdims=True)) + lambda_init)
    o = accs[0] / l1 - lam * (accs[1] / l2)
    ms = _dot(o * o, bd_ref[...], precision=HIGHEST)
    o_ref[...] = (o * lax.rsqrt(ms + RMS_EPS) * g_ref[...] * (1.0 - lambda_init)).astype(bf16)


def _diff_attention(dq, dk, dv, lq1, lk1, lq2, lk2, g_wide, bd, batch, seq, lambda_init):
    tq = 256
    nq = seq // tq
    const = lambda a: pl.BlockSpec(a.shape, lambda b, i: (0,) * a.ndim)
    return pl.pallas_call(
        functools.partial(_diff_body, lambda_init=lambda_init),
        grid=(batch, nq),
        in_specs=[
            pl.BlockSpec((tq, 256), lambda b, i: (b * nq + i, 0)),
            pl.BlockSpec((seq, 256), lambda b, i: (b, 0)),
            pl.BlockSpec((seq, 256), lambda b, i: (b, 0)),
            const(lq1), const(lk1), const(lq2), const(lk2), const(g_wide), const(bd),
        ],
        out_specs=pl.BlockSpec((tq, GROUP_W), lambda b, i: (b * nq + i, 0)),
        out_shape=jax.ShapeDtypeStruct((batch * seq, GROUP_W), bf16),
        scratch_shapes=[pltpu.VMEM((DF_HEADS, seq, GROUP_W), bf16)],
        compiler_params=_cparams(("arbitrary", "arbitrary")),
        name="diff_attn",
    )(dq, dk, dv, lq1, lk1, lq2, lk2, g_wide, bd)


def _mlstm_body(q_ref, k_ref, v_ref, o_ref, g_ref, y_ref, c_scr, nm_scr):
    ci = pl.program_id(1)
    L, W = q_ref.shape
    H = ML_HEADS

    @pl.when(ci == 0)
    def _():
        c_scr[...] = jnp.zeros_like(c_scr)
        nm_scr[...] = jnp.zeros_like(nm_scr)

    q = q_ref[...]
    k = k_ref[...]
    v = v_ref[...]
    g = g_ref[...]
    lf = jnp.minimum(g, 0.0) - jnp.log(1.0 + jnp.exp(-jnp.abs(g)))
    r64 = lax.broadcasted_iota(jnp.int32, (L, L), 0)
    c64 = lax.broadcasted_iota(jnp.int32, (L, L), 1)
    tril = (r64 >= c64)
    bc = _dot(tril.astype(f32), lf, precision=HIGHEST)
    n_prev = nm_scr[0:1, :]
    m_prev_w = nm_scr[1:2, :]
    qf = q.astype(f32)
    kf = k.astype(f32)
    qn_all = qf * n_prev

    a_inter, hden, aend, gtot, mloc, num_heads = [], [], [], [], [], None
    for h in range(H):
        hm = _lane_group_mask((L, W), ML_DH * h, ML_DH * (h + 1))
        bcol = bc[:, 4 + h:5 + h]
        icol = g[:, h:h + 1]
        m_prev = m_prev_w[:, ML_DH * h:ML_DH * h + 1]
        rowpart = jnp.sum(jnp.where(r64 == c64, jnp.broadcast_to(icol - bcol, (L, L)), 0.0),
                          axis=0, keepdims=True)
        dmat = jnp.where(tril, bcol + rowpart, NEG)
        inter = bcol + m_prev
        m_j = jnp.maximum(inter, jnp.max(dmat, axis=1, keepdims=True))
        a_int = jnp.exp(inter - m_j)
        sqk = _dot_nt(jnp.where(hm, q, jnp.zeros_like(q)), k) * jnp.exp(dmat - m_j)
        numh = jnp.where(hm, _dot(sqk.astype(bf16), v), 0.0)
        num_heads = numh if num_heads is None else num_heads + numh
        qn_h = jnp.sum(jnp.where(hm, qn_all, 0.0), axis=1, keepdims=True)
        den = a_int * qn_h + jnp.sum(sqk, axis=1, keepdims=True)
        hden.append(jnp.maximum(jnp.abs(den), jnp.exp(-m_j)))
        a_inter.append(a_int)
        g_t = bcol[L - 1:L, :]
        wend = g_t - bcol + icol
        m_l = jnp.max(wend, axis=0, keepdims=True)
        aend.append(jnp.exp(wend - m_l))
        gtot.append(g_t)
        mloc.append(m_l)

    num = _dot(q, c_scr[...].astype(bf16)) * _expand_groups(a_inter, L, W, ML_DH) + num_heads
    y = num / _expand_groups(hden, L, W, ML_DH) * _sigmoid(o_ref[...].astype(f32))
    y_ref[...] = y.astype(y_ref.dtype)

    kw = kf * _expand_groups(aend, L, W, ML_DH)
    rw = lax.broadcasted_iota(jnp.int32, (W, W), 0)
    cw = lax.broadcasted_iota(jnp.int32, (W, W), 1)
    c_loc = jnp.where(rw // ML_DH == cw // ML_DH, _dot_tn(kw.astype(bf16), v), 0.0)
    n_loc = jnp.sum(kw, axis=0, keepdims=True)
    gtot_w = _expand_groups(gtot, 1, W, ML_DH)
    mloc_w = _expand_groups(mloc, 1, W, ML_DH)
    m_new = jnp.maximum(gtot_w + m_prev_w, mloc_w)
    a_w = jnp.exp(gtot_w + m_prev_w - m_new)
    b_w = jnp.exp(mloc_w - m_new)
    c_scr[...] = c_scr[...] * a_w + c_loc * b_w
    nm_scr[0:1, :] = a_w * n_prev + b_w * n_loc
    nm_scr[1:2, :] = m_new


def _mlstm(mq, mk, mv, mo, gcol, batch, seq):
    L = ML_CHUNK
    nc = seq // L
    blk = lambda w: pl.BlockSpec((L, w), lambda b, c: (b * nc + c, 0))
    return pl.pallas_call(
        _mlstm_body,
        grid=(batch, nc),
        in_specs=[blk(256), blk(256), blk(256), blk(256), blk(128)],
        out_specs=blk(GROUP_W),
        out_shape=jax.ShapeDtypeStruct((batch * seq, GROUP_W), bf16),
        scratch_shapes=[pltpu.VMEM((GROUP_W, GROUP_W), f32), pltpu.VMEM((8, GROUP_W), f32)],
        compiler_params=_cparams(("arbitrary", "arbitrary")),
        name="mlstm",
    )(mq, mk, mv, mo, gcol)


def _outproj_body(ya_ref, yb_ref, yc_ref, yd_ref, x_ref, mod_ref, w_ref, g_ref, b_ref, o_ref):
    mix = None
    for j, y_ref in enumerate((ya_ref, yb_ref, yc_ref, yd_ref)):
        part = _dot(y_ref[...], w_ref[GROUP_W * j:GROUP_W * (j + 1), :])
        mix = part if mix is None else mix + part
    r = DN_ALPHA * x_ref[...] + mod_ref[0, 2:3, :] * mix
    o_ref[...] = _layer_norm(r) * g_ref[...] + b_ref[...]


def _outproj(ya, yb, yc, yd, x2d, mod, w_out, ln_g, ln_b, seq):
    t, d = x2d.shape
    tm = 512
    tps = seq // tm
    row = lambda w: pl.BlockSpec((tm, w), lambda i: (i, 0))
    const = lambda a: pl.BlockSpec(a.shape, lambda i: (0,) * a.ndim)
    return pl.pallas_call(
        _outproj_body,
        grid=(t // tm,),
        in_specs=[row(256), row(256), row(256), row(256), row(d),
                  pl.BlockSpec((1, 6, d), lambda i: (i // tps, 0, 0)),
                  const(w_out), const(ln_g), const(ln_b)],
        out_specs=row(d),
        out_shape=jax.ShapeDtypeStruct((t, d), f32),
        compiler_params=_cparams(("arbitrary",)),
        name="outproj",
    )(ya, yb, yc, yd, x2d, mod, w_out, ln_g, ln_b)


def _route(scores, sel):
    tm, ne = sel.shape
    lane = lax.broadcasted_iota(jnp.int32, (tm, ne), 1)
    grp = lane // E_PER_GROUP
    big = ne + 1

    def top2(vals):
        m1 = jnp.max(vals, axis=1, keepdims=True)
        i1 = jnp.min(jnp.where(vals == m1, lane, big), axis=1, keepdims=True)
        rest = jnp.where(lane == i1, NEG, vals)
        m2 = jnp.max(rest, axis=1, keepdims=True)
        i2 = jnp.min(jnp.where(rest == m2, lane, big), axis=1, keepdims=True)
        return m1, i1, m2, i2

    best_score, best_grp = None, None
    for gi in range(N_GROUPS):
        m1, _, m2, _ = top2(jnp.where(grp == gi, sel, NEG))
        sc = m1 + m2
        if best_score is None:
            best_score, best_grp = sc, jnp.zeros_like(sc, dtype=jnp.int32)
        else:
            better = sc > best_score
            best_grp = jnp.where(better, gi, best_grp)
            best_score = jnp.where(better, sc, best_score)
    _, i1, _, i2 = top2(jnp.where(grp == best_grp, sel, NEG))
    picked = jnp.where((lane == i1) | (lane == i2), scores, 0.0)
    return picked / jnp.sum(picked, axis=1, keepdims=True)


def _moe_body(x_ref, mod_ref, wr_ref, br_ref, w1_ref, w3_ref, w2_ref, g_ref, b_ref, o_ref,
              u_scr, gate_scr, acc_scr):
    e = pl.program_id(1)

    @pl.when(e == 0)
    def _():
        u = _layer_norm(x_ref[...]) * (1.0 + mod_ref[0, 4:5, :]) + mod_ref[0, 3:4, :]
        u_scr[...] = u.astype(bf16)
        scores = _sigmoid(_dot(u, wr_ref[...], precision=HIGHEST))
        gate_scr[...] = _route(scores, scores + br_ref[...])
        acc_scr[...] = jnp.zeros_like(acc_scr)

    u = u_scr[...]
    gates = gate_scr[...]
    lane = lax.broadcasted_iota(jnp.int32, gates.shape, 1)
    ge = jnp.sum(jnp.where(lane == e, gates, 0.0), axis=1, keepdims=True)
    h1 = _dot(u, w1_ref[0])
    h = h1 * _sigmoid(h1) * _dot(u, w3_ref[0]) * ge
    acc_scr[...] += _dot(h.astype(bf16), w2_ref[0])

    @pl.when(e == pl.num_programs(1) - 1)
    def _():
        r = DN_ALPHA * x_ref[...] + mod_ref[0, 5:6, :] * acc_scr[...]
        o_ref[...] = _layer_norm(r) * g_ref[...] + b_ref[...]


def _moe(x2d, mod, w_router, b_router, w1, w3, w2, ln_g, ln_b, seq):
    t, d = x2d.shape
    ne, _, fe = w1.shape
    tm = 1024
    tps = seq // tm
    const = lambda a: pl.BlockSpec(a.shape, lambda i, e: (0,) * a.ndim)
    return pl.pallas_call(
        _moe_body,
        grid=(t // tm, ne),
        in_specs=[
            pl.BlockSpec((tm, d), lambda i, e: (i, 0)),
            pl.BlockSpec((1, 6, d), lambda i, e: (i // tps, 0, 0)),
            const(w_router), const(b_router),
            pl.BlockSpec((1, d, fe), lambda i, e: (e, 0, 0)),
            pl.BlockSpec((1, d, fe), lambda i, e: (e, 0, 0)),
            pl.BlockSpec((1, fe, d), lambda i, e: (e, 0, 0)),
            const(ln_g), const(ln_b),
        ],
        out_specs=pl.BlockSpec((tm, d), lambda i, e: (i, 0)),
        out_shape=jax.ShapeDtypeStruct((t, d), f32),
        scratch_shapes=[pltpu.VMEM((tm, d), bf16), pltpu.VMEM((tm, ne), f32), pltpu.VMEM((tm, d), f32)],
        compiler_params=_cparams(("arbitrary", "arbitrary")),
        name="moe",
    )(x2d, mod, w_router, b_router, w1, w3, w2, ln_g, ln_b)


def _pad_cols(a, width):
    return jnp.pad(a, ((0, 0), (0, width - a.shape[1])))


def _pack_w_in(w):
    parts, acc = [], 0
    for sz in IN_SIZES:
        parts.append(w[:, acc:acc + sz])
        acc += sz
    (a_q, a_kv, a_kr, m_q, m_k, m_v, m_i, m_f, m_o, c_b, c_c, c_h, d_q, d_k, d_v) = parts
    gates = _pad_cols(jnp.concatenate([m_i, m_f], axis=1), LANE)
    packed = jnp.concatenate([
        _pad_cols(a_q, 256), a_kv, _pad_cols(a_kr, LANE),
        m_q, m_k, m_v, gates, m_o, c_b, c_c, c_h, d_q, d_k, d_v], axis=1)
    return packed.astype(bf16)


def _pack_mla(w_uq, w_ukv):
    half = MLA_ROPE // 2
    wq3 = w_uq.reshape(MLA_Q_LORA, MLA_HEADS, MLA_NOPE + MLA_ROPE)
    wkv3 = w_ukv.reshape(MLA_KV_LORA, MLA_HEADS, MLA_NOPE + MLA_V)
    wq, wqr, wk, wv = [], [], [], []
    for h in range(MLA_HEADS):
        nope, pe = wq3[:, h, :MLA_NOPE], wq3[:, h, MLA_NOPE:]
        rot = jnp.concatenate([-pe[:, half:], pe[:, :half]], axis=1)
        zq = jnp.zeros((MLA_Q_LORA, MLA_NOPE), f32)
        pad_rows = lambda a: jnp.pad(a, ((0, 256 - MLA_Q_LORA), (0, 0)))
        wq.append(pad_rows(_pad_cols(jnp.concatenate([nope, pe], axis=1), LANE)))
        wqr.append(pad_rows(_pad_cols(jnp.concatenate([zq, rot], axis=1), LANE)))
        wk.append(_pad_cols(wkv3[:, h, :MLA_NOPE], LANE))
        wv.append(jnp.pad(wkv3[:, h, MLA_NOPE:], ((0, 0), (MLA_V * h, GROUP_W - MLA_V * (h + 1)))))
    st = lambda xs: jnp.stack(xs).astype(bf16)
    return st(wq), st(wqr), st(wk), st(wv)


def _rope_tables(seq):
    half = DF_DK // 2
    inv = 1.0 / (ROPE_THETA ** (jnp.arange(0, DF_DK, 2, dtype=f32) / DF_DK))
    ang = jnp.arange(seq, dtype=f32)[:, None] * inv[None, :]
    cos, sin = jnp.cos(ang), jnp.sin(ang)
    cos_t = jnp.tile(jnp.concatenate([cos, cos], axis=1), (1, 256 // DF_DK))
    sin_t = jnp.tile(jnp.concatenate([-sin, sin], axis=1), (1, 256 // DF_DK))
    scale = (MLA_NOPE + MLA_ROPE) ** -0.5
    ones = jnp.ones((seq, MLA_NOPE), f32)
    zeros = jnp.zeros((seq, MLA_NOPE), f32)
    cq = _pad_cols(jnp.concatenate([ones, cos, cos], axis=1) * scale, LANE)
    sq = _pad_cols(jnp.concatenate([zeros, sin, sin], axis=1) * scale, LANE)
    return cos_t, sin_t, cq, sq


def kernel(x, c, w_ada, b_ada, w_in, mla_g_q, mla_g_kv, mla_w_uq, mla_w_ukv, ml_b_i, ml_b_f, sc_w,
           df_lq1, df_lk1, df_lq2, df_lk2, df_g, w_out, ln1_g, ln1_b, w_router, b_router, w1, w3, w2,
           ln2_g, ln2_b):
    batch, seq, d = x.shape
    depth = w_in.shape[0]
    assert MLA_ROPE == DF_DK, "both rotary blocks share one table"
    cos_t, sin_t, cq, sq = _rope_tables(seq)
    place = jnp.zeros((LANE, LANE), f32).at[jnp.arange(MLA_ROPE), MLA_NOPE + jnp.arange(MLA_ROPE)].set(1.0).astype(bf16)
    head_of = jnp.arange(GROUP_W) // DF_V
    bd = (head_of[:, None] == head_of[None, :]).astype(f32) / DF_V
    mod_all = _ada_mod(c, w_ada, b_ada).reshape(depth, batch, 6, d)
    xf = x.reshape(batch * seq, d)
    for l in range(depth):
        mod = mod_all[l]
        w_packed = _pack_w_in(w_in[l])
        gq = _pad_cols(mla_g_q[l][None, :], 256)
        gkv = mla_g_kv[l][None, :]
        gbias = _pad_cols(jnp.concatenate([ml_b_i[l], ml_b_f[l]])[None, :], LANE)
        (qn, kvn, kpe, mq, mk, mv, mo, gcol, yc, dq, dk, dv) = _inproj(
            xf, mod, w_packed, cos_t, sin_t, gq, gkv, gbias, sc_w[l], seq)
        wq, wqr, wk, wv = _pack_mla(mla_w_uq[l], mla_w_ukv[l])
        ya = _mla_attention(qn, kvn, kpe, wq, wqr, wk, wv, place, cq, sq, batch, seq)
        yb = _mlstm(mq, mk, mv, mo, gcol, batch, seq)
        lambda_init = 0.8 - 0.6 * math.exp(-0.3 * l)
        yd = _diff_attention(dq, dk, dv, df_lq1[l][None, :], df_lk1[l][None, :], df_lq2[l][None, :],
                             df_lk2[l][None, :], jnp.tile(df_g[l], DF_HEADS)[None, :], bd,
                             batch, seq, lambda_init)
        x1 = _outproj(ya, yb, yc, yd, xf, mod, w_out[l].astype(bf16), ln1_g[l][None, :], ln1_b[l][None, :], seq)
        xf = _moe(x1, mod, w_router, b_router[None, :], w1[l].astype(bf16), w3[l].astype(bf16),
                  w2[l].astype(bf16), ln2_g[l][None, :], ln2_b[l][None, :], seq)
    return xf.reshape(batch, seq, d)
```

```python
import functools
import math

import jax
import jax.numpy as jnp
from jax import lax
from jax.experimental import pallas as pl
from jax.experimental.pallas import tpu as pltpu

f32 = jnp.float32
bf16 = jnp.bfloat16
HIGHEST = lax.Precision.HIGHEST

GROUP_W = 256
MLA_HEADS, MLA_NOPE, MLA_ROPE, MLA_V = 4, 64, 32, 64
MLA_Q_LORA, MLA_KV_LORA = 192, 128
ML_HEADS, ML_DH, ML_CHUNK = 4, 64, 64
DF_HEADS, DF_DK, DF_V = 4, 32, 64
ROPE_THETA = 10000.0
N_EXPERTS, N_GROUPS, E_PER_GROUP, D_EXPERT = 16, 4, 4, 256
DEPTH = 2
DN_ALPHA = (2 * DEPTH) ** 0.25
LN_EPS = 1e-5
RMS_EPS = 1e-6
IN_SIZES = (192, 128, 32, 256, 256, 256, 4, 4, 256, 256, 256, 256, 256, 256, 256)

OFF_PQ, OFF_PKV, OFF_KR = 0, 256, 384
OFF_MQ, OFF_MK, OFF_MV, OFF_MG, OFF_MO = 512, 768, 1024, 1280, 1408
OFF_CB, OFF_CC, OFF_CH = 1664, 1920, 2176
OFF_DQ, OFF_DK, OFF_DV = 2432, 2688, 2944
IN_PACKED = 3200

LANE = 128
VMEM_LIMIT = 48 * 1024 * 1024
NEG = -1e30
LOG2E = math.log2(math.e)


def _cparams(sem):
    return pltpu.CompilerParams(dimension_semantics=sem, vmem_limit_bytes=VMEM_LIMIT)


def _sigmoid(x):
    return 1.0 / (1.0 + jnp.exp(-x))


def _layer_norm(x):
    mu = jnp.mean(x, axis=-1, keepdims=True)
    xc = x - mu
    var = jnp.mean(xc * xc, axis=-1, keepdims=True)
    return xc * lax.rsqrt(var + LN_EPS)


def _dot(a, b, **kw):
    return jnp.dot(a, b, preferred_element_type=f32, **kw)


def _dot_nt(a, b, **kw):
    return lax.dot_general(a, b, (((1,), (1,)), ((), ())), preferred_element_type=f32, **kw)


def _dot_tn(a, b, **kw):
    return lax.dot_general(a, b, (((0,), (0,)), ((), ())), preferred_element_type=f32, **kw)


def _expand_groups(cols, rows, width, group):
    lane = lax.broadcasted_iota(jnp.int32, (rows, width), 1)
    out = jnp.broadcast_to(cols[-1], (rows, width))
    for h in range(len(cols) - 2, -1, -1):
        out = jnp.where(lane < group * (h + 1), cols[h], out)
    return out


def _lane_group_mask(shape, lo, hi):
    lane = lax.broadcasted_iota(jnp.int32, shape, 1)
    return (lane >= lo) & (lane < hi)


def _ada_body(c_ref, w_ref, b_ref, o_ref):
    c = c_ref[...]
    ca = (c * _sigmoid(c)).astype(bf16)
    o_ref[0] = _dot(ca, w_ref[0].astype(bf16)) + b_ref[0]


def _ada_mod(c, w_ada, b_ada):
    depth, d, n = w_ada.shape
    b = c.shape[0]
    tn = 1536
    return pl.pallas_call(
        _ada_body,
        grid=(depth, n // tn),
        in_specs=[
            pl.BlockSpec((b, d), lambda l, j: (0, 0)),
            pl.BlockSpec((1, d, tn), lambda l, j: (l, 0, j)),
            pl.BlockSpec((1, 1, tn), lambda l, j: (l, 0, j)),
        ],
        out_specs=pl.BlockSpec((1, b, tn), lambda l, j: (l, 0, j)),
        out_shape=jax.ShapeDtypeStruct((depth, b, n), f32),
        compiler_params=_cparams(("arbitrary", "arbitrary")),
        name="ada_mod",
    )(c, w_ada, b_ada.reshape(depth, 1, n))


def _rope_lanes(x, cos, sin_signed):
    w = x.shape[1]
    lane = lax.broadcasted_iota(jnp.int32, x.shape, 1)
    rot = jnp.where(lane % 32 < 16, pltpu.roll(x, w - 16, axis=1), pltpu.roll(x, 16, axis=1))
    return x * cos + rot * sin_signed


def _inproj_body(x_ref, mod_ref, w_ref, cos_ref, sin_ref, gq_ref, gkv_ref, gb_ref, cw_ref,
                 qn_ref, kvn_ref, kpe_ref, mq_ref, mk_ref, mv_ref, mo_ref, gcol_ref, yc_ref,
                 dq_ref, dk_ref, dv_ref, carry_ref, *, tiles_per_seq):
    i = pl.program_id(0)
    tm = x_ref.shape[0]
    u = _layer_norm(x_ref[...]) * (1.0 + mod_ref[0, 1:2, :]) + mod_ref[0, 0:1, :]
    ub = u.astype(bf16)

    def seg(off, n):
        return _dot(ub, w_ref[:, off:off + n])

    cos = cos_ref[...]
    sin = sin_ref[...]

    pq = seg(OFF_PQ, 256)
    ms = jnp.sum(pq * pq, axis=-1, keepdims=True) * (1.0 / MLA_Q_LORA)
    qn_ref[...] = (pq * lax.rsqrt(ms + RMS_EPS) * gq_ref[...]).astype(bf16)
    pkv = seg(OFF_PKV, 128)
    ms = jnp.sum(pkv * pkv, axis=-1, keepdims=True) * (1.0 / MLA_KV_LORA)
    kvn_ref[...] = (pkv * lax.rsqrt(ms + RMS_EPS) * gkv_ref[...]).astype(bf16)
    kpe_ref[...] = _rope_lanes(seg(OFF_KR, 128), cos[:, :LANE], sin[:, :LANE]).astype(bf16)

    mq_ref[...] = (seg(OFF_MQ, 256) * (ML_DH ** -0.5)).astype(bf16)
    mk_ref[...] = seg(OFF_MK, 256).astype(bf16)
    mv_ref[...] = seg(OFF_MV, 256).astype(bf16)
    gcol_ref[...] = seg(OFF_MG, 128) + gb_ref[...]
    mo_ref[...] = seg(OFF_MO, 256).astype(bf16)

    uc = seg(OFF_CC, 256) * seg(OFF_CH, 256)

    @pl.when(i % tiles_per_seq == 0)
    def _():
        carry_ref[...] = jnp.zeros_like(carry_ref)

    prev = carry_ref[...]
    row =lax.broadcasted_iota(jnp.int32, uc.shape, 0)
    u1 = jnp.where(row == 0, prev[7:8, :], pltpu.roll(uc, 1, axis=0))
    u2 = jnp.where(row == 0, prev[6:7, :], jnp.where(row == 1, prev[7:8, :], pltpu.roll(uc, 2, axis=0)))
    carry_ref[...] = uc[tm - 8:, :]
    conv = cw_ref[0:1, :] * u2 + cw_ref[1:2, :] * u1 + cw_ref[2:3, :] * uc
    yc_ref[...] = (seg(OFF_CB, 256) * conv).astype(bf16)

    dq_ref[...] = (_rope_lanes(seg(OFF_DQ, 256), cos, sin) * (DF_DK ** -0.5 * LOG2E)).astype(bf16)
    dk_ref[...] = _rope_lanes(seg(OFF_DK, 256), cos, sin).astype(bf16)
    dv_ref[...] = seg(OFF_DV, 256).astype(bf16)


def _inproj(x2d, mod, w_packed, cos_t, sin_t, gq, gkv, gbias, conv_w, seq):
    t, d = x2d.shape
    tm = 512
    tps = seq // tm
    row = lambda w: pl.BlockSpec((tm, w), lambda i: (i, 0))
    const = lambda a: pl.BlockSpec(a.shape, lambda i: (0,) * a.ndim)
    widths = (256, 128, 128, 256, 256, 256, 256, 128, 256, 256, 256, 256)
    dtypes = (bf16, bf16, bf16, bf16, bf16, bf16, bf16, f32, bf16, bf16, bf16, bf16)
    return pl.pallas_call(
        functools.partial(_inproj_body, tiles_per_seq=tps),
        grid=(t // tm,),
        in_specs=[
            row(d),
            pl.BlockSpec((1, 6, d), lambda i: (i // tps, 0, 0)),
            const(w_packed),
            pl.BlockSpec((tm, 256), lambda i: (i % tps, 0)),
            pl.BlockSpec((tm, 256), lambda i: (i % tps, 0)),
            const(gq), const(gkv), const(gbias), const(conv_w),
        ],
        out_specs=[row(w) for w in widths],
        out_shape=[jax.ShapeDtypeStruct((t, w), dt) for w, dt in zip(widths, dtypes)],
        scratch_shapes=[pltpu.VMEM((8, 256), f32)],
        compiler_params=_cparams(("arbitrary",)),
        name="inproj",
    )(x2d, mod, w_packed, cos_t, sin_t, gq, gkv, gbias, conv_w)


def _causal_softmax_heads(q_heads, k_at, v_at, qi, s_scr, m_scr, l_scr, acc_scr):
    n_heads = len(q_heads)
    tq = q_heads[0].shape[0]
    half = tq // 2

    def scores(g, j):
        return _dot_nt(q_heads[g], k_at(g, pl.multiple_of(j * tq, tq)))

    def fold_max(g, s):
        m_scr[g] = jnp.maximum(m_scr[g], jnp.maximum(s[:, :half], s[:, half:]))

    m_scr[...] = jnp.full(m_scr.shape, NEG, f32)
    l_scr[...] = jnp.zeros_like(l_scr)
    acc_scr[...] = jnp.zeros_like(acc_scr)

    @pl.loop(0, qi)
    def _(j):
        for g in range(n_heads):
            s = scores(g, j)
            s_scr[g, j] = s
            fold_max(g, s)

    rowi = lax.broadcasted_iota(jnp.int32, (tq, tq), 0)
    coli = lax.broadcasted_iota(jnp.int32, (tq, tq), 1)
    for g in range(n_heads):
        s = jnp.where(coli <= rowi, scores(g, qi), NEG)
        s_scr[g, qi] = s
        fold_max(g, s)
        m_scr[g] = jnp.broadcast_to(jnp.max(m_scr[g], axis=1, keepdims=True), (tq, half))

    @pl.loop(0, qi + 1)
    def _(j):
        for g in range(n_heads):
            sj = s_scr[g, j]
            mb = m_scr[g]
            p_lo = jnp.exp2(sj[:, :half] - mb)
            p_hi = jnp.exp2(sj[:, half:] - mb)
            p = jnp.concatenate([p_lo, p_hi], axis=1).astype(bf16)
            l_scr[g] += p_lo + p_hi
            acc_scr[g] += _dot(p, v_at(g, pl.multiple_of(j * tq, tq)))

    return [acc_scr[g] / jnp.sum(l_scr[g], axis=1, keepdims=True) for g in range(n_heads)]


def _softmax_scratch(n_heads, nq, tq, dv):
    return [pltpu.VMEM((n_heads, nq, tq, tq), f32), pltpu.VMEM((n_heads, tq, tq // 2), f32),
            pltpu.VMEM((n_heads, tq, tq // 2), f32), pltpu.VMEM((n_heads, tq, dv), f32)]


def _mla_body(qn_ref, kvn_ref, kpe_ref, wq_ref, wqr_ref, wk_ref, wv_ref, place_ref, cq_ref, sq_ref,
              o_ref, k_scr, v_scr, s_scr, m_scr, l_scr, acc_scr):
    qi = pl.program_id(1)
    tq = qn_ref.shape[0]

    @pl.when(qi == 0)
    def _():
        kvn = kvn_ref[...]
        kpe_placed = _dot(kpe_ref[...], place_ref[...])
        for h in range(MLA_HEADS):
            k_scr[h] = (_dot(kvn, wk_ref[h]) + kpe_placed).astype(bf16)
            v_scr[h] = _dot(kvn, wv_ref[h]).astype(bf16)

    qn = qn_ref[...]
    cq = cq_ref[...]
    sq = sq_ref[...]
    q_heads = [(_dot(qn, wq_ref[h]) * cq + _dot(qn, wqr_ref[h]) * sq).astype(bf16)
               for h in range(MLA_HEADS)]
    outs = _causal_softmax_heads(
        q_heads, lambda g, start: k_scr[g, pl.ds(start, tq), :],
        lambda g, start: v_scr[g, pl.ds(start, tq), :], qi, s_scr, m_scr, l_scr, acc_scr)
    o_ref[...] = jnp.concatenate(outs, axis=1).astype(bf16)


def _mla_attention(qn, kvn, kpe, wq, wqr, wk, wv, place, cq, sq, batch, seq):
    tq = 256
    nq = seq // tq
    const = lambda a: pl.BlockSpec(a.shape, lambda b, i: (0,) * a.ndim)
    return pl.pallas_call(
        _mla_body,
        grid=(batch, nq),
        in_specs=[
            pl.BlockSpec((tq, 256), lambda b, i: (b * nq + i, 0)),
            pl.BlockSpec((seq, 128), lambda b, i: (b, 0)),
            pl.BlockSpec((seq, 128), lambda b, i: (b, 0)),
            const(wq), const(wqr), const(wk), const(wv), const(place),
            pl.BlockSpec((tq, 128), lambda b, i: (i, 0)),
            pl.BlockSpec((tq, 128), lambda b, i: (i, 0)),
        ],
        out_specs=pl.BlockSpec((tq, GROUP_W), lambda b, i: (b * nq + i, 0)),
        out_shape=jax.ShapeDtypeStruct((batch * seq, GROUP_W), bf16),
        scratch_shapes=[pltpu.VMEM((MLA_HEADS, seq, 128), bf16),
                        pltpu.VMEM((MLA_HEADS, seq, MLA_V), bf16)]
        + _softmax_scratch(MLA_HEADS, nq, tq, MLA_V),
        compiler_params=_cparams(("arbitrary", "arbitrary")),
        name="mla_attn",
    )(qn, kvn, kpe, wq, wqr, wk, wv, place, cq, sq)


def _diff_body(q_ref, k_ref, v_ref, lq1_ref, lk1_ref, lq2_ref, lk2_ref, g_ref,
               o_ref, v_scr, s_scr, m_scr, l_scr, acc_scr, *, lambda_init):
    qi = pl.program_id(1)
    tq = q_ref.shape[0]

    @pl.when(qi == 0)
    def _():
        v = v_ref[...]
        for h in range(DF_HEADS):
            v_scr[h] = v[:, DF_V * h:DF_V * (h + 1)]

    q = q_ref[...]
    lam = (jnp.exp(jnp.sum(lq1_ref[...] * lk1_ref[...], axis=1, keepdims=True))
           - jnp.exp(jnp.sum(lq2_ref[...] * lk2_ref[...], axis=1, keepdims=True)) + lambda_init)
    q_heads = [jnp.where(_lane_group_mask(q.shape, DF_DK * g, DF_DK * (g + 1)), q, jnp.zeros_like(q))
               for g in range(2 * DF_HEADS)]
    maps = _causal_softmax_heads(
        q_heads, lambda g, start: k_ref[pl.ds(start, tq), :],
        lambda g, start: v_scr[g // 2, pl.ds(start, tq), :], qi, s_scr, m_scr, l_scr, acc_scr)
    outs = []
    for h in range(DF_HEADS):
        o = maps[2 * h] - lam * maps[2 * h + 1]
        ms = jnp.mean(o * o, axis=1, keepdims=True)
        outs.append(o * lax.rsqrt(ms + RMS_EPS) * g_ref[...] * (1.0 - lambda_init))
    o_ref[...] = jnp.concatenate(outs, axis=1).astype(bf16)


def _diff_attention(dq, dk, dv, lq1, lk1, lq2, lk2, g_sub, batch, seq, lambda_init):
    tq = 256
    nq = seq // tq
    const = lambda a: pl.BlockSpec(a.shape, lambda b, i: (0,) * a.ndim)
    return pl.pallas_call(
        functools.partial(_diff_body, lambda_init=lambda_init),
        grid=(batch, nq),
        in_specs=[
            pl.BlockSpec((tq, 256), lambda b, i: (b * nq + i, 0)),
            pl.BlockSpec((seq, 256), lambda b, i: (b, 0)),
            pl.BlockSpec((seq, 256), lambda b, i: (b, 0)),
            const(lq1), const(lk1), const(lq2), const(lk2), const(g_sub),
        ],
        out_specs=pl.BlockSpec((tq, GROUP_W), lambda b, i: (b * nq + i, 0)),
        out_shape=jax.ShapeDtypeStruct((batch * seq, GROUP_W), bf16),
        scratch_shapes=[pltpu.VMEM((DF_HEADS, seq, DF_V), bf16)]
        + _softmax_scratch(2 * DF_HEADS, nq, tq, DF_V),
        compiler_params=_cparams(("arbitrary", "arbitrary")),
        name="diff_attn",
    )(dq, dk, dv, lq1, lk1, lq2, lk2, g_sub)


def _mlstm_body(q_ref, k_ref, v_ref, o_ref, g_ref, y_ref, c_scr, nm_scr):
    ci = pl.program_id(1)
    L, W = q_ref.shape
    H = ML_HEADS

    @pl.when(ci == 0)
    def _():
        c_scr[...] = jnp.zeros_like(c_scr)
        nm_scr[...] = jnp.zeros_like(nm_scr)

    q = q_ref[...]
    k = k_ref[...]
    v = v_ref[...]
    g = g_ref[...]
    lf = jnp.minimum(g, 0.0) - jnp.log(1.0 + jnp.exp(-jnp.abs(g)))
    r64 = lax.broadcasted_iota(jnp.int32, (L, L), 0)
    c64 = lax.broadcasted_iota(jnp.int32, (L, L), 1)
    tril = (r64 >= c64)
    bc = _dot(tril.astype(f32), lf, precision=HIGHEST)
    n_prev = nm_scr[0:1, :]
    m_prev_w = nm_scr[1:2, :]
    qf = q.astype(f32)
    kf = k.astype(f32)
    qn_all = qf * n_prev

    a_inter, hden, aend, gtot, mloc, num_heads = [], [], [], [], [], None
    for h in range(H):
        hm = _lane_group_mask((L, W), ML_DH * h, ML_DH * (h + 1))
        bcol = bc[:, 4 + h:5 + h]
        icol = g[:, h:h + 1]
        m_prev = m_prev_w[:, ML_DH * h:ML_DH * h + 1]
        rowpart = jnp.sum(jnp.where(r64 == c64, jnp.broadcast_to(icol - bcol, (L, L)), 0.0),
                          axis=0, keepdims=True)
        dmat = jnp.where(tril, bcol + rowpart, NEG)
        inter = bcol + m_prev
        m_j = jnp.maximum(inter, jnp.max(dmat, axis=1, keepdims=True))
        a_int = jnp.exp(inter - m_j)
        sqk = _dot_nt(jnp.where(hm, q, jnp.zeros_like(q)), k) * jnp.exp(dmat - m_j)
        numh = jnp.where(hm, _dot(sqk.astype(bf16), v), 0.0)
        num_heads = numh if num_heads is None else num_heads + numh
        qn_h = jnp.sum(jnp.where(hm, qn_all, 0.0), axis=1, keepdims=True)
        den = a_int * qn_h + jnp.sum(sqk, axis=1, keepdims=True)
        hden.append(jnp.maximum(jnp.abs(den), jnp.exp(-m_j)))
        a_inter.append(a_int)
        g_t = bcol[L - 1:L, :]
        wend = g_t - bcol + icol
        m_l = jnp.max(wend, axis=0, keepdims=True)
        aend.append(jnp.exp(wend - m_l))
        gtot.append(g_t)
        mloc.append(m_l)

    num = _dot(q, c_scr[...].astype(bf16)) * _expand_groups(a_inter, L, W, ML_DH) + num_heads
    y = num / _expand_groups(hden, L, W, ML_DH) * _sigmoid(o_ref[...].astype(f32))
    y_ref[...] = y.astype(y_ref.dtype)

    kw = kf * _expand_groups(aend, L, W, ML_DH)
    rw = lax.broadcasted_iota(jnp.int32, (W, W), 0)
    cw = lax.broadcasted_iota(jnp.int32, (W, W), 1)
    c_loc = jnp.where(rw // ML_DH == cw // ML_DH, _dot_tn(kw.astype(bf16), v), 0.0)
    n_loc = jnp.sum(kw, axis=0, keepdims=True)
    gtot_w = _expand_groups(gtot, 1, W, ML_DH)
    mloc_w = _expand_groups(mloc, 1, W, ML_DH)
    m_new = jnp.maximum(gtot_w + m_prev_w, mloc_w)
    a_w = jnp.exp(gtot_w + m_prev_w - m_new)
    b_w = jnp.exp(mloc_w - m_new)
    c_scr[...] = c_scr[...] * a_w + c_loc * b_w
    nm_scr[0:1, :] = a_w * n_prev + b_w * n_loc
    nm_scr[1:2, :] = m_new


def _mlstm(mq, mk, mv, mo, gcol, batch, seq):
    L = ML_CHUNK
    nc = seq // L
    blk = lambda w: pl.BlockSpec((L, w), lambda b, c: (b * nc + c, 0))
    return pl.pallas_call(
        _mlstm_body,
        grid=(batch, nc),
        in_specs=[blk(256), blk(256), blk(256), blk(256), blk(128)],
        out_specs=blk(GROUP_W),
        out_shape=jax.ShapeDtypeStruct((batch * seq, GROUP_W), bf16),
        scratch_shapes=[pltpu.VMEM((GROUP_W, GROUP_W), f32), pltpu.VMEM((8, GROUP_W), f32)],
        compiler_params=_cparams(("arbitrary", "arbitrary")),
        name="mlstm",
    )(mq, mk, mv, mo, gcol)


def _outproj_body(ya_ref, yb_ref, yc_ref, yd_ref, x_ref, mod_ref, w_ref, g_ref, b_ref, o_ref):
    mix = None
    for j, y_ref in enumerate((ya_ref, yb_ref, yc_ref, yd_ref)):
        part = _dot(y_ref[...], w_ref[GROUP_W * j:GROUP_W * (j + 1), :])
        mix = part if mix is None else mix + part
    r = DN_ALPHA * x_ref[...] + mod_ref[0, 2:3, :] * mix
    o_ref[...] = _layer_norm(r) * g_ref[...] + b_ref[...]


def _outproj(ya, yb, yc, yd, x2d, mod, w_out, ln_g, ln_b, seq):
    t, d = x2d.shape
    tm = 512
    tps = seq // tm
    row = lambda w: pl.BlockSpec((tm, w), lambda i: (i, 0))
    const = lambda a: pl.BlockSpec(a.shape, lambda i: (0,) * a.ndim)
    return pl.pallas_call(
        _outproj_body,
        grid=(t // tm,),
        in_specs=[row(256), row(256), row(256), row(256), row(d),
                  pl.BlockSpec((1, 6, d), lambda i: (i // tps, 0, 0)),
                  const(w_out), const(ln_g), const(ln_b)],
        out_specs=row(d),
        out_shape=jax.ShapeDtypeStruct((t, d), f32),
        compiler_params=_cparams(("arbitrary",)),
        name="outproj",
    )(ya, yb, yc, yd, x2d, mod, w_out, ln_g, ln_b)


def _route(scores, sel):
    tm, ne = sel.shape
    lane = lax.broadcasted_iota(jnp.int32, (tm, ne), 1)
    grp = lane // E_PER_GROUP
    big = ne + 1

    def top2(vals):
        m1 = jnp.max(vals, axis=1, keepdims=True)
        i1 = jnp.min(jnp.where(vals == m1, lane, big), axis=1, keepdims=True)
        rest = jnp.where(lane == i1, NEG, vals)
        m2 = jnp.max(rest, axis=1, keepdims=True)
        i2 = jnp.min(jnp.where(rest == m2, lane, big), axis=1, keepdims=True)
        return m1, i1, m2, i2

    best_score, best_grp = None, None
    for gi in range(N_GROUPS):
        m1, _, m2, _ = top2(jnp.where(grp == gi, sel, NEG))
        sc = m1 + m2
        if best_score is None:
            best_score, best_grp = sc, jnp.zeros_like(sc, dtype=jnp.int32)
        else:
            better = sc > best_score
            best_grp = jnp.where(better, gi, best_grp)
            best_score = jnp.where(better, sc, best_score)
    _, i1, _, i2 = top2(jnp.where(grp == best_grp, sel, NEG))
    picked = jnp.where((lane == i1) | (lane == i2), scores, 0.0)
    return picked / jnp.sum(picked, axis=1, keepdims=True)


def _moe_body(x_ref, mod_ref, wr_ref, br_ref, w1_ref, w3_ref, w2_ref, g_ref, b_ref, o_ref,
              u_scr, gate_scr, acc_scr):
    e = pl.program_id(1)

    @pl.when(e == 0)
    def _():
        u = _layer_norm(x_ref[...]) * (1.0 + mod_ref[0, 4:5, :]) + mod_ref[0, 3:4, :]
        u_scr[...] = u.astype(bf16)
        scores = _sigmoid(_dot(u, wr_ref[...], precision=HIGHEST))
        gate_scr[...] = _route(scores, scores + br_ref[...])
        acc_scr[...] = jnp.zeros_like(acc_scr)

    u = u_scr[...]
    gates = gate_scr[...]
    lane = lax.broadcasted_iota(jnp.int32, gates.shape, 1)
    ge = jnp.sum(jnp.where(lane == e, gates, 0.0), axis=1, keepdims=True)
    h1 = _dot(u, w1_ref[0])
    h = h1 * _sigmoid(h1) * _dot(u, w3_ref[0]) * ge
    acc_scr[...] += _dot(h.astype(bf16), w2_ref[0])

    @pl.when(e == pl.num_programs(1) - 1)
    def _():
        r = DN_ALPHA * x_ref[...] + mod_ref[0, 5:6, :] * acc_scr[...]
        o_ref[...] = _layer_norm(r) * g_ref[...] + b_ref[...]


def _moe(x2d, mod, w_router, b_router, w1, w3, w2, ln_g, ln_b, seq):
    t, d = x2d.shape
    ne, _, fe = w1.shape
    tm = 1024
    tps = seq // tm
    const = lambda a: pl.BlockSpec(a.shape, lambda i, e: (0,) * a.ndim)
    return pl.pallas_call(
        _moe_body,
        grid=(t // tm, ne),
        in_specs=[
            pl.BlockSpec((tm, d), lambda i, e: (i, 0)),
            pl.BlockSpec((1, 6, d), lambda i, e: (i // tps, 0, 0)),
            const(w_router), const(b_router),
            pl.BlockSpec((1, d, fe), lambda i, e: (e, 0, 0)),
            pl.BlockSpec((1, d, fe), lambda i, e: (e, 0, 0)),
            pl.BlockSpec((1, fe, d), lambda i, e: (e, 0, 0)),
            const(ln_g), const(ln_b),
        ],
        out_specs=pl.BlockSpec((tm, d), lambda i, e: (i, 0)),
        out_shape=jax.ShapeDtypeStruct((t, d), f32),
        scratch_shapes=[pltpu.VMEM((tm, d), bf16), pltpu.VMEM((tm, ne), f32), pltpu.VMEM((tm, d), f32)],
        compiler_params=_cparams(("arbitrary", "arbitrary")),
        name="moe",
    )(x2d, mod, w_router, b_router, w1, w3, w2, ln_g, ln_b)


def _pad_cols(a, width):
    return jnp.pad(a, ((0, 0), (0, width - a.shape[1])))


def _pack_w_in(w):
    parts, acc = [], 0
    for sz in IN_SIZES:
        parts.append(w[:, acc:acc + sz])
        acc += sz
    (a_q, a_kv, a_kr, m_q, m_k, m_v, m_i, m_f, m_o, c_b, c_c, c_h, d_q, d_k, d_v) = parts
    gates = _pad_cols(jnp.concatenate([m_i, m_f], axis=1), LANE)
    packed = jnp.concatenate([
        _pad_cols(a_q, 256), a_kv, _pad_cols(a_kr, LANE),
        m_q, m_k, m_v, gates, m_o, c_b, c_c, c_h, d_q, d_k, d_v], axis=1)
    return packed.astype(bf16)


def _pack_mla(w_uq, w_ukv):
    half = MLA_ROPE // 2
    wq3 = w_uq.reshape(MLA_Q_LORA, MLA_HEADS, MLA_NOPE + MLA_ROPE)
    wkv3 = w_ukv.reshape(MLA_KV_LORA, MLA_HEADS, MLA_NOPE + MLA_V)
    wq, wqr, wk, wv = [], [], [], []
    for h in range(MLA_HEADS):
        nope, pe = wq3[:, h, :MLA_NOPE], wq3[:, h, MLA_NOPE:]
        rot = jnp.concatenate([-pe[:, half:], pe[:, :half]], axis=1)
        zq = jnp.zeros((MLA_Q_LORA, MLA_NOPE), f32)
        pad_rows = lambda a: jnp.pad(a, ((0, 256 - MLA_Q_LORA), (0, 0)))
        wq.append(pad_rows(_pad_cols(jnp.concatenate([nope, pe], axis=1), LANE)))
        wqr.append(pad_rows(_pad_cols(jnp.concatenate([zq, rot], axis=1), LANE)))
        wk.append(_pad_cols(wkv3[:, h, :MLA_NOPE], LANE))
        wv.append(wkv3[:, h, MLA_NOPE:])
    st = lambda xs: jnp.stack(xs).astype(bf16)
    return st(wq), st(wqr), st(wk), st(wv)


def _rope_tables(seq):
    half = DF_DK // 2
    inv = 1.0 / (ROPE_THETA ** (jnp.arange(0, DF_DK, 2, dtype=f32) / DF_DK))
    ang = jnp.arange(seq, dtype=f32)[:, None] * inv[None, :]
    cos, sin = jnp.cos(ang), jnp.sin(ang)
    cos_t = jnp.tile(jnp.concatenate([cos, cos], axis=1), (1, 256 // DF_DK))
    sin_t = jnp.tile(jnp.concatenate([-sin, sin], axis=1), (1, 256 // DF_DK))
    scale = (MLA_NOPE + MLA_ROPE) ** -0.5 * LOG2E
    ones = jnp.ones((seq, MLA_NOPE), f32)
    zeros = jnp.zeros((seq, MLA_NOPE), f32)
    cq = _pad_cols(jnp.concatenate([ones, cos, cos], axis=1) * scale, LANE)
    sq = _pad_cols(jnp.concatenate([zeros, sin, sin], axis=1) * scale, LANE)
    return cos_t, sin_t, cq, sq


def kernel(x, c, w_ada, b_ada, w_in, mla_g_q, mla_g_kv, mla_w_uq, mla_w_ukv, ml_b_i, ml_b_f, sc_w,
           df_lq1, df_lk1, df_lq2, df_lk2, df_g, w_out, ln1_g, ln1_b, w_router, b_router, w1, w3, w2,
           ln2_g, ln2_b):
    batch, seq, d = x.shape
    depth = w_in.shape[0]
    assert MLA_ROPE == DF_DK, "both rotary blocks share one table"
    cos_t, sin_t, cq, sq = _rope_tables(seq)
    place = jnp.zeros((LANE, LANE), f32).at[jnp.arange(MLA_ROPE), MLA_NOPE + jnp.arange(MLA_ROPE)].set(1.0).astype(bf16)
    mod_all = _ada_mod(c, w_ada, b_ada).reshape(depth, batch, 6, d)
    xf = x.reshape(batch * seq, d)
    for l in range(depth):
        mod = mod_all[l]
        w_packed = _pack_w_in(w_in[l])
        gq = _pad_cols(mla_g_q[l][None, :], 256)
        gkv = mla_g_kv[l][None, :]
        gbias = _pad_cols(jnp.concatenate([ml_b_i[l], ml_b_f[l]])[None, :], LANE)
        (qn, kvn, kpe, mq, mk, mv, mo, gcol, yc, dq, dk, dv) = _inproj(
            xf, mod, w_packed, cos_t, sin_t, gq, gkv, gbias, sc_w[l], seq)
        wq, wqr, wk, wv = _pack_mla(mla_w_uq[l], mla_w_ukv[l])
        ya = _mla_attention(qn, kvn, kpe, wq, wqr, wk, wv, place, cq, sq, batch, seq)
        yb = _mlstm(mq, mk, mv, mo, gcol, batch, seq)
        lambda_init = 0.8 - 0.6 * math.exp(-0.3 * l)
        yd = _diff_attention(dq, dk, dv, df_lq1[l][None, :], df_lk1[l][None, :], df_lq2[l][None, :],
                             df_lk2[l][None, :], df_g[l][None, :],
                             batch, seq, lambda_init)
        x1 = _outproj(ya, yb, yc, yd, xf, mod, w_out[l].astype(bf16), ln1_g[l][None, :], ln1_b[l][None, :], seq)
        xf = _moe(x1, mod, w_router, b_router[None, :], w1[l].astype(bf16), w3[l].astype(bf16),
                  w2[l].astype(bf16), ln2_g[l][None, :], ln2_b[l][None, :], seq)
    return xf.reshape(batch, seq, d)
```

```python
import functools
import math

import jax
import jax.numpy as jnp
from jax import lax
from jax.experimental import pallas as pl
from jax.experimental.pallas import tpu as pltpu

f32 = jnp.float32
bf16 = jnp.bfloat16
HIGHEST = lax.Precision.HIGHEST

GROUP_W = 256
MLA_HEADS, MLA_NOPE, MLA_ROPE, MLA_V = 4, 64, 32, 64
MLA_Q_LORA, MLA_KV_LORA = 192, 128
ML_HEADS, ML_DH, ML_CHUNK = 4, 64, 64
DF_HEADS, DF_DK, DF_V = 4, 32, 64
ROPE_THETA = 10000.0
N_EXPERTS, N_GROUPS, E_PER_GROUP, D_EXPERT = 16, 4, 4, 256
DEPTH = 2
DN_ALPHA = (2 * DEPTH) ** 0.25
LN_EPS = 1e-5
RMS_EPS = 1e-6
IN_SIZES = (192, 128, 32, 256, 256, 256, 4, 4, 256, 256, 256, 256, 256, 256, 256)

OFF_PQ, OFF_PKV, OFF_KR = 0, 256, 384
OFF_MQ, OFF_MK, OFF_MV, OFF_MG, OFF_MO = 512, 768, 1024, 1280, 1408
OFF_CB, OFF_CC, OFF_CH = 1664, 1920, 2176
OFF_DQ, OFF_DK, OFF_DV = 2432, 2688, 2944
IN_PACKED = 3200

LANE = 128
VMEM_LIMIT = 48 * 1024 * 1024
NEG = -1e30
LOG2E = math.log2(math.e)


def _cparams(sem):
    return pltpu.CompilerParams(dimension_semantics=sem, vmem_limit_bytes=VMEM_LIMIT)


def _sigmoid(x):
    return 1.0 / (1.0 + jnp.exp(-x))


def _layer_norm(x):
    mu = jnp.mean(x, axis=-1, keepdims=True)
    xc = x - mu
    var = jnp.mean(xc * xc, axis=-1, keepdims=True)
    return xc * lax.rsqrt(var + LN_EPS)


def _dot(a, b, **kw):
    return jnp.dot(a, b, preferred_element_type=f32, **kw)


def _dot_nt(a, b, **kw):
    return lax.dot_general(a, b, (((1,), (1,)), ((), ())), preferred_element_type=f32, **kw)


def _dot_tn(a, b, **kw):
    return lax.dot_general(a, b, (((0,), (0,)), ((), ())), preferred_element_type=f32, **kw)


def _expand_groups(cols, rows, width, group):
    lane = lax.broadcasted_iota(jnp.int32, (rows, width), 1)
    out = jnp.broadcast_to(cols[-1], (rows, width))
    for h in range(len(cols) - 2, -1, -1):
        out = jnp.where(lane < group * (h + 1), cols[h], out)
    return out


def _lane_group_mask(shape, lo, hi):
    lane = lax.broadcasted_iota(jnp.int32, shape, 1)
    return (lane >= lo) & (lane < hi)


def _ada_body(c_ref, w_ref, b_ref, o_ref):
    c = c_ref[...]
    ca = (c * _sigmoid(c)).astype(bf16)
    o_ref[0] = _dot(ca, w_ref[0].astype(bf16)) + b_ref[0]


def _ada_mod(c, w_ada, b_ada):
    depth, d, n = w_ada.shape
    b = c.shape[0]
    tn = 1536
    return pl.pallas_call(
        _ada_body,
        grid=(depth, n // tn),
        in_specs=[
            pl.BlockSpec((b, d), lambda l, j: (0, 0)),
            pl.BlockSpec((1, d, tn), lambda l, j: (l, 0, j)),
            pl.BlockSpec((1, 1, tn), lambda l, j: (l, 0, j)),
        ],
        out_specs=pl.BlockSpec((1, b, tn), lambda l, j: (l, 0, j)),
        out_shape=jax.ShapeDtypeStruct((depth, b, n), f32),
        compiler_params=_cparams(("arbitrary", "arbitrary")),
        name="ada_mod",
    )(c, w_ada, b_ada.reshape(depth, 1, n))


def _rope_lanes(x, cos, sin_signed):
    w = x.shape[1]
    lane = lax.broadcasted_iota(jnp.int32, x.shape, 1)
    rot = jnp.where(lane % 32 < 16, pltpu.roll(x, w - 16, axis=1), pltpu.roll(x, 16, axis=1))
    return x * cos + rot * sin_signed


def _inproj_body(x_ref, mod_ref, w_ref, cos_ref, sin_ref, gq_ref, gkv_ref, gb_ref, cw_ref,
                 qn_ref, kvn_ref, kpe_ref, mq_ref, mk_ref, mv_ref, mo_ref, gcol_ref, yc_ref,
                 dq_ref, dk_ref, dv_ref, carry_ref, *, tiles_per_seq):
    i = pl.program_id(0)
    tm = x_ref.shape[0]
    u = _layer_norm(x_ref[...]) * (1.0 + mod_ref[0, 1:2, :]) + mod_ref[0, 0:1, :]
    ub = u.astype(bf16)

    def seg(off, n):
        return _dot(ub, w_ref[:, off:off + n])

    cos = cos_ref[...]
    sin = sin_ref[...]

    pq = seg(OFF_PQ, 256)
    ms = jnp.sum(pq * pq, axis=-1, keepdims=True) * (1.0 / MLA_Q_LORA)
    qn_ref[...] = (pq * lax.rsqrt(ms + RMS_EPS) * gq_ref[...]).astype(bf16)
    pkv = seg(OFF_PKV, 128)
    ms = jnp.sum(pkv * pkv, axis=-1, keepdims=True) * (1.0 / MLA_KV_LORA)
    kvn_ref[...] = (pkv * lax.rsqrt(ms + RMS_EPS) * gkv_ref[...]).astype(bf16)
    kpe_ref[...] = _rope_lanes(seg(OFF_KR, 128), cos[:, :LANE], sin[:, :LANE]).astype(bf16)

    mq_ref[...] = (seg(OFF_MQ, 256) * (ML_DH ** -0.5)).astype(bf16)
    mk_ref[...] = seg(OFF_MK, 256).astype(bf16)
    mv_ref[...] = seg(OFF_MV, 256).astype(bf16)
    gcol_ref[...] = seg(OFF_MG, 128) + gb_ref[...]
    mo_ref[...] = seg(OFF_MO, 256).astype(bf16)

    uc = seg(OFF_CC, 256) * seg(OFF_CH, 256)

    @pl.when(i % tiles_per_seq == 0)
    def _():
        carry_ref[...] = jnp.zeros_like(carry_ref)

    prev = carry_ref[...]
    row =lax.broadcasted_iota(jnp.int32, uc.shape, 0)
    u1 = jnp.where(row == 0, prev[7:8, :], pltpu.roll(uc, 1, axis=0))
    u2 = jnp.where(row == 0, prev[6:7, :], jnp.where(row == 1, prev[7:8, :], pltpu.roll(uc, 2, axis=0)))
    carry_ref[...] = uc[tm - 8:, :]
    conv = cw_ref[0:1, :] * u2 + cw_ref[1:2, :] * u1 + cw_ref[2:3, :] * uc
    yc_ref[...] = (seg(OFF_CB, 256) * conv).astype(bf16)

    dq_ref[...] = (_rope_lanes(seg(OFF_DQ, 256), cos, sin) * (DF_DK ** -0.5 * LOG2E)).astype(bf16)
    dk_ref[...] = _rope_lanes(seg(OFF_DK, 256), cos, sin).astype(bf16)
    dv_ref[...] = seg(OFF_DV, 256).astype(bf16)


def _inproj(x2d, mod, w_packed, cos_t, sin_t, gq, gkv, gbias, conv_w, seq):
    t, d = x2d.shape
    tm = 512
    tps = seq // tm
    row = lambda w: pl.BlockSpec((tm, w), lambda i: (i, 0))
    const = lambda a: pl.BlockSpec(a.shape, lambda i: (0,) * a.ndim)
    widths = (256, 128, 128, 256, 256, 256, 256, 128, 256, 256, 256, 256)
    dtypes = (bf16, bf16, bf16, bf16, bf16, bf16, bf16, f32, bf16, bf16, bf16, bf16)
    return pl.pallas_call(
        functools.partial(_inproj_body, tiles_per_seq=tps),
        grid=(t // tm,),
        in_specs=[
            row(d),
            pl.BlockSpec((1, 6, d), lambda i: (i // tps, 0, 0)),
            const(w_packed),
            pl.BlockSpec((tm, 256), lambda i: (i % tps, 0)),
            pl.BlockSpec((tm, 256), lambda i: (i % tps, 0)),
            const(gq), const(gkv), const(gbias), const(conv_w),
        ],
        out_specs=[row(w) for w in widths],
        out_shape=[jax.ShapeDtypeStruct((t, w), dt) for w, dt in zip(widths, dtypes)],
        scratch_shapes=[pltpu.VMEM((8, 256), f32)],
        compiler_params=_cparams(("arbitrary",)),
        name="inproj",
    )(x2d, mod, w_packed, cos_t, sin_t, gq, gkv, gbias, conv_w)


def _causal_softmax_heads(q_heads, k_at, v_at, qi, s_scr, m_scr, l_scr, acc_scr):
    n_heads = len(q_heads)
    tq = q_heads[0].shape[0]
    half = tq // 2

    def scores(g, j):
        return _dot_nt(q_heads[g], k_at(g, pl.multiple_of(j * tq, tq)))

    def fold_max(g, s):
        m_scr[g] = jnp.maximum(m_scr[g], jnp.maximum(s[:, :half], s[:, half:]))

    m_scr[...] = jnp.full(m_scr.shape, NEG, f32)
    l_scr[...] = jnp.zeros_like(l_scr)
    acc_scr[...] = jnp.zeros_like(acc_scr)

    @pl.loop(0, qi)
    def _(j):
        for g in range(n_heads):
            s = scores(g, j)
            s_scr[g, j] = s
            fold_max(g, s)

    rowi = lax.broadcasted_iota(jnp.int32, (tq, tq), 0)
    coli = lax.broadcasted_iota(jnp.int32, (tq, tq), 1)
    for g in range(n_heads):
        s = jnp.where(coli <= rowi, scores(g, qi), NEG)
        s_scr[g, qi] = s
        fold_max(g, s)
        m_scr[g] = jnp.broadcast_to(jnp.max(m_scr[g], axis=1, keepdims=True), (tq, half))

    @pl.loop(0, qi + 1)
    def _(j):
        for g in range(n_heads):
            sj = s_scr[g, j]
            mb = m_scr[g]
            p_lo = jnp.exp2(sj[:, :half] - mb)
            p_hi = jnp.exp2(sj[:, half:] - mb)
            p = jnp.concatenate([p_lo, p_hi], axis=1).astype(bf16)
            l_scr[g] += p_lo + p_hi
            acc_scr[g] += _dot(p, v_at(g, pl.multiple_of(j * tq, tq)))

    return [acc_scr[g] / jnp.sum(l_scr[g], axis=1, keepdims=True) for g in range(n_heads)]


def _softmax_scratch(n_heads, nq, tq, dv):
    return [pltpu.VMEM((n_heads, nq, tq, tq), f32), pltpu.VMEM((n_heads, tq, tq // 2), f32),
            pltpu.VMEM((n_heads, tq, tq // 2), f32), pltpu.VMEM((n_heads, tq, dv), f32)]


def _mla_body(qn_ref, kvn_ref, kpe_ref, wq_ref, wqr_ref, wk_ref, wv_ref, place_ref, cq_ref, sq_ref,
              o_ref, k_scr, v_scr, s_scr, m_scr, l_scr, acc_scr):
    qi = pl.program_id(1)
    tq = qn_ref.shape[0]

    @pl.when(qi == 0)
    def _():
        kvn = kvn_ref[...]
        kpe_placed = _dot(kpe_ref[...], place_ref[...])
        for h in range(MLA_HEADS):
            k_scr[h] = (_dot(kvn, wk_ref[h]) + kpe_placed).astype(bf16)
            v_scr[h] = _dot(kvn, wv_ref[h]).astype(bf16)

    qn = qn_ref[...]
    cq = cq_ref[...]
    sq = sq_ref[...]
    q_heads = [(_dot(qn, wq_ref[h]) * cq + _dot(qn, wqr_ref[h]) * sq).astype(bf16)
               for h in range(MLA_HEADS)]
    outs = _causal_softmax_heads(
        q_heads, lambda g, start: k_scr[g, pl.ds(start, tq), :],
        lambda g, start: v_scr[g, pl.ds(start, tq), :], qi, s_scr, m_scr, l_scr, acc_scr)
    o_ref[...] = jnp.concatenate(outs, axis=1).astype(bf16)


def _mla_attention(qn, kvn, kpe, wq, wqr, wk, wv, place, cq, sq, batch, seq):
    tq = 256
    nq = seq // tq
    const = lambda a: pl.BlockSpec(a.shape, lambda b, i: (0,) * a.ndim)
    return pl.pallas_call(
        _mla_body,
        grid=(batch, nq),
        in_specs=[
            pl.BlockSpec((tq, 256), lambda b, i: (b * nq + i, 0)),
            pl.BlockSpec((seq, 128), lambda b, i: (b, 0)),
            pl.BlockSpec((seq, 128), lambda b, i: (b, 0)),
            const(wq), const(wqr), const(wk), const(wv), const(place),
            pl.BlockSpec((tq, 128), lambda b, i: (i, 0)),
            pl.BlockSpec((tq, 128), lambda b, i: (i, 0)),
        ],
        out_specs=pl.BlockSpec((tq, GROUP_W), lambda b, i: (b * nq + i, 0)),
        out_shape=jax.ShapeDtypeStruct((batch * seq, GROUP_W), bf16),
        scratch_shapes=[pltpu.VMEM((MLA_HEADS, seq, 128), bf16),
                        pltpu.VMEM((MLA_HEADS, seq, MLA_V), bf16)]
        + _softmax_scratch(MLA_HEADS, nq, tq, MLA_V),
        compiler_params=_cparams(("arbitrary", "arbitrary")),
        name="mla_attn",
    )(qn, kvn, kpe, wq, wqr, wk, wv, place, cq, sq)


def _diff_body(q_ref, k_ref, v_ref, lq1_ref, lk1_ref, lq2_ref, lk2_ref, g_ref,
               o_ref, v_scr, s_scr, m_scr, l_scr, acc_scr, *, lambda_init):
    qi = pl.program_id(1)
    tq = q_ref.shape[0]

    @pl.when(qi == 0)
    def _():
        v = v_ref[...]
        for h in range(DF_HEADS):
            v_scr[h] = v[:, DF_V * h:DF_V * (h + 1)]

    q = q_ref[...]
    lam = (jnp.exp(jnp.sum(lq1_ref[...] * lk1_ref[...], axis=1, keepdims=True))
           - jnp.exp(jnp.sum(lq2_ref[...] * lk2_ref[...], axis=1, keepdims=True)) + lambda_init)
    q_heads = [jnp.where(_lane_group_mask(q.shape, DF_DK * g, DF_DK * (g + 1)), q, jnp.zeros_like(q))
               for g in range(2 * DF_HEADS)]
    maps = _causal_softmax_heads(
        q_heads, lambda g, start: k_ref[pl.ds(start, tq), :],
        lambda g, start: v_scr[g // 2, pl.ds(start, tq), :], qi, s_scr, m_scr, l_scr, acc_scr)
    outs = []
    for h in range(DF_HEADS):
        o = maps[2 * h] - lam * maps[2 * h + 1]
        ms = jnp.mean(o * o, axis=1, keepdims=True)
        outs.append(o * lax.rsqrt(ms + RMS_EPS) * g_ref[...] * (1.0 - lambda_init))
    o_ref[...] = jnp.concatenate(outs, axis=1).astype(bf16)


def _diff_attention(dq, dk, dv, lq1, lk1, lq2, lk2, g_sub, batch, seq, lambda_init):
    tq = 256
    nq = seq // tq
    const = lambda a: pl.BlockSpec(a.shape, lambda b, i: (0,) * a.ndim)
    return pl.pallas_call(
        functools.partial(_diff_body, lambda_init=lambda_init),
        grid=(batch, nq),
        in_specs=[
            pl.BlockSpec((tq, 256), lambda b, i: (b * nq + i, 0)),
            pl.BlockSpec((seq, 256), lambda b, i: (b, 0)),
            pl.BlockSpec((seq, 256), lambda b, i: (b, 0)),
            const(lq1), const(lk1), const(lq2), const(lk2), const(g_sub),
        ],
        out_specs=pl.BlockSpec((tq, GROUP_W), lambda b, i: (b * nq + i, 0)),
        out_shape=jax.ShapeDtypeStruct((batch * seq, GROUP_W), bf16),
        scratch_shapes=[pltpu.VMEM((DF_HEADS, seq, DF_V), bf16)]
        + _softmax_scratch(2 * DF_HEADS, nq, tq, DF_V),
        compiler_params=_cparams(("arbitrary", "arbitrary")),
        name="diff_attn",
    )(dq, dk, dv, lq1, lk1, lq2, lk2, g_sub)


def _mlstm_chunk(q, k, v, o, g, c_prev, n_prev, m_prev_w):
    L, W = q.shape
    H = ML_HEADS
    lf = jnp.minimum(g, 0.0) - jnp.log(1.0 + jnp.exp(-jnp.abs(g)))
    r64 = lax.broadcasted_iota(jnp.int32, (L, L), 0)
    c64 = lax.broadcasted_iota(jnp.int32, (L, L), 1)
    tril = (r64 >= c64)
    bc = _dot(tril.astype(f32), lf, precision=HIGHEST)
    qf = q.astype(f32)
    kf = k.astype(f32)
    qn_all = qf * n_prev

    a_inter, hden, aend, gtot, mloc, num_heads = [], [], [], [], [], None
    for h in range(H):
        hm = _lane_group_mask((L, W), ML_DH * h, ML_DH * (h + 1))
        bcol = bc[:, 4 + h:5 + h]
        icol = g[:, h:h + 1]
        m_prev = m_prev_w[:, ML_DH * h:ML_DH * h + 1]
        rowpart = jnp.sum(jnp.where(r64 == c64, jnp.broadcast_to(icol - bcol, (L, L)), 0.0),
                          axis=0, keepdims=True)
        dmat = jnp.where(tril, bcol + rowpart, NEG)
        inter = bcol + m_prev
        m_j = jnp.maximum(inter, jnp.max(dmat, axis=1, keepdims=True))
        a_int = jnp.exp(inter - m_j)
        sqk = _dot_nt(jnp.where(hm, q, jnp.zeros_like(q)), k) * jnp.exp(dmat - m_j)
        numh = jnp.where(hm, _dot(sqk.astype(bf16), v), 0.0)
        num_heads = numh if num_heads is None else num_heads + numh
        qn_h = jnp.sum(jnp.where(hm, qn_all, 0.0), axis=1, keepdims=True)
        den = a_int * qn_h + jnp.sum(sqk, axis=1, keepdims=True)
        hden.append(jnp.maximum(jnp.abs(den), jnp.exp(-m_j)))
        a_inter.append(a_int)
        g_t = bcol[L - 1:L, :]
        wend = g_t - bcol + icol
        m_l = jnp.max(wend, axis=0, keepdims=True)
        aend.append(jnp.exp(wend - m_l))
        gtot.append(g_t)
        mloc.append(m_l)

    num = _dot(q, c_prev.astype(bf16)) * _expand_groups(a_inter, L, W, ML_DH) + num_heads
    y = num / _expand_groups(hden, L, W, ML_DH) * _sigmoid(o.astype(f32))

    kw = kf * _expand_groups(aend, L, W, ML_DH)
    rw = lax.broadcasted_iota(jnp.int32, (W, W), 0)
    cw = lax.broadcasted_iota(jnp.int32, (W, W), 1)
    c_loc = jnp.where(rw // ML_DH == cw // ML_DH, _dot_tn(kw.astype(bf16), v), 0.0)
    n_loc = jnp.sum(kw, axis=0, keepdims=True)
    gtot_w = _expand_groups(gtot, 1, W, ML_DH)
    mloc_w = _expand_groups(mloc, 1, W, ML_DH)
    m_new = jnp.maximum(gtot_w + m_prev_w, mloc_w)
    a_w = jnp.exp(gtot_w + m_prev_w - m_new)
    b_w = jnp.exp(mloc_w - m_new)
    return y, c_prev * a_w + c_loc * b_w, a_w * n_prev + b_w * n_loc, m_new


def _mlstm_body(q_ref, k_ref, v_ref, o_ref, g_ref, y_ref, c_scr, nm_scr):
    @pl.when(pl.program_id(1) == 0)
    def _():
        c_scr[...] = jnp.zeros_like(c_scr)
        nm_scr[...] = jnp.zeros_like(nm_scr)

    c, n, m = c_scr[...], nm_scr[0:1, :], nm_scr[1:2, :]
    for i in range(q_ref.shape[0] // ML_CHUNK):
        rows = pl.ds(i * ML_CHUNK, ML_CHUNK)
        y, c, n, m = _mlstm_chunk(q_ref[rows, :], k_ref[rows, :], v_ref[rows, :], o_ref[rows, :],
                                  g_ref[rows, :], c, n, m)
        y_ref[rows, :] = y.astype(y_ref.dtype)
    c_scr[...] = c
    nm_scr[0:1, :] = n
    nm_scr[1:2, :] = m


def _mlstm(mq, mk, mv, mo, gcol, batch, seq):
    L = ML_CHUNK * 4
    nc = seq // L
    blk = lambda w: pl.BlockSpec((L, w), lambda b, c: (b * nc + c, 0))
    return pl.pallas_call(
        _mlstm_body,
        grid=(batch, nc),
        in_specs=[blk(256), blk(256), blk(256), blk(256), blk(128)],
        out_specs=blk(GROUP_W),
        out_shape=jax.ShapeDtypeStruct((batch * seq, GROUP_W), bf16),
        scratch_shapes=[pltpu.VMEM((GROUP_W, GROUP_W), f32), pltpu.VMEM((8, GROUP_W), f32)],
        compiler_params=_cparams(("arbitrary", "arbitrary")),
        name="mlstm",
    )(mq, mk, mv, mo, gcol)


def _outproj_body(ya_ref, yb_ref, yc_ref, yd_ref, x_ref, mod_ref, w_ref, g_ref, b_ref, o_ref):
    mix = None
    for j, y_ref in enumerate((ya_ref, yb_ref, yc_ref, yd_ref)):
        part = _dot(y_ref[...], w_ref[GROUP_W * j:GROUP_W * (j + 1), :])
        mix = part if mix is None else mix + part
    r = DN_ALPHA * x_ref[...] + mod_ref[0, 2:3, :] * mix
    o_ref[...] = _layer_norm(r) * g_ref[...] + b_ref[...]


def _outproj(ya, yb, yc, yd, x2d, mod, w_out, ln_g, ln_b, seq):
    t, d = x2d.shape
    tm = 512
    tps = seq // tm
    row = lambda w: pl.BlockSpec((tm, w), lambda i: (i, 0))
    const = lambda a: pl.BlockSpec(a.shape, lambda i: (0,) * a.ndim)
    return pl.pallas_call(
        _outproj_body,
        grid=(t // tm,),
        in_specs=[row(256), row(256), row(256), row(256), row(d),
                  pl.BlockSpec((1, 6, d), lambda i: (i // tps, 0, 0)),
                  const(w_out), const(ln_g), const(ln_b)],
        out_specs=row(d),
        out_shape=jax.ShapeDtypeStruct((t, d), f32),
        compiler_params=_cparams(("arbitrary",)),
        name="outproj",
    )(ya, yb, yc, yd, x2d, mod, w_out, ln_g, ln_b)


def _route(scores, sel):
    tm, ne = sel.shape
    lane = lax.broadcasted_iota(jnp.int32, (tm, ne), 1)
    grp = lane // E_PER_GROUP
    big = ne + 1

    def top2(vals):
        m1 = jnp.max(vals, axis=1, keepdims=True)
        i1 = jnp.min(jnp.where(vals == m1, lane, big), axis=1, keepdims=True)
        rest = jnp.where(lane == i1, NEG, vals)
        m2 = jnp.max(rest, axis=1, keepdims=True)
        i2 = jnp.min(jnp.where(rest == m2, lane, big), axis=1, keepdims=True)
        return m1, i1, m2, i2

    best_score, best_grp = None, None
    for gi in range(N_GROUPS):
        m1, _, m2, _ = top2(jnp.where(grp == gi, sel, NEG))
        sc = m1 + m2
        if best_score is None:
            best_score, best_grp = sc, jnp.zeros_like(sc, dtype=jnp.int32)
        else:
            better = sc > best_score
            best_grp = jnp.where(better, gi, best_grp)
            best_score = jnp.where(better, sc, best_score)
    _, i1, _, i2 = top2(jnp.where(grp == best_grp, sel, NEG))
    picked = jnp.where((lane == i1) | (lane == i2), scores, 0.0)
    return picked / jnp.sum(picked, axis=1, keepdims=True)


def _moe_body(x_ref, mod_ref, wr_ref, br_ref, w1_ref, w3_ref, w2_ref, g_ref, b_ref, o_ref,
              u_scr, gate_scr, acc_scr):
    e = pl.program_id(1)

    @pl.when(e == 0)
    def _():
        u = _layer_norm(x_ref[...]) * (1.0 + mod_ref[0, 4:5, :]) + mod_ref[0, 3:4, :]
        u_hi = u.astype(bf16)
        u_lo = (u - u_hi.astype(f32)).astype(bf16)
        u_scr[...] = u_hi
        ne = br_ref.shape[1]
        part = _dot(u_hi, wr_ref[...])
        logits = part[:, :ne] + part[:, ne:] + _dot(u_lo, wr_ref[:, :ne])
        scores = _sigmoid(logits)
        gate_scr[...] = _route(scores, scores + br_ref[...])
        acc_scr[...] = jnp.zeros_like(acc_scr)

    u = u_scr[...]
    gates = gate_scr[...]
    lane = lax.broadcasted_iota(jnp.int32, gates.shape, 1)
    ge = jnp.sum(jnp.where(lane == e, gates, 0.0), axis=1, keepdims=True)
    h1 = _dot(u, w1_ref[0])
    h = h1 * _sigmoid(h1) * _dot(u, w3_ref[0]) * ge
    acc_scr[...] += _dot(h.astype(bf16), w2_ref[0])

    @pl.when(e == pl.num_programs(1) - 1)
    def _():
        r = DN_ALPHA * x_ref[...] + mod_ref[0, 5:6, :] * acc_scr[...]
        o_ref[...] = _layer_norm(r) * g_ref[...] + b_ref[...]


def _moe(x2d, mod, w_router, b_router, w1, w3, w2, ln_g, ln_b, seq):
    t, d = x2d.shape
    ne, _, fe = w1.shape
    tm = 1024
    tps = seq // tm
    const = lambda a: pl.BlockSpec(a.shape, lambda i, e: (0,) * a.ndim)
    return pl.pallas_call(
        _moe_body,
        grid=(t // tm, ne),
        in_specs=[
            pl.BlockSpec((tm, d), lambda i, e: (i, 0)),
            pl.BlockSpec((1, 6, d), lambda i, e: (i // tps, 0, 0)),
            const(w_router), const(b_router),
            pl.BlockSpec((1, d, fe), lambda i, e: (e, 0, 0)),
            pl.BlockSpec((1, d, fe), lambda i, e: (e, 0, 0)),
            pl.BlockSpec((1, fe, d), lambda i, e: (e, 0, 0)),
            const(ln_g), const(ln_b),
        ],
        out_specs=pl.BlockSpec((tm, d), lambda i, e: (i, 0)),
        out_shape=jax.ShapeDtypeStruct((t, d), f32),
        scratch_shapes=[pltpu.VMEM((tm, d), bf16), pltpu.VMEM((tm, ne), f32), pltpu.VMEM((tm, d), f32)],
        compiler_params=_cparams(("arbitrary", "arbitrary")),
        name="moe",
    )(x2d, mod, w_router, b_router, w1, w3, w2, ln_g, ln_b)


def _pad_cols(a, width):
    return jnp.pad(a, ((0, 0), (0, width - a.shape[1])))


def _pack_w_in(w):
    parts, acc = [], 0
    for sz in IN_SIZES:
        parts.append(w[:, acc:acc + sz])
        acc += sz
    (a_q, a_kv, a_kr, m_q, m_k, m_v, m_i, m_f, m_o, c_b, c_c, c_h, d_q, d_k, d_v) = parts
    gates = _pad_cols(jnp.concatenate([m_i, m_f], axis=1), LANE)
    packed = jnp.concatenate([
        _pad_cols(a_q, 256), a_kv, _pad_cols(a_kr, LANE),
        m_q, m_k, m_v, gates, m_o, c_b, c_c, c_h, d_q, d_k, d_v], axis=1)
    return packed.astype(bf16)


def _pack_mla(w_uq, w_ukv):
    half = MLA_ROPE // 2
    wq3 = w_uq.reshape(MLA_Q_LORA, MLA_HEADS, MLA_NOPE + MLA_ROPE)
    wkv3 = w_ukv.reshape(MLA_KV_LORA, MLA_HEADS, MLA_NOPE + MLA_V)
    wq, wqr, wk, wv = [], [], [], []
    for h in range(MLA_HEADS):
        nope, pe = wq3[:, h, :MLA_NOPE], wq3[:, h, MLA_NOPE:]
        rot = jnp.concatenate([-pe[:, half:], pe[:, :half]], axis=1)
        zq = jnp.zeros((MLA_Q_LORA, MLA_NOPE), f32)
        pad_rows = lambda a: jnp.pad(a, ((0, 256 - MLA_Q_LORA), (0, 0)))
        wq.append(pad_rows(_pad_cols(jnp.concatenate([nope, pe], axis=1), LANE)))
        wqr.append(pad_rows(_pad_cols(jnp.concatenate([zq, rot], axis=1), LANE)))
        wk.append(_pad_cols(wkv3[:, h, :MLA_NOPE], LANE))
        wv.append(wkv3[:, h, MLA_NOPE:])
    st = lambda xs: jnp.stack(xs).astype(bf16)
    return st(wq), st(wqr), st(wk), st(wv)


def _rope_tables(seq):
    half = DF_DK // 2
    inv = 1.0 / (ROPE_THETA ** (jnp.arange(0, DF_DK, 2, dtype=f32) / DF_DK))
    ang = jnp.arange(seq, dtype=f32)[:, None] * inv[None, :]
    cos, sin = jnp.cos(ang), jnp.sin(ang)
    cos_t = jnp.tile(jnp.concatenate([cos, cos], axis=1), (1, 256 // DF_DK))
    sin_t = jnp.tile(jnp.concatenate([-sin, sin], axis=1), (1, 256 // DF_DK))
    scale = (MLA_NOPE + MLA_ROPE) ** -0.5 * LOG2E
    ones = jnp.ones((seq, MLA_NOPE), f32)
    zeros = jnp.zeros((seq, MLA_NOPE), f32)
    cq = _pad_cols(jnp.concatenate([ones, cos, cos], axis=1) * scale, LANE)
    sq = _pad_cols(jnp.concatenate([zeros, sin, sin], axis=1) * scale, LANE)
    return cos_t, sin_t, cq, sq


def kernel(x, c, w_ada, b_ada, w_in, mla_g_q, mla_g_kv, mla_w_uq, mla_w_ukv, ml_b_i, ml_b_f, sc_w,
           df_lq1, df_lk1, df_lq2, df_lk2, df_g, w_out, ln1_g, ln1_b, w_router, b_router, w1, w3, w2,
           ln2_g, ln2_b):
    batch, seq, d = x.shape
    depth = w_in.shape[0]
    assert MLA_ROPE == DF_DK, "both rotary blocks share one table"
    cos_t, sin_t, cq, sq = _rope_tables(seq)
    place = jnp.zeros((LANE, LANE), f32).at[jnp.arange(MLA_ROPE), MLA_NOPE + jnp.arange(MLA_ROPE)].set(1.0).astype(bf16)
    wr_hi = w_router.astype(bf16)
    wr_split = jnp.concatenate([wr_hi, (w_router - wr_hi.astype(f32)).astype(bf16)], axis=1)
    mod_all = _ada_mod(c, w_ada, b_ada).reshape(depth, batch, 6, d)
    xf = x.reshape(batch * seq, d)
    for l in range(depth):
        mod = mod_all[l]
        w_packed = _pack_w_in(w_in[l])
        gq = _pad_cols(mla_g_q[l][None, :], 256)
        gkv = mla_g_kv[l][None, :]
        gbias = _pad_cols(jnp.concatenate([ml_b_i[l], ml_b_f[l]])[None, :], LANE)
        (qn, kvn, kpe, mq, mk, mv, mo, gcol, yc, dq, dk, dv) = _inproj(
            xf, mod, w_packed, cos_t, sin_t, gq, gkv, gbias, sc_w[l], seq)
        wq, wqr, wk, wv = _pack_mla(mla_w_uq[l], mla_w_ukv[l])
        ya = _mla_attention(qn, kvn, kpe, wq, wqr, wk, wv, place, cq, sq, batch, seq)
        yb = _mlstm(mq, mk, mv, mo, gcol, batch, seq)
        lambda_init = 0.8 - 0.6 * math.exp(-0.3 * l)
        yd = _diff_attention(dq, dk, dv, df_lq1[l][None, :], df_lk1[l][None, :], df_lq2[l][None, :],
                             df_lk2[l][None, :], df_g[l][None, :],
                             batch, seq, lambda_init)
        x1 = _outproj(ya, yb, yc, yd, xf, mod, w_out[l].astype(bf16), ln1_g[l][None, :], ln1_b[l][None, :], seq)
        xf = _moe(x1, mod, wr_split, b_router[None, :], w1[l].astype(bf16), w3[l].astype(bf16),
                  w2[l].astype(bf16), ln2_g[l][None, :], ln2_b[l][None, :], seq)
    return xf.reshape(batch, seq, d)
```

```python
import functools
import math

import jax
import jax.numpy as jnp
from jax import lax
from jax.experimental import pallas as pl
from jax.experimental.pallas import tpu as pltpu

f32 = jnp.float32
bf16 = jnp.bfloat16
HIGHEST = lax.Precision.HIGHEST

GROUP_W = 256
MLA_HEADS, MLA_NOPE, MLA_ROPE, MLA_V = 4, 64, 32, 64
MLA_Q_LORA, MLA_KV_LORA = 192, 128
ML_HEADS, ML_DH, ML_CHUNK = 4, 64, 64
DF_HEADS, DF_DK, DF_V = 4, 32, 64
ROPE_THETA = 10000.0
N_EXPERTS, N_GROUPS, E_PER_GROUP, D_EXPERT = 16, 4, 4, 256
DEPTH = 2
DN_ALPHA = (2 * DEPTH) ** 0.25
LN_EPS = 1e-5
RMS_EPS = 1e-6
IN_SIZES = (192, 128, 32, 256, 256, 256, 4, 4, 256, 256, 256, 256, 256, 256, 256)

OFF_PQ, OFF_PKV, OFF_KR = 0, 256, 384
OFF_MQ, OFF_MK, OFF_MV, OFF_MG, OFF_MO = 512, 768, 1024, 1280, 1408
OFF_CB, OFF_CC, OFF_CH = 1664, 1920, 2176
OFF_DQ, OFF_DK, OFF_DV = 2432, 2688, 2944
IN_PACKED = 3200

LANE = 128
VMEM_LIMIT = 48 * 1024 * 1024
NEG = -1e30
LOG2E = math.log2(math.e)


def _cparams(sem):
    return pltpu.CompilerParams(dimension_semantics=sem, vmem_limit_bytes=VMEM_LIMIT)


def _sigmoid(x):
    return 1.0 / (1.0 + jnp.exp(-x))


def _layer_norm(x):
    mu = jnp.mean(x, axis=-1, keepdims=True)
    xc = x - mu
    var = jnp.mean(xc * xc, axis=-1, keepdims=True)
    return xc * lax.rsqrt(var + LN_EPS)


def _dot(a, b, **kw):
    return jnp.dot(a, b, preferred_element_type=f32, **kw)


def _dot_nt(a, b, **kw):
    return lax.dot_general(a, b, (((1,), (1,)), ((), ())), preferred_element_type=f32, **kw)


def _dot_tn(a, b, **kw):
    return lax.dot_general(a, b, (((0,), (0,)), ((), ())), preferred_element_type=f32, **kw)


def _expand_groups(cols, rows, width, group):
    lane = lax.broadcasted_iota(jnp.int32, (rows, width), 1)
    out = jnp.broadcast_to(cols[-1], (rows, width))
    for h in range(len(cols) - 2, -1, -1):
        out = jnp.where(lane < group * (h + 1), cols[h], out)
    return out


def _lane_group_mask(shape, lo, hi):
    lane = lax.broadcasted_iota(jnp.int32, shape, 1)
    return (lane >= lo) & (lane < hi)


def _ada_body(c_ref, w_ref, b_ref, o_ref):
    c = c_ref[...]
    ca = (c * _sigmoid(c)).astype(bf16)
    o_ref[0] = _dot(ca, w_ref[0].astype(bf16)) + b_ref[0]


def _ada_mod(c, w_ada, b_ada):
    depth, d, n = w_ada.shape
    b = c.shape[0]
    tn = 1536
    return pl.pallas_call(
        _ada_body,
        grid=(depth, n // tn),
        in_specs=[
            pl.BlockSpec((b, d), lambda l, j: (0, 0)),
            pl.BlockSpec((1, d, tn), lambda l, j: (l, 0, j)),
            pl.BlockSpec((1, 1, tn), lambda l, j: (l, 0, j)),
        ],
        out_specs=pl.BlockSpec((1, b, tn), lambda l, j: (l, 0, j)),
        out_shape=jax.ShapeDtypeStruct((depth, b, n), f32),
        compiler_params=_cparams(("arbitrary", "arbitrary")),
        name="ada_mod",
    )(c, w_ada, b_ada.reshape(depth, 1, n))


def _rope_lanes(x, cos, sin_signed):
    w = x.shape[1]
    lane = lax.broadcasted_iota(jnp.int32, x.shape, 1)
    rot = jnp.where(lane % 32 < 16, pltpu.roll(x, w - 16, axis=1), pltpu.roll(x, 16, axis=1))
    return x * cos + rot * sin_signed


def _inproj_body(x_ref, mod_ref, w_ref, wkt_ref, cos_ref, sin_ref, gq_ref, gkv_ref, gb_ref, cw_ref,
                 qn_ref, kvn_ref, kpe_ref, mq_ref, mk_ref, mv_ref, mo_ref, gcol_ref, yc_ref,
                 dq_ref, dk_ref, dv_ref, mkt_ref, carry_ref, *, tiles_per_seq):
    i = pl.program_id(0)
    tm = x_ref.shape[0]
    u = _layer_norm(x_ref[...]) * (1.0 + mod_ref[0, 1:2, :]) + mod_ref[0, 0:1, :]
    ub = u.astype(bf16)

    def seg(off, n):
        return _dot(ub, w_ref[:, off:off + n])

    cos = cos_ref[...]
    sin = sin_ref[...]

    pq = seg(OFF_PQ, 256)
    ms = jnp.sum(pq * pq, axis=-1, keepdims=True) * (1.0 / MLA_Q_LORA)
    qn_ref[...] = (pq * lax.rsqrt(ms + RMS_EPS) * gq_ref[...]).astype(bf16)
    pkv = seg(OFF_PKV, 128)
    ms = jnp.sum(pkv * pkv, axis=-1, keepdims=True) * (1.0 / MLA_KV_LORA)
    kvn_ref[...] = (pkv * lax.rsqrt(ms + RMS_EPS) * gkv_ref[...]).astype(bf16)
    kpe_ref[...] = _rope_lanes(seg(OFF_KR, 128), cos[:, :LANE], sin[:, :LANE]).astype(bf16)

    mq_ref[...] = (seg(OFF_MQ, 256) * (ML_DH ** -0.5)).astype(bf16)
    mk_ref[...] = seg(OFF_MK, 256).astype(bf16)
    mkt_ref[...] = _dot_nt(wkt_ref[...], ub).astype(bf16)
    mv_ref[...] = seg(OFF_MV, 256).astype(bf16)
    gcol_ref[...] = seg(OFF_MG, 128) + gb_ref[...]
    mo_ref[...] = seg(OFF_MO, 256).astype(bf16)

    uc = seg(OFF_CC, 256) * seg(OFF_CH, 256)

    @pl.when(i % tiles_per_seq == 0)
    def _():
        carry_ref[...] = jnp.zeros_like(carry_ref)

    prev = carry_ref[...]
    row =lax.broadcasted_iota(jnp.int32, uc.shape, 0)
    u1 = jnp.where(row == 0, prev[7:8, :], pltpu.roll(uc, 1, axis=0))
    u2 = jnp.where(row == 0, prev[6:7, :], jnp.where(row == 1, prev[7:8, :], pltpu.roll(uc, 2, axis=0)))
    carry_ref[...] = uc[tm - 8:, :]
    conv = cw_ref[0:1, :] * u2 + cw_ref[1:2, :] * u1 + cw_ref[2:3, :] * uc
    yc_ref[...] = (seg(OFF_CB, 256) * conv).astype(bf16)

    dq_ref[...] = (_rope_lanes(seg(OFF_DQ, 256), cos, sin) * (DF_DK ** -0.5 * LOG2E)).astype(bf16)
    dk_ref[...] = _rope_lanes(seg(OFF_DK, 256), cos, sin).astype(bf16)
    dv_ref[...] = seg(OFF_DV, 256).astype(bf16)


def _inproj(x2d, mod, w_packed, wkt, cos_t, sin_t, gq, gkv, gbias, conv_w, seq):
    t, d = x2d.shape
    tm = 512
    tps = seq // tm
    row = lambda w: pl.BlockSpec((tm, w), lambda i: (i, 0))
    const = lambda a: pl.BlockSpec(a.shape, lambda i: (0,) * a.ndim)
    widths = (256, 128, 128, 256, 256, 256, 256, 128, 256, 256, 256, 256)
    dtypes = (bf16, bf16, bf16, bf16, bf16, bf16, bf16, f32, bf16, bf16, bf16, bf16)
    return pl.pallas_call(
        functools.partial(_inproj_body, tiles_per_seq=tps),
        grid=(t // tm,),
        in_specs=[
            row(d),
            pl.BlockSpec((1, 6, d), lambda i: (i // tps, 0, 0)),
            const(w_packed), const(wkt),
            pl.BlockSpec((tm, 256), lambda i: (i % tps, 0)),
            pl.BlockSpec((tm, 256), lambda i: (i % tps, 0)),
            const(gq), const(gkv), const(gbias), const(conv_w),
        ],
        out_specs=[row(w) for w in widths] + [pl.BlockSpec((GROUP_W, tm), lambda i: (0, i))],
        out_shape=[jax.ShapeDtypeStruct((t, w), dt) for w, dt in zip(widths, dtypes)]
        + [jax.ShapeDtypeStruct((GROUP_W, t), bf16)],
        scratch_shapes=[pltpu.VMEM((8, 256), f32)],
        compiler_params=_cparams(("arbitrary",)),
        name="inproj",
    )(x2d, mod, w_packed, wkt, cos_t, sin_t, gq, gkv, gbias, conv_w)


def _causal_softmax_heads(q_heads, k_at, v_at, qi, s_scr, m_scr, l_scr, acc_scr):
    n_heads = len(q_heads)
    tq = q_heads[0].shape[0]
    half = tq // 2

    def scores(g, j):
        return _dot_nt(q_heads[g], k_at(g, pl.multiple_of(j * tq, tq)))

    def fold_max(g, s):
        m_scr[g] = jnp.maximum(m_scr[g], jnp.maximum(s[:, :half], s[:, half:]))

    m_scr[...] = jnp.full(m_scr.shape, NEG, f32)
    l_scr[...] = jnp.zeros_like(l_scr)
    acc_scr[...] = jnp.zeros_like(acc_scr)

    @pl.loop(0, qi)
    def _(j):
        for g in range(n_heads):
            s = scores(g, j)
            s_scr[g, j] = s
            fold_max(g, s)

    rowi = lax.broadcasted_iota(jnp.int32, (tq, tq), 0)
    coli = lax.broadcasted_iota(jnp.int32, (tq, tq), 1)
    for g in range(n_heads):
        s = jnp.where(coli <= rowi, scores(g, qi), NEG)
        s_scr[g, qi] = s
        fold_max(g, s)
        m_scr[g] = jnp.broadcast_to(jnp.max(m_scr[g], axis=1, keepdims=True), (tq, half))

    @pl.loop(0, qi + 1)
    def _(j):
        for g in range(n_heads):
            sj = s_scr[g, j]
            mb = m_scr[g]
            p_lo = jnp.exp2(sj[:, :half] - mb)
            p_hi = jnp.exp2(sj[:, half:] - mb)
            p = jnp.concatenate([p_lo, p_hi], axis=1).astype(bf16)
            l_scr[g] += p_lo + p_hi
            acc_scr[g] += _dot(p, v_at(g, pl.multiple_of(j * tq, tq)))

    return [acc_scr[g] / jnp.sum(l_scr[g], axis=1, keepdims=True) for g in range(n_heads)]


def _softmax_scratch(n_heads, nq, tq, dv):
    return [pltpu.VMEM((n_heads, nq, tq, tq), f32), pltpu.VMEM((n_heads, tq, tq // 2), f32),
            pltpu.VMEM((n_heads, tq, tq // 2), f32), pltpu.VMEM((n_heads, tq, dv), f32)]


def _mla_body(qn_ref, kvn_ref, kpe_ref, wq_ref, wqr_ref, wk_ref, wv_ref, place_ref, cq_ref, sq_ref,
              o_ref, k_scr, v_scr, s_scr, m_scr, l_scr, acc_scr):
    qi = pl.program_id(1)
    tq = qn_ref.shape[0]

    @pl.when(qi == 0)
    def _():
        kvn = kvn_ref[...]
        kpe_placed = _dot(kpe_ref[...], place_ref[...])
        for h in range(MLA_HEADS):
            k_scr[h] = (_dot(kvn, wk_ref[h]) + kpe_placed).astype(bf16)
            v_scr[h] = _dot(kvn, wv_ref[h]).astype(bf16)

    qn = qn_ref[...]
    cq = cq_ref[...]
    sq = sq_ref[...]
    q_heads = [(_dot(qn, wq_ref[h]) * cq + _dot(qn, wqr_ref[h]) * sq).astype(bf16)
               for h in range(MLA_HEADS)]
    outs = _causal_softmax_heads(
        q_heads, lambda g, start: k_scr[g, pl.ds(start, tq), :],
        lambda g, start: v_scr[g, pl.ds(start, tq), :], qi, s_scr, m_scr, l_scr, acc_scr)
    o_ref[...] = jnp.concatenate(outs, axis=1).astype(bf16)


def _mla_attention(qn, kvn, kpe, wq, wqr, wk, wv, place, cq, sq, batch, seq):
    tq = 256
    nq = seq // tq
    const = lambda a: pl.BlockSpec(a.shape, lambda b, i: (0,) * a.ndim)
    return pl.pallas_call(
        _mla_body,
        grid=(batch, nq),
        in_specs=[
            pl.BlockSpec((tq, 256), lambda b, i: (b * nq + i, 0)),
            pl.BlockSpec((seq, 128), lambda b, i: (b, 0)),
            pl.BlockSpec((seq, 128), lambda b, i: (b, 0)),
            const(wq), const(wqr), const(wk), const(wv), const(place),
            pl.BlockSpec((tq, 128), lambda b, i: (i, 0)),
            pl.BlockSpec((tq, 128), lambda b, i: (i, 0)),
        ],
        out_specs=pl.BlockSpec((tq, GROUP_W), lambda b, i: (b * nq + i, 0)),
        out_shape=jax.ShapeDtypeStruct((batch * seq, GROUP_W), bf16),
        scratch_shapes=[pltpu.VMEM((MLA_HEADS, seq, 128), bf16),
                        pltpu.VMEM((MLA_HEADS, seq, MLA_V), bf16)]
        + _softmax_scratch(MLA_HEADS, nq, tq, MLA_V),
        compiler_params=_cparams(("arbitrary", "arbitrary")),
        name="mla_attn",
    )(qn, kvn, kpe, wq, wqr, wk, wv, place, cq, sq)


def _diff_body(q_ref, k_ref, v_ref, lq1_ref, lk1_ref, lq2_ref, lk2_ref, g_ref,
               o_ref, v_scr, s_scr, m_scr, l_scr, acc_scr, *, lambda_init):
    qi = pl.program_id(1)
    tq = q_ref.shape[0]

    @pl.when(qi == 0)
    def _():
        v = v_ref[...]
        for h in range(DF_HEADS):
            v_scr[h] = v[:, DF_V * h:DF_V * (h + 1)]

    q = q_ref[...]
    lam = (jnp.exp(jnp.sum(lq1_ref[...] * lk1_ref[...], axis=1, keepdims=True))
           - jnp.exp(jnp.sum(lq2_ref[...] * lk2_ref[...], axis=1, keepdims=True)) + lambda_init)
    q_heads = [jnp.where(_lane_group_mask(q.shape, DF_DK * g, DF_DK * (g + 1)), q, jnp.zeros_like(q))
               for g in range(2 * DF_HEADS)]
    maps = _causal_softmax_heads(
        q_heads, lambda g, start: k_ref[pl.ds(start, tq), :],
        lambda g, start: v_scr[g // 2, pl.ds(start, tq), :], qi, s_scr, m_scr, l_scr, acc_scr)
    outs = []
    for h in range(DF_HEADS):
        o = maps[2 * h] - lam * maps[2 * h + 1]
        ms = jnp.mean(o * o, axis=1, keepdims=True)
        outs.append(o * lax.rsqrt(ms + RMS_EPS) * g_ref[...] * (1.0 - lambda_init))
    o_ref[...] = jnp.concatenate(outs, axis=1).astype(bf16)


def _diff_attention(dq, dk, dv, lq1, lk1, lq2, lk2, g_sub, batch, seq, lambda_init):
    tq = 256
    nq = seq // tq
    const = lambda a: pl.BlockSpec(a.shape, lambda b, i: (0,) * a.ndim)
    return pl.pallas_call(
        functools.partial(_diff_body, lambda_init=lambda_init),
        grid=(batch, nq),
        in_specs=[
            pl.BlockSpec((tq, 256), lambda b, i: (b * nq + i, 0)),
            pl.BlockSpec((seq, 256), lambda b, i: (b, 0)),
            pl.BlockSpec((seq, 256), lambda b, i: (b, 0)),
            const(lq1), const(lk1), const(lq2), const(lk2), const(g_sub),
        ],
        out_specs=pl.BlockSpec((tq, GROUP_W), lambda b, i: (b * nq + i, 0)),
        out_shape=jax.ShapeDtypeStruct((batch * seq, GROUP_W), bf16),
        scratch_shapes=[pltpu.VMEM((DF_HEADS, seq, DF_V), bf16)]
        + _softmax_scratch(2 * DF_HEADS, nq, tq, DF_V),
        compiler_params=_cparams(("arbitrary", "arbitrary")),
        name="diff_attn",
    )(dq, dk, dv, lq1, lk1, lq2, lk2, g_sub)


def _split3(a):
    hi = a.astype(bf16)
    r1 = a - hi.astype(f32)
    mid = r1.astype(bf16)
    return hi, mid, (r1 - mid.astype(f32)).astype(bf16)


def _dot_exact_rhs01(a, b01):
    hi, mid, lo = _split3(a)
    return _dot(hi, b01) + _dot(mid, b01) + _dot(lo, b01)


def _dot_exact_lhs01(a01, b):
    hi, mid, lo = _split3(b)
    return _dot(a01, hi) + _dot(a01, mid) + _dot(a01, lo)


def _mlstm_chunk(q, k, kt_bd, v, o, g, eif, c_prev, n_prev, m_prev):
    L, W = q.shape
    row = lax.broadcasted_iota(jnp.int32, (L, W), 0)
    pos = lax.broadcasted_iota(jnp.int32, (L, W), 1) % L
    lf = jnp.minimum(g, 0.0) - jnp.log(1.0 + jnp.exp(-jnp.abs(g)))
    glane = lax.broadcasted_iota(jnp.int32, g.shape, 1)
    wide = _dot_exact_rhs01(jnp.where(glane < ML_HEADS, g, lf), eif)
    iw, lfw = wide[:, :W], wide[:, W:]
    tril = (lax.broadcasted_iota(jnp.int32, (L, L), 0) >= lax.broadcasted_iota(jnp.int32, (L, L), 1))
    bw = _dot_exact_lhs01(tril.astype(bf16), lfw)
    rw = iw - bw
    cm = rw
    for sh in (1, 2, 4, 8, 16, 32):
        cm = jnp.maximum(cm, jnp.where(row >= sh, pltpu.roll(cm, sh, axis=0), NEG))
    mw = jnp.maximum(m_prev, cm)
    a_inter = jnp.exp(m_prev - mw)
    r_row = jnp.sum(jnp.where(pos == row, rw, 0.0), axis=0, keepdims=True)
    ew = jnp.exp(jnp.where(pos <= row, r_row - mw, NEG))
    pw = _dot(q, kt_bd.astype(bf16)) * ew
    head_eq = (lax.broadcasted_iota(jnp.int32, (W, W), 0) // ML_DH
               == lax.broadcasted_iota(jnp.int32, (W, W), 1) // ML_DH)
    v4 = jnp.concatenate([v] * ML_HEADS, axis=0)
    vbd = jnp.where(head_eq, v4, jnp.zeros_like(v4))
    obd = head_eq.astype(bf16)
    den = a_inter * _dot_exact_rhs01(q.astype(f32) * n_prev, obd) + _dot_exact_rhs01(pw, obd)
    hden = jnp.maximum(jnp.abs(den), jnp.exp(-(bw + mw)))
    num = a_inter * _dot(q, c_prev.astype(bf16)) + _dot(pw.astype(bf16), vbd)
    y = num / hden * _sigmoid(o.astype(f32))

    rmax, m_last = cm[L - 1:L, :], mw[L - 1:L, :]
    a_w = jnp.exp(m_prev - m_last)
    b_w = jnp.exp(rmax - m_last)
    c_loc = _dot((kt_bd * jnp.exp(r_row - rmax)).astype(bf16), vbd)
    n_loc = jnp.sum(k.astype(f32) * jnp.exp(rw - rmax), axis=0, keepdims=True)
    return y, c_prev * a_w + c_loc * b_w, a_w * n_prev + b_w * n_loc, bw[L - 1:L, :] + m_last


def _mlstm_body(q_ref, k_ref, kt_ref, v_ref, o_ref, g_ref, eif_ref, sel_ref, y_ref, c_scr, nm_scr):
    @pl.when(pl.program_id(1) == 0)
    def _():
        c_scr[...] = jnp.zeros_like(c_scr)
        nm_scr[...] = jnp.zeros_like(nm_scr)

    c, n, m = c_scr[...], nm_scr[0:1, :], nm_scr[1:2, :]
    kt = kt_ref[...]
    eif = eif_ref[...]
    w = kt.shape[0]
    head_eq = (lax.broadcasted_iota(jnp.int32, (w, w), 0) // ML_DH
               == lax.broadcasted_iota(jnp.int32, (w, w), 1) // ML_DH)
    for i in range(q_ref.shape[0] // ML_CHUNK):
        rows = pl.ds(i * ML_CHUNK, ML_CHUNK)
        kt_bd = jnp.where(head_eq, _dot(kt, sel_ref[i]), 0.0)
        y, c, n, m = _mlstm_chunk(q_ref[rows, :], k_ref[rows, :], kt_bd, v_ref[rows, :], o_ref[rows, :],
                                  g_ref[rows, :], eif, c, n, m)
        y_ref[rows, :] = y.astype(y_ref.dtype)
    c_scr[...] = c
    nm_scr[0:1, :] = n
    nm_scr[1:2, :] = m


MLSTM_CHUNKS_PER_STEP = 4


def _mlstm(mq, mk, mkt, mv, mo, gcol, batch, seq):
    cps = MLSTM_CHUNKS_PER_STEP
    L = ML_CHUNK * cps
    nc = seq // L
    blk = lambda w: pl.BlockSpec((L, w), lambda b, c: (b * nc + c, 0))
    const = lambda a: pl.BlockSpec(a.shape, lambda b, c: (0,) * a.ndim)
    gl = jnp.arange(LANE)[:, None]
    col = jnp.arange(2 * GROUP_W)[None, :]
    eif = (gl == (col // GROUP_W) * ML_HEADS + (col % GROUP_W) // ML_DH).astype(bf16)
    p = jnp.arange(L)[None, :, None]
    cc = jnp.arange(GROUP_W)[None, None, :]
    sel = (p == jnp.arange(cps)[:, None, None] * ML_CHUNK + cc % ML_CHUNK).astype(bf16)
    return pl.pallas_call(
        _mlstm_body,
        grid=(batch, nc),
        in_specs=[blk(256), blk(256), pl.BlockSpec((GROUP_W, L), lambda b, c: (0, b * nc + c)),
                  blk(256), blk(256), blk(128), const(eif), const(sel)],
        out_specs=blk(GROUP_W),
        out_shape=jax.ShapeDtypeStruct((batch * seq, GROUP_W), bf16),
        scratch_shapes=[pltpu.VMEM((GROUP_W, GROUP_W), f32), pltpu.VMEM((8, GROUP_W), f32)],
        compiler_params=_cparams(("arbitrary", "arbitrary")),
        name="mlstm",
    )(mq, mk, mkt, mv, mo, gcol, eif, sel)


def _outproj_body(ya_ref, yb_ref, yc_ref, yd_ref, x_ref, mod_ref, w_ref, g_ref, b_ref, o_ref):
    mix = None
    for j, y_ref in enumerate((ya_ref, yb_ref, yc_ref, yd_ref)):
        part = _dot(y_ref[...], w_ref[GROUP_W * j:GROUP_W * (j + 1), :])
        mix = part if mix is None else mix + part
    r = DN_ALPHA * x_ref[...] + mod_ref[0, 2:3, :] * mix
    o_ref[...] = _layer_norm(r) * g_ref[...] + b_ref[...]


def _outproj(ya, yb, yc, yd, x2d, mod, w_out, ln_g, ln_b, seq):
    t, d = x2d.shape
    tm = 512
    tps = seq // tm
    row = lambda w: pl.BlockSpec((tm, w), lambda i: (i, 0))
    const = lambda a: pl.BlockSpec(a.shape, lambda i: (0,) * a.ndim)
    return pl.pallas_call(
        _outproj_body,
        grid=(t // tm,),
        in_specs=[row(256), row(256), row(256), row(256), row(d),
                  pl.BlockSpec((1, 6, d), lambda i: (i // tps, 0, 0)),
                  const(w_out), const(ln_g), const(ln_b)],
        out_specs=row(d),
        out_shape=jax.ShapeDtypeStruct((t, d), f32),
        compiler_params=_cparams(("arbitrary",)),
        name="outproj",
    )(ya, yb, yc, yd, x2d, mod, w_out, ln_g, ln_b)


def _route(scores, sel):
    ne, tm = sel.shape
    eid = lax.broadcasted_iota(jnp.int32, (ne, tm), 0)
    grp = eid // E_PER_GROUP
    big = ne + 1

    def top2(vals):
        m1 = jnp.max(vals, axis=0, keepdims=True)
        i1 = jnp.min(jnp.where(vals == m1, eid, big), axis=0, keepdims=True)
        rest = jnp.where(eid == i1, NEG, vals)
        m2 = jnp.max(rest, axis=0, keepdims=True)
        i2 = jnp.min(jnp.where(rest == m2, eid, big), axis=0, keepdims=True)
        return m1, i1, m2, i2

    best_score, best_grp = None, None
    for gi in range(N_GROUPS):
        m1, _, m2, _ = top2(jnp.where(grp == gi, sel, NEG))
        sc = m1 + m2
        if best_score is None:
            best_score, best_grp = sc, jnp.zeros_like(sc, dtype=jnp.int32)
        else:
            better = sc > best_score
            best_grp = jnp.where(better, gi, best_grp)
            best_score = jnp.where(better, sc, best_score)
    _, i1, _, i2 = top2(jnp.where(grp == best_grp, sel, NEG))
    picked = jnp.where((eid == i1) | (eid == i2), scores, 0.0)
    return picked / jnp.sum(picked, axis=0, keepdims=True)


def _moe_body(x_ref, mod_ref, wr_ref, br_ref, w1_ref, w3_ref, w2_ref, g_ref, b_ref, o_ref,
              u_scr, gate_scr, acc_scr):
    e = pl.program_id(1)

    @pl.when(e == 0)
    def _():
        u = _layer_norm(x_ref[...]) * (1.0 + mod_ref[0, 4:5, :]) + mod_ref[0, 3:4, :]
        u_hi = u.astype(bf16)
        u_lo = (u - u_hi.astype(f32)).astype(bf16)
        u_scr[...] = u_hi
        ne = br_ref.shape[0]
        part = _dot_nt(wr_ref[...], u_hi)
        logits = part[:ne] + part[ne:] + _dot_nt(wr_ref[:ne, :], u_lo)
        scores = _sigmoid(logits)
        gate_scr[...] = _route(scores, scores + br_ref[...]).T
        acc_scr[...] = jnp.zeros_like(acc_scr)

    u = u_scr[...]
    gates = gate_scr[...]
    lane = lax.broadcasted_iota(jnp.int32, gates.shape, 1)
    ge = jnp.sum(jnp.where(lane == e, gates, 0.0), axis=1, keepdims=True)
    h1 = _dot(u, w1_ref[0])
    h = h1 * _sigmoid(h1) * _dot(u, w3_ref[0]) * ge
    acc_scr[...] += _dot(h.astype(bf16), w2_ref[0])

    @pl.when(e == pl.num_programs(1) - 1)
    def _():
        r = DN_ALPHA * x_ref[...] + mod_ref[0, 5:6, :] * acc_scr[...]
        o_ref[...] = _layer_norm(r) * g_ref[...] + b_ref[...]


def _moe(x2d, mod, w_router, b_router, w1, w3, w2, ln_g, ln_b, seq):
    t, d = x2d.shape
    ne, _, fe = w1.shape
    tm = 1024
    tps = seq // tm
    const = lambda a: pl.BlockSpec(a.shape, lambda i, e: (0,) * a.ndim)
    return pl.pallas_call(
        _moe_body,
        grid=(t // tm, ne),
        in_specs=[
            pl.BlockSpec((tm, d), lambda i, e: (i, 0)),
            pl.BlockSpec((1, 6, d), lambda i, e: (i // tps, 0, 0)),
            const(w_router), const(b_router),
            pl.BlockSpec((1, d, fe), lambda i, e: (e, 0, 0)),
            pl.BlockSpec((1, d, fe), lambda i, e: (e, 0, 0)),
            pl.BlockSpec((1, fe, d), lambda i, e: (e, 0, 0)),
            const(ln_g), const(ln_b),
        ],
        out_specs=pl.BlockSpec((tm, d), lambda i, e: (i, 0)),
        out_shape=jax.ShapeDtypeStruct((t, d), f32),
        scratch_shapes=[pltpu.VMEM((tm, d), bf16), pltpu.VMEM((tm, ne), f32), pltpu.VMEM((tm, d), f32)],
        compiler_params=_cparams(("arbitrary", "arbitrary")),
        name="moe",
    )(x2d, mod, w_router, b_router, w1, w3, w2, ln_g, ln_b)


def _pad_cols(a, width):
    return jnp.pad(a, ((0, 0), (0, width - a.shape[1])))


def _pack_w_in(w):
    parts, acc = [], 0
    for sz in IN_SIZES:
        parts.append(w[:, acc:acc + sz])
        acc += sz
    (a_q, a_kv, a_kr, m_q, m_k, m_v, m_i, m_f, m_o, c_b, c_c, c_h, d_q, d_k, d_v) = parts
    gates = _pad_cols(jnp.concatenate([m_i, m_f], axis=1), LANE)
    packed = jnp.concatenate([
        _pad_cols(a_q, 256), a_kv, _pad_cols(a_kr, LANE),
        m_q, m_k, m_v, gates, m_o, c_b, c_c, c_h, d_q, d_k, d_v], axis=1)
    return packed.astype(bf16)


def _pack_mla(w_uq, w_ukv):
    half = MLA_ROPE // 2
    wq3 = w_uq.reshape(MLA_Q_LORA, MLA_HEADS, MLA_NOPE + MLA_ROPE)
    wkv3 = w_ukv.reshape(MLA_KV_LORA, MLA_HEADS, MLA_NOPE + MLA_V)
    wq, wqr, wk, wv = [], [], [], []
    for h in range(MLA_HEADS):
        nope, pe = wq3[:, h, :MLA_NOPE], wq3[:, h, MLA_NOPE:]
        rot = jnp.concatenate([-pe[:, half:], pe[:, :half]], axis=1)
        zq = jnp.zeros((MLA_Q_LORA, MLA_NOPE), f32)
        pad_rows = lambda a: jnp.pad(a, ((0, 256 - MLA_Q_LORA), (0, 0)))
        wq.append(pad_rows(_pad_cols(jnp.concatenate([nope, pe], axis=1), LANE)))
        wqr.append(pad_rows(_pad_cols(jnp.concatenate([zq, rot], axis=1), LANE)))
        wk.append(_pad_cols(wkv3[:, h, :MLA_NOPE], LANE))
        wv.append(wkv3[:, h, MLA_NOPE:])
    st = lambda xs: jnp.stack(xs).astype(bf16)
    return st(wq), st(wqr), st(wk), st(wv)


def _rope_tables(seq):
    half = DF_DK // 2
    inv = 1.0 / (ROPE_THETA ** (jnp.arange(0, DF_DK, 2, dtype=f32) / DF_DK))
    ang = jnp.arange(seq, dtype=f32)[:, None] * inv[None, :]
    cos, sin = jnp.cos(ang), jnp.sin(ang)
    cos_t = jnp.tile(jnp.concatenate([cos, cos], axis=1), (1, 256 // DF_DK))
    sin_t = jnp.tile(jnp.concatenate([-sin, sin], axis=1), (1, 256 // DF_DK))
    scale = (MLA_NOPE + MLA_ROPE) ** -0.5 * LOG2E
    ones = jnp.ones((seq, MLA_NOPE), f32)
    zeros = jnp.zeros((seq, MLA_NOPE), f32)
    cq = _pad_cols(jnp.concatenate([ones, cos, cos], axis=1) * scale, LANE)
    sq = _pad_cols(jnp.concatenate([zeros, sin, sin], axis=1) * scale, LANE)
    return cos_t, sin_t, cq, sq


def kernel(x, c, w_ada, b_ada, w_in, mla_g_q, mla_g_kv, mla_w_uq, mla_w_ukv, ml_b_i, ml_b_f, sc_w,
           df_lq1, df_lk1, df_lq2, df_lk2, df_g, w_out, ln1_g, ln1_b, w_router, b_router, w1, w3, w2,
           ln2_g, ln2_b):
    batch, seq, d = x.shape
    depth = w_in.shape[0]
    assert MLA_ROPE == DF_DK, "both rotary blocks share one table"
    cos_t, sin_t, cq, sq = _rope_tables(seq)
    place = jnp.zeros((LANE, LANE), f32).at[jnp.arange(MLA_ROPE), MLA_NOPE + jnp.arange(MLA_ROPE)].set(1.0).astype(bf16)
    wr_hi = w_router.astype(bf16)
    wr_split = jnp.concatenate([wr_hi, (w_router - wr_hi.astype(f32)).astype(bf16)], axis=1).T
    mod_all = _ada_mod(c, w_ada, b_ada).reshape(depth, batch, 6, d)
    xf = x.reshape(batch * seq, d)
    for l in range(depth):
        mod = mod_all[l]
        w_packed = _pack_w_in(w_in[l])
        gq = _pad_cols(mla_g_q[l][None, :], 256)
        gkv = mla_g_kv[l][None, :]
        gbias = _pad_cols(jnp.concatenate([ml_b_i[l], ml_b_f[l]])[None, :], LANE)
        wkt = w_packed[:, OFF_MK:OFF_MK + GROUP_W].T
        (qn, kvn, kpe, mq, mk, mv, mo, gcol, yc, dq, dk, dv, mkt) = _inproj(
            xf, mod, w_packed, wkt, cos_t, sin_t, gq, gkv, gbias, sc_w[l], seq)
        wq, wqr, wk, wv = _pack_mla(mla_w_uq[l], mla_w_ukv[l])
        ya = _mla_attention(qn, kvn, kpe, wq, wqr, wk, wv, place, cq, sq, batch, seq)
        yb = _mlstm(mq, mk, mkt, mv, mo, gcol, batch, seq)
        lambda_init = 0.8 - 0.6 * math.exp(-0.3 * l)
        yd = _diff_attention(dq, dk, dv, df_lq1[l][None, :], df_lk1[l][None, :], df_lq2[l][None, :],
                             df_lk2[l][None, :], df_g[l][None, :],
                             batch, seq, lambda_init)
        x1 = _outproj(ya, yb, yc, yd, xf, mod, w_out[l].astype(bf16), ln1_g[l][None, :], ln1_b[l][None, :], seq)
        xf = _moe(x1, mod, wr_split, b_router[:, None], w1[l].astype(bf16), w3[l].astype(bf16),
                  w2[l].astype(bf16), ln2_g[l][None, :], ln2_b[l][None, :], seq)
    return xf.reshape(batch, seq, d)
```

```python
import functools
import math

import jax
import jax.numpy as jnp
from jax import lax
from jax.experimental import pallas as pl
from jax.experimental.pallas import tpu as pltpu

f32 = jnp.float32
bf16 = jnp.bfloat16
HIGHEST = lax.Precision.HIGHEST

GROUP_W = 256
MLA_HEADS, MLA_NOPE, MLA_ROPE, MLA_V = 4, 64, 32, 64
MLA_Q_LORA, MLA_KV_LORA = 192, 128
ML_HEADS, ML_DH, ML_CHUNK = 4, 64, 64
DF_HEADS, DF_DK, DF_V = 4, 32, 64
ROPE_THETA = 10000.0
N_EXPERTS, N_GROUPS, E_PER_GROUP, D_EXPERT = 16, 4, 4, 256
DEPTH = 2
DN_ALPHA = (2 * DEPTH) ** 0.25
LN_EPS = 1e-5
RMS_EPS = 1e-6
IN_SIZES = (192, 128, 32, 256, 256, 256, 4, 4, 256, 256, 256, 256, 256, 256, 256)

OFF_PQ, OFF_PKV, OFF_KR = 0, 256, 384
OFF_MQ, OFF_MK, OFF_MV, OFF_MG, OFF_MO = 512, 768, 1024, 1280, 1408
OFF_CB, OFF_CC, OFF_CH = 1664, 1920, 2176
OFF_DQ, OFF_DK, OFF_DV = 2432, 2688, 2944
IN_PACKED = 3200

LANE = 128
VMEM_LIMIT = 48 * 1024 * 1024
NEG = -1e30
LOG2E = math.log2(math.e)


def _cparams(sem):
    return pltpu.CompilerParams(dimension_semantics=sem, vmem_limit_bytes=VMEM_LIMIT)


def _sigmoid(x):
    return 1.0 / (1.0 + jnp.exp(-x))


def _layer_norm(x):
    mu = jnp.mean(x, axis=-1, keepdims=True)
    xc = x - mu
    var = jnp.mean(xc * xc, axis=-1, keepdims=True)
    return xc * lax.rsqrt(var + LN_EPS)


def _dot(a, b, **kw):
    return jnp.dot(a, b, preferred_element_type=f32, **kw)


def _dot_nt(a, b, **kw):
    return lax.dot_general(a, b, (((1,), (1,)), ((), ())), preferred_element_type=f32, **kw)


def _dot_tn(a, b, **kw):
    return lax.dot_general(a, b, (((0,), (0,)), ((), ())), preferred_element_type=f32, **kw)


def _expand_groups(cols, rows, width, group):
    lane = lax.broadcasted_iota(jnp.int32, (rows, width), 1)
    out = jnp.broadcast_to(cols[-1], (rows, width))
    for h in range(len(cols) - 2, -1, -1):
        out = jnp.where(lane < group * (h + 1), cols[h], out)
    return out


def _lane_group_mask(shape, lo, hi):
    lane = lax.broadcasted_iota(jnp.int32, shape, 1)
    return (lane >= lo) & (lane < hi)


def _ada_body(c_ref, w_ref, b_ref, o_ref):
    c = c_ref[...]
    ca = (c * _sigmoid(c)).astype(bf16)
    o_ref[0] = _dot(ca, w_ref[0].astype(bf16)) + b_ref[0]


def _ada_mod(c, w_ada, b_ada):
    depth, d, n = w_ada.shape
    b = c.shape[0]
    tn = 1536
    return pl.pallas_call(
        _ada_body,
        grid=(depth, n // tn),
        in_specs=[
            pl.BlockSpec((b, d), lambda l, j: (0, 0)),
            pl.BlockSpec((1, d, tn), lambda l, j: (l, 0, j)),
            pl.BlockSpec((1, 1, tn), lambda l, j: (l, 0, j)),
        ],
        out_specs=pl.BlockSpec((1, b, tn), lambda l, j: (l, 0, j)),
        out_shape=jax.ShapeDtypeStruct((depth, b, n), f32),
        compiler_params=_cparams(("arbitrary", "arbitrary")),
        name="ada_mod",
    )(c, w_ada, b_ada.reshape(depth, 1, n))


def _rope_lanes(x, cos, sin_signed):
    w = x.shape[1]
    lane = lax.broadcasted_iota(jnp.int32, x.shape, 1)
    rot = jnp.where(lane % 32 < 16, pltpu.roll(x, w - 16, axis=1), pltpu.roll(x, 16, axis=1))
    return x * cos + rot * sin_signed


def _inproj_body(x_ref, mod_ref, w_ref, wkt_ref, cos_ref, sin_ref, gq_ref, gkv_ref, gb_ref, cw_ref,
                 qn_ref, kvn_ref, kpe_ref, mq_ref, mk_ref, mv_ref, mo_ref, gcol_ref, yc_ref,
                 dq_ref, dk_ref, dv_ref, mkt_ref, carry_ref, *, tiles_per_seq):
    i = pl.program_id(0)
    tm = x_ref.shape[0]
    u = _layer_norm(x_ref[...]) * (1.0 + mod_ref[0, 1:2, :]) + mod_ref[0, 0:1, :]
    ub = u.astype(bf16)

    def seg(off, n):
        return _dot(ub, w_ref[:, off:off + n])

    cos = cos_ref[...]
    sin = sin_ref[...]

    pq = seg(OFF_PQ, 256)
    ms = jnp.sum(pq * pq, axis=-1, keepdims=True) * (1.0 / MLA_Q_LORA)
    qn_ref[...] = (pq * lax.rsqrt(ms + RMS_EPS) * gq_ref[...]).astype(bf16)
    pkv = seg(OFF_PKV, 128)
    ms = jnp.sum(pkv * pkv, axis=-1, keepdims=True) * (1.0 / MLA_KV_LORA)
    kvn_ref[...] = (pkv * lax.rsqrt(ms + RMS_EPS) * gkv_ref[...]).astype(bf16)
    kpe_ref[...] = _rope_lanes(seg(OFF_KR, 128), cos[:, :LANE], sin[:, :LANE]).astype(bf16)

    mq_ref[...] = (seg(OFF_MQ, 256) * (ML_DH ** -0.5)).astype(bf16)
    mk_ref[...] = seg(OFF_MK, 256).astype(bf16)
    mkt_ref[...] = _dot_nt(wkt_ref[...], ub).astype(bf16)
    mv_ref[...] = seg(OFF_MV, 256).astype(bf16)
    gcol_ref[...] = seg(OFF_MG, 128) + gb_ref[...]
    mo_ref[...] = seg(OFF_MO, 256).astype(bf16)

    uc = seg(OFF_CC, 256) * seg(OFF_CH, 256)

    @pl.when(i % tiles_per_seq == 0)
    def _():
        carry_ref[...] = jnp.zeros_like(carry_ref)

    prev = carry_ref[...]
    row =lax.broadcasted_iota(jnp.int32, uc.shape, 0)
    u1 = jnp.where(row == 0, prev[7:8, :], pltpu.roll(uc, 1, axis=0))
    u2 = jnp.where(row == 0, prev[6:7, :], jnp.where(row == 1, prev[7:8, :], pltpu.roll(uc, 2, axis=0)))
    carry_ref[...] = uc[tm - 8:, :]
    conv = cw_ref[0:1, :] * u2 + cw_ref[1:2, :] * u1 + cw_ref[2:3, :] * uc
    yc_ref[...] = (seg(OFF_CB, 256) * conv).astype(bf16)

    dq_ref[...] = (_rope_lanes(seg(OFF_DQ, 256), cos, sin) * (DF_DK ** -0.5 * LOG2E)).astype(bf16)
    dk_ref[...] = _rope_lanes(seg(OFF_DK, 256), cos, sin).astype(bf16)
    dv_ref[...] = seg(OFF_DV, 256).astype(bf16)


def _inproj(x2d, mod, w_packed, wkt, cos_t, sin_t, gq, gkv, gbias, conv_w, seq):
    t, d = x2d.shape
    tm = 512
    tps = seq // tm
    row = lambda w: pl.BlockSpec((tm, w), lambda i: (i, 0))
    const = lambda a: pl.BlockSpec(a.shape, lambda i: (0,) * a.ndim)
    widths = (256, 128, 128, 256, 256, 256, 256, 128, 256, 256, 256, 256)
    dtypes = (bf16, bf16, bf16, bf16, bf16, bf16, bf16, f32, bf16, bf16, bf16, bf16)
    return pl.pallas_call(
        functools.partial(_inproj_body, tiles_per_seq=tps),
        grid=(t // tm,),
        in_specs=[
            row(d),
            pl.BlockSpec((1, 6, d), lambda i: (i // tps, 0, 0)),
            const(w_packed), const(wkt),
            pl.BlockSpec((tm, 256), lambda i: (i % tps, 0)),
            pl.BlockSpec((tm, 256), lambda i: (i % tps, 0)),
            const(gq), const(gkv), const(gbias), const(conv_w),
        ],
        out_specs=[row(w) for w in widths] + [pl.BlockSpec((GROUP_W, tm), lambda i: (0, i))],
        out_shape=[jax.ShapeDtypeStruct((t, w), dt) for w, dt in zip(widths, dtypes)]
        + [jax.ShapeDtypeStruct((GROUP_W, t), bf16)],
        scratch_shapes=[pltpu.VMEM((8, 256), f32)],
        compiler_params=_cparams(("arbitrary",)),
        name="inproj",
    )(x2d, mod, w_packed, wkt, cos_t, sin_t, gq, gkv, gbias, conv_w)


def _causal_softmax_heads(q_heads, k_at, v_at, dv, qi, s_scr, m_scr, acc_scr):
    n_heads = len(q_heads)
    tq = q_heads[0].shape[0]
    half = tq // 2

    def scores(g, j):
        return _dot_nt(q_heads[g], k_at(g, pl.multiple_of(j * tq, tq)))

    def fold_max(g, s):
        m_scr[g] = jnp.maximum(m_scr[g], jnp.maximum(s[:, :half], s[:, half:]))

    m_scr[...] = jnp.full(m_scr.shape, NEG, f32)
    acc_scr[...] = jnp.zeros_like(acc_scr)

    def score_blocks(js):
        for g in range(n_heads):
            mx = None
            for j in js:
                s = scores(g, j)
                s_scr[g, j] = s
                s = jnp.maximum(s[:, :half], s[:, half:])
                mx = s if mx is None else jnp.maximum(mx, s)
            m_scr[g] = jnp.maximum(m_scr[g], mx)

    def prob_blocks(js):
        for g in range(n_heads):
            mb = m_scr[g]
            pv = None
            for j in js:
                sj = s_scr[g, j]
                p = jnp.concatenate([jnp.exp2(sj[:, :half] - mb), jnp.exp2(sj[:, half:] - mb)],
                                    axis=1).astype(bf16)
                d = _dot(p, v_at(g, pl.multiple_of(j * tq, tq)))
                pv = d if pv is None else pv + d
            acc_scr[g] += pv

    def over_full_blocks(blocks_fn):
        @pl.loop(0, qi // 2)
        def _(jj):
            blocks_fn((2 * jj, 2 * jj + 1))

        @pl.when(qi % 2 == 1)
        def _():
            blocks_fn((qi - 1,))

    over_full_blocks(score_blocks)
    rowi = lax.broadcasted_iota(jnp.int32, (tq, tq), 0)
    coli = lax.broadcasted_iota(jnp.int32, (tq, tq), 1)
    for g in range(n_heads):
        s = jnp.where(coli <= rowi, scores(g, qi), NEG)
        s_scr[g, qi] = s
        fold_max(g, s)
        m_scr[g] = jnp.broadcast_to(jnp.max(m_scr[g], axis=1, keepdims=True), (tq, half))

    over_full_blocks(prob_blocks)
    prob_blocks((qi,))
    outs = []
    for g in range(n_heads):
        acc = acc_scr[g]
        outs.append(acc[:, :dv] / acc[:, dv:dv + 1])
    return outs


def _softmax_scratch(n_heads, nq, tq):
    return [pltpu.VMEM((n_heads, nq, tq, tq), f32), pltpu.VMEM((n_heads, tq, tq // 2), f32),
            pltpu.VMEM((n_heads, tq, LANE), f32)]


def _with_ones_lane(v, dv):
    rows = v.shape[0]
    lane = lax.broadcasted_iota(jnp.int32, (rows, LANE - dv), 1)
    return jnp.concatenate([v.astype(bf16), (lane == 0).astype(bf16)], axis=1)


def _mla_body(qn_ref, kvn_ref, kpe_ref, wq_ref, wqr_ref, wk_ref, wv_ref, place_ref, cq_ref, sq_ref,
              o_ref, k_scr, v_scr, s_scr, m_scr, acc_scr):
    qi = pl.program_id(1)
    tq = qn_ref.shape[0]

    @pl.when(qi == 0)
    def _():
        kvn = kvn_ref[...]
        kpe_placed = _dot(kpe_ref[...], place_ref[...])
        for h in range(MLA_HEADS):
            k_scr[h] = (_dot(kvn, wk_ref[h]) + kpe_placed).astype(bf16)
            v_scr[h] = _with_ones_lane(_dot(kvn, wv_ref[h]), MLA_V)

    qn = qn_ref[...]
    cq = cq_ref[...]
    sq = sq_ref[...]
    q_heads = [(_dot(qn, wq_ref[h]) * cq + _dot(qn, wqr_ref[h]) * sq).astype(bf16)
               for h in range(MLA_HEADS)]
    outs = _causal_softmax_heads(
        q_heads, lambda g, start: k_scr[g, pl.ds(start, tq), :],
        lambda g, start: v_scr[g, pl.ds(start, tq), :], MLA_V, qi, s_scr, m_scr, acc_scr)
    o_ref[...] = jnp.concatenate(outs, axis=1).astype(bf16)


def _mla_attention(qn, kvn, kpe, wq, wqr, wk, wv, place, cq, sq, batch, seq):
    tq = 256
    nq = seq // tq
    const = lambda a: pl.BlockSpec(a.shape, lambda b, i: (0,) * a.ndim)
    return pl.pallas_call(
        _mla_body,
        grid=(batch, nq),
        in_specs=[
            pl.BlockSpec((tq, 256), lambda b, i: (b * nq + i, 0)),
            pl.BlockSpec((seq, 128), lambda b, i: (b, 0)),
            pl.BlockSpec((seq, 128), lambda b, i: (b, 0)),
            const(wq), const(wqr), const(wk), const(wv), const(place),
            pl.BlockSpec((tq, 128), lambda b, i: (i, 0)),
            pl.BlockSpec((tq, 128), lambda b, i: (i, 0)),
        ],
        out_specs=pl.BlockSpec((tq, GROUP_W), lambda b, i: (b * nq + i, 0)),
        out_shape=jax.ShapeDtypeStruct((batch * seq, GROUP_W), bf16),
        scratch_shapes=[pltpu.VMEM((MLA_HEADS, seq, 128), bf16),
                        pltpu.VMEM((MLA_HEADS, seq, LANE), bf16)]
        + _softmax_scratch(MLA_HEADS, nq, tq),
        compiler_params=_cparams(("arbitrary", "arbitrary")),
        name="mla_attn",
    )(qn, kvn, kpe, wq, wqr, wk, wv, place, cq, sq)


def _diff_body(q_ref, k_ref, v_ref, lq1_ref, lk1_ref, lq2_ref, lk2_ref, g_ref,
               o_ref, v_scr, s_scr, m_scr, acc_scr, *, lambda_init):
    qi = pl.program_id(1)
    tq = q_ref.shape[0]

    @pl.when(qi == 0)
    def _():
        v = v_ref[...]
        for h in range(DF_HEADS):
            v_scr[h] = _with_ones_lane(v[:, DF_V * h:DF_V * (h + 1)], DF_V)

    q = q_ref[...]
    lam = (jnp.exp(jnp.sum(lq1_ref[...] * lk1_ref[...], axis=1, keepdims=True))
           - jnp.exp(jnp.sum(lq2_ref[...] * lk2_ref[...], axis=1, keepdims=True)) + lambda_init)
    q_heads = [jnp.where(_lane_group_mask(q.shape, DF_DK * g, DF_DK * (g + 1)), q, jnp.zeros_like(q))
               for g in range(2 * DF_HEADS)]
    maps = _causal_softmax_heads(
        q_heads, lambda g, start: k_ref[pl.ds(start, tq), :],
        lambda g, start: v_scr[g // 2, pl.ds(start, tq), :], DF_V, qi, s_scr, m_scr, acc_scr)
    outs = []
    for h in range(DF_HEADS):
        o = maps[2 * h] - lam * maps[2 * h + 1]
        ms = jnp.mean(o * o, axis=1, keepdims=True)
        outs.append(o * lax.rsqrt(ms + RMS_EPS) * g_ref[...] * (1.0 - lambda_init))
    o_ref[...] = jnp.concatenate(outs, axis=1).astype(bf16)


def _diff_attention(dq, dk, dv, lq1, lk1, lq2, lk2, g_sub, batch, seq, lambda_init):
    tq = 256
    nq = seq // tq
    const = lambda a: pl.BlockSpec(a.shape, lambda b, i: (0,) * a.ndim)
    return pl.pallas_call(
        functools.partial(_diff_body, lambda_init=lambda_init),
        grid=(batch, nq),
        in_specs=[
            pl.BlockSpec((tq, 256), lambda b, i: (b * nq + i, 0)),
            pl.BlockSpec((seq, 256), lambda b, i: (b, 0)),
            pl.BlockSpec((seq, 256), lambda b, i: (b, 0)),
            const(lq1), const(lk1), const(lq2), const(lk2), const(g_sub),
        ],
        out_specs=pl.BlockSpec((tq, GROUP_W), lambda b, i: (b * nq + i, 0)),
        out_shape=jax.ShapeDtypeStruct((batch * seq, GROUP_W), bf16),
        scratch_shapes=[pltpu.VMEM((DF_HEADS, seq, LANE), bf16)]
        + _softmax_scratch(2 * DF_HEADS, nq, tq),
        compiler_params=_cparams(("arbitrary", "arbitrary")),
        name="diff_attn",
    )(dq, dk, dv, lq1, lk1, lq2, lk2, g_sub)


def _split3(a):
    hi = a.astype(bf16)
    r1 = a - hi.astype(f32)
    mid = r1.astype(bf16)
    return hi, mid, (r1 - mid.astype(f32)).astype(bf16)


def _dot_exact_rhs01(a, b01):
    hi, mid, lo = _split3(a)
    return _dot(hi, b01) + _dot(mid, b01) + _dot(lo, b01)


def _dot_exact_lhs01(a01, b):
    hi, mid, lo = _split3(b)
    return _dot(a01, hi) + _dot(a01, mid) + _dot(a01, lo)


def _mlstm_chunk(q, k, kt_bd, v, o, g, eif, c_prev, n_prev, m_prev):
    L, W = q.shape
    row = lax.broadcasted_iota(jnp.int32, (L, W), 0)
    pos = lax.broadcasted_iota(jnp.int32, (L, W), 1) % L
    lf = jnp.minimum(g, 0.0) - jnp.log(1.0 + jnp.exp(-jnp.abs(g)))
    glane = lax.broadcasted_iota(jnp.int32, g.shape, 1)
    wide = _dot_exact_rhs01(jnp.where(glane < ML_HEADS, g, lf), eif)
    iw, lfw = wide[:, :W], wide[:, W:]
    tril = (lax.broadcasted_iota(jnp.int32, (L, L), 0) >= lax.broadcasted_iota(jnp.int32, (L, L), 1))
    bw = _dot_exact_lhs01(tril.astype(bf16), lfw)
    rw = iw - bw
    cm = rw
    for sh in (1, 2, 4, 8, 16, 32):
        cm = jnp.maximum(cm, jnp.where(row >= sh, pltpu.roll(cm, sh, axis=0), NEG))
    mw = jnp.maximum(m_prev, cm)
    a_inter = jnp.exp(m_prev - mw)
    r_row = jnp.sum(jnp.where(pos == row, rw, 0.0), axis=0, keepdims=True)
    ew = jnp.exp(jnp.where(pos <= row, r_row - mw, NEG))
    pw = _dot(q, kt_bd.astype(bf16)) * ew
    head_eq = (lax.broadcasted_iota(jnp.int32, (W, W), 0) // ML_DH
               == lax.broadcasted_iota(jnp.int32, (W, W), 1) // ML_DH)
    v4 = jnp.concatenate([v] * ML_HEADS, axis=0)
    vbd = jnp.where(head_eq, v4, jnp.zeros_like(v4))
    obd = head_eq.astype(bf16)
    den = a_inter * _dot_exact_rhs01(q.astype(f32) * n_prev, obd) + _dot_exact_rhs01(pw, obd)
    hden = jnp.maximum(jnp.abs(den), jnp.exp(-(bw + mw)))
    num = a_inter * _dot(q, c_prev.astype(bf16)) + _dot(pw.astype(bf16), vbd)
    y = num / hden * _sigmoid(o.astype(f32))

    rmax, m_last = cm[L - 1:L, :], mw[L - 1:L, :]
    a_w = jnp.exp(m_prev - m_last)
    b_w = jnp.exp(rmax - m_last)
    c_loc = _dot((kt_bd * jnp.exp(r_row - rmax)).astype(bf16), vbd)
    n_loc = jnp.sum(k.astype(f32) * jnp.exp(rw - rmax), axis=0, keepdims=True)
    return y, c_prev * a_w + c_loc * b_w, a_w * n_prev + b_w * n_loc, bw[L - 1:L, :] + m_last


def _mlstm_body(q_ref, k_ref, kt_ref, v_ref, o_ref, g_ref, eif_ref, sel_ref, y_ref, c_scr, nm_scr):
    @pl.when(pl.program_id(1) == 0)
    def _():
        c_scr[...] = jnp.zeros_like(c_scr)
        nm_scr[...] = jnp.zeros_like(nm_scr)

    c, n, m = c_scr[...], nm_scr[0:1, :], nm_scr[1:2, :]
    kt = kt_ref[...]
    eif = eif_ref[...]
    w = kt.shape[0]
    head_eq = (lax.broadcasted_iota(jnp.int32, (w, w), 0) // ML_DH
               == lax.broadcasted_iota(jnp.int32, (w, w), 1) // ML_DH)
    for i in range(q_ref.shape[0] // ML_CHUNK):
        rows = pl.ds(i * ML_CHUNK, ML_CHUNK)
        kt_bd = jnp.where(head_eq, _dot(kt, sel_ref[i]), 0.0)
        y, c, n, m = _mlstm_chunk(q_ref[rows, :], k_ref[rows, :], kt_bd, v_ref[rows, :], o_ref[rows, :],
                                  g_ref[rows, :], eif, c, n, m)
        y_ref[rows, :] = y.astype(y_ref.dtype)
    c_scr[...] = c
    nm_scr[0:1, :] = n
    nm_scr[1:2, :] = m


MLSTM_CHUNKS_PER_STEP = 4


def _mlstm(mq, mk, mkt, mv, mo, gcol, batch, seq):
    cps = MLSTM_CHUNKS_PER_STEP
    L = ML_CHUNK * cps
    nc = seq // L
    blk = lambda w: pl.BlockSpec((L, w), lambda b, c: (b * nc + c, 0))
    const = lambda a: pl.BlockSpec(a.shape, lambda b, c: (0,) * a.ndim)
    gl = jnp.arange(LANE)[:, None]
    col = jnp.arange(2 * GROUP_W)[None, :]
    eif = (gl == (col // GROUP_W) * ML_HEADS + (col % GROUP_W) // ML_DH).astype(bf16)
    p = jnp.arange(L)[None, :, None]
    cc = jnp.arange(GROUP_W)[None, None, :]
    sel = (p == jnp.arange(cps)[:, None, None] * ML_CHUNK + cc % ML_CHUNK).astype(bf16)
    return pl.pallas_call(
        _mlstm_body,
        grid=(batch, nc),
        in_specs=[blk(256), blk(256), pl.BlockSpec((GROUP_W, L), lambda b, c: (0, b * nc + c)),
                  blk(256), blk(256), blk(128), const(eif), const(sel)],
        out_specs=blk(GROUP_W),
        out_shape=jax.ShapeDtypeStruct((batch * seq, GROUP_W), bf16),
        scratch_shapes=[pltpu.VMEM((GROUP_W, GROUP_W), f32), pltpu.VMEM((8, GROUP_W), f32)],
        compiler_params=_cparams(("arbitrary", "arbitrary")),
        name="mlstm",
    )(mq, mk, mkt, mv, mo, gcol, eif, sel)


def _outproj_body(ya_ref, yb_ref, yc_ref, yd_ref, x_ref, mod_ref, w_ref, g_ref, b_ref, o_ref):
    mix = None
    for j, y_ref in enumerate((ya_ref, yb_ref, yc_ref, yd_ref)):
        part = _dot(y_ref[...], w_ref[GROUP_W * j:GROUP_W * (j + 1), :])
        mix = part if mix is None else mix + part
    r = DN_ALPHA * x_ref[...] + mod_ref[0, 2:3, :] * mix
    o_ref[...] = _layer_norm(r) * g_ref[...] + b_ref[...]


def _outproj(ya, yb, yc, yd, x2d, mod, w_out, ln_g, ln_b, seq):
    t, d = x2d.shape
    tm = 512
    tps = seq // tm
    row = lambda w: pl.BlockSpec((tm, w), lambda i: (i, 0))
    const = lambda a: pl.BlockSpec(a.shape, lambda i: (0,) * a.ndim)
    return pl.pallas_call(
        _outproj_body,
        grid=(t // tm,),
        in_specs=[row(256), row(256), row(256), row(256), row(d),
                  pl.BlockSpec((1, 6, d), lambda i: (i // tps, 0, 0)),
                  const(w_out), const(ln_g), const(ln_b)],
        out_specs=row(d),
        out_shape=jax.ShapeDtypeStruct((t, d), f32),
        compiler_params=_cparams(("arbitrary",)),
        name="outproj",
    )(ya, yb, yc, yd, x2d, mod, w_out, ln_g, ln_b)


def _route(scores, sel):
    ne, tm = sel.shape
    eid = lax.broadcasted_iota(jnp.int32, (ne, tm), 0)
    grp = eid // E_PER_GROUP
    big = ne + 1

    def top2(vals):
        m1 = jnp.max(vals, axis=0, keepdims=True)
        i1 = jnp.min(jnp.where(vals == m1, eid, big), axis=0, keepdims=True)
        rest = jnp.where(eid == i1, NEG, vals)
        m2 = jnp.max(rest, axis=0, keepdims=True)
        i2 = jnp.min(jnp.where(rest == m2, eid, big), axis=0, keepdims=True)
        return m1, i1, m2, i2

    best_score, best_grp = None, None
    for gi in range(N_GROUPS):
        m1, _, m2, _ = top2(jnp.where(grp == gi, sel, NEG))
        sc = m1 + m2
        if best_score is None:
            best_score, best_grp = sc, jnp.zeros_like(sc, dtype=jnp.int32)
        else:
            better = sc > best_score
            best_grp = jnp.where(better, gi, best_grp)
            best_score = jnp.where(better, sc, best_score)
    _, i1, _, i2 = top2(jnp.where(grp == best_grp, sel, NEG))
    picked = jnp.where((eid == i1) | (eid == i2), scores, 0.0)
    return picked / jnp.sum(picked, axis=0, keepdims=True)


def _moe_body(x_ref, mod_ref, wr_ref, br_ref, w1_ref, w3_ref, w2_ref, g_ref, b_ref, o_ref,
              u_scr, gate_scr, acc_scr):
    e = pl.program_id(1)

    @pl.when(e == 0)
    def _():
        u = _layer_norm(x_ref[...]) * (1.0 + mod_ref[0, 4:5, :]) + mod_ref[0, 3:4, :]
        u_hi = u.astype(bf16)
        u_lo = (u - u_hi.astype(f32)).astype(bf16)
        u_scr[...] = u_hi
        ne = br_ref.shape[0]
        part = _dot_nt(wr_ref[...], u_hi)
        logits = part[:ne] + part[ne:] + _dot_nt(wr_ref[:ne, :], u_lo)
        scores = _sigmoid(logits)
        gate_scr[...] = _route(scores, scores + br_ref[...]).T
        acc_scr[...] = jnp.zeros_like(acc_scr)

    u = u_scr[...]
    gates = gate_scr[...]
    lane = lax.broadcasted_iota(jnp.int32, gates.shape, 1)
    ge = jnp.sum(jnp.where(lane == e, gates, 0.0), axis=1, keepdims=True)
    h1 = _dot(u, w1_ref[0])
    h = h1 * _sigmoid(h1) * _dot(u, w3_ref[0]) * ge
    acc_scr[...] += _dot(h.astype(bf16), w2_ref[0])

    @pl.when(e == pl.num_programs(1) - 1)
    def _():
        r = DN_ALPHA * x_ref[...] + mod_ref[0, 5:6, :] * acc_scr[...]
        o_ref[...] = _layer_norm(r) * g_ref[...] + b_ref[...]


def _moe(x2d, mod, w_router, b_router, w1, w3, w2, ln_g, ln_b, seq):
    t, d = x2d.shape
    ne, _, fe = w1.shape
    tm = 1024
    tps = seq // tm
    const = lambda a: pl.BlockSpec(a.shape, lambda i, e: (0,) * a.ndim)
    return pl.pallas_call(
        _moe_body,
        grid=(t // tm, ne),
        in_specs=[
            pl.BlockSpec((tm, d), lambda i, e: (i, 0)),
            pl.BlockSpec((1, 6, d), lambda i, e: (i // tps, 0, 0)),
            const(w_router), const(b_router),
            pl.BlockSpec((1, d, fe), lambda i, e: (e, 0, 0)),
            pl.BlockSpec((1, d, fe), lambda i, e: (e, 0, 0)),
            pl.BlockSpec((1, fe, d), lambda i, e: (e, 0, 0)),
            const(ln_g), const(ln_b),
        ],
        out_specs=pl.BlockSpec((tm, d), lambda i, e: (i, 0)),
        out_shape=jax.ShapeDtypeStruct((t, d), f32),
        scratch_shapes=[pltpu.VMEM((tm, d), bf16), pltpu.VMEM((tm, ne), f32), pltpu.VMEM((tm, d), f32)],
        compiler_params=_cparams(("arbitrary", "arbitrary")),
        name="moe",
    )(x2d, mod, w_router, b_router, w1, w3, w2, ln_g, ln_b)


def _pad_cols(a, width):
    return jnp.pad(a, ((0, 0), (0, width - a.shape[1])))


def _pack_w_in(w):
    parts, acc = [], 0
    for sz in IN_SIZES:
        parts.append(w[:, acc:acc + sz])
        acc += sz
    (a_q, a_kv, a_kr, m_q, m_k, m_v, m_i, m_f, m_o, c_b, c_c, c_h, d_q, d_k, d_v) = parts
    gates = _pad_cols(jnp.concatenate([m_i, m_f], axis=1), LANE)
    packed = jnp.concatenate([
        _pad_cols(a_q, 256), a_kv, _pad_cols(a_kr, LANE),
        m_q, m_k, m_v, gates, m_o, c_b, c_c, c_h, d_q, d_k, d_v], axis=1)
    return packed.astype(bf16)


def _pack_mla(w_uq, w_ukv):
    half = MLA_ROPE // 2
    wq3 = w_uq.reshape(MLA_Q_LORA, MLA_HEADS, MLA_NOPE + MLA_ROPE)
    wkv3 = w_ukv.reshape(MLA_KV_LORA, MLA_HEADS, MLA_NOPE + MLA_V)
    wq, wqr, wk, wv = [], [], [], []
    for h in range(MLA_HEADS):
        nope, pe = wq3[:, h, :MLA_NOPE], wq3[:, h, MLA_NOPE:]
        rot = jnp.concatenate([-pe[:, half:], pe[:, :half]], axis=1)
        zq = jnp.zeros((MLA_Q_LORA, MLA_NOPE), f32)
        pad_rows = lambda a: jnp.pad(a, ((0, 256 - MLA_Q_LORA), (0, 0)))
        wq.append(pad_rows(_pad_cols(jnp.concatenate([nope, pe], axis=1), LANE)))
        wqr.append(pad_rows(_pad_cols(jnp.concatenate([zq, rot], axis=1), LANE)))
        wk.append(_pad_cols(wkv3[:, h, :MLA_NOPE], LANE))
        wv.append(wkv3[:, h, MLA_NOPE:])
    st = lambda xs: jnp.stack(xs).astype(bf16)
    return st(wq), st(wqr), st(wk), st(wv)


def _rope_tables(seq):
    half = DF_DK // 2
    inv = 1.0 / (ROPE_THETA ** (jnp.arange(0, DF_DK, 2, dtype=f32) / DF_DK))
    ang = jnp.arange(seq, dtype=f32)[:, None] * inv[None, :]
    cos, sin = jnp.cos(ang), jnp.sin(ang)
    cos_t = jnp.tile(jnp.concatenate([cos, cos], axis=1), (1, 256 // DF_DK))
    sin_t = jnp.tile(jnp.concatenate([-sin, sin], axis=1), (1, 256 // DF_DK))
    scale = (MLA_NOPE + MLA_ROPE) ** -0.5 * LOG2E
    ones = jnp.ones((seq, MLA_NOPE), f32)
    zeros = jnp.zeros((seq, MLA_NOPE), f32)
    cq = _pad_cols(jnp.concatenate([ones, cos, cos], axis=1) * scale, LANE)
    sq = _pad_cols(jnp.concatenate([zeros, sin, sin], axis=1) * scale, LANE)
    return cos_t, sin_t, cq, sq


def kernel(x, c, w_ada, b_ada, w_in, mla_g_q, mla_g_kv, mla_w_uq, mla_w_ukv, ml_b_i, ml_b_f, sc_w,
           df_lq1, df_lk1, df_lq2, df_lk2, df_g, w_out, ln1_g, ln1_b, w_router, b_router, w1, w3, w2,
           ln2_g, ln2_b):
    batch, seq, d = x.shape
    depth = w_in.shape[0]
    assert MLA_ROPE == DF_DK, "both rotary blocks share one table"
    cos_t, sin_t, cq, sq = _rope_tables(seq)
    place = jnp.zeros((LANE, LANE), f32).at[jnp.arange(MLA_ROPE), MLA_NOPE + jnp.arange(MLA_ROPE)].set(1.0).astype(bf16)
    wr_hi = w_router.astype(bf16)
    wr_split = jnp.concatenate([wr_hi, (w_router - wr_hi.astype(f32)).astype(bf16)], axis=1).T
    mod_all = _ada_mod(c, w_ada, b_ada).reshape(depth, batch, 6, d)
    xf = x.reshape(batch * seq, d)
    for l in range(depth):
        mod = mod_all[l]
        w_packed = _pack_w_in(w_in[l])
        gq = _pad_cols(mla_g_q[l][None, :], 256)
        gkv = mla_g_kv[l][None, :]
        gbias = _pad_cols(jnp.concatenate([ml_b_i[l], ml_b_f[l]])[None, :], LANE)
        wkt = w_packed[:, OFF_MK:OFF_MK + GROUP_W].T
        (qn, kvn, kpe, mq, mk, mv, mo, gcol, yc, dq, dk, dv, mkt) = _inproj(
            xf, mod, w_packed, wkt, cos_t, sin_t, gq, gkv, gbias, sc_w[l], seq)
        wq, wqr, wk, wv = _pack_mla(mla_w_uq[l], mla_w_ukv[l])
        ya = _mla_attention(qn, kvn, kpe, wq, wqr, wk, wv, place, cq, sq, batch, seq)
        yb = _mlstm(mq, mk, mkt, mv, mo, gcol, batch, seq)
        lambda_init = 0.8 - 0.6 * math.exp(-0.3 * l)
        yd = _diff_attention(dq, dk, dv, df_lq1[l][None, :], df_lk1[l][None, :], df_lq2[l][None, :],
                             df_lk2[l][None, :], df_g[l][None, :],
                             batch, seq, lambda_init)
        x1 = _outproj(ya, yb, yc, yd, xf, mod, w_out[l].astype(bf16), ln1_g[l][None, :], ln1_b[l][None, :], seq)
        xf = _moe(x1, mod, wr_split, b_router[:, None], w1[l].astype(bf16), w3[l].astype(bf16),
                  w2[l].astype(bf16), ln2_g[l][None, :], ln2_b[l][None, :], seq)
    return xf.reshape(batch, seq, d)
```

```python
import functools
import math

import jax
import jax.numpy as jnp
from jax import lax
from jax.experimental import pallas as pl
from jax.experimental.pallas import tpu as pltpu

f32 = jnp.float32
bf16 = jnp.bfloat16
HIGHEST = lax.Precision.HIGHEST

GROUP_W = 256
MLA_HEADS, MLA_NOPE, MLA_ROPE, MLA_V = 4, 64, 32, 64
MLA_Q_LORA, MLA_KV_LORA = 192, 128
ML_HEADS, ML_DH, ML_CHUNK = 4, 64, 64
DF_HEADS, DF_DK, DF_V = 4, 32, 64
ROPE_THETA = 10000.0
N_EXPERTS, N_GROUPS, E_PER_GROUP, D_EXPERT = 16, 4, 4, 256
DEPTH = 2
DN_ALPHA = (2 * DEPTH) ** 0.25
LN_EPS = 1e-5
RMS_EPS = 1e-6
IN_SIZES = (192, 128, 32, 256, 256, 256, 4, 4, 256, 256, 256, 256, 256, 256, 256)

OFF_PQ, OFF_PKV, OFF_KR = 0, 256, 384
OFF_MQ, OFF_MK, OFF_MV, OFF_MG, OFF_MO = 512, 768, 1024, 1280, 1408
OFF_CB, OFF_CC, OFF_CH = 1664, 1920, 2176
OFF_DQ, OFF_DK, OFF_DV = 2432, 2688, 2944
IN_PACKED = 3200

LANE = 128
VMEM_LIMIT = 48 * 1024 * 1024
NEG = -1e30
LOG2E = math.log2(math.e)


def _cparams(sem):
    return pltpu.CompilerParams(dimension_semantics=sem, vmem_limit_bytes=VMEM_LIMIT)


def _sigmoid(x):
    return 1.0 / (1.0 + jnp.exp(-x))


def _layer_norm(x):
    mu = jnp.mean(x, axis=-1, keepdims=True)
    xc = x - mu
    var = jnp.mean(xc * xc, axis=-1, keepdims=True)
    return xc * lax.rsqrt(var + LN_EPS)


def _dot(a, b, **kw):
    return jnp.dot(a, b, preferred_element_type=f32, **kw)


def _dot_nt(a, b, **kw):
    return lax.dot_general(a, b, (((1,), (1,)), ((), ())), preferred_element_type=f32, **kw)


def _dot_tn(a, b, **kw):
    return lax.dot_general(a, b, (((0,), (0,)), ((), ())), preferred_element_type=f32, **kw)


def _expand_groups(cols, rows, width, group):
    lane = lax.broadcasted_iota(jnp.int32, (rows, width), 1)
    out = jnp.broadcast_to(cols[-1], (rows, width))
    for h in range(len(cols) - 2, -1, -1):
        out = jnp.where(lane < group * (h + 1), cols[h], out)
    return out


def _lane_group_mask(shape, lo, hi):
    lane = lax.broadcasted_iota(jnp.int32, shape, 1)
    return (lane >= lo) & (lane < hi)


def _ada_body(c_ref, w_ref, b_ref, o_ref):
    c = c_ref[...]
    ca = (c * _sigmoid(c)).astype(bf16)
    o_ref[0] = _dot(ca, w_ref[0].astype(bf16)) + b_ref[0]


def _ada_mod(c, w_ada, b_ada):
    depth, d, n = w_ada.shape
    b = c.shape[0]
    tn = 1536
    return pl.pallas_call(
        _ada_body,
        grid=(depth, n // tn),
        in_specs=[
            pl.BlockSpec((b, d), lambda l, j: (0, 0)),
            pl.BlockSpec((1, d, tn), lambda l, j: (l, 0, j)),
            pl.BlockSpec((1, 1, tn), lambda l, j: (l, 0, j)),
        ],
        out_specs=pl.BlockSpec((1, b, tn), lambda l, j: (l, 0, j)),
        out_shape=jax.ShapeDtypeStruct((depth, b, n), f32),
        compiler_params=_cparams(("arbitrary", "arbitrary")),
        name="ada_mod",
    )(c, w_ada, b_ada.reshape(depth, 1, n))


def _rope_lanes(x, cos, sin_signed):
    w = x.shape[1]
    lane = lax.broadcasted_iota(jnp.int32, x.shape, 1)
    rot = jnp.where(lane % 32 < 16, pltpu.roll(x, w - 16, axis=1), pltpu.roll(x, 16, axis=1))
    return x * cos + rot * sin_signed


def _inproj_body(x_ref, mod_ref, w_ref, wkt_ref, cos_ref, sin_ref, gq_ref, gkv_ref, gb_ref, cw_ref,
                 qn_ref, kvn_ref, kpe_ref, mq_ref, mk_ref, mv_ref, mo_ref, gcol_ref, yc_ref,
                 dq_ref, dk_ref, dv_ref, mkt_ref, carry_ref, *, tiles_per_seq):
    i = pl.program_id(0)
    tm = x_ref.shape[0]
    u = _layer_norm(x_ref[...]) * (1.0 + mod_ref[0, 1:2, :]) + mod_ref[0, 0:1, :]
    ub = u.astype(bf16)

    def seg(off, n):
        return _dot(ub, w_ref[:, off:off + n])

    cos = cos_ref[...]
    sin = sin_ref[...]

    pq = seg(OFF_PQ, 256)
    ms = jnp.sum(pq * pq, axis=-1, keepdims=True) * (1.0 / MLA_Q_LORA)
    qn_ref[...] = (pq * lax.rsqrt(ms + RMS_EPS) * gq_ref[...]).astype(bf16)
    pkv = seg(OFF_PKV, 128)
    ms = jnp.sum(pkv * pkv, axis=-1, keepdims=True) * (1.0 / MLA_KV_LORA)
    kvn_ref[...] = (pkv * lax.rsqrt(ms + RMS_EPS) * gkv_ref[...]).astype(bf16)
    kpe_ref[...] = _rope_lanes(seg(OFF_KR, 128), cos[:, :LANE], sin[:, :LANE]).astype(bf16)

    mq_ref[...] = (seg(OFF_MQ, 256) * (ML_DH ** -0.5)).astype(bf16)
    mk_ref[...] = seg(OFF_MK, 256).astype(bf16)
    mkt_ref[...] = _dot_nt(wkt_ref[...], ub).astype(bf16)
    mv_ref[...] = seg(OFF_MV, 256).astype(bf16)
    gcol_ref[...] = seg(OFF_MG, 128) + gb_ref[...]
    mo_ref[...] = seg(OFF_MO, 256).astype(bf16)

    uc = seg(OFF_CC, 256) * seg(OFF_CH, 256)

    @pl.when(i % tiles_per_seq == 0)
    def _():
        carry_ref[...] = jnp.zeros_like(carry_ref)

    prev = carry_ref[...]
    row =lax.broadcasted_iota(jnp.int32, uc.shape, 0)
    u1 = jnp.where(row == 0, prev[7:8, :], pltpu.roll(uc, 1, axis=0))
    u2 = jnp.where(row == 0, prev[6:7, :], jnp.where(row == 1, prev[7:8, :], pltpu.roll(uc, 2, axis=0)))
    carry_ref[...] = uc[tm - 8:, :]
    conv = cw_ref[0:1, :] * u2 + cw_ref[1:2, :] * u1 + cw_ref[2:3, :] * uc
    yc_ref[...] = (seg(OFF_CB, 256) * conv).astype(bf16)

    dq_ref[...] = (_rope_lanes(seg(OFF_DQ, 256), cos, sin) * (DF_DK ** -0.5 * LOG2E)).astype(bf16)
    dk_ref[...] = _rope_lanes(seg(OFF_DK, 256), cos, sin).astype(bf16)
    dv_ref[...] = seg(OFF_DV, 256).astype(bf16)


def _inproj(x2d, mod, w_packed, wkt, cos_t, sin_t, gq, gkv, gbias, conv_w, seq):
    t, d = x2d.shape
    tm = 512
    tps = seq // tm
    row = lambda w: pl.BlockSpec((tm, w), lambda i: (i, 0))
    const = lambda a: pl.BlockSpec(a.shape, lambda i: (0,) * a.ndim)
    widths = (256, 128, 128, 256, 256, 256, 256, 128, 256, 256, 256, 256)
    dtypes = (bf16, bf16, bf16, bf16, bf16, bf16, bf16, f32, bf16, bf16, bf16, bf16)
    return pl.pallas_call(
        functools.partial(_inproj_body, tiles_per_seq=tps),
        grid=(t // tm,),
        in_specs=[
            row(d),
            pl.BlockSpec((1, 6, d), lambda i: (i // tps, 0, 0)),
            const(w_packed), const(wkt),
            pl.BlockSpec((tm, 256), lambda i: (i % tps, 0)),
            pl.BlockSpec((tm, 256), lambda i: (i % tps, 0)),
            const(gq), const(gkv), const(gbias), const(conv_w),
        ],
        out_specs=[row(w) for w in widths] + [pl.BlockSpec((GROUP_W, tm), lambda i: (0, i))],
        out_shape=[jax.ShapeDtypeStruct((t, w), dt) for w, dt in zip(widths, dtypes)]
        + [jax.ShapeDtypeStruct((GROUP_W, t), bf16)],
        scratch_shapes=[pltpu.VMEM((8, 256), f32)],
        compiler_params=_cparams(("arbitrary",)),
        name="inproj",
    )(x2d, mod, w_packed, wkt, cos_t, sin_t, gq, gkv, gbias, conv_w)


def _causal_softmax_heads(q_heads, k_at, v_at, dv, qi, s_scr, m_scr, acc_scr):
    n_heads = len(q_heads)
    tq = q_heads[0].shape[0]
    half = tq // 2

    def scores(g, j):
        return _dot_nt(q_heads[g], k_at(g, pl.multiple_of(j * tq, tq)))

    def fold_max(g, s):
        m_scr[g] = jnp.maximum(m_scr[g], jnp.maximum(s[:, :half], s[:, half:]))

    m_scr[...] = jnp.full(m_scr.shape, NEG, f32)
    acc_scr[...] = jnp.zeros_like(acc_scr)

    def score_blocks(js):
        for g in range(n_heads):
            mx = None
            for j in js:
                s = scores(g, j)
                s_scr[g, j] = s
                s = jnp.maximum(s[:, :half], s[:, half:])
                mx = s if mx is None else jnp.maximum(mx, s)
            m_scr[g] = jnp.maximum(m_scr[g], mx)

    def prob_blocks(js):
        for g in range(n_heads):
            mb = m_scr[g]
            pv = None
            for j in js:
                sj = s_scr[g, j]
                p = jnp.concatenate([jnp.exp2(sj[:, :half] - mb), jnp.exp2(sj[:, half:] - mb)],
                                    axis=1).astype(bf16)
                d = _dot(p, v_at(g, pl.multiple_of(j * tq, tq)))
                pv = d if pv is None else pv + d
            acc_scr[g] += pv

    def over_full_blocks(blocks_fn):
        @pl.loop(0, qi // 2)
        def _(jj):
            blocks_fn((2 * jj, 2 * jj + 1))

        @pl.when(qi % 2 == 1)
        def _():
            blocks_fn((qi - 1,))

    over_full_blocks(score_blocks)
    rowi = lax.broadcasted_iota(jnp.int32, (tq, tq), 0)
    coli = lax.broadcasted_iota(jnp.int32, (tq, tq), 1)
    for g in range(n_heads):
        s = jnp.where(coli <= rowi, scores(g, qi), NEG)
        s_scr[g, qi] = s
        fold_max(g, s)
        m_scr[g] = jnp.broadcast_to(jnp.max(m_scr[g], axis=1, keepdims=True), (tq, half))

    over_full_blocks(prob_blocks)
    prob_blocks((qi,))
    outs = []
    for g in range(n_heads):
        acc = acc_scr[g]
        outs.append(acc[:, :dv] / acc[:, dv:dv + 1])
    return outs


def _softmax_scratch(n_heads, nq, tq):
    return [pltpu.VMEM((n_heads, nq, tq, tq), f32), pltpu.VMEM((n_heads, tq, tq // 2), f32),
            pltpu.VMEM((n_heads, tq, LANE), f32)]


def _with_ones_lane(v, dv):
    rows = v.shape[0]
    lane = lax.broadcasted_iota(jnp.int32, (rows, LANE - dv), 1)
    return jnp.concatenate([v.astype(bf16), (lane == 0).astype(bf16)], axis=1)


def _mla_body(qn_ref, kvn_ref, kpe_ref, wq_ref, wqr_ref, wk_ref, wv_ref, place_ref, cq_ref, sq_ref,
              o_ref, k_scr, v_scr, s_scr, m_scr, acc_scr):
    qi = pl.program_id(1)
    tq = qn_ref.shape[0]

    @pl.when(qi == 0)
    def _():
        kvn = kvn_ref[...]
        kpe_placed = _dot(kpe_ref[...], place_ref[...])
        for h in range(MLA_HEADS):
            k_scr[h] = (_dot(kvn, wk_ref[h]) + kpe_placed).astype(bf16)
            v_scr[h] = _with_ones_lane(_dot(kvn, wv_ref[h]), MLA_V)

    qn = qn_ref[...]
    cq = cq_ref[...]
    sq = sq_ref[...]
    q_heads = [(_dot(qn, wq_ref[h]) * cq + _dot(qn, wqr_ref[h]) * sq).astype(bf16)
               for h in range(MLA_HEADS)]
    outs = _causal_softmax_heads(
        q_heads, lambda g, start: k_scr[g, pl.ds(start, tq), :],
        lambda g, start: v_scr[g, pl.ds(start, tq), :], MLA_V, qi, s_scr, m_scr, acc_scr)
    o_ref[...] = jnp.concatenate(outs, axis=1).astype(bf16)


def _mla_attention(qn, kvn, kpe, wq, wqr, wk, wv, place, cq, sq, batch, seq):
    tq = 256
    nq = seq // tq
    const = lambda a: pl.BlockSpec(a.shape, lambda b, i: (0,) * a.ndim)
    return pl.pallas_call(
        _mla_body,
        grid=(batch, nq),
        in_specs=[
            pl.BlockSpec((tq, 256), lambda b, i: (b * nq + i, 0)),
            pl.BlockSpec((seq, 128), lambda b, i: (b, 0)),
            pl.BlockSpec((seq, 128), lambda b, i: (b, 0)),
            const(wq), const(wqr), const(wk), const(wv), const(place),
            pl.BlockSpec((tq, 128), lambda b, i: (i, 0)),
            pl.BlockSpec((tq, 128), lambda b, i: (i, 0)),
        ],
        out_specs=pl.BlockSpec((tq, GROUP_W), lambda b, i: (b * nq + i, 0)),
        out_shape=jax.ShapeDtypeStruct((batch * seq, GROUP_W), bf16),
        scratch_shapes=[pltpu.VMEM((MLA_HEADS, seq, 128), bf16),
                        pltpu.VMEM((MLA_HEADS, seq, LANE), bf16)]
        + _softmax_scratch(MLA_HEADS, nq, tq),
        compiler_params=_cparams(("arbitrary", "arbitrary")),
        name="mla_attn",
    )(qn, kvn, kpe, wq, wqr, wk, wv, place, cq, sq)


def _diff_body(q_ref, k_ref, v_ref, lq1_ref, lk1_ref, lq2_ref, lk2_ref, g_ref,
               o_ref, v_scr, s_scr, m_scr, acc_scr, *, lambda_init):
    qi = pl.program_id(1)
    tq = q_ref.shape[0]

    @pl.when(qi == 0)
    def _():
        v = v_ref[...]
        for h in range(DF_HEADS):
            v_scr[h] = _with_ones_lane(v[:, DF_V * h:DF_V * (h + 1)], DF_V)

    q = q_ref[...]
    lam = (jnp.exp(jnp.sum(lq1_ref[...] * lk1_ref[...], axis=1, keepdims=True))
           - jnp.exp(jnp.sum(lq2_ref[...] * lk2_ref[...], axis=1, keepdims=True)) + lambda_init)
    q_heads = [jnp.where(_lane_group_mask(q.shape, DF_DK * g, DF_DK * (g + 1)), q, jnp.zeros_like(q))
               for g in range(2 * DF_HEADS)]
    maps = _causal_softmax_heads(
        q_heads, lambda g, start: k_ref[pl.ds(start, tq), :],
        lambda g, start: v_scr[g // 2, pl.ds(start, tq), :], DF_V, qi, s_scr, m_scr, acc_scr)
    outs = []
    for h in range(DF_HEADS):
        o = maps[2 * h] - lam * maps[2 * h + 1]
        ms = jnp.mean(o * o, axis=1, keepdims=True)
        outs.append(o * lax.rsqrt(ms + RMS_EPS) * g_ref[...] * (1.0 - lambda_init))
    o_ref[...] = jnp.concatenate(outs, axis=1).astype(bf16)


def _diff_attention(dq, dk, dv, lq1, lk1, lq2, lk2, g_sub, batch, seq, lambda_init):
    tq = 256
    nq = seq // tq
    const = lambda a: pl.BlockSpec(a.shape, lambda b, i: (0,) * a.ndim)
    return pl.pallas_call(
        functools.partial(_diff_body, lambda_init=lambda_init),
        grid=(batch, nq),
        in_specs=[
            pl.BlockSpec((tq, 256), lambda b, i: (b * nq + i, 0)),
            pl.BlockSpec((seq, 256), lambda b, i: (b, 0)),
            pl.BlockSpec((seq, 256), lambda b, i: (b, 0)),
            const(lq1), const(lk1), const(lq2), const(lk2), const(g_sub),
        ],
        out_specs=pl.BlockSpec((tq, GROUP_W), lambda b, i: (b * nq + i, 0)),
        out_shape=jax.ShapeDtypeStruct((batch * seq, GROUP_W), bf16),
        scratch_shapes=[pltpu.VMEM((DF_HEADS, seq, LANE), bf16)]
        + _softmax_scratch(2 * DF_HEADS, nq, tq),
        compiler_params=_cparams(("arbitrary", "arbitrary")),
        name="diff_attn",
    )(dq, dk, dv, lq1, lk1, lq2, lk2, g_sub)


def _split3(a):
    hi = a.astype(bf16)
    r1 = a - hi.astype(f32)
    mid = r1.astype(bf16)
    return hi, mid, (r1 - mid.astype(f32)).astype(bf16)


def _dot_exact_rhs01(a, b01):
    hi, mid, lo = _split3(a)
    return _dot(hi, b01) + _dot(mid, b01) + _dot(lo, b01)


def _dot_exact_lhs01(a01, b):
    hi, mid, lo = _split3(b)
    return _dot(a01, hi) + _dot(a01, mid) + _dot(a01, lo)


def _mlstm_chunk(q, k, kt_bd, v, o, g, eif, c_prev, n_prev, m_prev):
    L, W = q.shape
    row = lax.broadcasted_iota(jnp.int32, (L, W), 0)
    pos = lax.broadcasted_iota(jnp.int32, (L, W), 1) % L
    lf = jnp.minimum(g, 0.0) - jnp.log(1.0 + jnp.exp(-jnp.abs(g)))
    glane = lax.broadcasted_iota(jnp.int32, g.shape, 1)
    wide = _dot_exact_rhs01(jnp.where(glane < ML_HEADS, g, lf), eif)
    iw, lfw = wide[:, :W], wide[:, W:]
    tril = (lax.broadcasted_iota(jnp.int32, (L, L), 0) >= lax.broadcasted_iota(jnp.int32, (L, L), 1))
    bw = _dot_exact_lhs01(tril.astype(bf16), lfw)
    rw = iw - bw
    cm = rw
    for sh in (1, 2, 4, 8, 16, 32):
        cm = jnp.maximum(cm, jnp.where(row >= sh, pltpu.roll(cm, sh, axis=0), NEG))
    mw = jnp.maximum(m_prev, cm)
    a_inter = jnp.exp(m_prev - mw)
    r_row = jnp.sum(jnp.where(pos == row, rw, 0.0), axis=0, keepdims=True)
    ew = jnp.exp(jnp.where(pos <= row, r_row - mw, NEG))
    pw = _dot(q, kt_bd.astype(bf16)) * ew
    head_eq = (lax.broadcasted_iota(jnp.int32, (W, W), 0) // ML_DH
               == lax.broadcasted_iota(jnp.int32, (W, W), 1) // ML_DH)
    v4 = jnp.concatenate([v] * ML_HEADS, axis=0)
    vbd = jnp.where(head_eq, v4, jnp.zeros_like(v4))
    obd = head_eq.astype(bf16)
    den = a_inter * _dot_exact_rhs01(q.astype(f32) * n_prev, obd) + _dot_exact_rhs01(pw, obd)
    hden = jnp.maximum(jnp.abs(den), jnp.exp(-(bw + mw)))
    num = a_inter * _dot(q, c_prev.astype(bf16)) + _dot(pw.astype(bf16), vbd)
    y = num / hden * _sigmoid(o.astype(f32))

    rmax, m_last = cm[L - 1:L, :], mw[L - 1:L, :]
    a_w = jnp.exp(m_prev - m_last)
    b_w = jnp.exp(rmax - m_last)
    c_loc = _dot((kt_bd * jnp.exp(r_row - rmax)).astype(bf16), vbd)
    n_loc = jnp.sum(k.astype(f32) * jnp.exp(rw - rmax), axis=0, keepdims=True)
    return y, c_prev * a_w + c_loc * b_w, a_w * n_prev + b_w * n_loc, bw[L - 1:L, :] + m_last


def _mlstm_body(q_ref, k_ref, kt_ref, v_ref, o_ref, g_ref, eif_ref, sel_ref, y_ref, c_scr, nm_scr):
    @pl.when(pl.program_id(1) == 0)
    def _():
        c_scr[...] = jnp.zeros_like(c_scr)
        nm_scr[...] = jnp.zeros_like(nm_scr)

    c, n, m = c_scr[...], nm_scr[0:1, :], nm_scr[1:2, :]
    kt = kt_ref[...]
    eif = eif_ref[...]
    w = kt.shape[0]
    head_eq = (lax.broadcasted_iota(jnp.int32, (w, w), 0) // ML_DH
               == lax.broadcasted_iota(jnp.int32, (w, w), 1) // ML_DH)
    for i in range(q_ref.shape[0] // ML_CHUNK):
        rows = pl.ds(i * ML_CHUNK, ML_CHUNK)
        kt_bd = jnp.where(head_eq, _dot(kt, sel_ref[i]), 0.0)
        y, c, n, m = _mlstm_chunk(q_ref[rows, :], k_ref[rows, :], kt_bd, v_ref[rows, :], o_ref[rows, :],
                                  g_ref[rows, :], eif, c, n, m)
        y_ref[rows, :] = y.astype(y_ref.dtype)
    c_scr[...] = c
    nm_scr[0:1, :] = n
    nm_scr[1:2, :] = m


MLSTM_CHUNKS_PER_STEP = 4


def _mlstm(mq, mk, mkt, mv, mo, gcol, batch, seq):
    cps = MLSTM_CHUNKS_PER_STEP
    L = ML_CHUNK * cps
    nc = seq // L
    blk = lambda w: pl.BlockSpec((L, w), lambda b, c: (b * nc + c, 0))
    const = lambda a: pl.BlockSpec(a.shape, lambda b, c: (0,) * a.ndim)
    gl = jnp.arange(LANE)[:, None]
    col = jnp.arange(2 * GROUP_W)[None, :]
    eif = (gl == (col // GROUP_W) * ML_HEADS + (col % GROUP_W) // ML_DH).astype(bf16)
    p = jnp.arange(L)[None, :, None]
    cc = jnp.arange(GROUP_W)[None, None, :]
    sel = (p == jnp.arange(cps)[:, None, None] * ML_CHUNK + cc % ML_CHUNK).astype(bf16)
    return pl.pallas_call(
        _mlstm_body,
        grid=(batch, nc),
        in_specs=[blk(256), blk(256), pl.BlockSpec((GROUP_W, L), lambda b, c: (0, b * nc + c)),
                  blk(256), blk(256), blk(128), const(eif), const(sel)],
        out_specs=blk(GROUP_W),
        out_shape=jax.ShapeDtypeStruct((batch * seq, GROUP_W), bf16),
        scratch_shapes=[pltpu.VMEM((GROUP_W, GROUP_W), f32), pltpu.VMEM((8, GROUP_W), f32)],
        compiler_params=_cparams(("arbitrary", "arbitrary")),
        name="mlstm",
    )(mq, mk, mkt, mv, mo, gcol, eif, sel)


def _route(scores, sel):
    ne, tm = sel.shape
    eid = lax.broadcasted_iota(jnp.int32, (ne, tm), 0)
    grp = eid // E_PER_GROUP
    big = ne + 1

    def top2(vals):
        m1 = jnp.max(vals, axis=0, keepdims=True)
        i1 = jnp.min(jnp.where(vals == m1, eid, big), axis=0, keepdims=True)
        rest = jnp.where(eid == i1, NEG, vals)
        m2 = jnp.max(rest, axis=0, keepdims=True)
        i2 = jnp.min(jnp.where(rest == m2, eid, big), axis=0, keepdims=True)
        return m1, i1, m2, i2

    best_score, best_grp = None, None
    for gi in range(N_GROUPS):
        m1, _, m2, _ = top2(jnp.where(grp == gi, sel, NEG))
        sc = m1 + m2
        if best_score is None:
            best_score, best_grp = sc, jnp.zeros_like(sc, dtype=jnp.int32)
        else:
            better = sc > best_score
            best_grp = jnp.where(better, gi, best_grp)
            best_score = jnp.where(better, sc, best_score)
    _, i1, _, i2 = top2(jnp.where(grp == best_grp, sel, NEG))
    picked = jnp.where((eid == i1) | (eid == i2), scores, 0.0)
    return picked / jnp.sum(picked, axis=0, keepdims=True), best_grp


MOE_ROW_TILE = 256


def _outproj_route_body(ya_ref, yb_ref, yc_ref, yd_ref, x_ref, mod_ref, w_ref, g_ref, b_ref,
                        wr_ref, br_ref, x1_ref, u2e_ref, grp_ref):
    d = x_ref.shape[1]
    mix = None
    for j, y_ref in enumerate((ya_ref, yb_ref, yc_ref, yd_ref)):
        part = _dot(y_ref[...], w_ref[GROUP_W * j:GROUP_W * (j + 1), :])
        mix = part if mix is None else mix + part
    x1 = _layer_norm(DN_ALPHA * x_ref[...] + mod_ref[0, 2:3, :] * mix) * g_ref[...] + b_ref[...]
    x1_ref[...] = x1
    u = _layer_norm(x1) * (1.0 + mod_ref[0, 4:5, :]) + mod_ref[0, 3:4, :]
    u_hi = u.astype(bf16)
    u_lo = (u - u_hi.astype(f32)).astype(bf16)
    ne = br_ref.shape[0]
    part = _dot_nt(wr_ref[...], u_hi)
    logits = part[:ne] + part[ne:] + _dot_nt(wr_ref[:ne, :], u_lo)
    scores = _sigmoid(logits)
    gates_t, best_grp = _route(scores, scores + br_ref[...])
    u2e_ref[:, :d] = u_hi.astype(f32)
    u2e_ref[:, d:] = jnp.zeros((x1.shape[0], LANE), f32)
    u2e_ref[:, d:d + ne] = gates_t.T
    grp_ref[...] = best_grp


def _outproj_route(ya, yb, yc, yd, x2d, mod, w_out, ln_g, ln_b, wr_split, b_router, seq):
    t, d = x2d.shape
    tm = 512
    tps = seq // tm
    row = lambda w: pl.BlockSpec((tm, w), lambda i: (i, 0))
    const = lambda a: pl.BlockSpec(a.shape, lambda i: (0,) * a.ndim)
    return pl.pallas_call(
        _outproj_route_body,
        grid=(t // tm,),
        in_specs=[row(256), row(256), row(256), row(256), row(d),
                  pl.BlockSpec((1, 6, d), lambda i: (i // tps, 0, 0)),
                  const(w_out), const(ln_g), const(ln_b), const(wr_split), const(b_router)],
        out_specs=[row(d), row(d + LANE), pl.BlockSpec((1, tm), lambda i: (0, i))],
        out_shape=[jax.ShapeDtypeStruct((t, d), f32), jax.ShapeDtypeStruct((t, d + LANE), f32),
                   jax.ShapeDtypeStruct((1, t), jnp.int32)],
        compiler_params=_cparams(("arbitrary",)),
        name="outproj_route",
    )(ya, yb, yc, yd, x2d, mod, w_out, ln_g, ln_b, wr_split, b_router)


def _dispatch_plan(grp, tr):
    t = grp.shape[0]
    rows = t + N_GROUPS * tr
    onehot = (grp[:, None] == jnp.arange(N_GROUPS)[None, :]).astype(jnp.int32)
    counts = jnp.sum(onehot, axis=0)
    rank = jnp.sum((jnp.cumsum(onehot, axis=0) - 1) * onehot, axis=1)
    padded = ((counts + tr - 1) // tr) * tr
    ends = jnp.cumsum(padded)
    pos = (ends - padded)[grp] + rank
    src = jnp.zeros((rows,), jnp.int32).at[pos].set(jnp.arange(t, dtype=jnp.int32))
    tile_start = jnp.arange(rows // tr, dtype=jnp.int32) * tr
    tile_grp = jnp.minimum(jnp.sum((tile_start[:, None] >= ends[None, :]).astype(jnp.int32), axis=1),
                           N_GROUPS - 1)
    return pos.astype(jnp.int32), src, tile_grp.astype(jnp.int32), (ends[-1:] // tr).astype(jnp.int32)


def _start_row_gather(idx_ref, first, n_rows, src_hbm, dst, sem):
    for i in range(n_rows):
        pltpu.make_async_copy(src_hbm.at[pl.ds(idx_ref[first + i], 1), :], dst.at[pl.ds(i, 1), :], sem).start()


def _wait_row_gather(n_rows, src_hbm, dst, sem):
    pltpu.make_async_copy(src_hbm.at[pl.ds(0, n_rows), :], dst, sem).wait()


def _experts_body(src_ref, tgrp_ref, ntile_ref, u2e_hbm, w1_ref, w3_ref, w2_ref, o_ref, buf, sem):
    j = pl.program_id(0)
    slot = lax.rem(j, 2)
    n_tiles = ntile_ref[0]
    tr, d = o_ref.shape

    @pl.when(j == 0)
    def _():
        _start_row_gather(src_ref, 0, tr, u2e_hbm, buf.at[0], sem.at[0])

    @pl.when(j + 1 < n_tiles)
    def _():
        _start_row_gather(src_ref, (j + 1) * tr, tr, u2e_hbm, buf.at[1 - slot], sem.at[1 - slot])

    @pl.when(j < n_tiles)
    def _():
        _wait_row_gather(tr, u2e_hbm, buf.at[slot], sem.at[slot])
        xs = buf[slot]
        xb = xs[:, :d].astype(bf16)
        gslab = xs[:, d:]
        lane = lax.broadcasted_iota(jnp.int32, gslab.shape, 1)
        first_expert = tgrp_ref[j] * E_PER_GROUP
        acc = None
        for e in range(E_PER_GROUP):
            ge = jnp.sum(jnp.where(lane == first_expert + e, gslab, 0.0), axis=1, keepdims=True)
            h1 = _dot(xb, w1_ref[0, e])
            h = h1 * _sigmoid(h1) * _dot(xb, w3_ref[0, e]) * ge
            part = _dot(h.astype(bf16), w2_ref[0, e])
            acc = part if acc is None else acc + part
        o_ref[...] = acc

    @pl.when(j >= n_tiles)
    def _():
        o_ref[...] = jnp.zeros_like(o_ref)


def _experts(src, tile_grp, n_tiles, u2e, w1, w3, w2):
    rows = src.shape[0]
    tr = MOE_ROW_TILE
    ne, d, fe = w1.shape
    grouped = lambda w: w.reshape(N_GROUPS, E_PER_GROUP, *w.shape[1:])
    wspec = lambda a, b: pl.BlockSpec((1, E_PER_GROUP, a, b), lambda j, src, tg, nt: (tg[j], 0, 0, 0))
    return pl.pallas_call(
        _experts_body,
        grid_spec=pltpu.PrefetchScalarGridSpec(
            num_scalar_prefetch=3, grid=(rows // tr,),
            in_specs=[pl.BlockSpec(memory_space=pl.ANY), wspec(d, fe), wspec(d, fe), wspec(fe, d)],
            out_specs=pl.BlockSpec((tr, d), lambda j, src, tg, nt: (j, 0)),
            scratch_shapes=[pltpu.VMEM((2, tr, u2e.shape[1]), f32), pltpu.SemaphoreType.DMA((2,))]),
        out_shape=jax.ShapeDtypeStruct((rows, d), f32),
        compiler_params=_cparams(("arbitrary",)),
        name="experts",
    )(src, tile_grp, n_tiles, u2e, grouped(w1), grouped(w3), grouped(w2))


def _moe_combine_body(pos_ref, ys_hbm, x_ref, mod_ref, g_ref, b_ref, o_ref, buf, sem):
    i = pl.program_id(0)
    slot = lax.rem(i, 2)
    tm = x_ref.shape[0]

    @pl.when(i == 0)
    def _():
        _start_row_gather(pos_ref, 0, tm, ys_hbm, buf.at[0], sem.at[0])

    @pl.when(i + 1 < pl.num_programs(0))
    def _():
        _start_row_gather(pos_ref, (i + 1) * tm, tm, ys_hbm, buf.at[1 - slot], sem.at[1 - slot])

    _wait_row_gather(tm, ys_hbm, buf.at[slot], sem.at[slot])
    r = DN_ALPHA * x_ref[...] + mod_ref[0, 5:6, :] * buf[slot]
    o_ref[...] = _layer_norm(r) * g_ref[...] + b_ref[...]


def _moe_combine(pos, ys, x1, mod, ln_g, ln_b, seq):
    t, d = x1.shape
    tm = 256
    tps = seq // tm
    const = lambda a: pl.BlockSpec(a.shape, lambda i, pos: (0,) * a.ndim)
    return pl.pallas_call(
        _moe_combine_body,
        grid_spec=pltpu.PrefetchScalarGridSpec(
            num_scalar_prefetch=1, grid=(t // tm,),
            in_specs=[pl.BlockSpec(memory_space=pl.ANY),
                      pl.BlockSpec((tm, d), lambda i, pos: (i, 0)),
                      pl.BlockSpec((1, 6, d), lambda i, pos: (i // tps, 0, 0)),
                      const(ln_g), const(ln_b)],
            out_specs=pl.BlockSpec((tm, d), lambda i, pos: (i, 0)),
            scratch_shapes=[pltpu.VMEM((2, tm, d), f32), pltpu.SemaphoreType.DMA((2,))]),
        out_shape=jax.ShapeDtypeStruct((t, d), f32),
        compiler_params=_cparams(("arbitrary",)),
        name="moe_combine",
    )(pos, ys, x1, mod, ln_g, ln_b)


def _pad_cols(a, width):
    return jnp.pad(a, ((0, 0), (0, width - a.shape[1])))


def _pack_w_in(w):
    parts, acc = [], 0
    for sz in IN_SIZES:
        parts.append(w[:, acc:acc + sz])
        acc += sz
    (a_q, a_kv, a_kr, m_q, m_k, m_v, m_i, m_f, m_o, c_b, c_c, c_h, d_q, d_k, d_v) = parts
    gates = _pad_cols(jnp.concatenate([m_i, m_f], axis=1), LANE)
    packed = jnp.concatenate([
        _pad_cols(a_q, 256), a_kv, _pad_cols(a_kr, LANE),
        m_q, m_k, m_v, gates, m_o, c_b, c_c, c_h, d_q, d_k, d_v], axis=1)
    return packed.astype(bf16)


def _pack_mla(w_uq, w_ukv):
    half = MLA_ROPE // 2
    wq3 = w_uq.reshape(MLA_Q_LORA, MLA_HEADS, MLA_NOPE + MLA_ROPE)
    wkv3 = w_ukv.reshape(MLA_KV_LORA, MLA_HEADS, MLA_NOPE + MLA_V)
    wq, wqr, wk, wv = [], [], [], []
    for h in range(MLA_HEADS):
        nope, pe = wq3[:, h, :MLA_NOPE], wq3[:, h, MLA_NOPE:]
        rot = jnp.concatenate([-pe[:, half:], pe[:, :half]], axis=1)
        zq = jnp.zeros((MLA_Q_LORA, MLA_NOPE), f32)
        pad_rows = lambda a: jnp.pad(a, ((0, 256 - MLA_Q_LORA), (0, 0)))
        wq.append(pad_rows(_pad_cols(jnp.concatenate([nope, pe], axis=1), LANE)))
        wqr.append(pad_rows(_pad_cols(jnp.concatenate([zq, rot], axis=1), LANE)))
        wk.append(_pad_cols(wkv3[:, h, :MLA_NOPE], LANE))
        wv.append(wkv3[:, h, MLA_NOPE:])
    st = lambda xs: jnp.stack(xs).astype(bf16)
    return st(wq), st(wqr), st(wk), st(wv)


def _rope_tables(seq):
    half = DF_DK // 2
    inv = 1.0 / (ROPE_THETA ** (jnp.arange(0, DF_DK, 2, dtype=f32) / DF_DK))
    ang = jnp.arange(seq, dtype=f32)[:, None] * inv[None, :]
    cos, sin = jnp.cos(ang), jnp.sin(ang)
    cos_t = jnp.tile(jnp.concatenate([cos, cos], axis=1), (1, 256 // DF_DK))
    sin_t = jnp.tile(jnp.concatenate([-sin, sin], axis=1), (1, 256 // DF_DK))
    scale = (MLA_NOPE + MLA_ROPE) ** -0.5 * LOG2E
    ones = jnp.ones((seq, MLA_NOPE), f32)
    zeros = jnp.zeros((seq, MLA_NOPE), f32)
    cq = _pad_cols(jnp.concatenate([ones, cos, cos], axis=1) * scale, LANE)
    sq = _pad_cols(jnp.concatenate([zeros, sin, sin], axis=1) * scale, LANE)
    return cos_t, sin_t, cq, sq


def kernel(x, c, w_ada, b_ada, w_in, mla_g_q, mla_g_kv, mla_w_uq, mla_w_ukv, ml_b_i, ml_b_f, sc_w,
           df_lq1, df_lk1, df_lq2, df_lk2, df_g, w_out, ln1_g, ln1_b, w_router, b_router, w1, w3, w2,
           ln2_g, ln2_b):
    batch, seq, d = x.shape
    depth = w_in.shape[0]
    assert MLA_ROPE == DF_DK, "both rotary blocks share one table"
    cos_t, sin_t, cq, sq = _rope_tables(seq)
    place = jnp.zeros((LANE, LANE), f32).at[jnp.arange(MLA_ROPE), MLA_NOPE + jnp.arange(MLA_ROPE)].set(1.0).astype(bf16)
    wr_hi = w_router.astype(bf16)
    wr_split = jnp.concatenate([wr_hi, (w_router - wr_hi.astype(f32)).astype(bf16)], axis=1).T
    mod_all = _ada_mod(c, w_ada, b_ada).reshape(depth, batch, 6, d)
    xf = x.reshape(batch * seq, d)
    for l in range(depth):
        mod = mod_all[l]
        w_packed = _pack_w_in(w_in[l])
        gq = _pad_cols(mla_g_q[l][None, :], 256)
        gkv = mla_g_kv[l][None, :]
        gbias = _pad_cols(jnp.concatenate([ml_b_i[l], ml_b_f[l]])[None, :], LANE)
        wkt = w_packed[:, OFF_MK:OFF_MK + GROUP_W].T
        (qn, kvn, kpe, mq, mk, mv, mo, gcol, yc, dq, dk, dv, mkt) = _inproj(
            xf, mod, w_packed, wkt, cos_t, sin_t, gq, gkv, gbias, sc_w[l], seq)
        wq, wqr, wk, wv = _pack_mla(mla_w_uq[l], mla_w_ukv[l])
        ya = _mla_attention(qn, kvn, kpe, wq, wqr, wk, wv, place, cq, sq, batch, seq)
        yb = _mlstm(mq, mk, mkt, mv, mo, gcol, batch, seq)
        lambda_init = 0.8 - 0.6 * math.exp(-0.3 * l)
        yd = _diff_attention(dq, dk, dv, df_lq1[l][None, :], df_lk1[l][None, :], df_lq2[l][None, :],
                             df_lk2[l][None, :], df_g[l][None, :],
                             batch, seq, lambda_init)
        x1, u2e, grp = _outproj_route(ya, yb, yc, yd, xf, mod, w_out[l].astype(bf16), ln1_g[l][None, :],
                                      ln1_b[l][None, :], wr_split, b_router[:, None], seq)
        pos, src, tile_grp, n_tiles = _dispatch_plan(grp[0], MOE_ROW_TILE)
        ys = _experts(src, tile_grp, n_tiles, u2e, w1[l].astype(bf16), w3[l].astype(bf16), w2[l].astype(bf16))
        xf = _moe_combine(pos, ys, x1, mod, ln2_g[l][None, :], ln2_b[l][None, :], seq)
    return xf.reshape(batch, seq, d)
```

```python
import functools
import math

import jax
import jax.numpy as jnp
from jax import lax
from jax.experimental import pallas as pl
from jax.experimental.pallas import tpu as pltpu

f32 = jnp.float32
bf16 = jnp.bfloat16
HIGHEST = lax.Precision.HIGHEST

GROUP_W = 256
MLA_HEADS, MLA_NOPE, MLA_ROPE, MLA_V = 4, 64, 32, 64
MLA_Q_LORA, MLA_KV_LORA = 192, 128
ML_HEADS, ML_DH, ML_CHUNK = 4, 64, 64
DF_HEADS, DF_DK, DF_V = 4, 32, 64
ROPE_THETA = 10000.0
N_EXPERTS, N_GROUPS, E_PER_GROUP, D_EXPERT = 16, 4, 4, 256
DEPTH = 2
DN_ALPHA = (2 * DEPTH) ** 0.25
LN_EPS = 1e-5
RMS_EPS = 1e-6
IN_SIZES = (192, 128, 32, 256, 256, 256, 4, 4, 256, 256, 256, 256, 256, 256, 256)

OFF_PQ, OFF_PKV, OFF_KR = 0, 256, 384
OFF_MQ, OFF_MK, OFF_MV, OFF_MG, OFF_MO = 512, 768, 1024, 1280, 1408
OFF_CB, OFF_CC, OFF_CH = 1664, 1920, 2176
OFF_DQ, OFF_DK, OFF_DV = 2432, 2688, 2944
IN_PACKED = 3200

LANE = 128
VMEM_LIMIT = 48 * 1024 * 1024
NEG = -1e30
LOG2E = math.log2(math.e)


def _cparams(sem):
    return pltpu.CompilerParams(dimension_semantics=sem, vmem_limit_bytes=VMEM_LIMIT)


def _sigmoid(x):
    return 1.0 / (1.0 + jnp.exp(-x))


def _layer_norm(x):
    mu = jnp.mean(x, axis=-1, keepdims=True)
    xc = x - mu
    var = jnp.mean(xc * xc, axis=-1, keepdims=True)
    return xc * lax.rsqrt(var + LN_EPS)


def _dot(a, b, **kw):
    return jnp.dot(a, b, preferred_element_type=f32, **kw)


def _dot_nt(a, b, **kw):
    return lax.dot_general(a, b, (((1,), (1,)), ((), ())), preferred_element_type=f32, **kw)


def _dot_tn(a, b, **kw):
    return lax.dot_general(a, b, (((0,), (0,)), ((), ())), preferred_element_type=f32, **kw)


def _expand_groups(cols, rows, width, group):
    lane = lax.broadcasted_iota(jnp.int32, (rows, width), 1)
    out = jnp.broadcast_to(cols[-1], (rows, width))
    for h in range(len(cols) - 2, -1, -1):
        out = jnp.where(lane < group * (h + 1), cols[h], out)
    return out


def _lane_group_mask(shape, lo, hi):
    lane = lax.broadcasted_iota(jnp.int32, shape, 1)
    return (lane >= lo) & (lane < hi)


def _ada_body(c_ref, w_ref, b_ref, o_ref):
    c = c_ref[...]
    ca = (c * _sigmoid(c)).astype(bf16)
    o_ref[0] = _dot(ca, w_ref[0].astype(bf16)) + b_ref[0]


def _ada_mod(c, w_ada, b_ada):
    depth, d, n = w_ada.shape
    b = c.shape[0]
    tn = 1536
    return pl.pallas_call(
        _ada_body,
        grid=(depth, n // tn),
        in_specs=[
            pl.BlockSpec((b, d), lambda l, j: (0, 0)),
            pl.BlockSpec((1, d, tn), lambda l, j: (l, 0, j)),
            pl.BlockSpec((1, 1, tn), lambda l, j: (l, 0, j)),
        ],
        out_specs=pl.BlockSpec((1, b, tn), lambda l, j: (l, 0, j)),
        out_shape=jax.ShapeDtypeStruct((depth, b, n), f32),
        compiler_params=_cparams(("arbitrary", "arbitrary")),
        name="ada_mod",
    )(c, w_ada, b_ada.reshape(depth, 1, n))


def _rope_lanes(x, cos, sin_signed):
    w = x.shape[1]
    lane = lax.broadcasted_iota(jnp.int32, x.shape, 1)
    rot = jnp.where(lane % 32 < 16, pltpu.roll(x, w - 16, axis=1), pltpu.roll(x, 16, axis=1))
    return x * cos + rot * sin_signed


def _inproj_body(x_ref, mod_ref, w_ref, wkt_ref, cos_ref, sin_ref, gq_ref, gkv_ref, gb_ref, cw_ref,
                 qn_ref, kvn_ref, kpe_ref, mq_ref, mk_ref, mv_ref, mo_ref, gcol_ref, yc_ref,
                 dq_ref, dk_ref, dv_ref, mkt_ref, carry_ref, *, tiles_per_seq):
    i = pl.program_id(0)
    tm = x_ref.shape[0]
    u = _layer_norm(x_ref[...]) * (1.0 + mod_ref[0, 1:2, :]) + mod_ref[0, 0:1, :]
    ub = u.astype(bf16)

    def seg(off, n):
        return _dot(ub, w_ref[:, off:off + n])

    cos = cos_ref[...]
    sin = sin_ref[...]

    pq = seg(OFF_PQ, 256)
    ms = jnp.sum(pq * pq, axis=-1, keepdims=True) * (1.0 / MLA_Q_LORA)
    qn_ref[...] = (pq * lax.rsqrt(ms + RMS_EPS) * gq_ref[...]).astype(bf16)
    pkv = seg(OFF_PKV, 128)
    ms = jnp.sum(pkv * pkv, axis=-1, keepdims=True) * (1.0 / MLA_KV_LORA)
    kvn_ref[...] = (pkv * lax.rsqrt(ms + RMS_EPS) * gkv_ref[...]).astype(bf16)
    kpe_ref[...] = _rope_lanes(seg(OFF_KR, 128), cos[:, :LANE], sin[:, :LANE]).astype(bf16)

    mq_ref[...] = (seg(OFF_MQ, 256) * (ML_DH ** -0.5)).astype(bf16)
    mk_ref[...] = seg(OFF_MK, 256).astype(bf16)
    mkt_ref[...] = _dot_nt(wkt_ref[...], ub).astype(bf16)
    mv_ref[...] = seg(OFF_MV, 256).astype(bf16)
    gcol_ref[...] = seg(OFF_MG, 128) + gb_ref[...]
    mo_ref[...] = seg(OFF_MO, 256).astype(bf16)

    uc = seg(OFF_CC, 256) * seg(OFF_CH, 256)

    @pl.when(i % tiles_per_seq == 0)
    def _():
        carry_ref[...] = jnp.zeros_like(carry_ref)

    prev = carry_ref[...]
    row =lax.broadcasted_iota(jnp.int32, uc.shape, 0)
    u1 = jnp.where(row == 0, prev[7:8, :], pltpu.roll(uc, 1, axis=0))
    u2 = jnp.where(row == 0, prev[6:7, :], jnp.where(row == 1, prev[7:8, :], pltpu.roll(uc, 2, axis=0)))
    carry_ref[...] = uc[tm - 8:, :]
    conv = cw_ref[0:1, :] * u2 + cw_ref[1:2, :] * u1 + cw_ref[2:3, :] * uc
    yc_ref[...] = (seg(OFF_CB, 256) * conv).astype(bf16)

    dq_ref[...] = (_rope_lanes(seg(OFF_DQ, 256), cos, sin) * (DF_DK ** -0.5 * LOG2E)).astype(bf16)
    dk_ref[...] = _rope_lanes(seg(OFF_DK, 256), cos, sin).astype(bf16)
    dv_ref[...] = seg(OFF_DV, 256).astype(bf16)


def _inproj(x2d, mod, w_packed, wkt, cos_t, sin_t, gq, gkv, gbias, conv_w, seq):
    t, d = x2d.shape
    tm = 512
    tps = seq // tm
    row = lambda w: pl.BlockSpec((tm, w), lambda i: (i, 0))
    const = lambda a: pl.BlockSpec(a.shape, lambda i: (0,) * a.ndim)
    widths = (256, 128, 128, 256, 256, 256, 256, 128, 256, 256, 256, 256)
    dtypes = (bf16, bf16, bf16, bf16, bf16, bf16, bf16, f32, bf16, bf16, bf16, bf16)
    return pl.pallas_call(
        functools.partial(_inproj_body, tiles_per_seq=tps),
        grid=(t // tm,),
        in_specs=[
            row(d),
            pl.BlockSpec((1, 6, d), lambda i: (i // tps, 0, 0)),
            const(w_packed), const(wkt),
            pl.BlockSpec((tm, 256), lambda i: (i % tps, 0)),
            pl.BlockSpec((tm, 256), lambda i: (i % tps, 0)),
            const(gq), const(gkv), const(gbias), const(conv_w),
        ],
        out_specs=[row(w) for w in widths] + [pl.BlockSpec((GROUP_W, tm), lambda i: (0, i))],
        out_shape=[jax.ShapeDtypeStruct((t, w), dt) for w, dt in zip(widths, dtypes)]
        + [jax.ShapeDtypeStruct((GROUP_W, t), bf16)],
        scratch_shapes=[pltpu.VMEM((8, 256), f32)],
        compiler_params=_cparams(("arbitrary",)),
        name="inproj",
    )(x2d, mod, w_packed, wkt, cos_t, sin_t, gq, gkv, gbias, conv_w)


def _causal_softmax_heads(q_heads, k_at, v_at, dv, qi, s_scr, m_scr, acc_scr):
    n_heads = len(q_heads)
    tq = q_heads[0].shape[0]
    half = tq // 2

    def scores(g, j):
        return _dot_nt(q_heads[g], k_at(g, pl.multiple_of(j * tq, tq)))

    def fold_max(g, s):
        m_scr[g] = jnp.maximum(m_scr[g], jnp.maximum(s[:, :half], s[:, half:]))

    m_scr[...] = jnp.full(m_scr.shape, NEG, f32)
    acc_scr[...] = jnp.zeros_like(acc_scr)

    def score_blocks(js):
        for g in range(n_heads):
            mx = None
            for j in js:
                s = scores(g, j)
                s_scr[g, j] = s
                s = jnp.maximum(s[:, :half], s[:, half:])
                mx = s if mx is None else jnp.maximum(mx, s)
            m_scr[g] = jnp.maximum(m_scr[g], mx)

    def prob_blocks(js):
        for g in range(n_heads):
            mb = m_scr[g]
            pv = None
            for j in js:
                sj = s_scr[g, j]
                p = jnp.concatenate([jnp.exp2(sj[:, :half] - mb), jnp.exp2(sj[:, half:] - mb)],
                                    axis=1).astype(bf16)
                d = _dot(p, v_at(g, pl.multiple_of(j * tq, tq)))
                pv = d if pv is None else pv + d
            acc_scr[g] += pv

    def over_full_blocks(blocks_fn):
        @pl.loop(0, qi // 2)
        def _(jj):
            blocks_fn((2 * jj, 2 * jj + 1))

        @pl.when(qi % 2 == 1)
        def _():
            blocks_fn((qi - 1,))

    over_full_blocks(score_blocks)
    rowi = lax.broadcasted_iota(jnp.int32, (tq, tq), 0)
    coli = lax.broadcasted_iota(jnp.int32, (tq, tq), 1)
    for g in range(n_heads):
        s = jnp.where(coli <= rowi, scores(g, qi), NEG)
        s_scr[g, qi] = s
        fold_max(g, s)
        m_scr[g] = jnp.broadcast_to(jnp.max(m_scr[g], axis=1, keepdims=True), (tq, half))

    over_full_blocks(prob_blocks)
    prob_blocks((qi,))
    outs = []
    for g in range(n_heads):
        acc = acc_scr[g]
        outs.append(acc[:, :dv] / acc[:, dv:dv + 1])
    return outs


def _softmax_scratch(n_heads, nq, tq):
    return [pltpu.VMEM((n_heads, nq, tq, tq), f32), pltpu.VMEM((n_heads, tq, tq // 2), f32),
            pltpu.VMEM((n_heads, tq, LANE), f32)]


def _with_ones_lane(v, dv):
    rows = v.shape[0]
    lane = lax.broadcasted_iota(jnp.int32, (rows, LANE - dv), 1)
    return jnp.concatenate([v.astype(bf16), (lane == 0).astype(bf16)], axis=1)


def _mla_body(qn_ref, kvn_ref, kpe_ref, wq_ref, wqr_ref, wk_ref, wv_ref, place_ref, cq_ref, sq_ref,
              o_ref, k_scr, v_scr, s_scr, m_scr, acc_scr):
    qi = pl.program_id(1)
    tq = qn_ref.shape[0]

    @pl.when(qi == 0)
    def _():
        kvn = kvn_ref[...]
        kpe_placed = _dot(kpe_ref[...], place_ref[...])
        for h in range(MLA_HEADS):
            k_scr[h] = (_dot(kvn, wk_ref[h]) + kpe_placed).astype(bf16)
            v_scr[h] = _with_ones_lane(_dot(kvn, wv_ref[h]), MLA_V)

    qn = qn_ref[...]
    cq = cq_ref[...]
    sq = sq_ref[...]
    q_heads = [(_dot(qn, wq_ref[h]) * cq + _dot(qn, wqr_ref[h]) * sq).astype(bf16)
               for h in range(MLA_HEADS)]
    outs = _causal_softmax_heads(
        q_heads, lambda g, start: k_scr[g, pl.ds(start, tq), :],
        lambda g, start: v_scr[g, pl.ds(start, tq), :], MLA_V, qi, s_scr, m_scr, acc_scr)
    o_ref[...] = jnp.concatenate(outs, axis=1).astype(bf16)


def _mla_attention(qn, kvn, kpe, wq, wqr, wk, wv, place, cq, sq, batch, seq):
    tq = 256
    nq = seq // tq
    const = lambda a: pl.BlockSpec(a.shape, lambda b, i: (0,) * a.ndim)
    return pl.pallas_call(
        _mla_body,
        grid=(batch, nq),
        in_specs=[
            pl.BlockSpec((tq, 256), lambda b, i: (b * nq + i, 0)),
            pl.BlockSpec((seq, 128), lambda b, i: (b, 0)),
            pl.BlockSpec((seq, 128), lambda b, i: (b, 0)),
            const(wq), const(wqr), const(wk), const(wv), const(place),
            pl.BlockSpec((tq, 128), lambda b, i: (i, 0)),
            pl.BlockSpec((tq, 128), lambda b, i: (i, 0)),
        ],
        out_specs=pl.BlockSpec((tq, GROUP_W), lambda b, i: (b * nq + i, 0)),
        out_shape=jax.ShapeDtypeStruct((batch * seq, GROUP_W), bf16),
        scratch_shapes=[pltpu.VMEM((MLA_HEADS, seq, 128), bf16),
                        pltpu.VMEM((MLA_HEADS, seq, LANE), bf16)]
        + _softmax_scratch(MLA_HEADS, nq, tq),
        compiler_params=_cparams(("arbitrary", "arbitrary")),
        name="mla_attn",
    )(qn, kvn, kpe, wq, wqr, wk, wv, place, cq, sq)


def _diff_body(q_ref, k_ref, v_ref, lq1_ref, lk1_ref, lq2_ref, lk2_ref, g_ref,
               o_ref, v_scr, s_scr, m_scr, acc_scr, *, lambda_init):
    qi = pl.program_id(1)
    tq = q_ref.shape[0]

    @pl.when(qi == 0)
    def _():
        v = v_ref[...]
        for h in range(DF_HEADS):
            v_scr[h] = _with_ones_lane(v[:, DF_V * h:DF_V * (h + 1)], DF_V)

    q = q_ref[...]
    lam = (jnp.exp(jnp.sum(lq1_ref[...] * lk1_ref[...], axis=1, keepdims=True))
           - jnp.exp(jnp.sum(lq2_ref[...] * lk2_ref[...], axis=1, keepdims=True)) + lambda_init)
    q_heads = [jnp.where(_lane_group_mask(q.shape, DF_DK * g, DF_DK * (g + 1)), q, jnp.zeros_like(q))
               for g in range(2 * DF_HEADS)]
    maps = _causal_softmax_heads(
        q_heads, lambda g, start: k_ref[pl.ds(start, tq), :],
        lambda g, start: v_scr[g // 2, pl.ds(start, tq), :], DF_V, qi, s_scr, m_scr, acc_scr)
    outs = []
    for h in range(DF_HEADS):
        o = maps[2 * h] - lam * maps[2 * h + 1]
        ms = jnp.mean(o * o, axis=1, keepdims=True)
        outs.append(o * lax.rsqrt(ms + RMS_EPS) * g_ref[...] * (1.0 - lambda_init))
    o_ref[...] = jnp.concatenate(outs, axis=1).astype(bf16)


def _diff_attention(dq, dk, dv, lq1, lk1, lq2, lk2, g_sub, batch, seq, lambda_init):
    tq = 256
    nq = seq // tq
    const = lambda a: pl.BlockSpec(a.shape, lambda b, i: (0,) * a.ndim)
    return pl.pallas_call(
        functools.partial(_diff_body, lambda_init=lambda_init),
        grid=(batch, nq),
        in_specs=[
            pl.BlockSpec((tq, 256), lambda b, i: (b * nq + i, 0)),
            pl.BlockSpec((seq, 256), lambda b, i: (b, 0)),
            pl.BlockSpec((seq, 256), lambda b, i: (b, 0)),
            const(lq1), const(lk1), const(lq2), const(lk2), const(g_sub),
        ],
        out_specs=pl.BlockSpec((tq, GROUP_W), lambda b, i: (b * nq + i, 0)),
        out_shape=jax.ShapeDtypeStruct((batch * seq, GROUP_W), bf16),
        scratch_shapes=[pltpu.VMEM((DF_HEADS, seq, LANE), bf16)]
        + _softmax_scratch(2 * DF_HEADS, nq, tq),
        compiler_params=_cparams(("arbitrary", "arbitrary")),
        name="diff_attn",
    )(dq, dk, dv, lq1, lk1, lq2, lk2, g_sub)


def _split3(a):
    hi = a.astype(bf16)
    r1 = a - hi.astype(f32)
    mid = r1.astype(bf16)
    return hi, mid, (r1 - mid.astype(f32)).astype(bf16)


def _dot_exact_rhs01(a, b01):
    hi, mid, lo = _split3(a)
    return _dot(hi, b01) + _dot(mid, b01) + _dot(lo, b01)


def _dot_exact_lhs01(a01, b):
    hi, mid, lo = _split3(b)
    return _dot(a01, hi) + _dot(a01, mid) + _dot(a01, lo)


def _mlstm_chunk(q, k, kt_bd, v, o, g, eif, c_prev, n_prev, m_prev):
    L, W = q.shape
    row = lax.broadcasted_iota(jnp.int32, (L, W), 0)
    pos = lax.broadcasted_iota(jnp.int32, (L, W), 1) % L
    lf = jnp.minimum(g, 0.0) - jnp.log(1.0 + jnp.exp(-jnp.abs(g)))
    glane = lax.broadcasted_iota(jnp.int32, g.shape, 1)
    wide = _dot_exact_rhs01(jnp.where(glane < ML_HEADS, g, lf), eif)
    iw, lfw = wide[:, :W], wide[:, W:]
    tril = (lax.broadcasted_iota(jnp.int32, (L, L), 0) >= lax.broadcasted_iota(jnp.int32, (L, L), 1))
    bw = _dot_exact_lhs01(tril.astype(bf16), lfw)
    rw = iw - bw
    cm = rw
    for sh in (1, 2, 4, 8, 16, 32):
        cm = jnp.maximum(cm, jnp.where(row >= sh, pltpu.roll(cm, sh, axis=0), NEG))
    mw = jnp.maximum(m_prev, cm)
    a_inter = jnp.exp(m_prev - mw)
    r_row = jnp.sum(jnp.where(pos == row, rw, 0.0), axis=0, keepdims=True)
    ew = jnp.exp(jnp.where(pos <= row, r_row - mw, NEG))
    pw = _dot(q, kt_bd.astype(bf16)) * ew
    head_eq = (lax.broadcasted_iota(jnp.int32, (W, W), 0) // ML_DH
               == lax.broadcasted_iota(jnp.int32, (W, W), 1) // ML_DH)
    v4 = jnp.concatenate([v] * ML_HEADS, axis=0)
    vbd = jnp.where(head_eq, v4, jnp.zeros_like(v4))
    obd = head_eq.astype(bf16)
    den = a_inter * _dot_exact_rhs01(q.astype(f32) * n_prev, obd) + _dot_exact_rhs01(pw, obd)
    hden = jnp.maximum(jnp.abs(den), jnp.exp(-(bw + mw)))
    num = a_inter * _dot(q, c_prev.astype(bf16)) + _dot(pw.astype(bf16), vbd)
    y = num / hden * _sigmoid(o.astype(f32))

    rmax, m_last = cm[L - 1:L, :], mw[L - 1:L, :]
    a_w = jnp.exp(m_prev - m_last)
    b_w = jnp.exp(rmax - m_last)
    c_loc = _dot((kt_bd * jnp.exp(r_row - rmax)).astype(bf16), vbd)
    n_loc = jnp.sum(k.astype(f32) * jnp.exp(rw - rmax), axis=0, keepdims=True)
    return y, c_prev * a_w + c_loc * b_w, a_w * n_prev + b_w * n_loc, bw[L - 1:L, :] + m_last


def _mlstm_body(q_ref, k_ref, kt_ref, v_ref, o_ref, g_ref, eif_ref, sel_ref, y_ref, c_scr, nm_scr):
    @pl.when(pl.program_id(1) == 0)
    def _():
        c_scr[...] = jnp.zeros_like(c_scr)
        nm_scr[...] = jnp.zeros_like(nm_scr)

    c, n, m = c_scr[...], nm_scr[0:1, :], nm_scr[1:2, :]
    kt = kt_ref[...]
    eif = eif_ref[...]
    w = kt.shape[0]
    head_eq = (lax.broadcasted_iota(jnp.int32, (w, w), 0) // ML_DH
               == lax.broadcasted_iota(jnp.int32, (w, w), 1) // ML_DH)
    for i in range(q_ref.shape[0] // ML_CHUNK):
        rows = pl.ds(i * ML_CHUNK, ML_CHUNK)
        kt_bd = jnp.where(head_eq, _dot(kt, sel_ref[i]), 0.0)
        y, c, n, m = _mlstm_chunk(q_ref[rows, :], k_ref[rows, :], kt_bd, v_ref[rows, :], o_ref[rows, :],
                                  g_ref[rows, :], eif, c, n, m)
        y_ref[rows, :] = y.astype(y_ref.dtype)
    c_scr[...] = c
    nm_scr[0:1, :] = n
    nm_scr[1:2, :] = m


MLSTM_CHUNKS_PER_STEP = 4


def _mlstm(mq, mk, mkt, mv, mo, gcol, batch, seq):
    cps = MLSTM_CHUNKS_PER_STEP
    L = ML_CHUNK * cps
    nc = seq // L
    blk = lambda w: pl.BlockSpec((L, w), lambda b, c: (b * nc + c, 0))
    const = lambda a: pl.BlockSpec(a.shape, lambda b, c: (0,) * a.ndim)
    gl = jnp.arange(LANE)[:, None]
    col = jnp.arange(2 * GROUP_W)[None, :]
    eif = (gl == (col // GROUP_W) * ML_HEADS + (col % GROUP_W) // ML_DH).astype(bf16)
    p = jnp.arange(L)[None, :, None]
    cc = jnp.arange(GROUP_W)[None, None, :]
    sel = (p == jnp.arange(cps)[:, None, None] * ML_CHUNK + cc % ML_CHUNK).astype(bf16)
    return pl.pallas_call(
        _mlstm_body,
        grid=(batch, nc),
        in_specs=[blk(256), blk(256), pl.BlockSpec((GROUP_W, L), lambda b, c: (0, b * nc + c)),
                  blk(256), blk(256), blk(128), const(eif), const(sel)],
        out_specs=blk(GROUP_W),
        out_shape=jax.ShapeDtypeStruct((batch * seq, GROUP_W), bf16),
        scratch_shapes=[pltpu.VMEM((GROUP_W, GROUP_W), f32), pltpu.VMEM((8, GROUP_W), f32)],
        compiler_params=_cparams(("arbitrary", "arbitrary")),
        name="mlstm",
    )(mq, mk, mkt, mv, mo, gcol, eif, sel)


def _route(scores, sel):
    ne, tm = sel.shape
    eid = lax.broadcasted_iota(jnp.int32, (ne, tm), 0)
    grp = eid // E_PER_GROUP
    big = ne + 1

    def top2(vals):
        m1 = jnp.max(vals, axis=0, keepdims=True)
        i1 = jnp.min(jnp.where(vals == m1, eid, big), axis=0, keepdims=True)
        rest = jnp.where(eid == i1, NEG, vals)
        m2 = jnp.max(rest, axis=0, keepdims=True)
        i2 = jnp.min(jnp.where(rest == m2, eid, big), axis=0, keepdims=True)
        return m1, i1, m2, i2

    best_score, best_grp = None, None
    for gi in range(N_GROUPS):
        m1, _, m2, _ = top2(jnp.where(grp == gi, sel, NEG))
        sc = m1 + m2
        if best_score is None:
            best_score, best_grp = sc, jnp.zeros_like(sc, dtype=jnp.int32)
        else:
            better = sc > best_score
            best_grp = jnp.where(better, gi, best_grp)
            best_score = jnp.where(better, sc, best_score)
    _, i1, _, i2 = top2(jnp.where(grp == best_grp, sel, NEG))
    picked = jnp.where((eid == i1) | (eid == i2), scores, 0.0)
    return picked / jnp.sum(picked, axis=0, keepdims=True), best_grp


MOE_ROW_TILE = 256


def _outproj_route_body(ya_ref, yb_ref, yc_ref, yd_ref, x_ref, mod_ref, w_ref, g_ref, b_ref,
                        wr_ref, br_ref, x1_ref, u2e_ref, grp_ref):
    d = x_ref.shape[1]
    mix = None
    for j, y_ref in enumerate((ya_ref, yb_ref, yc_ref, yd_ref)):
        part = _dot(y_ref[...], w_ref[GROUP_W * j:GROUP_W * (j + 1), :])
        mix = part if mix is None else mix + part
    x1 = _layer_norm(DN_ALPHA * x_ref[...] + mod_ref[0, 2:3, :] * mix) * g_ref[...] + b_ref[...]
    x1_ref[...] = x1
    u = _layer_norm(x1) * (1.0 + mod_ref[0, 4:5, :]) + mod_ref[0, 3:4, :]
    u_hi = u.astype(bf16)
    u_lo = (u - u_hi.astype(f32)).astype(bf16)
    ne = br_ref.shape[0]
    part = _dot_nt(wr_ref[...], u_hi)
    logits = part[:ne] + part[ne:] + _dot_nt(wr_ref[:ne, :], u_lo)
    scores = _sigmoid(logits)
    gates_t, best_grp = _route(scores, scores + br_ref[...])
    u2e_ref[:, :d] = u_hi.astype(f32)
    u2e_ref[:, d:] = jnp.zeros((x1.shape[0], LANE), f32)
    u2e_ref[:, d:d + ne] = gates_t.T
    grp_ref[...] = best_grp


def _outproj_route(ya, yb, yc, yd, x2d, mod, w_out, ln_g, ln_b, wr_split, b_router, seq):
    t, d = x2d.shape
    tm = 512
    tps = seq // tm
    row = lambda w: pl.BlockSpec((tm, w), lambda i: (i, 0))
    const = lambda a: pl.BlockSpec(a.shape, lambda i: (0,) * a.ndim)
    return pl.pallas_call(
        _outproj_route_body,
        grid=(t // tm,),
        in_specs=[row(256), row(256), row(256), row(256), row(d),
                  pl.BlockSpec((1, 6, d), lambda i: (i // tps, 0, 0)),
                  const(w_out), const(ln_g), const(ln_b), const(wr_split), const(b_router)],
        out_specs=[row(d), row(d + LANE), pl.BlockSpec((1, tm), lambda i: (0, i))],
        out_shape=[jax.ShapeDtypeStruct((t, d), f32), jax.ShapeDtypeStruct((t, d + LANE), f32),
                   jax.ShapeDtypeStruct((1, t), jnp.int32)],
        compiler_params=_cparams(("arbitrary",)),
        name="outproj_route",
    )(ya, yb, yc, yd, x2d, mod, w_out, ln_g, ln_b, wr_split, b_router)


def _dispatch_plan(grp, tr):
    t = grp.shape[0]
    rows = t + N_GROUPS * tr
    order = jnp.sort(grp * t + jnp.arange(t, dtype=jnp.int32)) % t
    order = jnp.concatenate([order, jnp.zeros((rows - t,), order.dtype)])
    counts = jnp.sum((grp[:, None] == jnp.arange(N_GROUPS)[None, :]).astype(jnp.int32), axis=0)
    padded = ((counts + tr - 1) // tr) * tr
    row_end = jnp.cumsum(padded)
    r = jnp.arange(rows, dtype=jnp.int32)
    read_tok = jnp.zeros((rows,), jnp.int32)
    write_row = jnp.zeros((rows,), jnp.int32)
    for g in range(N_GROUPS):
        row_start = row_end[g] - padded[g]
        tok_start = jnp.sum(counts[:g])
        shifted = jnp.roll(order, row_start - tok_start)
        in_grp = (r >= row_start) & (r < row_end[g])
        valid = in_grp & (r < row_start + counts[g])
        read_tok = jnp.where(valid, shifted, jnp.where(in_grp, order[jnp.minimum(tok_start, t - 1)], read_tok))
        write_row = jnp.where(valid, shifted, jnp.where(in_grp, t + g * tr + r % tr, write_row))
    tile_start = jnp.arange(rows // tr, dtype=jnp.int32) * tr
    tile_grp = jnp.minimum(jnp.sum((tile_start[:, None] >= row_end[None, :]).astype(jnp.int32), axis=1),
                           N_GROUPS - 1)
    i32 = lambda a: a.astype(jnp.int32)
    return i32(read_tok), i32(write_row), i32(tile_grp), i32(row_end[-1:] // tr)


def _wait_row_gather(n_rows, hbm, vmem, sem):
    pltpu.make_async_copy(hbm.at[pl.ds(0, n_rows), :], vmem, sem).wait()


def _wait_row_scatter(n_rows, vmem, hbm, sem):
    pltpu.make_async_copy(vmem, hbm.at[pl.ds(0, n_rows), :], sem).wait()


def _experts_body(read_ref, write_ref, tgrp_ref, ntile_ref,
                  u2e_hbm, w1_ref, w3_ref, w2_ref, ys_hbm, xbuf0, xbuf1, obuf0, obuf1, gsem, ssem):
    j = pl.program_id(0)
    n_tiles = ntile_ref[0]
    tr, d = obuf0.shape
    xbuf, obuf = (xbuf0, xbuf1), (obuf0, obuf1)

    def start_gather(tile, s):
        for i in range(tr):
            pltpu.make_async_copy(u2e_hbm.at[pl.ds(read_ref[tile * tr + i], 1), :],
                                  xbuf[s].at[pl.ds(i, 1), :], gsem.at[s]).start()

    def start_scatter(tile, s):
        for i in range(tr):
            pltpu.make_async_copy(obuf[s].at[pl.ds(i, 1), :],
                                  ys_hbm.at[pl.ds(write_ref[tile * tr + i], 1), :], ssem.at[s]).start()

    def load_tile(s):
        xs = xbuf[s][...]
        return xs[:, :d].astype(bf16), xs[:, d:]

    def compute(tile, s, xb, gslab):
        lane = lax.broadcasted_iota(jnp.int32, gslab.shape, 1)
        first_expert = tgrp_ref[tile] * E_PER_GROUP
        acc = None
        for e in range(E_PER_GROUP):
            ge = jnp.sum(jnp.where(lane == first_expert + e, gslab, 0.0), axis=1, keepdims=True)
            h1 = _dot(xb, w1_ref[0, e])
            h = h1 * _sigmoid(h1) * _dot(xb, w3_ref[0, e]) * ge
            part = _dot(h.astype(bf16), w2_ref[0, e])
            acc = part if acc is None else acc + part
        obuf[s][...] = acc

    nxt = jnp.minimum(j + 1, n_tiles - 1)

    @pl.when(j == 0)
    def _():
        start_gather(0, 0)
        start_gather(nxt, 1)
        obuf[1][...] = jnp.zeros_like(obuf[1])
        n_tok = ys_hbm.shape[0] - N_GROUPS * tr
        spare = [pltpu.make_async_copy(obuf[1], ys_hbm.at[pl.ds(n_tok + g * tr, tr), :], ssem.at[1])
                 for g in range(N_GROUPS)]
        for cp in spare:
            cp.start()
        for cp in spare:
            cp.wait()
        _wait_row_gather(tr, u2e_hbm, xbuf[0], gsem.at[0])
        compute(0, 0, *load_tile(0))

    for s in (0, 1):
        @pl.when((j > 0) & (j < n_tiles) & (j % 2 == s))
        def _(s=s):
            _wait_row_gather(tr, u2e_hbm, xbuf[s], gsem.at[s])
            tile = load_tile(s)
            start_gather(nxt, 1 - s)
            start_scatter(j - 1, 1 - s)
            compute(j, s, *tile)
            _wait_row_scatter(tr, obuf[1 - s], ys_hbm, ssem.at[1 - s])

        @pl.when((j == n_tiles) & (j % 2 == s))
        def _(s=s):
            _wait_row_gather(tr, u2e_hbm, xbuf[s], gsem.at[s])
            start_scatter(j - 1, 1 - s)
            _wait_row_scatter(tr, obuf[1 - s], ys_hbm, ssem.at[1 - s])


def _experts(read_tok, write_row, tile_grp, n_tiles, u2e, w1, w3, w2):
    rows = read_tok.shape[0]
    tr = MOE_ROW_TILE
    max_tiles = tile_grp.shape[0]
    ne, d, fe = w1.shape
    grouped = lambda w: w.reshape(N_GROUPS, E_PER_GROUP, *w.shape[1:])
    wspec = lambda a, b: pl.BlockSpec(
        (1, E_PER_GROUP, a, b), lambda j, rd, wr, tg, nt: (tg[jnp.minimum(j, max_tiles - 1)], 0, 0, 0))
    return pl.pallas_call(
        _experts_body,
        grid_spec=pltpu.PrefetchScalarGridSpec(
            num_scalar_prefetch=4, grid=(max_tiles + 1,),
            in_specs=[pl.BlockSpec(memory_space=pl.ANY), wspec(d, fe), wspec(d, fe), wspec(fe, d)],
            out_specs=pl.BlockSpec(memory_space=pl.ANY),
            scratch_shapes=[pltpu.VMEM((tr, u2e.shape[1]), f32), pltpu.VMEM((tr, u2e.shape[1]), f32),
                            pltpu.VMEM((tr, d), f32), pltpu.VMEM((tr, d), f32),
                            pltpu.SemaphoreType.DMA((2,)), pltpu.SemaphoreType.DMA((2,))]),
        out_shape=jax.ShapeDtypeStruct((rows, d), f32),
        compiler_params=_cparams(("arbitrary",)),
        name="experts",
    )(read_tok, write_row, tile_grp, n_tiles, u2e, grouped(w1), grouped(w3), grouped(w2))


def _moe_combine_body(y_ref, x_ref, mod_ref, g_ref, b_ref, o_ref):
    r = DN_ALPHA * x_ref[...] + mod_ref[0, 5:6, :] * y_ref[...]
    o_ref[...] = _layer_norm(r) * g_ref[...] + b_ref[...]


def _moe_combine(ys, x1, mod, ln_g, ln_b, seq):
    t, d = x1.shape
    tm = 512
    tps = seq // tm
    row = pl.BlockSpec((tm, d), lambda i: (i, 0))
    const = lambda a: pl.BlockSpec(a.shape, lambda i: (0,) * a.ndim)
    return pl.pallas_call(
        _moe_combine_body,
        grid=(t // tm,),
        in_specs=[row, row, pl.BlockSpec((1, 6, d), lambda i: (i // tps, 0, 0)), const(ln_g), const(ln_b)],
        out_specs=row,
        out_shape=jax.ShapeDtypeStruct((t, d), f32),
        compiler_params=_cparams(("arbitrary",)),
        name="moe_combine",
    )(ys, x1, mod, ln_g, ln_b)


def _pad_cols(a, width):
    return jnp.pad(a, ((0, 0), (0, width - a.shape[1])))


def _pack_w_in(w):
    parts, acc = [], 0
    for sz in IN_SIZES:
        parts.append(w[:, acc:acc + sz])
        acc += sz
    (a_q, a_kv, a_kr, m_q, m_k, m_v, m_i, m_f, m_o, c_b, c_c, c_h, d_q, d_k, d_v) = parts
    gates = _pad_cols(jnp.concatenate([m_i, m_f], axis=1), LANE)
    packed = jnp.concatenate([
        _pad_cols(a_q, 256), a_kv, _pad_cols(a_kr, LANE),
        m_q, m_k, m_v, gates, m_o, c_b, c_c, c_h, d_q, d_k, d_v], axis=1)
    return packed.astype(bf16)


def _pack_mla(w_uq, w_ukv):
    half = MLA_ROPE // 2
    wq3 = w_uq.reshape(MLA_Q_LORA, MLA_HEADS, MLA_NOPE + MLA_ROPE)
    wkv3 = w_ukv.reshape(MLA_KV_LORA, MLA_HEADS, MLA_NOPE + MLA_V)
    wq, wqr, wk, wv = [], [], [], []
    for h in range(MLA_HEADS):
        nope, pe = wq3[:, h, :MLA_NOPE], wq3[:, h, MLA_NOPE:]
        rot = jnp.concatenate([-pe[:, half:], pe[:, :half]], axis=1)
        zq = jnp.zeros((MLA_Q_LORA, MLA_NOPE), f32)
        pad_rows = lambda a: jnp.pad(a, ((0, 256 - MLA_Q_LORA), (0, 0)))
        wq.append(pad_rows(_pad_cols(jnp.concatenate([nope, pe], axis=1), LANE)))
        wqr.append(pad_rows(_pad_cols(jnp.concatenate([zq, rot], axis=1), LANE)))
        wk.append(_pad_cols(wkv3[:, h, :MLA_NOPE], LANE))
        wv.append(wkv3[:, h, MLA_NOPE:])
    st = lambda xs: jnp.stack(xs).astype(bf16)
    return st(wq), st(wqr), st(wk), st(wv)


def _rope_tables(seq):
    half = DF_DK // 2
    inv = 1.0 / (ROPE_THETA ** (jnp.arange(0, DF_DK, 2, dtype=f32) / DF_DK))
    ang = jnp.arange(seq, dtype=f32)[:, None] * inv[None, :]
    cos, sin = jnp.cos(ang), jnp.sin(ang)
    cos_t = jnp.tile(jnp.concatenate([cos, cos], axis=1), (1, 256 // DF_DK))
    sin_t = jnp.tile(jnp.concatenate([-sin, sin], axis=1), (1, 256 // DF_DK))
    scale = (MLA_NOPE + MLA_ROPE) ** -0.5 * LOG2E
    ones = jnp.ones((seq, MLA_NOPE), f32)
    zeros = jnp.zeros((seq, MLA_NOPE), f32)
    cq = _pad_cols(jnp.concatenate([ones, cos, cos], axis=1) * scale, LANE)
    sq = _pad_cols(jnp.concatenate([zeros, sin, sin], axis=1) * scale, LANE)
    return cos_t, sin_t, cq, sq


def kernel(x, c, w_ada, b_ada, w_in, mla_g_q, mla_g_kv, mla_w_uq, mla_w_ukv, ml_b_i, ml_b_f, sc_w,
           df_lq1, df_lk1, df_lq2, df_lk2, df_g, w_out, ln1_g, ln1_b, w_router, b_router, w1, w3, w2,
           ln2_g, ln2_b):
    batch, seq, d = x.shape
    depth = w_in.shape[0]
    assert MLA_ROPE == DF_DK, "both rotary blocks share one table"
    cos_t, sin_t, cq, sq = _rope_tables(seq)
    place = jnp.zeros((LANE, LANE), f32).at[jnp.arange(MLA_ROPE), MLA_NOPE + jnp.arange(MLA_ROPE)].set(1.0).astype(bf16)
    wr_hi = w_router.astype(bf16)
    wr_split = jnp.concatenate([wr_hi, (w_router - wr_hi.astype(f32)).astype(bf16)], axis=1).T
    mod_all = _ada_mod(c, w_ada, b_ada).reshape(depth, batch, 6, d)
    xf = x.reshape(batch * seq, d)
    for l in range(depth):
        mod = mod_all[l]
        w_packed = _pack_w_in(w_in[l])
        gq = _pad_cols(mla_g_q[l][None, :], 256)
        gkv = mla_g_kv[l][None, :]
        gbias = _pad_cols(jnp.concatenate([ml_b_i[l], ml_b_f[l]])[None, :], LANE)
        wkt = w_packed[:, OFF_MK:OFF_MK + GROUP_W].T
        (qn, kvn, kpe, mq, mk, mv, mo, gcol, yc, dq, dk, dv, mkt) = _inproj(
            xf, mod, w_packed, wkt, cos_t, sin_t, gq, gkv, gbias, sc_w[l], seq)
        wq, wqr, wk, wv = _pack_mla(mla_w_uq[l], mla_w_ukv[l])
        ya = _mla_attention(qn, kvn, kpe, wq, wqr, wk, wv, place, cq, sq, batch, seq)
        yb = _mlstm(mq, mk, mkt, mv, mo, gcol, batch, seq)
        lambda_init = 0.8 - 0.6 * math.exp(-0.3 * l)
        yd = _diff_attention(dq, dk, dv, df_lq1[l][None, :], df_lk1[l][None, :], df_lq2[l][None, :],
                             df_lk2[l][None, :], df_g[l][None, :],
                             batch, seq, lambda_init)
        x1, u2e, grp = _outproj_route(ya, yb, yc, yd, xf, mod, w_out[l].astype(bf16), ln1_g[l][None, :],
                                      ln1_b[l][None, :], wr_split, b_router[:, None], seq)
        plan = _dispatch_plan(grp[0], MOE_ROW_TILE)
        ys = _experts(*plan, u2e, w1[l].astype(bf16), w3[l].astype(bf16), w2[l].astype(bf16))
        xf = _moe_combine(ys, x1, mod, ln2_g[l][None, :], ln2_b[l][None, :], seq)
    return xf.reshape(batch, seq, d)
```

```python
import functools
import math

import jax
import jax.numpy as jnp
from jax import lax
from jax.experimental import pallas as pl
from jax.experimental.pallas import tpu as pltpu

f32 = jnp.float32
bf16 = jnp.bfloat16
HIGHEST = lax.Precision.HIGHEST

GROUP_W = 256
MLA_HEADS, MLA_NOPE, MLA_ROPE, MLA_V = 4, 64, 32, 64
MLA_Q_LORA, MLA_KV_LORA = 192, 128
ML_HEADS, ML_DH, ML_CHUNK = 4, 64, 64
DF_HEADS, DF_DK, DF_V = 4, 32, 64
ROPE_THETA = 10000.0
N_EXPERTS, N_GROUPS, E_PER_GROUP, D_EXPERT = 16, 4, 4, 256
DEPTH = 2
DN_ALPHA = (2 * DEPTH) ** 0.25
LN_EPS = 1e-5
RMS_EPS = 1e-6
IN_SIZES = (192, 128, 32, 256, 256, 256, 4, 4, 256, 256, 256, 256, 256, 256, 256)

OFF_PQ, OFF_PKV, OFF_KR = 0, 256, 384
OFF_MQ, OFF_MK, OFF_MV, OFF_MG, OFF_MO = 512, 768, 1024, 1280, 1408
OFF_CB, OFF_CC, OFF_CH = 1664, 1920, 2176
OFF_DQ, OFF_DK, OFF_DV = 2432, 2688, 2944
IN_PACKED = 3200

LANE = 128
VMEM_LIMIT = 48 * 1024 * 1024
NEG = -1e30
LOG2E = math.log2(math.e)


def _cparams(sem):
    return pltpu.CompilerParams(dimension_semantics=sem, vmem_limit_bytes=VMEM_LIMIT)


def _sigmoid(x):
    return 1.0 / (1.0 + jnp.exp(-x))


def _layer_norm(x):
    mu = jnp.mean(x, axis=-1, keepdims=True)
    xc = x - mu
    var = jnp.mean(xc * xc, axis=-1, keepdims=True)
    return xc * lax.rsqrt(var + LN_EPS)


def _dot(a, b, **kw):
    return jnp.dot(a, b, preferred_element_type=f32, **kw)


def _dot_nt(a, b, **kw):
    return lax.dot_general(a, b, (((1,), (1,)), ((), ())), preferred_element_type=f32, **kw)


def _dot_tn(a, b, **kw):
    return lax.dot_general(a, b, (((0,), (0,)), ((), ())), preferred_element_type=f32, **kw)


def _expand_groups(cols, rows, width, group):
    lane = lax.broadcasted_iota(jnp.int32, (rows, width), 1)
    out = jnp.broadcast_to(cols[-1], (rows, width))
    for h in range(len(cols) - 2, -1, -1):
        out = jnp.where(lane < group * (h + 1), cols[h], out)
    return out


def _lane_group_mask(shape, lo, hi):
    lane = lax.broadcasted_iota(jnp.int32, shape, 1)
    return (lane >= lo) & (lane < hi)


def _ada_body(c_ref, w_ref, b_ref, o_ref):
    c = c_ref[...]
    ca = (c * _sigmoid(c)).astype(bf16)
    o_ref[0] = _dot(ca, w_ref[0].astype(bf16)) + b_ref[0]


def _ada_mod(c, w_ada, b_ada):
    depth, d, n = w_ada.shape
    b = c.shape[0]
    tn = 1536
    return pl.pallas_call(
        _ada_body,
        grid=(depth, n // tn),
        in_specs=[
            pl.BlockSpec((b, d), lambda l, j: (0, 0)),
            pl.BlockSpec((1, d, tn), lambda l, j: (l, 0, j)),
            pl.BlockSpec((1, 1, tn), lambda l, j: (l, 0, j)),
        ],
        out_specs=pl.BlockSpec((1, b, tn), lambda l, j: (l, 0, j)),
        out_shape=jax.ShapeDtypeStruct((depth, b, n), f32),
        compiler_params=_cparams(("arbitrary", "arbitrary")),
        name="ada_mod",
    )(c, w_ada, b_ada.reshape(depth, 1, n))


def _rope_lanes(x, cos, sin_signed):
    w = x.shape[1]
    lane = lax.broadcasted_iota(jnp.int32, x.shape, 1)
    rot = jnp.where(lane % 32 < 16, pltpu.roll(x, w - 16, axis=1), pltpu.roll(x, 16, axis=1))
    return x * cos + rot * sin_signed


def _inproj_body(x_ref, mod_ref, w_ref, wkt_ref, cos_ref, sin_ref, gq_ref, gkv_ref, gb_ref, cw_ref,
                 qn_ref, kvn_ref, kpe_ref, mq_ref, mk_ref, mv_ref, mo_ref, gcol_ref, yc_ref,
                 dq_ref, dk_ref, dv_ref, mkt_ref, carry_ref, *, tiles_per_seq):
    i = pl.program_id(0)
    tm = x_ref.shape[0]
    u = _layer_norm(x_ref[...]) * (1.0 + mod_ref[0, 1:2, :]) + mod_ref[0, 0:1, :]
    ub = u.astype(bf16)

    def seg(off, n):
        return _dot(ub, w_ref[:, off:off + n])

    cos = cos_ref[...]
    sin = sin_ref[...]

    pq = seg(OFF_PQ, 256)
    ms = jnp.sum(pq * pq, axis=-1, keepdims=True) * (1.0 / MLA_Q_LORA)
    qn_ref[...] = (pq * lax.rsqrt(ms + RMS_EPS) * gq_ref[...]).astype(bf16)
    pkv = seg(OFF_PKV, 128)
    ms = jnp.sum(pkv * pkv, axis=-1, keepdims=True) * (1.0 / MLA_KV_LORA)
    kvn_ref[...] = (pkv * lax.rsqrt(ms + RMS_EPS) * gkv_ref[...]).astype(bf16)
    kpe_ref[...] = _rope_lanes(seg(OFF_KR, 128), cos[:, :LANE], sin[:, :LANE]).astype(bf16)

    mq_ref[...] = (seg(OFF_MQ, 256) * (ML_DH ** -0.5)).astype(bf16)
    mk_ref[...] = seg(OFF_MK, 256).astype(bf16)
    mkt_ref[...] = _dot_nt(wkt_ref[...], ub).astype(bf16)
    mv_ref[...] = seg(OFF_MV, 256).astype(bf16)
    gcol_ref[...] = seg(OFF_MG, 128) + gb_ref[...]
    mo_ref[...] = seg(OFF_MO, 256).astype(bf16)

    uc = seg(OFF_CC, 256) * seg(OFF_CH, 256)

    @pl.when(i % tiles_per_seq == 0)
    def _():
        carry_ref[...] = jnp.zeros_like(carry_ref)

    prev = carry_ref[...]
    row =lax.broadcasted_iota(jnp.int32, uc.shape, 0)
    u1 = jnp.where(row == 0, prev[7:8, :], pltpu.roll(uc, 1, axis=0))
    u2 = jnp.where(row == 0, prev[6:7, :], jnp.where(row == 1, prev[7:8, :], pltpu.roll(uc, 2, axis=0)))
    carry_ref[...] = uc[tm - 8:, :]
    conv = cw_ref[0:1, :] * u2 + cw_ref[1:2, :] * u1 + cw_ref[2:3, :] * uc
    yc_ref[...] = (seg(OFF_CB, 256) * conv).astype(bf16)

    dq_ref[...] = (_rope_lanes(seg(OFF_DQ, 256), cos, sin) * (DF_DK ** -0.5 * LOG2E)).astype(bf16)
    dk_ref[...] = _rope_lanes(seg(OFF_DK, 256), cos, sin).astype(bf16)
    dv_ref[...] = seg(OFF_DV, 256).astype(bf16)


def _inproj(x2d, mod, w_packed, wkt, cos_t, sin_t, gq, gkv, gbias, conv_w, seq):
    t, d = x2d.shape
    tm = 512
    tps = seq // tm
    row = lambda w: pl.BlockSpec((tm, w), lambda i: (i, 0))
    const = lambda a: pl.BlockSpec(a.shape, lambda i: (0,) * a.ndim)
    widths = (256, 128, 128, 256, 256, 256, 256, 128, 256, 256, 256, 256)
    dtypes = (bf16, bf16, bf16, bf16, bf16, bf16, bf16, f32, bf16, bf16, bf16, bf16)
    return pl.pallas_call(
        functools.partial(_inproj_body, tiles_per_seq=tps),
        grid=(t // tm,),
        in_specs=[
            row(d),
            pl.BlockSpec((1, 6, d), lambda i: (i // tps, 0, 0)),
            const(w_packed), const(wkt),
            pl.BlockSpec((tm, 256), lambda i: (i % tps, 0)),
            pl.BlockSpec((tm, 256), lambda i: (i % tps, 0)),
            const(gq), const(gkv), const(gbias), const(conv_w),
        ],
        out_specs=[row(w) for w in widths] + [pl.BlockSpec((GROUP_W, tm), lambda i: (0, i))],
        out_shape=[jax.ShapeDtypeStruct((t, w), dt) for w, dt in zip(widths, dtypes)]
        + [jax.ShapeDtypeStruct((GROUP_W, t), bf16)],
        scratch_shapes=[pltpu.VMEM((8, 256), f32)],
        compiler_params=_cparams(("arbitrary",)),
        name="inproj",
    )(x2d, mod, w_packed, wkt, cos_t, sin_t, gq, gkv, gbias, conv_w)


def _causal_softmax_heads(q_heads, k_at, v_at, dv, qi, s_scr, m_scr, acc_scr):
    n_heads = len(q_heads)
    tq = q_heads[0].shape[0]
    half = tq // 2

    def scores(g, j):
        return _dot_nt(q_heads[g], k_at(g, pl.multiple_of(j * tq, tq)))

    def fold_max(g, s):
        m_scr[g] = jnp.maximum(m_scr[g], jnp.maximum(s[:, :half], s[:, half:]))

    m_scr[...] = jnp.full(m_scr.shape, NEG, f32)
    acc_scr[...] = jnp.zeros_like(acc_scr)

    def score_blocks(js):
        for g in range(n_heads):
            mx = None
            for j in js:
                s = scores(g, j)
                s_scr[g, j] = s
                s = jnp.maximum(s[:, :half], s[:, half:])
                mx = s if mx is None else jnp.maximum(mx, s)
            m_scr[g] = jnp.maximum(m_scr[g], mx)

    def prob_blocks(js):
        for g in range(n_heads):
            mb = m_scr[g]
            pv = None
            for j in js:
                sj = s_scr[g, j]
                p = jnp.concatenate([jnp.exp2(sj[:, :half] - mb), jnp.exp2(sj[:, half:] - mb)],
                                    axis=1).astype(bf16)
                d = _dot(p, v_at(g, pl.multiple_of(j * tq, tq)))
                pv = d if pv is None else pv + d
            acc_scr[g] += pv

    def over_full_blocks(blocks_fn):
        @pl.loop(0, qi // 2)
        def _(jj):
            blocks_fn((2 * jj, 2 * jj + 1))

        @pl.when(qi % 2 == 1)
        def _():
            blocks_fn((qi - 1,))

    over_full_blocks(score_blocks)
    rowi = lax.broadcasted_iota(jnp.int32, (tq, tq), 0)
    coli = lax.broadcasted_iota(jnp.int32, (tq, tq), 1)
    for g in range(n_heads):
        s = jnp.where(coli <= rowi, scores(g, qi), NEG)
        s_scr[g, qi] = s
        fold_max(g, s)
        m_scr[g] = jnp.broadcast_to(jnp.max(m_scr[g], axis=1, keepdims=True), (tq, half))

    over_full_blocks(prob_blocks)
    prob_blocks((qi,))
    outs = []
    for g in range(n_heads):
        acc = acc_scr[g]
        outs.append(acc[:, :dv] / acc[:, dv:dv + 1])
    return outs


def _softmax_scratch(n_heads, nq, tq):
    return [pltpu.VMEM((n_heads, nq, tq, tq), f32), pltpu.VMEM((n_heads, tq, tq // 2), f32),
            pltpu.VMEM((n_heads, tq, LANE), f32)]


def _with_ones_lane(v, dv):
    rows = v.shape[0]
    lane = lax.broadcasted_iota(jnp.int32, (rows, LANE - dv), 1)
    return jnp.concatenate([v.astype(bf16), (lane == 0).astype(bf16)], axis=1)


def _mla_body(qn_ref, kvn_ref, kpe_ref, wq_ref, wqr_ref, wk_ref, wv_ref, place_ref, cq_ref, sq_ref,
              o_ref, k_scr, v_scr, s_scr, m_scr, acc_scr):
    qi = pl.program_id(1)
    tq = qn_ref.shape[0]

    @pl.when(qi == 0)
    def _():
        kvn = kvn_ref[...]
        kpe_placed = _dot(kpe_ref[...], place_ref[...])
        for h in range(MLA_HEADS):
            k_scr[h] = (_dot(kvn, wk_ref[h]) + kpe_placed).astype(bf16)
            v_scr[h] = _with_ones_lane(_dot(kvn, wv_ref[h]), MLA_V)

    qn = qn_ref[...]
    cq = cq_ref[...]
    sq = sq_ref[...]
    q_heads = [(_dot(qn, wq_ref[h]) * cq + _dot(qn, wqr_ref[h]) * sq).astype(bf16)
               for h in range(MLA_HEADS)]
    outs = _causal_softmax_heads(
        q_heads, lambda g, start: k_scr[g, pl.ds(start, tq), :],
        lambda g, start: v_scr[g, pl.ds(start, tq), :], MLA_V, qi, s_scr, m_scr, acc_scr)
    o_ref[...] = jnp.concatenate(outs, axis=1).astype(bf16)


def _mla_attention(qn, kvn, kpe, wq, wqr, wk, wv, place, cq, sq, batch, seq):
    tq = 256
    nq = seq // tq
    const = lambda a: pl.BlockSpec(a.shape, lambda b, i: (0,) * a.ndim)
    return pl.pallas_call(
        _mla_body,
        grid=(batch, nq),
        in_specs=[
            pl.BlockSpec((tq, 256), lambda b, i: (b * nq + i, 0)),
            pl.BlockSpec((seq, 128), lambda b, i: (b, 0)),
            pl.BlockSpec((seq, 128), lambda b, i: (b, 0)),
            const(wq), const(wqr), const(wk), const(wv), const(place),
            pl.BlockSpec((tq, 128), lambda b, i: (i, 0)),
            pl.BlockSpec((tq, 128), lambda b, i: (i, 0)),
        ],
        out_specs=pl.BlockSpec((tq, GROUP_W), lambda b, i: (b * nq + i, 0)),
        out_shape=jax.ShapeDtypeStruct((batch * seq, GROUP_W), bf16),
        scratch_shapes=[pltpu.VMEM((MLA_HEADS, seq, 128), bf16),
                        pltpu.VMEM((MLA_HEADS, seq, LANE), bf16)]
        + _softmax_scratch(MLA_HEADS, nq, tq),
        compiler_params=_cparams(("arbitrary", "arbitrary")),
        name="mla_attn",
    )(qn, kvn, kpe, wq, wqr, wk, wv, place, cq, sq)


def _diff_body(q_ref, k_ref, v_ref, lq1_ref, lk1_ref, lq2_ref, lk2_ref, g_ref,
               o_ref, v_scr, s_scr, m_scr, acc_scr, *, lambda_init):
    qi = pl.program_id(1)
    tq = q_ref.shape[0]

    @pl.when(qi == 0)
    def _():
        v = v_ref[...]
        for h in range(DF_HEADS):
            v_scr[h] = _with_ones_lane(v[:, DF_V * h:DF_V * (h + 1)], DF_V)

    q = q_ref[...]
    lam = (jnp.exp(jnp.sum(lq1_ref[...] * lk1_ref[...], axis=1, keepdims=True))
           - jnp.exp(jnp.sum(lq2_ref[...] * lk2_ref[...], axis=1, keepdims=True)) + lambda_init)
    q_heads = [jnp.where(_lane_group_mask(q.shape, DF_DK * g, DF_DK * (g + 1)), q, jnp.zeros_like(q))
               for g in range(2 * DF_HEADS)]
    maps = _causal_softmax_heads(
        q_heads, lambda g, start: k_ref[pl.ds(start, tq), :],
        lambda g, start: v_scr[g // 2, pl.ds(start, tq), :], DF_V, qi, s_scr, m_scr, acc_scr)
    outs = []
    for h in range(DF_HEADS):
        o = maps[2 * h] - lam * maps[2 * h + 1]
        ms = jnp.mean(o * o, axis=1, keepdims=True)
        outs.append(o * lax.rsqrt(ms + RMS_EPS) * g_ref[...] * (1.0 - lambda_init))
    o_ref[...] = jnp.concatenate(outs, axis=1).astype(bf16)


def _diff_attention(dq, dk, dv, lq1, lk1, lq2, lk2, g_sub, batch, seq, lambda_init):
    tq = 256
    nq = seq // tq
    const = lambda a: pl.BlockSpec(a.shape, lambda b, i: (0,) * a.ndim)
    return pl.pallas_call(
        functools.partial(_diff_body, lambda_init=lambda_init),
        grid=(batch, nq),
        in_specs=[
            pl.BlockSpec((tq, 256), lambda b, i: (b * nq + i, 0)),
            pl.BlockSpec((seq, 256), lambda b, i: (b, 0)),
            pl.BlockSpec((seq, 256), lambda b, i: (b, 0)),
            const(lq1), const(lk1), const(lq2), const(lk2), const(g_sub),
        ],
        out_specs=pl.BlockSpec((tq, GROUP_W), lambda b, i: (b * nq + i, 0)),
        out_shape=jax.ShapeDtypeStruct((batch * seq, GROUP_W), bf16),
        scratch_shapes=[pltpu.VMEM((DF_HEADS, seq, LANE), bf16)]
        + _softmax_scratch(2 * DF_HEADS, nq, tq),
        compiler_params=_cparams(("arbitrary", "arbitrary")),
        name="diff_attn",
    )(dq, dk, dv, lq1, lk1, lq2, lk2, g_sub)


def _split3(a):
    hi = a.astype(bf16)
    r1 = a - hi.astype(f32)
    mid = r1.astype(bf16)
    return hi, mid, (r1 - mid.astype(f32)).astype(bf16)


def _dot_exact_rhs01(a, b01):
    hi, mid, lo = _split3(a)
    return _dot(hi, b01) + _dot(mid, b01) + _dot(lo, b01)


def _dot_exact_lhs01(a01, b):
    hi, mid, lo = _split3(b)
    return _dot(a01, hi) + _dot(a01, mid) + _dot(a01, lo)


def _mlstm_chunk(q, k, kt_bd, v, o, g, eif, c_prev, n_prev, m_prev):
    L, W = q.shape
    row = lax.broadcasted_iota(jnp.int32, (L, W), 0)
    pos = lax.broadcasted_iota(jnp.int32, (L, W), 1) % L
    lf = jnp.minimum(g, 0.0) - jnp.log(1.0 + jnp.exp(-jnp.abs(g)))
    glane = lax.broadcasted_iota(jnp.int32, g.shape, 1)
    wide = _dot_exact_rhs01(jnp.where(glane < ML_HEADS, g, lf), eif)
    iw, lfw = wide[:, :W], wide[:, W:]
    tril = (lax.broadcasted_iota(jnp.int32, (L, L), 0) >= lax.broadcasted_iota(jnp.int32, (L, L), 1))
    bw = _dot_exact_lhs01(tril.astype(bf16), lfw)
    rw = iw - bw
    cm = rw
    for sh in (1, 2, 4, 8, 16, 32):
        cm = jnp.maximum(cm, jnp.where(row >= sh, pltpu.roll(cm, sh, axis=0), NEG))
    mw = jnp.maximum(m_prev, cm)
    a_inter = jnp.exp(m_prev - mw)
    r_row = jnp.sum(jnp.where(pos == row, rw, 0.0), axis=0, keepdims=True)
    ew = jnp.exp(jnp.where(pos <= row, r_row - mw, NEG))
    pw = _dot(q, kt_bd.astype(bf16)) * ew
    head_eq = (lax.broadcasted_iota(jnp.int32, (W, W), 0) // ML_DH
               == lax.broadcasted_iota(jnp.int32, (W, W), 1) // ML_DH)
    v4 = jnp.concatenate([v] * ML_HEADS, axis=0)
    vbd = jnp.where(head_eq, v4, jnp.zeros_like(v4))
    obd = head_eq.astype(bf16)
    den = a_inter * _dot_exact_rhs01(q.astype(f32) * n_prev, obd) + _dot_exact_rhs01(pw, obd)
    hden = jnp.maximum(jnp.abs(den), jnp.exp(-(bw + mw)))
    num = a_inter * _dot(q, c_prev.astype(bf16)) + _dot(pw.astype(bf16), vbd)
    y = num / hden * _sigmoid(o.astype(f32))

    rmax, m_last = cm[L - 1:L, :], mw[L - 1:L, :]
    a_w = jnp.exp(m_prev - m_last)
    b_w = jnp.exp(rmax - m_last)
    c_loc = _dot((kt_bd * jnp.exp(r_row - rmax)).astype(bf16), vbd)
    n_loc = jnp.sum(k.astype(f32) * jnp.exp(rw - rmax), axis=0, keepdims=True)
    return y, c_prev * a_w + c_loc * b_w, a_w * n_prev + b_w * n_loc, bw[L - 1:L, :] + m_last


def _mlstm_body(q_ref, k_ref, kt_ref, v_ref, o_ref, g_ref, eif_ref, sel_ref, y_ref, c_scr, nm_scr):
    @pl.when(pl.program_id(1) == 0)
    def _():
        c_scr[...] = jnp.zeros_like(c_scr)
        nm_scr[...] = jnp.zeros_like(nm_scr)

    c, n, m = c_scr[...], nm_scr[0:1, :], nm_scr[1:2, :]
    kt = kt_ref[...]
    eif = eif_ref[...]
    w = kt.shape[0]
    head_eq = (lax.broadcasted_iota(jnp.int32, (w, w), 0) // ML_DH
               == lax.broadcasted_iota(jnp.int32, (w, w), 1) // ML_DH)
    for i in range(q_ref.shape[0] // ML_CHUNK):
        rows = pl.ds(i * ML_CHUNK, ML_CHUNK)
        kt_bd = jnp.where(head_eq, _dot(kt, sel_ref[i]), 0.0)
        y, c, n, m = _mlstm_chunk(q_ref[rows, :], k_ref[rows, :], kt_bd, v_ref[rows, :], o_ref[rows, :],
                                  g_ref[rows, :], eif, c, n, m)
        y_ref[rows, :] = y.astype(y_ref.dtype)
    c_scr[...] = c
    nm_scr[0:1, :] = n
    nm_scr[1:2, :] = m


MLSTM_CHUNKS_PER_STEP = 4


def _mlstm(mq, mk, mkt, mv, mo, gcol, batch, seq):
    cps = MLSTM_CHUNKS_PER_STEP
    L = ML_CHUNK * cps
    nc = seq // L
    blk = lambda w: pl.BlockSpec((L, w), lambda b, c: (b * nc + c, 0))
    const = lambda a: pl.BlockSpec(a.shape, lambda b, c: (0,) * a.ndim)
    gl = jnp.arange(LANE)[:, None]
    col = jnp.arange(2 * GROUP_W)[None, :]
    eif = (gl == (col // GROUP_W) * ML_HEADS + (col % GROUP_W) // ML_DH).astype(bf16)
    p = jnp.arange(L)[None, :, None]
    cc = jnp.arange(GROUP_W)[None, None, :]
    sel = (p == jnp.arange(cps)[:, None, None] * ML_CHUNK + cc % ML_CHUNK).astype(bf16)
    return pl.pallas_call(
        _mlstm_body,
        grid=(batch, nc),
        in_specs=[blk(256), blk(256), pl.BlockSpec((GROUP_W, L), lambda b, c: (0, b * nc + c)),
                  blk(256), blk(256), blk(128), const(eif), const(sel)],
        out_specs=blk(GROUP_W),
        out_shape=jax.ShapeDtypeStruct((batch * seq, GROUP_W), bf16),
        scratch_shapes=[pltpu.VMEM((GROUP_W, GROUP_W), f32), pltpu.VMEM((8, GROUP_W), f32)],
        compiler_params=_cparams(("arbitrary", "arbitrary")),
        name="mlstm",
    )(mq, mk, mkt, mv, mo, gcol, eif, sel)


def _route(scores, sel):
    ne, tm = sel.shape
    eid = lax.broadcasted_iota(jnp.int32, (ne, tm), 0)
    grp = eid // E_PER_GROUP
    big = ne + 1

    def top2(vals):
        m1 = jnp.max(vals, axis=0, keepdims=True)
        i1 = jnp.min(jnp.where(vals == m1, eid, big), axis=0, keepdims=True)
        rest = jnp.where(eid == i1, NEG, vals)
        m2 = jnp.max(rest, axis=0, keepdims=True)
        i2 = jnp.min(jnp.where(rest == m2, eid, big), axis=0, keepdims=True)
        return m1, i1, m2, i2

    best_score, best_grp = None, None
    for gi in range(N_GROUPS):
        m1, _, m2, _ = top2(jnp.where(grp == gi, sel, NEG))
        sc = m1 + m2
        if best_score is None:
            best_score, best_grp = sc, jnp.zeros_like(sc, dtype=jnp.int32)
        else:
            better = sc > best_score
            best_grp = jnp.where(better, gi, best_grp)
            best_score = jnp.where(better, sc, best_score)
    _, i1, _, i2 = top2(jnp.where(grp == best_grp, sel, NEG))
    picked = jnp.where((eid == i1) | (eid == i2), scores, 0.0)
    return picked / jnp.sum(picked, axis=0, keepdims=True), best_grp


MOE_ROW_TILE = 256


def _outproj_route_body(ya_ref, yb_ref, yc_ref, yd_ref, x_ref, mod_ref, w_ref, g_ref, b_ref,
                        wr_ref, br_ref, x1_ref, u2e_ref, grp_ref):
    d = x_ref.shape[1]
    mix = None
    for j, y_ref in enumerate((ya_ref, yb_ref, yc_ref, yd_ref)):
        part = _dot(y_ref[...], w_ref[GROUP_W * j:GROUP_W * (j + 1), :])
        mix = part if mix is None else mix + part
    x1 = _layer_norm(DN_ALPHA * x_ref[...] + mod_ref[0, 2:3, :] * mix) * g_ref[...] + b_ref[...]
    x1_ref[...] = x1
    u = _layer_norm(x1) * (1.0 + mod_ref[0, 4:5, :]) + mod_ref[0, 3:4, :]
    u_hi = u.astype(bf16)
    u_lo = (u - u_hi.astype(f32)).astype(bf16)
    ne = br_ref.shape[0]
    part = _dot_nt(wr_ref[...], u_hi)
    logits = part[:ne] + part[ne:] + _dot_nt(wr_ref[:ne, :], u_lo)
    scores = _sigmoid(logits)
    gates_t, best_grp = _route(scores, scores + br_ref[...])
    u2e_ref[:, :d] = u_hi.astype(f32)
    u2e_ref[:, d:] = jnp.zeros((x1.shape[0], LANE), f32)
    u2e_ref[:, d:d + ne] = gates_t.T
    grp_ref[...] = best_grp


def _outproj_route(ya, yb, yc, yd, x2d, mod, w_out, ln_g, ln_b, wr_split, b_router, seq):
    t, d = x2d.shape
    tm = 512
    tps = seq // tm
    row = lambda w: pl.BlockSpec((tm, w), lambda i: (i, 0))
    const = lambda a: pl.BlockSpec(a.shape, lambda i: (0,) * a.ndim)
    return pl.pallas_call(
        _outproj_route_body,
        grid=(t // tm,),
        in_specs=[row(256), row(256), row(256), row(256), row(d),
                  pl.BlockSpec((1, 6, d), lambda i: (i // tps, 0, 0)),
                  const(w_out), const(ln_g), const(ln_b), const(wr_split), const(b_router)],
        out_specs=[row(d), row(d + LANE), pl.BlockSpec((1, tm), lambda i: (0, i))],
        out_shape=[jax.ShapeDtypeStruct((t, d), f32), jax.ShapeDtypeStruct((t, d + LANE), f32),
                   jax.ShapeDtypeStruct((1, t), jnp.int32)],
        compiler_params=_cparams(("arbitrary",)),
        name="outproj_route",
    )(ya, yb, yc, yd, x2d, mod, w_out, ln_g, ln_b, wr_split, b_router)


def _dispatch_plan(grp, tr):
    t = grp.shape[0]
    rows = t + N_GROUPS * tr
    order = jnp.sort(grp * t + jnp.arange(t, dtype=jnp.int32)) % t
    order = jnp.concatenate([order, jnp.zeros((rows - t,), order.dtype)])
    counts = jnp.sum((grp[:, None] == jnp.arange(N_GROUPS)[None, :]).astype(jnp.int32), axis=0)
    padded = ((counts + tr - 1) // tr) * tr
    row_end = jnp.cumsum(padded)
    r = jnp.arange(rows, dtype=jnp.int32)
    read_tok = jnp.zeros((rows,), jnp.int32)
    write_row = jnp.zeros((rows,), jnp.int32)
    for g in range(N_GROUPS):
        row_start = row_end[g] - padded[g]
        tok_start = jnp.sum(counts[:g])
        shifted = jnp.roll(order, row_start - tok_start)
        in_grp = (r >= row_start) & (r < row_end[g])
        valid = in_grp & (r < row_start + counts[g])
        read_tok = jnp.where(valid, shifted, jnp.where(in_grp, order[jnp.minimum(tok_start, t - 1)], read_tok))
        write_row = jnp.where(valid, shifted, jnp.where(in_grp, t + g * tr + r % tr, write_row))
    tile_start = jnp.arange(rows // tr, dtype=jnp.int32) * tr
    tile_grp = jnp.minimum(jnp.sum((tile_start[:, None] >= row_end[None, :]).astype(jnp.int32), axis=1),
                           N_GROUPS - 1)
    i32 = lambda a: a.astype(jnp.int32)
    return i32(read_tok), i32(write_row), i32(tile_grp), i32(row_end[-1:] // tr)


def _wait_row_gather(n_rows, hbm, vmem, sem):
    pltpu.make_async_copy(hbm.at[pl.ds(0, n_rows), :], vmem, sem).wait()


def _wait_row_scatter(n_rows, vmem, hbm, sem):
    pltpu.make_async_copy(vmem, hbm.at[pl.ds(0, n_rows), :], sem).wait()


def _experts_body(read_ref, write_ref, tgrp_ref, ntile_ref,
                  u2e_hbm, w1_ref, w3_ref, w2_ref, ys_hbm, xbuf0, xbuf1, obuf0, obuf1, gsem, ssem):
    j = pl.program_id(0)
    n_tiles = ntile_ref[0]
    tr, d = obuf0.shape
    xbuf, obuf = (xbuf0, xbuf1), (obuf0, obuf1)

    def start_gather(tile, s):
        for i in range(tr):
            pltpu.make_async_copy(u2e_hbm.at[pl.ds(read_ref[tile * tr + i], 1), :],
                                  xbuf[s].at[pl.ds(i, 1), :], gsem.at[s]).start()

    def start_scatter(tile, s):
        for i in range(tr):
            pltpu.make_async_copy(obuf[s].at[pl.ds(i, 1), :],
                                  ys_hbm.at[pl.ds(write_ref[tile * tr + i], 1), :], ssem.at[s]).start()

    def load_tile(s):
        xs = xbuf[s][...]
        return xs[:, :d].astype(bf16), xs[:, d:]

    def compute(tile, s, xb, gslab):
        lane = lax.broadcasted_iota(jnp.int32, gslab.shape, 1)
        first_expert = tgrp_ref[tile] * E_PER_GROUP
        acc = None
        for e in range(E_PER_GROUP):
            ge = jnp.sum(jnp.where(lane == first_expert + e, gslab, 0.0), axis=1, keepdims=True)
            h1 = _dot(xb, w1_ref[0, e])
            h = h1 * _sigmoid(h1) * _dot(xb, w3_ref[0, e]) * ge
            part = _dot(h.astype(bf16), w2_ref[0, e])
            acc = part if acc is None else acc + part
        obuf[s][...] = acc

    nxt = jnp.minimum(j + 1, n_tiles - 1)

    @pl.when(j == 0)
    def _():
        start_gather(0, 0)
        start_gather(nxt, 1)
        obuf[1][...] = jnp.zeros_like(obuf[1])
        n_tok = ys_hbm.shape[0] - N_GROUPS * tr
        spare = [pltpu.make_async_copy(obuf[1], ys_hbm.at[pl.ds(n_tok + g * tr, tr), :], ssem.at[1])
                 for g in range(N_GROUPS)]
        for cp in spare:
            cp.start()
        for cp in spare:
            cp.wait()
        _wait_row_gather(tr, u2e_hbm, xbuf[0], gsem.at[0])
        compute(0, 0, *load_tile(0))

    for s in (0, 1):
        @pl.when((j > 0) & (j < n_tiles) & (j % 2 == s))
        def _(s=s):
            start_gather(nxt, 1 - s)
            start_scatter(j - 1, 1 - s)

        @pl.when((j > 0) & (j < n_tiles) & (j % 2 == s))
        def _(s=s):
            _wait_row_gather(tr, u2e_hbm, xbuf[s], gsem.at[s])
            compute(j, s, *load_tile(s))
            _wait_row_scatter(tr, obuf[1 - s], ys_hbm, ssem.at[1 - s])

        @pl.when((j == n_tiles) & (j % 2 == s))
        def _(s=s):
            _wait_row_gather(tr, u2e_hbm, xbuf[s], gsem.at[s])
            start_scatter(j - 1, 1 - s)
            _wait_row_scatter(tr, obuf[1 - s], ys_hbm, ssem.at[1 - s])


def _experts(read_tok, write_row, tile_grp, n_tiles, u2e, w1, w3, w2):
    rows = read_tok.shape[0]
    tr = MOE_ROW_TILE
    max_tiles = tile_grp.shape[0]
    ne, d, fe = w1.shape
    grouped = lambda w: w.reshape(N_GROUPS, E_PER_GROUP, *w.shape[1:])
    wspec = lambda a, b: pl.BlockSpec(
        (1, E_PER_GROUP, a, b), lambda j, rd, wr, tg, nt: (tg[jnp.minimum(j, max_tiles - 1)], 0, 0, 0))
    return pl.pallas_call(
        _experts_body,
        grid_spec=pltpu.PrefetchScalarGridSpec(
            num_scalar_prefetch=4, grid=(max_tiles + 1,),
            in_specs=[pl.BlockSpec(memory_space=pl.ANY), wspec(d, fe), wspec(d, fe), wspec(fe, d)],
            out_specs=pl.BlockSpec(memory_space=pl.ANY),
            scratch_shapes=[pltpu.VMEM((tr, u2e.shape[1]), f32), pltpu.VMEM((tr, u2e.shape[1]), f32),
                            pltpu.VMEM((tr, d), f32), pltpu.VMEM((tr, d), f32),
                            pltpu.SemaphoreType.DMA((2,)), pltpu.SemaphoreType.DMA((2,))]),
        out_shape=jax.ShapeDtypeStruct((rows, d), f32),
        compiler_params=_cparams(("arbitrary",)),
        name="experts",
    )(read_tok, write_row, tile_grp, n_tiles, u2e, grouped(w1), grouped(w3), grouped(w2))


def _moe_combine_body(y_ref, x_ref, mod_ref, g_ref, b_ref, o_ref):
    r = DN_ALPHA * x_ref[...] + mod_ref[0, 5:6, :] * y_ref[...]
    o_ref[...] = _layer_norm(r) * g_ref[...] + b_ref[...]


def _moe_combine(ys, x1, mod, ln_g, ln_b, seq):
    t, d = x1.shape
    tm = 512
    tps = seq // tm
    row = pl.BlockSpec((tm, d), lambda i: (i, 0))
    const = lambda a: pl.BlockSpec(a.shape, lambda i: (0,) * a.ndim)
    return pl.pallas_call(
        _moe_combine_body,
        grid=(t // tm,),
        in_specs=[row, row, pl.BlockSpec((1, 6, d), lambda i: (i // tps, 0, 0)), const(ln_g), const(ln_b)],
        out_specs=row,
        out_shape=jax.ShapeDtypeStruct((t, d), f32),
        compiler_params=_cparams(("arbitrary",)),
        name="moe_combine",
    )(ys, x1, mod, ln_g, ln_b)


def _pad_cols(a, width):
    return jnp.pad(a, ((0, 0), (0, width - a.shape[1])))


def _pack_w_in(w):
    parts, acc = [], 0
    for sz in IN_SIZES:
        parts.append(w[:, acc:acc + sz])
        acc += sz
    (a_q, a_kv, a_kr, m_q, m_k, m_v, m_i, m_f, m_o, c_b, c_c, c_h, d_q, d_k, d_v) = parts
    gates = _pad_cols(jnp.concatenate([m_i, m_f], axis=1), LANE)
    packed = jnp.concatenate([
        _pad_cols(a_q, 256), a_kv, _pad_cols(a_kr, LANE),
        m_q, m_k, m_v, gates, m_o, c_b, c_c, c_h, d_q, d_k, d_v], axis=1)
    return packed.astype(bf16)


def _pack_mla(w_uq, w_ukv):
    half = MLA_ROPE // 2
    wq3 = w_uq.reshape(MLA_Q_LORA, MLA_HEADS, MLA_NOPE + MLA_ROPE)
    wkv3 = w_ukv.reshape(MLA_KV_LORA, MLA_HEADS, MLA_NOPE + MLA_V)
    wq, wqr, wk, wv = [], [], [], []
    for h in range(MLA_HEADS):
        nope, pe = wq3[:, h, :MLA_NOPE], wq3[:, h, MLA_NOPE:]
        rot = jnp.concatenate([-pe[:, half:], pe[:, :half]], axis=1)
        zq = jnp.zeros((MLA_Q_LORA, MLA_NOPE), f32)
        pad_rows = lambda a: jnp.pad(a, ((0, 256 - MLA_Q_LORA), (0, 0)))
        wq.append(pad_rows(_pad_cols(jnp.concatenate([nope, pe], axis=1), LANE)))
        wqr.append(pad_rows(_pad_cols(jnp.concatenate([zq, rot], axis=1), LANE)))
        wk.append(_pad_cols(wkv3[:, h, :MLA_NOPE], LANE))
        wv.append(wkv3[:, h, MLA_NOPE:])
    st = lambda xs: jnp.stack(xs).astype(bf16)
    return st(wq), st(wqr), st(wk), st(wv)


def _rope_tables(seq):
    half = DF_DK // 2
    inv = 1.0 / (ROPE_THETA ** (jnp.arange(0, DF_DK, 2, dtype=f32) / DF_DK))
    ang = jnp.arange(seq, dtype=f32)[:, None] * inv[None, :]
    cos, sin = jnp.cos(ang), jnp.sin(ang)
    cos_t = jnp.tile(jnp.concatenate([cos, cos], axis=1), (1, 256 // DF_DK))
    sin_t = jnp.tile(jnp.concatenate([-sin, sin], axis=1), (1, 256 // DF_DK))
    scale = (MLA_NOPE + MLA_ROPE) ** -0.5 * LOG2E
    ones = jnp.ones((seq, MLA_NOPE), f32)
    zeros = jnp.zeros((seq, MLA_NOPE), f32)
    cq = _pad_cols(jnp.concatenate([ones, cos, cos], axis=1) * scale, LANE)
    sq = _pad_cols(jnp.concatenate([zeros, sin, sin], axis=1) * scale, LANE)
    return cos_t, sin_t, cq, sq


def kernel(x, c, w_ada, b_ada, w_in, mla_g_q, mla_g_kv, mla_w_uq, mla_w_ukv, ml_b_i, ml_b_f, sc_w,
           df_lq1, df_lk1, df_lq2, df_lk2, df_g, w_out, ln1_g, ln1_b, w_router, b_router, w1, w3, w2,
           ln2_g, ln2_b):
    batch, seq, d = x.shape
    depth = w_in.shape[0]
    assert MLA_ROPE == DF_DK, "both rotary blocks share one table"
    cos_t, sin_t, cq, sq = _rope_tables(seq)
    place = jnp.zeros((LANE, LANE), f32).at[jnp.arange(MLA_ROPE), MLA_NOPE + jnp.arange(MLA_ROPE)].set(1.0).astype(bf16)
    wr_hi = w_router.astype(bf16)
    wr_split = jnp.concatenate([wr_hi, (w_router - wr_hi.astype(f32)).astype(bf16)], axis=1).T
    mod_all = _ada_mod(c, w_ada, b_ada).reshape(depth, batch, 6, d)
    xf = x.reshape(batch * seq, d)
    for l in range(depth):
        mod = mod_all[l]
        w_packed = _pack_w_in(w_in[l])
        gq = _pad_cols(mla_g_q[l][None, :], 256)
        gkv = mla_g_kv[l][None, :]
        gbias = _pad_cols(jnp.concatenate([ml_b_i[l], ml_b_f[l]])[None, :], LANE)
        wkt = w_packed[:, OFF_MK:OFF_MK + GROUP_W].T
        (qn, kvn, kpe, mq, mk, mv, mo, gcol, yc, dq, dk, dv, mkt) = _inproj(
            xf, mod, w_packed, wkt, cos_t, sin_t, gq, gkv, gbias, sc_w[l], seq)
        wq, wqr, wk, wv = _pack_mla(mla_w_uq[l], mla_w_ukv[l])
        ya = _mla_attention(qn, kvn, kpe, wq, wqr, wk, wv, place, cq, sq, batch, seq)
        yb = _mlstm(mq, mk, mkt, mv, mo, gcol, batch, seq)
        lambda_init = 0.8 - 0.6 * math.exp(-0.3 * l)
        yd = _diff_attention(dq, dk, dv, df_lq1[l][None, :], df_lk1[l][None, :], df_lq2[l][None, :],
                             df_lk2[l][None, :], df_g[l][None, :],
                             batch, seq, lambda_init)
        x1, u2e, grp = _outproj_route(ya, yb, yc, yd, xf, mod, w_out[l].astype(bf16), ln1_g[l][None, :],
                                      ln1_b[l][None, :], wr_split, b_router[:, None], seq)
        plan = _dispatch_plan(grp[0], MOE_ROW_TILE)
        ys = _experts(*plan, u2e, w1[l].astype(bf16), w3[l].astype(bf16), w2[l].astype(bf16))
        xf = _moe_combine(ys, x1, mod, ln2_g[l][None, :], ln2_b[l][None, :], seq)
    return xf.reshape(batch, seq, d)
```

```python
import functools
import math

import jax
import jax.numpy as jnp
from jax import lax
from jax.experimental import pallas as pl
from jax.experimental.pallas import tpu as pltpu

f32 = jnp.float32
bf16 = jnp.bfloat16
HIGHEST = lax.Precision.HIGHEST

GROUP_W = 256
MLA_HEADS, MLA_NOPE, MLA_ROPE, MLA_V = 4, 64, 32, 64
MLA_Q_LORA, MLA_KV_LORA = 192, 128
ML_HEADS, ML_DH, ML_CHUNK = 4, 64, 64
DF_HEADS, DF_DK, DF_V = 4, 32, 64
ROPE_THETA = 10000.0
N_EXPERTS, N_GROUPS, E_PER_GROUP, D_EXPERT = 16, 4, 4, 256
DEPTH = 2
DN_ALPHA = (2 * DEPTH) ** 0.25
LN_EPS = 1e-5
RMS_EPS = 1e-6
IN_SIZES = (192, 128, 32, 256, 256, 256, 4, 4, 256, 256, 256, 256, 256, 256, 256)

OFF_PQ, OFF_PKV, OFF_KR = 0, 256, 384
OFF_MQ, OFF_MK, OFF_MV, OFF_MG, OFF_MO = 512, 768, 1024, 1280, 1408
OFF_CB, OFF_CC, OFF_CH = 1664, 1920, 2176
OFF_DQ, OFF_DK, OFF_DV = 2432, 2688, 2944
IN_PACKED = 3200

LANE = 128
VMEM_LIMIT = 48 * 1024 * 1024
NEG = -1e30
LOG2E = math.log2(math.e)


def _cparams(sem):
    return pltpu.CompilerParams(dimension_semantics=sem, vmem_limit_bytes=VMEM_LIMIT)


def _sigmoid(x):
    return 1.0 / (1.0 + jnp.exp(-x))


def _layer_norm(x):
    mu = jnp.mean(x, axis=-1, keepdims=True)
    xc = x - mu
    var = jnp.mean(xc * xc, axis=-1, keepdims=True)
    return xc * lax.rsqrt(var + LN_EPS)


def _dot(a, b, **kw):
    return jnp.dot(a, b, preferred_element_type=f32, **kw)


def _dot_nt(a, b, **kw):
    return lax.dot_general(a, b, (((1,), (1,)), ((), ())), preferred_element_type=f32, **kw)


def _dot_tn(a, b, **kw):
    return lax.dot_general(a, b, (((0,), (0,)), ((), ())), preferred_element_type=f32, **kw)


def _expand_groups(cols, rows, width, group):
    lane = lax.broadcasted_iota(jnp.int32, (rows, width), 1)
    out = jnp.broadcast_to(cols[-1], (rows, width))
    for h in range(len(cols) - 2, -1, -1):
        out = jnp.where(lane < group * (h + 1), cols[h], out)
    return out


def _lane_group_mask(shape, lo, hi):
    lane = lax.broadcasted_iota(jnp.int32, shape, 1)
    return (lane >= lo) & (lane < hi)


def _ada_body(c_ref, w_ref, b_ref, o_ref):
    c = c_ref[...]
    ca = (c * _sigmoid(c)).astype(bf16)
    o_ref[0] = _dot(ca, w_ref[0].astype(bf16)) + b_ref[0]


def _ada_mod(c, w_ada, b_ada):
    depth, d, n = w_ada.shape
    b = c.shape[0]
    tn = 1536
    return pl.pallas_call(
        _ada_body,
        grid=(depth, n // tn),
        in_specs=[
            pl.BlockSpec((b, d), lambda l, j: (0, 0)),
            pl.BlockSpec((1, d, tn), lambda l, j: (l, 0, j)),
            pl.BlockSpec((1, 1, tn), lambda l, j: (l, 0, j)),
        ],
        out_specs=pl.BlockSpec((1, b, tn), lambda l, j: (l, 0, j)),
        out_shape=jax.ShapeDtypeStruct((depth, b, n), f32),
        compiler_params=_cparams(("arbitrary", "arbitrary")),
        name="ada_mod",
    )(c, w_ada, b_ada.reshape(depth, 1, n))


def _rope_lanes(x, cos, sin_signed):
    w = x.shape[1]
    lane = lax.broadcasted_iota(jnp.int32, x.shape, 1)
    rot = jnp.where(lane % 32 < 16, pltpu.roll(x, w - 16, axis=1), pltpu.roll(x, 16, axis=1))
    return x * cos + rot * sin_signed


def _inproj_body(x_ref, mod_ref, w_ref, wkt_ref, cos_ref, sin_ref, gq_ref, gkv_ref, gb_ref, cw_ref,
                 qn_ref, kvn_ref, kpe_ref, mq_ref, mk_ref, mv_ref, mo_ref, gcol_ref, yc_ref,
                 dq_ref, dk_ref, dv_ref, mkt_ref, carry_ref, *, tiles_per_seq):
    i = pl.program_id(0)
    tm = x_ref.shape[0]
    u = _layer_norm(x_ref[...]) * (1.0 + mod_ref[0, 1:2, :]) + mod_ref[0, 0:1, :]
    ub = u.astype(bf16)

    def seg(off, n):
        return _dot(ub, w_ref[:, off:off + n])

    cos = cos_ref[...]
    sin = sin_ref[...]

    pq = seg(OFF_PQ, 256)
    ms = jnp.sum(pq * pq, axis=-1, keepdims=True) * (1.0 / MLA_Q_LORA)
    qn_ref[...] = (pq * lax.rsqrt(ms + RMS_EPS) * gq_ref[...]).astype(bf16)
    pkv = seg(OFF_PKV, 128)
    ms = jnp.sum(pkv * pkv, axis=-1, keepdims=True) * (1.0 / MLA_KV_LORA)
    kvn_ref[...] = (pkv * lax.rsqrt(ms + RMS_EPS) * gkv_ref[...]).astype(bf16)
    kpe_ref[...] = _rope_lanes(seg(OFF_KR, 128), cos[:, :LANE], sin[:, :LANE]).astype(bf16)

    mq_ref[...] = (seg(OFF_MQ, 256) * (ML_DH ** -0.5)).astype(bf16)
    mk_ref[...] = seg(OFF_MK, 256).astype(bf16)
    mkt_ref[...] = _dot_nt(wkt_ref[...], ub).astype(bf16)
    mv_ref[...] = seg(OFF_MV, 256).astype(bf16)
    gcol_ref[...] = seg(OFF_MG, 128) + gb_ref[...]
    mo_ref[...] = seg(OFF_MO, 256).astype(bf16)

    uc = seg(OFF_CC, 256) * seg(OFF_CH, 256)

    @pl.when(i % tiles_per_seq == 0)
    def _():
        carry_ref[...] = jnp.zeros_like(carry_ref)

    prev = carry_ref[...]
    row =lax.broadcasted_iota(jnp.int32, uc.shape, 0)
    u1 = jnp.where(row == 0, prev[7:8, :], pltpu.roll(uc, 1, axis=0))
    u2 = jnp.where(row == 0, prev[6:7, :], jnp.where(row == 1, prev[7:8, :], pltpu.roll(uc, 2, axis=0)))
    carry_ref[...] = uc[tm - 8:, :]
    conv = cw_ref[0:1, :] * u2 + cw_ref[1:2, :] * u1 + cw_ref[2:3, :] * uc
    yc_ref[...] = (seg(OFF_CB, 256) * conv).astype(bf16)

    dq_ref[...] = (_rope_lanes(seg(OFF_DQ, 256), cos, sin) * (DF_DK ** -0.5 * LOG2E)).astype(bf16)
    dk_ref[...] = _rope_lanes(seg(OFF_DK, 256), cos, sin).astype(bf16)
    dv_ref[...] = seg(OFF_DV, 256).astype(bf16)


def _inproj(x2d, mod, w_packed, wkt, cos_t, sin_t, gq, gkv, gbias, conv_w, seq):
    t, d = x2d.shape
    tm = 512
    tps = seq // tm
    row = lambda w: pl.BlockSpec((tm, w), lambda i: (i, 0))
    const = lambda a: pl.BlockSpec(a.shape, lambda i: (0,) * a.ndim)
    widths = (256, 128, 128, 256, 256, 256, 256, 128, 256, 256, 256, 256)
    dtypes = (bf16, bf16, bf16, bf16, bf16, bf16, bf16, f32, bf16, bf16, bf16, bf16)
    return pl.pallas_call(
        functools.partial(_inproj_body, tiles_per_seq=tps),
        grid=(t // tm,),
        in_specs=[
            row(d),
            pl.BlockSpec((1, 6, d), lambda i: (i // tps, 0, 0)),
            const(w_packed), const(wkt),
            pl.BlockSpec((tm, 256), lambda i: (i % tps, 0)),
            pl.BlockSpec((tm, 256), lambda i: (i % tps, 0)),
            const(gq), const(gkv), const(gbias), const(conv_w),
        ],
        out_specs=[row(w) for w in widths] + [pl.BlockSpec((GROUP_W, tm), lambda i: (0, i))],
        out_shape=[jax.ShapeDtypeStruct((t, w), dt) for w, dt in zip(widths, dtypes)]
        + [jax.ShapeDtypeStruct((GROUP_W, t), bf16)],
        scratch_shapes=[pltpu.VMEM((8, 256), f32)],
        compiler_params=_cparams(("arbitrary",)),
        name="inproj",
    )(x2d, mod, w_packed, wkt, cos_t, sin_t, gq, gkv, gbias, conv_w)


def _causal_softmax_heads(q_heads, k_at, v_at, dv, qi, s_scr, m_scr, acc_scr):
    n_heads = len(q_heads)
    tq = q_heads[0].shape[0]
    half = tq // 2

    def scores(g, j):
        return _dot_nt(q_heads[g], k_at(g, pl.multiple_of(j * tq, tq)))

    def fold_max(g, s):
        m_scr[g] = jnp.maximum(m_scr[g], jnp.maximum(s[:, :half], s[:, half:]))

    m_scr[...] = jnp.full(m_scr.shape, NEG, f32)
    acc_scr[...] = jnp.zeros_like(acc_scr)

    def score_blocks(js):
        for g in range(n_heads):
            mx = None
            for j in js:
                s = scores(g, j)
                s_scr[g, j] = s
                s = jnp.maximum(s[:, :half], s[:, half:])
                mx = s if mx is None else jnp.maximum(mx, s)
            m_scr[g] = jnp.maximum(m_scr[g], mx)

    def prob_blocks(js):
        for g in range(n_heads):
            mb = m_scr[g]
            pv = None
            for j in js:
                sj = s_scr[g, j]
                p = jnp.concatenate([jnp.exp2(sj[:, :half] - mb), jnp.exp2(sj[:, half:] - mb)],
                                    axis=1).astype(bf16)
                d = _dot(p, v_at(g, pl.multiple_of(j * tq, tq)))
                pv = d if pv is None else pv + d
            acc_scr[g] += pv

    def over_full_blocks(blocks_fn):
        @pl.loop(0, qi // 2)
        def _(jj):
            blocks_fn((2 * jj, 2 * jj + 1))

        @pl.when(qi % 2 == 1)
        def _():
            blocks_fn((qi - 1,))

    over_full_blocks(score_blocks)
    rowi = lax.broadcasted_iota(jnp.int32, (tq, tq), 0)
    coli = lax.broadcasted_iota(jnp.int32, (tq, tq), 1)
    for g in range(n_heads):
        s = jnp.where(coli <= rowi, scores(g, qi), NEG)
        s_scr[g, qi] = s
        fold_max(g, s)
        m_scr[g] = jnp.broadcast_to(jnp.max(m_scr[g], axis=1, keepdims=True), (tq, half))

    over_full_blocks(prob_blocks)
    prob_blocks((qi,))
    outs = []
    for g in range(n_heads):
        acc = acc_scr[g]
        outs.append(acc[:, :dv] / acc[:, dv:dv + 1])
    return outs


def _softmax_scratch(n_heads, nq, tq):
    return [pltpu.VMEM((n_heads, nq, tq, tq), f32), pltpu.VMEM((n_heads, tq, tq // 2), f32),
            pltpu.VMEM((n_heads, tq, LANE), f32)]


def _with_ones_lane(v, dv):
    rows = v.shape[0]
    lane = lax.broadcasted_iota(jnp.int32, (rows, LANE - dv), 1)
    return jnp.concatenate([v.astype(bf16), (lane == 0).astype(bf16)], axis=1)


def _mla_body(qn_ref, kvn_ref, kpe_ref, wq_ref, wqr_ref, wk_ref, wv_ref, place_ref, cq_ref, sq_ref,
              o_ref, k_scr, v_scr, s_scr, m_scr, acc_scr):
    qi = pl.program_id(1)
    tq = qn_ref.shape[0]

    @pl.when(qi == 0)
    def _():
        kvn = kvn_ref[...]
        kpe_placed = _dot(kpe_ref[...], place_ref[...])
        for h in range(MLA_HEADS):
            k_scr[h] = (_dot(kvn, wk_ref[h]) + kpe_placed).astype(bf16)
            v_scr[h] = _with_ones_lane(_dot(kvn, wv_ref[h]), MLA_V)

    qn = qn_ref[...]
    cq = cq_ref[...]
    sq = sq_ref[...]
    q_heads = [(_dot(qn, wq_ref[h]) * cq + _dot(qn, wqr_ref[h]) * sq).astype(bf16)
               for h in range(MLA_HEADS)]
    outs = _causal_softmax_heads(
        q_heads, lambda g, start: k_scr[g, pl.ds(start, tq), :],
        lambda g, start: v_scr[g, pl.ds(start, tq), :], MLA_V, qi, s_scr, m_scr, acc_scr)
    o_ref[...] = jnp.concatenate(outs, axis=1).astype(bf16)


def _mla_attention(qn, kvn, kpe, wq, wqr, wk, wv, place, cq, sq, batch, seq):
    tq = 256
    nq = seq // tq
    const = lambda a: pl.BlockSpec(a.shape, lambda b, i: (0,) * a.ndim)
    return pl.pallas_call(
        _mla_body,
        grid=(batch, nq),
        in_specs=[
            pl.BlockSpec((tq, 256), lambda b, i: (b * nq + i, 0)),
            pl.BlockSpec((seq, 128), lambda b, i: (b, 0)),
            pl.BlockSpec((seq, 128), lambda b, i: (b, 0)),
            const(wq), const(wqr), const(wk), const(wv), const(place),
            pl.BlockSpec((tq, 128), lambda b, i: (i, 0)),
            pl.BlockSpec((tq, 128), lambda b, i: (i, 0)),
        ],
        out_specs=pl.BlockSpec((tq, GROUP_W), lambda b, i: (b * nq + i, 0)),
        out_shape=jax.ShapeDtypeStruct((batch * seq, GROUP_W), bf16),
        scratch_shapes=[pltpu.VMEM((MLA_HEADS, seq, 128), bf16),
                        pltpu.VMEM((MLA_HEADS, seq, LANE), bf16)]
        + _softmax_scratch(MLA_HEADS, nq, tq),
        compiler_params=_cparams(("arbitrary", "arbitrary")),
        name="mla_attn",
    )(qn, kvn, kpe, wq, wqr, wk, wv, place, cq, sq)


def _diff_body(q_ref, k_ref, v_ref, lq1_ref, lk1_ref, lq2_ref, lk2_ref, g_ref,
               o_ref, v_scr, s_scr, m_scr, acc_scr, *, lambda_init):
    qi = pl.program_id(1)
    tq = q_ref.shape[0]

    @pl.when(qi == 0)
    def _():
        v = v_ref[...]
        for h in range(DF_HEADS):
            v_scr[h] = _with_ones_lane(v[:, DF_V * h:DF_V * (h + 1)], DF_V)

    q = q_ref[...]
    lam = (jnp.exp(jnp.sum(lq1_ref[...] * lk1_ref[...], axis=1, keepdims=True))
           - jnp.exp(jnp.sum(lq2_ref[...] * lk2_ref[...], axis=1, keepdims=True)) + lambda_init)
    q_heads = [jnp.where(_lane_group_mask(q.shape, DF_DK * g, DF_DK * (g + 1)), q, jnp.zeros_like(q))
               for g in range(2 * DF_HEADS)]
    maps = _causal_softmax_heads(
        q_heads, lambda g, start: k_ref[pl.ds(start, tq), :],
        lambda g, start: v_scr[g // 2, pl.ds(start, tq), :], DF_V, qi, s_scr, m_scr, acc_scr)
    outs = []
    for h in range(DF_HEADS):
        o = maps[2 * h] - lam * maps[2 * h + 1]
        ms = jnp.mean(o * o, axis=1, keepdims=True)
        outs.append(o * lax.rsqrt(ms + RMS_EPS) * g_ref[...] * (1.0 - lambda_init))
    o_ref[...] = jnp.concatenate(outs, axis=1).astype(bf16)


def _diff_attention(dq, dk, dv, lq1, lk1, lq2, lk2, g_sub, batch, seq, lambda_init):
    tq = 256
    nq = seq // tq
    const = lambda a: pl.BlockSpec(a.shape, lambda b, i: (0,) * a.ndim)
    return pl.pallas_call(
        functools.partial(_diff_body, lambda_init=lambda_init),
        grid=(batch, nq),
        in_specs=[
            pl.BlockSpec((tq, 256), lambda b, i: (b * nq + i, 0)),
            pl.BlockSpec((seq, 256), lambda b, i: (b, 0)),
            pl.BlockSpec((seq, 256), lambda b, i: (b, 0)),
            const(lq1), const(lk1), const(lq2), const(lk2), const(g_sub),
        ],
        out_specs=pl.BlockSpec((tq, GROUP_W), lambda b, i: (b * nq + i, 0)),
        out_shape=jax.ShapeDtypeStruct((batch * seq, GROUP_W), bf16),
        scratch_shapes=[pltpu.VMEM((DF_HEADS, seq, LANE), bf16)]
        + _softmax_scratch(2 * DF_HEADS, nq, tq),
        compiler_params=_cparams(("arbitrary", "arbitrary")),
        name="diff_attn",
    )(dq, dk, dv, lq1, lk1, lq2, lk2, g_sub)


def _split3(a):
    hi = a.astype(bf16)
    r1 = a - hi.astype(f32)
    mid = r1.astype(bf16)
    return hi, mid, (r1 - mid.astype(f32)).astype(bf16)


def _dot_exact_rhs01(a, b01):
    hi, mid, lo = _split3(a)
    return _dot(hi, b01) + _dot(mid, b01) + _dot(lo, b01)


def _dot_exact_lhs01(a01, b):
    hi, mid, lo = _split3(b)
    return _dot(a01, hi) + _dot(a01, mid) + _dot(a01, lo)


def _mlstm_chunk(q, k, kt_bd, v, o, g, eif, c_prev, n_prev, m_prev):
    L, W = q.shape
    row = lax.broadcasted_iota(jnp.int32, (L, W), 0)
    pos = lax.broadcasted_iota(jnp.int32, (L, W), 1) % L
    lf = jnp.minimum(g, 0.0) - jnp.log(1.0 + jnp.exp(-jnp.abs(g)))
    glane = lax.broadcasted_iota(jnp.int32, g.shape, 1)
    wide = _dot_exact_rhs01(jnp.where(glane < ML_HEADS, g, lf), eif)
    iw, lfw = wide[:, :W], wide[:, W:]
    tril = (lax.broadcasted_iota(jnp.int32, (L, L), 0) >= lax.broadcasted_iota(jnp.int32, (L, L), 1))
    bw = _dot_exact_lhs01(tril.astype(bf16), lfw)
    rw = iw - bw
    cm = rw
    for sh in (1, 2, 4, 8, 16, 32):
        cm = jnp.maximum(cm, jnp.where(row >= sh, pltpu.roll(cm, sh, axis=0), NEG))
    mw = jnp.maximum(m_prev, cm)
    a_inter = jnp.exp(m_prev - mw)
    r_row = jnp.sum(jnp.where(pos == row, rw, 0.0), axis=0, keepdims=True)
    ew = jnp.exp(jnp.where(pos <= row, r_row - mw, NEG))
    pw = _dot(q, kt_bd.astype(bf16)) * ew
    head_eq = (lax.broadcasted_iota(jnp.int32, (W, W), 0) // ML_DH
               == lax.broadcasted_iota(jnp.int32, (W, W), 1) // ML_DH)
    v4 = jnp.concatenate([v] * ML_HEADS, axis=0)
    vbd = jnp.where(head_eq, v4, jnp.zeros_like(v4))
    obd = head_eq.astype(bf16)
    den = a_inter * _dot_exact_rhs01(q.astype(f32) * n_prev, obd) + _dot_exact_rhs01(pw, obd)
    hden = jnp.maximum(jnp.abs(den), jnp.exp(-(bw + mw)))
    num = a_inter * _dot(q, c_prev.astype(bf16)) + _dot(pw.astype(bf16), vbd)
    y = num / hden * _sigmoid(o.astype(f32))

    rmax, m_last = cm[L - 1:L, :], mw[L - 1:L, :]
    a_w = jnp.exp(m_prev - m_last)
    b_w = jnp.exp(rmax - m_last)
    c_loc = _dot((kt_bd * jnp.exp(r_row - rmax)).astype(bf16), vbd)
    n_loc = jnp.sum(k.astype(f32) * jnp.exp(rw - rmax), axis=0, keepdims=True)
    return y, c_prev * a_w + c_loc * b_w, a_w * n_prev + b_w * n_loc, bw[L - 1:L, :] + m_last


def _mlstm_body(q_ref, k_ref, kt0_ref, kt1_ref, v_ref, o_ref, g_ref, eif_ref, sel_ref, y_ref, c_scr, nm_scr):
    @pl.when(pl.program_id(1) == 0)
    def _():
        c_scr[...] = jnp.zeros_like(c_scr)
        nm_scr[...] = jnp.zeros_like(nm_scr)

    n_seq = q_ref.shape[0]
    state = [(c_scr[s], nm_scr[s, 0:1, :], nm_scr[s, 1:2, :]) for s in range(n_seq)]
    kts = (kt0_ref[...], kt1_ref[...])
    eif = eif_ref[...]
    w = kts[0].shape[0]
    head_eq = (lax.broadcasted_iota(jnp.int32, (w, w), 0) // ML_DH
               == lax.broadcasted_iota(jnp.int32, (w, w), 1) // ML_DH)
    for i in range(q_ref.shape[1] // ML_CHUNK):
        rows = pl.ds(i * ML_CHUNK, ML_CHUNK)
        for s in range(n_seq):
            kt_bd = jnp.where(head_eq, _dot(kts[s], sel_ref[i]), 0.0)
            y, *state[s] = _mlstm_chunk(q_ref[s, rows, :], k_ref[s, rows, :], kt_bd, v_ref[s, rows, :],
                                        o_ref[s, rows, :], g_ref[s, rows, :], eif, *state[s])
            y_ref[s, rows, :] = y.astype(y_ref.dtype)
    for s in range(n_seq):
        c_scr[s], nm_scr[s, 0:1, :], nm_scr[s, 1:2, :] = state[s]


MLSTM_CHUNKS_PER_STEP = 4
MLSTM_SEQS_PER_STEP = 2


def _mlstm(mq, mk, mkt, mv, mo, gcol, batch, seq):
    cps, nseq = MLSTM_CHUNKS_PER_STEP, MLSTM_SEQS_PER_STEP
    L = ML_CHUNK * cps
    nc = seq // L
    pairs = lambda a: a.reshape(batch // nseq, nseq, seq, a.shape[1])
    blk = lambda w: pl.BlockSpec((None, nseq, L, w), lambda b, c: (b, 0, c, 0))
    kt_blk = lambda s: pl.BlockSpec((GROUP_W, L), lambda b, c: (0, (nseq * b + s) * nc + c))
    const = lambda a: pl.BlockSpec(a.shape, lambda b, c: (0,) * a.ndim)
    gl = jnp.arange(LANE)[:, None]
    col = jnp.arange(2 * GROUP_W)[None, :]
    eif = (gl == (col // GROUP_W) * ML_HEADS + (col % GROUP_W) // ML_DH).astype(bf16)
    p = jnp.arange(L)[None, :, None]
    cc = jnp.arange(GROUP_W)[None, None, :]
    sel = (p == jnp.arange(cps)[:, None, None] * ML_CHUNK + cc % ML_CHUNK).astype(bf16)
    assert nseq == 2 and batch % nseq == 0
    y = pl.pallas_call(
        _mlstm_body,
        grid=(batch // nseq, nc),
        in_specs=[blk(256), blk(256), kt_blk(0), kt_blk(1), blk(256), blk(256), blk(128),
                  const(eif), const(sel)],
        out_specs=blk(GROUP_W),
        out_shape=jax.ShapeDtypeStruct((batch // nseq, nseq, seq, GROUP_W), bf16),
        scratch_shapes=[pltpu.VMEM((nseq, GROUP_W, GROUP_W), f32), pltpu.VMEM((nseq, 8, GROUP_W), f32)],
        compiler_params=_cparams(("arbitrary", "arbitrary")),
        name="mlstm",
    )(pairs(mq), pairs(mk), mkt, mkt, pairs(mv), pairs(mo), pairs(gcol), eif, sel)
    return y.reshape(batch * seq, GROUP_W)


def _route(scores, sel):
    ne, tm = sel.shape
    eid = lax.broadcasted_iota(jnp.int32, (ne, tm), 0)
    grp = eid // E_PER_GROUP
    big = ne + 1

    def top2(vals):
        m1 = jnp.max(vals, axis=0, keepdims=True)
        i1 = jnp.min(jnp.where(vals == m1, eid, big), axis=0, keepdims=True)
        rest = jnp.where(eid == i1, NEG, vals)
        m2 = jnp.max(rest, axis=0, keepdims=True)
        i2 = jnp.min(jnp.where(rest == m2, eid, big), axis=0, keepdims=True)
        return m1, i1, m2, i2

    best_score, best_grp = None, None
    for gi in range(N_GROUPS):
        m1, _, m2, _ = top2(jnp.where(grp == gi, sel, NEG))
        sc = m1 + m2
        if best_score is None:
            best_score, best_grp = sc, jnp.zeros_like(sc, dtype=jnp.int32)
        else:
            better = sc > best_score
            best_grp = jnp.where(better, gi, best_grp)
            best_score = jnp.where(better, sc, best_score)
    _, i1, _, i2 = top2(jnp.where(grp == best_grp, sel, NEG))
    picked = jnp.where((eid == i1) | (eid == i2), scores, 0.0)
    return picked / jnp.sum(picked, axis=0, keepdims=True), best_grp


MOE_ROW_TILE = 256


def _outproj_route_body(ya_ref, yb_ref, yc_ref, yd_ref, x_ref, mod_ref, w_ref, g_ref, b_ref,
                        wr_ref, br_ref, x1_ref, u2e_ref, grp_ref):
    d = x_ref.shape[1]
    mix = None
    for j, y_ref in enumerate((ya_ref, yb_ref, yc_ref, yd_ref)):
        part = _dot(y_ref[...], w_ref[GROUP_W * j:GROUP_W * (j + 1), :])
        mix = part if mix is None else mix + part
    x1 = _layer_norm(DN_ALPHA * x_ref[...] + mod_ref[0, 2:3, :] * mix) * g_ref[...] + b_ref[...]
    x1_ref[...] = x1
    u = _layer_norm(x1) * (1.0 + mod_ref[0, 4:5, :]) + mod_ref[0, 3:4, :]
    u_hi = u.astype(bf16)
    u_lo = (u - u_hi.astype(f32)).astype(bf16)
    ne = br_ref.shape[0]
    part = _dot_nt(wr_ref[...], u_hi)
    logits = part[:ne] + part[ne:] + _dot_nt(wr_ref[:ne, :], u_lo)
    scores = _sigmoid(logits)
    gates_t, best_grp = _route(scores, scores + br_ref[...])
    u2e_ref[:, :d] = u_hi.astype(f32)
    u2e_ref[:, d:] = jnp.zeros((x1.shape[0], LANE), f32)
    u2e_ref[:, d:d + ne] = gates_t.T
    grp_ref[...] = best_grp


def _outproj_route(ya, yb, yc, yd, x2d, mod, w_out, ln_g, ln_b, wr_split, b_router, seq):
    t, d = x2d.shape
    tm = 512
    tps = seq // tm
    row = lambda w: pl.BlockSpec((tm, w), lambda i: (i, 0))
    const = lambda a: pl.BlockSpec(a.shape, lambda i: (0,) * a.ndim)
    return pl.pallas_call(
        _outproj_route_body,
        grid=(t // tm,),
        in_specs=[row(256), row(256), row(256), row(256), row(d),
                  pl.BlockSpec((1, 6, d), lambda i: (i // tps, 0, 0)),
                  const(w_out), const(ln_g), const(ln_b), const(wr_split), const(b_router)],
        out_specs=[row(d), row(d + LANE), pl.BlockSpec((1, tm), lambda i: (0, i))],
        out_shape=[jax.ShapeDtypeStruct((t, d), f32), jax.ShapeDtypeStruct((t, d + LANE), f32),
                   jax.ShapeDtypeStruct((1, t), jnp.int32)],
        compiler_params=_cparams(("arbitrary",)),
        name="outproj_route",
    )(ya, yb, yc, yd, x2d, mod, w_out, ln_g, ln_b, wr_split, b_router)


def _dispatch_plan(grp, tr):
    t = grp.shape[0]
    rows = t + N_GROUPS * tr
    order = jnp.sort(grp * t + jnp.arange(t, dtype=jnp.int32)) % t
    order = jnp.concatenate([order, jnp.zeros((rows - t,), order.dtype)])
    counts = jnp.sum((grp[:, None] == jnp.arange(N_GROUPS)[None, :]).astype(jnp.int32), axis=0)
    padded = ((counts + tr - 1) // tr) * tr
    row_end = jnp.cumsum(padded)
    r = jnp.arange(rows, dtype=jnp.int32)
    read_tok = jnp.zeros((rows,), jnp.int32)
    write_row = jnp.zeros((rows,), jnp.int32)
    for g in range(N_GROUPS):
        row_start = row_end[g] - padded[g]
        tok_start = jnp.sum(counts[:g])
        shifted = jnp.roll(order, row_start - tok_start)
        in_grp = (r >= row_start) & (r < row_end[g])
        valid = in_grp & (r < row_start + counts[g])
        read_tok = jnp.where(valid, shifted, jnp.where(in_grp, order[jnp.minimum(tok_start, t - 1)], read_tok))
        write_row = jnp.where(valid, shifted, jnp.where(in_grp, t + g * tr + r % tr, write_row))
    tile_start = jnp.arange(rows // tr, dtype=jnp.int32) * tr
    tile_grp = jnp.minimum(jnp.sum((tile_start[:, None] >= row_end[None, :]).astype(jnp.int32), axis=1),
                           N_GROUPS - 1)
    i32 = lambda a: a.astype(jnp.int32)
    return i32(read_tok), i32(write_row), i32(tile_grp), i32(row_end[-1:] // tr)


def _wait_row_gather(n_rows, hbm, vmem, sem):
    pltpu.make_async_copy(hbm.at[pl.ds(0, n_rows), :], vmem, sem).wait()


def _wait_row_scatter(n_rows, vmem, hbm, sem):
    pltpu.make_async_copy(vmem, hbm.at[pl.ds(0, n_rows), :], sem).wait()


def _experts_body(read_ref, write_ref, tgrp_ref, ntile_ref,
                  u2e_hbm, w1_ref, w3_ref, w2_ref, ys_hbm, xbuf0, xbuf1, obuf0, obuf1, w1b, w3b, w2b,
                  gsem, ssem):
    j = pl.program_id(0)
    n_tiles = ntile_ref[0]
    tr, d = obuf0.shape
    xbuf, obuf = (xbuf0, xbuf1), (obuf0, obuf1)

    def start_gather(tile, s):
        for i in range(tr):
            pltpu.make_async_copy(u2e_hbm.at[pl.ds(read_ref[tile * tr + i], 1), :],
                                  xbuf[s].at[pl.ds(i, 1), :], gsem.at[s]).start()

    def start_scatter(tile, s):
        for i in range(tr):
            pltpu.make_async_copy(obuf[s].at[pl.ds(i, 1), :],
                                  ys_hbm.at[pl.ds(write_ref[tile * tr + i], 1), :], ssem.at[s]).start()

    def load_tile(s):
        xs = xbuf[s][...]
        return xs[:, :d].astype(bf16), xs[:, d:]

    def compute(tile, s, xb, gslab):
        lane = lax.broadcasted_iota(jnp.int32, gslab.shape, 1)
        first_expert = tgrp_ref[tile] * E_PER_GROUP
        acc = None
        for e in range(E_PER_GROUP):
            ge = jnp.sum(jnp.where(lane == first_expert + e, gslab, 0.0), axis=1, keepdims=True)
            h1 = _dot(xb, w1b[e])
            h = h1 * _sigmoid(h1) * _dot(xb, w3b[e]) * ge
            part = _dot(h.astype(bf16), w2b[e])
            acc = part if acc is None else acc + part
        obuf[s][...] = acc

    nxt = jnp.minimum(j + 1, n_tiles - 1)

    last = tgrp_ref.shape[0] - 1
    @pl.when((j < n_tiles) & ((j == 0) | (tgrp_ref[jnp.minimum(j, last)] != tgrp_ref[jnp.clip(j - 1, 0, last)])))
    def _():
        w1b[...] = w1_ref[0].astype(bf16)
        w3b[...] = w3_ref[0].astype(bf16)
        w2b[...] = w2_ref[0].astype(bf16)

    @pl.when(j == 0)
    def _():
        start_gather(0, 0)
        start_gather(nxt, 1)
        obuf[1][...] = jnp.zeros_like(obuf[1])
        n_tok = ys_hbm.shape[0] - N_GROUPS * tr
        spare = [pltpu.make_async_copy(obuf[1], ys_hbm.at[pl.ds(n_tok + g * tr, tr), :], ssem.at[1])
                 for g in range(N_GROUPS)]
        for cp in spare:
            cp.start()
        for cp in spare:
            cp.wait()
        _wait_row_gather(tr, u2e_hbm, xbuf[0], gsem.at[0])
        compute(0, 0, *load_tile(0))

    for s in (0, 1):
        @pl.when((j > 0) & (j < n_tiles) & (j % 2 == s))
        def _(s=s):
            start_gather(nxt, 1 - s)
            start_scatter(j - 1, 1 - s)

        @pl.when((j > 0) & (j < n_tiles) & (j % 2 == s))
        def _(s=s):
            _wait_row_gather(tr, u2e_hbm, xbuf[s], gsem.at[s])
            compute(j, s, *load_tile(s))
            _wait_row_scatter(tr, obuf[1 - s], ys_hbm, ssem.at[1 - s])

        @pl.when((j == n_tiles) & (j % 2 == s))
        def _(s=s):
            _wait_row_gather(tr, u2e_hbm, xbuf[s], gsem.at[s])
            start_scatter(j - 1, 1 - s)
            _wait_row_scatter(tr, obuf[1 - s], ys_hbm, ssem.at[1 - s])


def _experts(read_tok, write_row, tile_grp, n_tiles, u2e, w1, w3, w2):
    rows = read_tok.shape[0]
    tr = MOE_ROW_TILE
    max_tiles = tile_grp.shape[0]
    ne, d, fe = w1.shape
    grouped = lambda w: w.reshape(N_GROUPS, E_PER_GROUP, *w.shape[1:])
    wspec = lambda a, b: pl.BlockSpec(
        (1, E_PER_GROUP, a, b), lambda j, rd, wr, tg, nt: (tg[jnp.minimum(j, max_tiles - 1)], 0, 0, 0))
    return pl.pallas_call(
        _experts_body,
        grid_spec=pltpu.PrefetchScalarGridSpec(
            num_scalar_prefetch=4, grid=(max_tiles + 1,),
            in_specs=[pl.BlockSpec(memory_space=pl.ANY), wspec(d, fe), wspec(d, fe), wspec(fe, d)],
            out_specs=pl.BlockSpec(memory_space=pl.ANY),
            scratch_shapes=[pltpu.VMEM((tr, u2e.shape[1]), f32), pltpu.VMEM((tr, u2e.shape[1]), f32),
                            pltpu.VMEM((tr, d), f32), pltpu.VMEM((tr, d), f32),
                            pltpu.VMEM((E_PER_GROUP, d, fe), bf16), pltpu.VMEM((E_PER_GROUP, d, fe), bf16),
                            pltpu.VMEM((E_PER_GROUP, fe, d), bf16),
                            pltpu.SemaphoreType.DMA((2,)), pltpu.SemaphoreType.DMA((2,))]),
        out_shape=jax.ShapeDtypeStruct((rows, d), f32),
        compiler_params=_cparams(("arbitrary",)),
        name="experts",
    )(read_tok, write_row, tile_grp, n_tiles, u2e, grouped(w1), grouped(w3), grouped(w2))


def _moe_combine_body(y_ref, x_ref, mod_ref, g_ref, b_ref, o_ref):
    r = DN_ALPHA * x_ref[...] + mod_ref[0, 5:6, :] * y_ref[...]
    o_ref[...] = _layer_norm(r) * g_ref[...] + b_ref[...]


def _moe_combine(ys, x1, mod, ln_g, ln_b, seq):
    t, d = x1.shape
    tm = 512
    tps = seq // tm
    row = pl.BlockSpec((tm, d), lambda i: (i, 0))
    const = lambda a: pl.BlockSpec(a.shape, lambda i: (0,) * a.ndim)
    return pl.pallas_call(
        _moe_combine_body,
        grid=(t // tm,),
        in_specs=[row, row, pl.BlockSpec((1, 6, d), lambda i: (i // tps, 0, 0)), const(ln_g), const(ln_b)],
        out_specs=row,
        out_shape=jax.ShapeDtypeStruct((t, d), f32),
        compiler_params=_cparams(("arbitrary",)),
        name="moe_combine",
    )(ys, x1, mod, ln_g, ln_b)


def _pad_cols(a, width):
    return jnp.pad(a, ((0, 0), (0, width - a.shape[1])))


def _pack_w_in(w):
    parts, acc = [], 0
    for sz in IN_SIZES:
        parts.append(w[:, acc:acc + sz])
        acc += sz
    (a_q, a_kv, a_kr, m_q, m_k, m_v, m_i, m_f, m_o, c_b, c_c, c_h, d_q, d_k, d_v) = parts
    gates = _pad_cols(jnp.concatenate([m_i, m_f], axis=1), LANE)
    packed = jnp.concatenate([
        _pad_cols(a_q, 256), a_kv, _pad_cols(a_kr, LANE),
        m_q, m_k, m_v, gates, m_o, c_b, c_c, c_h, d_q, d_k, d_v], axis=1)
    return packed.astype(bf16)


def _pack_mla(w_uq, w_ukv):
    half = MLA_ROPE // 2
    wq3 = w_uq.reshape(MLA_Q_LORA, MLA_HEADS, MLA_NOPE + MLA_ROPE)
    wkv3 = w_ukv.reshape(MLA_KV_LORA, MLA_HEADS, MLA_NOPE + MLA_V)
    wq, wqr, wk, wv = [], [], [], []
    for h in range(MLA_HEADS):
        nope, pe = wq3[:, h, :MLA_NOPE], wq3[:, h, MLA_NOPE:]
        rot = jnp.concatenate([-pe[:, half:], pe[:, :half]], axis=1)
        zq = jnp.zeros((MLA_Q_LORA, MLA_NOPE), f32)
        pad_rows = lambda a: jnp.pad(a, ((0, 256 - MLA_Q_LORA), (0, 0)))
        wq.append(pad_rows(_pad_cols(jnp.concatenate([nope, pe], axis=1), LANE)))
        wqr.append(pad_rows(_pad_cols(jnp.concatenate([zq, rot], axis=1), LANE)))
        wk.append(_pad_cols(wkv3[:, h, :MLA_NOPE], LANE))
        wv.append(wkv3[:, h, MLA_NOPE:])
    st = lambda xs: jnp.stack(xs).astype(bf16)
    return st(wq), st(wqr), st(wk), st(wv)


def _rope_tables(seq):
    half = DF_DK // 2
    inv = 1.0 / (ROPE_THETA ** (jnp.arange(0, DF_DK, 2, dtype=f32) / DF_DK))
    ang = jnp.arange(seq, dtype=f32)[:, None] * inv[None, :]
    cos, sin = jnp.cos(ang), jnp.sin(ang)
    cos_t = jnp.tile(jnp.concatenate([cos, cos], axis=1), (1, 256 // DF_DK))
    sin_t = jnp.tile(jnp.concatenate([-sin, sin], axis=1), (1, 256 // DF_DK))
    scale = (MLA_NOPE + MLA_ROPE) ** -0.5 * LOG2E
    ones = jnp.ones((seq, MLA_NOPE), f32)
    zeros = jnp.zeros((seq, MLA_NOPE), f32)
    cq = _pad_cols(jnp.concatenate([ones, cos, cos], axis=1) * scale, LANE)
    sq = _pad_cols(jnp.concatenate([zeros, sin, sin], axis=1) * scale, LANE)
    return cos_t, sin_t, cq, sq


def kernel(x, c, w_ada, b_ada, w_in, mla_g_q, mla_g_kv, mla_w_uq, mla_w_ukv, ml_b_i, ml_b_f, sc_w,
           df_lq1, df_lk1, df_lq2, df_lk2, df_g, w_out, ln1_g, ln1_b, w_router, b_router, w1, w3, w2,
           ln2_g, ln2_b):
    batch, seq, d = x.shape
    depth = w_in.shape[0]
    assert MLA_ROPE == DF_DK, "both rotary blocks share one table"
    cos_t, sin_t, cq, sq = _rope_tables(seq)
    place = jnp.zeros((LANE, LANE), f32).at[jnp.arange(MLA_ROPE), MLA_NOPE + jnp.arange(MLA_ROPE)].set(1.0).astype(bf16)
    wr_hi = w_router.astype(bf16)
    wr_split = jnp.concatenate([wr_hi, (w_router - wr_hi.astype(f32)).astype(bf16)], axis=1).T
    mod_all = _ada_mod(c, w_ada, b_ada).reshape(depth, batch, 6, d)
    xf = x.reshape(batch * seq, d)
    for l in range(depth):
        mod = mod_all[l]
        w_packed = _pack_w_in(w_in[l])
        gq = _pad_cols(mla_g_q[l][None, :], 256)
        gkv = mla_g_kv[l][None, :]
        gbias = _pad_cols(jnp.concatenate([ml_b_i[l], ml_b_f[l]])[None, :], LANE)
        wkt = w_packed[:, OFF_MK:OFF_MK + GROUP_W].T
        (qn, kvn, kpe, mq, mk, mv, mo, gcol, yc, dq, dk, dv, mkt) = _inproj(
            xf, mod, w_packed, wkt, cos_t, sin_t, gq, gkv, gbias, sc_w[l], seq)
        wq, wqr, wk, wv = _pack_mla(mla_w_uq[l], mla_w_ukv[l])
        ya = _mla_attention(qn, kvn, kpe, wq, wqr, wk, wv, place, cq, sq, batch, seq)
        yb = _mlstm(mq, mk, mkt, mv, mo, gcol, batch, seq)
        lambda_init = 0.8 - 0.6 * math.exp(-0.3 * l)
        yd = _diff_attention(dq, dk, dv, df_lq1[l][None, :], df_lk1[l][None, :], df_lq2[l][None, :],
                             df_lk2[l][None, :], df_g[l][None, :],
                             batch, seq, lambda_init)
        x1, u2e, grp = _outproj_route(ya, yb, yc, yd, xf, mod, w_out[l].astype(bf16), ln1_g[l][None, :],
                                      ln1_b[l][None, :], wr_split, b_router[:, None], seq)
        plan = _dispatch_plan(grp[0], MOE_ROW_TILE)
        ys = _experts(*plan, u2e, w1[l], w3[l], w2[l])
        xf = _moe_combine(ys, x1, mod, ln2_g[l][None, :], ln2_b[l][None, :], seq)
    return xf.reshape(batch, seq, d)
```

```python
import functools
import math

import jax
import jax.numpy as jnp
from jax import lax
from jax.experimental import pallas as pl
from jax.experimental.pallas import tpu as pltpu

f32 = jnp.float32
bf16 = jnp.bfloat16
HIGHEST = lax.Precision.HIGHEST

GROUP_W = 256
MLA_HEADS, MLA_NOPE, MLA_ROPE, MLA_V = 4, 64, 32, 64
MLA_Q_LORA, MLA_KV_LORA = 192, 128
ML_HEADS, ML_DH, ML_CHUNK = 4, 64, 64
DF_HEADS, DF_DK, DF_V = 4, 32, 64
ROPE_THETA = 10000.0
N_EXPERTS, N_GROUPS, E_PER_GROUP, D_EXPERT = 16, 4, 4, 256
DEPTH = 2
DN_ALPHA = (2 * DEPTH) ** 0.25
LN_EPS = 1e-5
RMS_EPS = 1e-6
IN_SIZES = (192, 128, 32, 256, 256, 256, 4, 4, 256, 256, 256, 256, 256, 256, 256)

OFF_PQ, OFF_PKV, OFF_KR = 0, 256, 384
OFF_MQ, OFF_MK, OFF_MV, OFF_MG, OFF_MO = 512, 768, 1024, 1280, 1408
OFF_CB, OFF_CC, OFF_CH = 1664, 1920, 2176
OFF_DQ, OFF_DK, OFF_DV = 2432, 2688, 2944
IN_PACKED = 3200

LANE = 128
VMEM_LIMIT = 48 * 1024 * 1024
NEG = -1e30
LOG2E = math.log2(math.e)


def _cparams(sem):
    return pltpu.CompilerParams(dimension_semantics=sem, vmem_limit_bytes=VMEM_LIMIT)


def _sigmoid(x):
    return 1.0 / (1.0 + jnp.exp(-x))


def _layer_norm(x):
    mu = jnp.mean(x, axis=-1, keepdims=True)
    xc = x - mu
    var = jnp.mean(xc * xc, axis=-1, keepdims=True)
    return xc * lax.rsqrt(var + LN_EPS)


def _dot(a, b, **kw):
    return jnp.dot(a, b, preferred_element_type=f32, **kw)


def _dot_nt(a, b, **kw):
    return lax.dot_general(a, b, (((1,), (1,)), ((), ())), preferred_element_type=f32, **kw)


def _dot_tn(a, b, **kw):
    return lax.dot_general(a, b, (((0,), (0,)), ((), ())), preferred_element_type=f32, **kw)


def _expand_groups(cols, rows, width, group):
    lane = lax.broadcasted_iota(jnp.int32, (rows, width), 1)
    out = jnp.broadcast_to(cols[-1], (rows, width))
    for h in range(len(cols) - 2, -1, -1):
        out = jnp.where(lane < group * (h + 1), cols[h], out)
    return out


def _lane_group_mask(shape, lo, hi):
    lane = lax.broadcasted_iota(jnp.int32, shape, 1)
    return (lane >= lo) & (lane < hi)


def _ada_body(c_ref, w_ref, b_ref, o_ref):
    c = c_ref[...]
    ca = (c * _sigmoid(c)).astype(bf16)
    o_ref[0] = _dot(ca, w_ref[0].astype(bf16)) + b_ref[0]


def _ada_mod(c, w_ada, b_ada):
    depth, d, n = w_ada.shape
    b = c.shape[0]
    tn = 1536
    return pl.pallas_call(
        _ada_body,
        grid=(depth, n // tn),
        in_specs=[
            pl.BlockSpec((b, d), lambda l, j: (0, 0)),
            pl.BlockSpec((1, d, tn), lambda l, j: (l, 0, j)),
            pl.BlockSpec((1, 1, tn), lambda l, j: (l, 0, j)),
        ],
        out_specs=pl.BlockSpec((1, b, tn), lambda l, j: (l, 0, j)),
        out_shape=jax.ShapeDtypeStruct((depth, b, n), f32),
        compiler_params=_cparams(("arbitrary", "arbitrary")),
        name="ada_mod",
    )(c, w_ada, b_ada.reshape(depth, 1, n))


def _rope_lanes(x, cos, sin_signed):
    w = x.shape[1]
    lane = lax.broadcasted_iota(jnp.int32, x.shape, 1)
    rot = jnp.where(lane % 32 < 16, pltpu.roll(x, w - 16, axis=1), pltpu.roll(x, 16, axis=1))
    return x * cos + rot * sin_signed


def _inproj_body(x_ref, mod_ref, w_ref, wkt_ref, cos_ref, sin_ref, gq_ref, gkv_ref, gb_ref, cw_ref,
                 qn_ref, kvn_ref, kpe_ref, mq_ref, mk_ref, mv_ref, mo_ref, gcol_ref, yc_ref,
                 dq_ref, dk_ref, dv_ref, mkt_ref, carry_ref, *, tiles_per_seq):
    i = pl.program_id(0)
    tm = x_ref.shape[0]
    u = _layer_norm(x_ref[...]) * (1.0 + mod_ref[0, 1:2, :]) + mod_ref[0, 0:1, :]
    ub = u.astype(bf16)

    def seg(off, n):
        return _dot(ub, w_ref[:, off:off + n])

    cos = cos_ref[...]
    sin = sin_ref[...]

    pq = seg(OFF_PQ, 256)
    ms = jnp.sum(pq * pq, axis=-1, keepdims=True) * (1.0 / MLA_Q_LORA)
    qn_ref[...] = (pq * lax.rsqrt(ms + RMS_EPS) * gq_ref[...]).astype(bf16)
    pkv = seg(OFF_PKV, 128)
    ms = jnp.sum(pkv * pkv, axis=-1, keepdims=True) * (1.0 / MLA_KV_LORA)
    kvn_ref[...] = (pkv * lax.rsqrt(ms + RMS_EPS) * gkv_ref[...]).astype(bf16)
    kpe_ref[...] = _rope_lanes(seg(OFF_KR, 128), cos[:, :LANE], sin[:, :LANE]).astype(bf16)

    mq_ref[...] = (seg(OFF_MQ, 256) * (ML_DH ** -0.5)).astype(bf16)
    mk_ref[...] = seg(OFF_MK, 256).astype(bf16)
    mkt_ref[...] = _dot_nt(wkt_ref[...], ub).astype(bf16)
    mv_ref[...] = seg(OFF_MV, 256).astype(bf16)
    gcol_ref[...] = seg(OFF_MG, 128) + gb_ref[...]
    mo_ref[...] = seg(OFF_MO, 256).astype(bf16)

    uc = seg(OFF_CC, 256) * seg(OFF_CH, 256)

    @pl.when(i % tiles_per_seq == 0)
    def _():
        carry_ref[...] = jnp.zeros_like(carry_ref)

    prev = carry_ref[...]
    row =lax.broadcasted_iota(jnp.int32, uc.shape, 0)
    u1 = jnp.where(row == 0, prev[7:8, :], pltpu.roll(uc, 1, axis=0))
    u2 = jnp.where(row == 0, prev[6:7, :], jnp.where(row == 1, prev[7:8, :], pltpu.roll(uc, 2, axis=0)))
    carry_ref[...] = uc[tm - 8:, :]
    conv = cw_ref[0:1, :] * u2 + cw_ref[1:2, :] * u1 + cw_ref[2:3, :] * uc
    yc_ref[...] = (seg(OFF_CB, 256) * conv).astype(bf16)

    dq_ref[...] = (_rope_lanes(seg(OFF_DQ, 256), cos, sin) * (DF_DK ** -0.5 * LOG2E)).astype(bf16)
    dk_ref[...] = _rope_lanes(seg(OFF_DK, 256), cos, sin).astype(bf16)
    dv_ref[...] = seg(OFF_DV, 256).astype(bf16)


def _inproj(x2d, mod, w_packed, wkt, cos_t, sin_t, gq, gkv, gbias, conv_w, seq):
    t, d = x2d.shape
    tm = 512
    tps = seq // tm
    row = lambda w: pl.BlockSpec((tm, w), lambda i: (i, 0))
    const = lambda a: pl.BlockSpec(a.shape, lambda i: (0,) * a.ndim)
    widths = (256, 128, 128, 256, 256, 256, 256, 128, 256, 256, 256, 256)
    dtypes = (bf16, bf16, bf16, bf16, bf16, bf16, bf16, f32, bf16, bf16, bf16, bf16)
    return pl.pallas_call(
        functools.partial(_inproj_body, tiles_per_seq=tps),
        grid=(t // tm,),
        in_specs=[
            row(d),
            pl.BlockSpec((1, 6, d), lambda i: (i // tps, 0, 0)),
            const(w_packed), const(wkt),
            pl.BlockSpec((tm, 256), lambda i: (i % tps, 0)),
            pl.BlockSpec((tm, 256), lambda i: (i % tps, 0)),
            const(gq), const(gkv), const(gbias), const(conv_w),
        ],
        out_specs=[row(w) for w in widths] + [pl.BlockSpec((GROUP_W, tm), lambda i: (0, i))],
        out_shape=[jax.ShapeDtypeStruct((t, w), dt) for w, dt in zip(widths, dtypes)]
        + [jax.ShapeDtypeStruct((GROUP_W, t), bf16)],
        scratch_shapes=[pltpu.VMEM((8, 256), f32)],
        compiler_params=_cparams(("arbitrary",)),
        name="inproj",
    )(x2d, mod, w_packed, wkt, cos_t, sin_t, gq, gkv, gbias, conv_w)


def _causal_softmax_heads(q_heads, k_at, v_at, qi, s_scr, m_scr, acc_scr):
    n_heads = len(q_heads)
    tq = q_heads[0].shape[0]
    half = tq // 2
    rows = pl.ds(pl.multiple_of(qi * tq, tq), tq)

    def scores(g, j):
        return _dot_nt(q_heads[g], k_at(g, pl.multiple_of(j * tq, tq)))

    def fold_max(g, s):
        m_scr[g] = jnp.maximum(m_scr[g], jnp.maximum(s[:, :half], s[:, half:]))

    m_scr[...] = jnp.full(m_scr.shape, NEG, f32)
    for g in range(n_heads):
        acc_scr[g, rows, :] = jnp.zeros((tq, LANE), f32)

    def score_blocks(js):
        for g in range(n_heads):
            mx = None
            for j in js:
                s = scores(g, j)
                s_scr[g, j] = s
                s = jnp.maximum(s[:, :half], s[:, half:])
                mx = s if mx is None else jnp.maximum(mx, s)
            m_scr[g] = jnp.maximum(m_scr[g], mx)

    def prob_blocks(js):
        for g in range(n_heads):
            mb = m_scr[g]
            pv = None
            for j in js:
                sj = s_scr[g, j]
                p = jnp.concatenate([jnp.exp2(sj[:, :half] - mb), jnp.exp2(sj[:, half:] - mb)],
                                    axis=1).astype(bf16)
                d = _dot(p, v_at(g, pl.multiple_of(j * tq, tq)))
                pv = d if pv is None else pv + d
            acc_scr[g, rows, :] += pv

    def over_full_blocks(blocks_fn):
        @pl.loop(0, qi // 2)
        def _(jj):
            blocks_fn((2 * jj, 2 * jj + 1))

        @pl.when(qi % 2 == 1)
        def _():
            blocks_fn((qi - 1,))

    over_full_blocks(score_blocks)
    rowi = lax.broadcasted_iota(jnp.int32, (tq, tq), 0)
    coli = lax.broadcasted_iota(jnp.int32, (tq, tq), 1)
    for g in range(n_heads):
        s = jnp.where(coli <= rowi, scores(g, qi), NEG)
        s_scr[g, qi] = s
        fold_max(g, s)
        m_scr[g] = jnp.broadcast_to(jnp.max(m_scr[g], axis=1, keepdims=True), (tq, half))

    over_full_blocks(prob_blocks)
    prob_blocks((qi,))


def _softmax_scratch(n_heads, nq, tq):
    return [pltpu.VMEM((n_heads, nq, tq, tq), f32), pltpu.VMEM((n_heads, tq, tq // 2), f32),
            pltpu.VMEM((n_heads, nq * tq, LANE), f32)]


EPILOGUE_ROWS = 512


def _with_ones_lanes(v, dv):
    return jnp.concatenate([v.astype(bf16), jnp.ones((v.shape[0], LANE - dv), bf16)], axis=1)


def _softmax_normalise(acc, dv):
    lane = lax.broadcasted_iota(jnp.int32, acc.shape, 1)
    return jnp.where(lane < dv, acc / pltpu.roll(acc, LANE - dv, axis=1), 0.0)


def _pair_lanes(a, b, dv):
    lane = lax.broadcasted_iota(jnp.int32, a.shape, 1)
    return jnp.where(lane < dv, a, pltpu.roll(b, dv, axis=1))


def _mla_body(qn_ref, kvn_ref, kpe_ref, wq_ref, wqr_ref, wk_ref, wv_ref, place_ref, cq_ref, sq_ref,
              o_ref, k_scr, v_scr, s_scr, m_scr, acc_scr):
    qi = pl.program_id(1)
    tq = qn_ref.shape[0]

    @pl.when(qi == 0)
    def _():
        kvn = kvn_ref[...]
        kpe_placed = _dot(kpe_ref[...], place_ref[...])
        for h in range(MLA_HEADS):
            k_scr[h] = (_dot(kvn, wk_ref[h]) + kpe_placed).astype(bf16)
            v_scr[h] = _with_ones_lanes(_dot(kvn, wv_ref[h]), MLA_V)

    qn = qn_ref[...]
    cq = cq_ref[...]
    sq = sq_ref[...]
    q_heads = [(_dot(qn, wq_ref[h]) * cq + _dot(qn, wqr_ref[h]) * sq).astype(bf16)
               for h in range(MLA_HEADS)]
    _causal_softmax_heads(
        q_heads, lambda g, start: k_scr[g, pl.ds(start, tq), :],
        lambda g, start: v_scr[g, pl.ds(start, tq), :], qi, s_scr, m_scr, acc_scr)

    @pl.when(qi == pl.num_programs(1) - 1)
    def _():
        for r0 in range(0, o_ref.shape[0], EPILOGUE_ROWS):
            rows = pl.ds(r0, EPILOGUE_ROWS)
            o = [_softmax_normalise(acc_scr[h, rows, :], MLA_V) for h in range(MLA_HEADS)]
            o_ref[rows, :] = jnp.concatenate(
                [_pair_lanes(o[0], o[1], MLA_V), _pair_lanes(o[2], o[3], MLA_V)], axis=1).astype(bf16)


def _mla_attention(qn, kvn, kpe, wq, wqr, wk, wv, place, cq, sq, batch, seq):
    tq = 256
    nq = seq // tq
    const = lambda a: pl.BlockSpec(a.shape, lambda b, i: (0,) * a.ndim)
    return pl.pallas_call(
        _mla_body,
        grid=(batch, nq),
        in_specs=[
            pl.BlockSpec((tq, 256), lambda b, i: (b * nq + i, 0)),
            pl.BlockSpec((seq, 128), lambda b, i: (b, 0)),
            pl.BlockSpec((seq, 128), lambda b, i: (b, 0)),
            const(wq), const(wqr), const(wk), const(wv), const(place),
            pl.BlockSpec((tq, 128), lambda b, i: (i, 0)),
            pl.BlockSpec((tq, 128), lambda b, i: (i, 0)),
        ],
        out_specs=pl.BlockSpec((seq, GROUP_W), lambda b, i: (b, 0)),
        out_shape=jax.ShapeDtypeStruct((batch * seq, GROUP_W), bf16),
        scratch_shapes=[pltpu.VMEM((MLA_HEADS, seq, 128), bf16),
                        pltpu.VMEM((MLA_HEADS, seq, LANE), bf16)]
        + _softmax_scratch(MLA_HEADS, nq, tq),
        compiler_params=_cparams(("arbitrary", "arbitrary")),
        name="mla_attn",
    )(qn, kvn, kpe, wq, wqr, wk, wv, place, cq, sq)


def _diff_body(q_ref, k_ref, v_ref, lq1_ref, lk1_ref, lq2_ref, lk2_ref, g_ref,
               o_ref, v_scr, s_scr, m_scr, acc_scr, *, lambda_init):
    qi = pl.program_id(1)
    tq = q_ref.shape[0]

    @pl.when(qi == 0)
    def _():
        v = v_ref[...]
        for h in range(DF_HEADS):
            v_scr[h] = _with_ones_lanes(v[:, DF_V * h:DF_V * (h + 1)], DF_V)

    q = q_ref[...]
    q_heads = [jnp.where(_lane_group_mask(q.shape, DF_DK * g, DF_DK * (g + 1)), q, jnp.zeros_like(q))
               for g in range(2 * DF_HEADS)]
    _causal_softmax_heads(
        q_heads, lambda g, start: k_ref[pl.ds(start, tq), :],
        lambda g, start: v_scr[g // 2, pl.ds(start, tq), :], qi, s_scr, m_scr, acc_scr)

    @pl.when(qi == pl.num_programs(1) - 1)
    def _():
        lam = (jnp.exp(jnp.sum(lq1_ref[...] * lk1_ref[...], axis=1, keepdims=True))
               - jnp.exp(jnp.sum(lq2_ref[...] * lk2_ref[...], axis=1, keepdims=True)) + lambda_init)
        gain = g_ref[...] * (1.0 - lambda_init)
        for r0 in range(0, o_ref.shape[0], EPILOGUE_ROWS):
            rows = pl.ds(r0, EPILOGUE_ROWS)
            ys = []
            for h in range(DF_HEADS):
                o = (_softmax_normalise(acc_scr[2 * h, rows, :], DF_V)
                     - lam * _softmax_normalise(acc_scr[2 * h + 1, rows, :], DF_V))
                ms = jnp.sum(o * o, axis=1, keepdims=True) * (1.0 / DF_V)
                ys.append(o * lax.rsqrt(ms + RMS_EPS) * gain)
            o_ref[rows, :] = jnp.concatenate(
                [_pair_lanes(ys[0], ys[1], DF_V), _pair_lanes(ys[2], ys[3], DF_V)], axis=1).astype(bf16)


def _diff_attention(dq, dk, dv, lq1, lk1, lq2, lk2, g_sub, batch, seq, lambda_init):
    tq = 256
    nq = seq // tq
    const = lambda a: pl.BlockSpec(a.shape, lambda b, i: (0,) * a.ndim)
    return pl.pallas_call(
        functools.partial(_diff_body, lambda_init=lambda_init),
        grid=(batch, nq),
        in_specs=[
            pl.BlockSpec((tq, 256), lambda b, i: (b * nq + i, 0)),
            pl.BlockSpec((seq, 256), lambda b, i: (b, 0)),
            pl.BlockSpec((seq, 256), lambda b, i: (b, 0)),
            const(lq1), const(lk1), const(lq2), const(lk2), const(g_sub),
        ],
        out_specs=pl.BlockSpec((seq, GROUP_W), lambda b, i: (b, 0)),
        out_shape=jax.ShapeDtypeStruct((batch * seq, GROUP_W), bf16),
        scratch_shapes=[pltpu.VMEM((DF_HEADS, seq, LANE), bf16)]
        + _softmax_scratch(2 * DF_HEADS, nq, tq),
        compiler_params=_cparams(("arbitrary", "arbitrary")),
        name="diff_attn",
    )(dq, dk, dv, lq1, lk1, lq2, lk2, g_sub)


def _split3(a):
    hi = a.astype(bf16)
    r1 = a - hi.astype(f32)
    mid = r1.astype(bf16)
    return hi, mid, (r1 - mid.astype(f32)).astype(bf16)


def _dot_exact_rhs01(a, b01):
    hi, mid, lo = _split3(a)
    return _dot(hi, b01) + _dot(mid, b01) + _dot(lo, b01)


def _dot_exact_lhs01(a01, b):
    hi, mid, lo = _split3(b)
    return _dot(a01, hi) + _dot(a01, mid) + _dot(a01, lo)


def _mlstm_chunk(q, k, kt_bd, v, o, g, eif, c_prev, n_prev, m_prev):
    L, W = q.shape
    row = lax.broadcasted_iota(jnp.int32, (L, W), 0)
    pos = lax.broadcasted_iota(jnp.int32, (L, W), 1) % L
    lf = jnp.minimum(g, 0.0) - jnp.log(1.0 + jnp.exp(-jnp.abs(g)))
    glane = lax.broadcasted_iota(jnp.int32, g.shape, 1)
    wide = _dot_exact_rhs01(jnp.where(glane < ML_HEADS, g, lf), eif)
    iw, lfw = wide[:, :W], wide[:, W:]
    tril = (lax.broadcasted_iota(jnp.int32, (L, L), 0) >= lax.broadcasted_iota(jnp.int32, (L, L), 1))
    bw = _dot_exact_lhs01(tril.astype(bf16), lfw)
    rw = iw - bw
    cm = rw
    for sh in (1, 2, 4, 8, 16, 32):
        cm = jnp.maximum(cm, jnp.where(row >= sh, pltpu.roll(cm, sh, axis=0), NEG))
    mw = jnp.maximum(m_prev, cm)
    a_inter = jnp.exp(m_prev - mw)
    r_row = jnp.sum(jnp.where(pos == row, rw, 0.0), axis=0, keepdims=True)
    ew = jnp.exp(jnp.where(pos <= row, r_row - mw, NEG))
    pw = _dot(q, kt_bd.astype(bf16)) * ew
    head_eq = (lax.broadcasted_iota(jnp.int32, (W, W), 0) // ML_DH
               == lax.broadcasted_iota(jnp.int32, (W, W), 1) // ML_DH)
    v4 = jnp.concatenate([v] * ML_HEADS, axis=0)
    vbd = jnp.where(head_eq, v4, jnp.zeros_like(v4))
    obd = head_eq.astype(bf16)
    den = a_inter * _dot_exact_rhs01(q.astype(f32) * n_prev, obd) + _dot_exact_rhs01(pw, obd)
    hden = jnp.maximum(jnp.abs(den), jnp.exp(-(bw + mw)))
    num = a_inter * _dot(q, c_prev.astype(bf16)) + _dot(pw.astype(bf16), vbd)
    y = num / hden * _sigmoid(o.astype(f32))

    rmax, m_last = cm[L - 1:L, :], mw[L - 1:L, :]
    a_w = jnp.exp(m_prev - m_last)
    b_w = jnp.exp(rmax - m_last)
    c_loc = _dot((kt_bd * jnp.exp(r_row - rmax)).astype(bf16), vbd)
    n_loc = jnp.sum(k.astype(f32) * jnp.exp(rw - rmax), axis=0, keepdims=True)
    return y, c_prev * a_w + c_loc * b_w, a_w * n_prev + b_w * n_loc, bw[L - 1:L, :] + m_last


def _mlstm_body(q_ref, k_ref, kt_ref, v_ref, o_ref, g_ref, eif_ref, sel_ref, y_ref, c_scr, nm_scr):
    @pl.when(pl.program_id(1) == 0)
    def _():
        c_scr[...] = jnp.zeros_like(c_scr)
        nm_scr[...] = jnp.zeros_like(nm_scr)

    c, n, m = c_scr[...], nm_scr[0:1, :], nm_scr[1:2, :]
    kt = kt_ref[...]
    eif = eif_ref[...]
    w = kt.shape[0]
    head_eq = (lax.broadcasted_iota(jnp.int32, (w, w), 0) // ML_DH
               == lax.broadcasted_iota(jnp.int32, (w, w), 1) // ML_DH)
    for i in range(q_ref.shape[0] // ML_CHUNK):
        rows = pl.ds(i * ML_CHUNK, ML_CHUNK)
        kt_bd = jnp.where(head_eq, _dot(kt, sel_ref[i]), 0.0)
        y, c, n, m = _mlstm_chunk(q_ref[rows, :], k_ref[rows, :], kt_bd, v_ref[rows, :], o_ref[rows, :],
                                  g_ref[rows, :], eif, c, n, m)
        y_ref[rows, :] = y.astype(y_ref.dtype)
    c_scr[...] = c
    nm_scr[0:1, :] = n
    nm_scr[1:2, :] = m


MLSTM_CHUNKS_PER_STEP = 4


def _mlstm(mq, mk, mkt, mv, mo, gcol, batch, seq):
    cps = MLSTM_CHUNKS_PER_STEP
    L = ML_CHUNK * cps
    nc = seq // L
    blk = lambda w: pl.BlockSpec((L, w), lambda b, c: (b * nc + c, 0))
    const = lambda a: pl.BlockSpec(a.shape, lambda b, c: (0,) * a.ndim)
    gl = jnp.arange(LANE)[:, None]
    col = jnp.arange(2 * GROUP_W)[None, :]
    eif = (gl == (col // GROUP_W) * ML_HEADS + (col % GROUP_W) // ML_DH).astype(bf16)
    p = jnp.arange(L)[None, :, None]
    cc = jnp.arange(GROUP_W)[None, None, :]
    sel = (p == jnp.arange(cps)[:, None, None] * ML_CHUNK + cc % ML_CHUNK).astype(bf16)
    return pl.pallas_call(
        _mlstm_body,
        grid=(batch, nc),
        in_specs=[blk(256), blk(256), pl.BlockSpec((GROUP_W, L), lambda b, c: (0, b * nc + c)),
                  blk(256), blk(256), blk(128), const(eif), const(sel)],
        out_specs=blk(GROUP_W),
        out_shape=jax.ShapeDtypeStruct((batch * seq, GROUP_W), bf16),
        scratch_shapes=[pltpu.VMEM((GROUP_W, GROUP_W), f32), pltpu.VMEM((8, GROUP_W), f32)],
        compiler_params=_cparams(("arbitrary", "arbitrary")),
        name="mlstm",
    )(mq, mk, mkt, mv, mo, gcol, eif, sel)


def _route(scores, sel):
    ne, tm = sel.shape
    eid = lax.broadcasted_iota(jnp.int32, (ne, tm), 0)
    grp = eid // E_PER_GROUP
    big = ne + 1

    def top2(vals):
        m1 = jnp.max(vals, axis=0, keepdims=True)
        i1 = jnp.min(jnp.where(vals == m1, eid, big), axis=0, keepdims=True)
        rest = jnp.where(eid == i1, NEG, vals)
        m2 = jnp.max(rest, axis=0, keepdims=True)
        i2 = jnp.min(jnp.where(rest == m2, eid, big), axis=0, keepdims=True)
        return m1, i1, m2, i2

    best_score, best_grp = None, None
    for gi in range(N_GROUPS):
        m1, _, m2, _ = top2(jnp.where(grp == gi, sel, NEG))
        sc = m1 + m2
        if best_score is None:
            best_score, best_grp = sc, jnp.zeros_like(sc, dtype=jnp.int32)
        else:
            better = sc > best_score
            best_grp = jnp.where(better, gi, best_grp)
            best_score = jnp.where(better, sc, best_score)
    _, i1, _, i2 = top2(jnp.where(grp == best_grp, sel, NEG))
    picked = jnp.where((eid == i1) | (eid == i2), scores, 0.0)
    return picked / jnp.sum(picked, axis=0, keepdims=True), best_grp


MOE_ROW_TILE = 256


def _outproj_route_body(ya_ref, yb_ref, yc_ref, yd_ref, x_ref, mod_ref, w_ref, g_ref, b_ref,
                        wr_ref, br_ref, x1_ref, u2e_ref, grp_ref):
    d = x_ref.shape[1]
    mix = None
    for j, y_ref in enumerate((ya_ref, yb_ref, yc_ref, yd_ref)):
        part = _dot(y_ref[...], w_ref[GROUP_W * j:GROUP_W * (j + 1), :])
        mix = part if mix is None else mix + part
    x1 = _layer_norm(DN_ALPHA * x_ref[...] + mod_ref[0, 2:3, :] * mix) * g_ref[...] + b_ref[...]
    x1_ref[...] = x1
    u = _layer_norm(x1) * (1.0 + mod_ref[0, 4:5, :]) + mod_ref[0, 3:4, :]
    u_hi = u.astype(bf16)
    u_lo = (u - u_hi.astype(f32)).astype(bf16)
    ne = br_ref.shape[0]
    part = _dot_nt(wr_ref[...], u_hi)
    logits = part[:ne] + part[ne:] + _dot_nt(wr_ref[:ne, :], u_lo)
    scores = _sigmoid(logits)
    gates_t, best_grp = _route(scores, scores + br_ref[...])
    u2e_ref[:, :d] = u_hi.astype(f32)
    u2e_ref[:, d:] = jnp.zeros((x1.shape[0], LANE), f32)
    u2e_ref[:, d:d + ne] = gates_t.T
    grp_ref[...] = best_grp


def _outproj_route(ya, yb, yc, yd, x2d, mod, w_out, ln_g, ln_b, wr_split, b_router, seq):
    t, d = x2d.shape
    tm = 512
    tps = seq // tm
    row = lambda w: pl.BlockSpec((tm, w), lambda i: (i, 0))
    const = lambda a: pl.BlockSpec(a.shape, lambda i: (0,) * a.ndim)
    return pl.pallas_call(
        _outproj_route_body,
        grid=(t // tm,),
        in_specs=[row(256), row(256), row(256), row(256), row(d),
                  pl.BlockSpec((1, 6, d), lambda i: (i // tps, 0, 0)),
                  const(w_out), const(ln_g), const(ln_b), const(wr_split), const(b_router)],
        out_specs=[row(d), row(d + LANE), pl.BlockSpec((1, tm), lambda i: (0, i))],
        out_shape=[jax.ShapeDtypeStruct((t, d), f32), jax.ShapeDtypeStruct((t, d + LANE), f32),
                   jax.ShapeDtypeStruct((1, t), jnp.int32)],
        compiler_params=_cparams(("arbitrary",)),
        name="outproj_route",
    )(ya, yb, yc, yd, x2d, mod, w_out, ln_g, ln_b, wr_split, b_router)


def _dispatch_plan(grp, tr):
    t = grp.shape[0]
    rows = t + N_GROUPS * tr
    order = jnp.sort(grp * t + jnp.arange(t, dtype=jnp.int32)) % t
    order = jnp.concatenate([order, jnp.zeros((rows - t,), order.dtype)])
    counts = jnp.sum((grp[:, None] == jnp.arange(N_GROUPS)[None, :]).astype(jnp.int32), axis=0)
    padded = ((counts + tr - 1) // tr) * tr
    row_end = jnp.cumsum(padded)
    r = jnp.arange(rows, dtype=jnp.int32)
    read_tok = jnp.zeros((rows,), jnp.int32)
    write_row = jnp.zeros((rows,), jnp.int32)
    for g in range(N_GROUPS):
        row_start = row_end[g] - padded[g]
        tok_start = jnp.sum(counts[:g])
        shifted = jnp.roll(order, row_start - tok_start)
        in_grp = (r >= row_start) & (r < row_end[g])
        valid = in_grp & (r < row_start + counts[g])
        read_tok = jnp.where(valid, shifted, jnp.where(in_grp, order[jnp.minimum(tok_start, t - 1)], read_tok))
        write_row = jnp.where(valid, shifted, jnp.where(in_grp, t + g * tr + r % tr, write_row))
    tile_start = jnp.arange(rows // tr, dtype=jnp.int32) * tr
    tile_grp = jnp.minimum(jnp.sum((tile_start[:, None] >= row_end[None, :]).astype(jnp.int32), axis=1),
                           N_GROUPS - 1)
    i32 = lambda a: a.astype(jnp.int32)
    return i32(read_tok), i32(write_row), i32(tile_grp), i32(row_end[-1:] // tr)


def _wait_row_gather(n_rows, hbm, vmem, sem):
    pltpu.make_async_copy(hbm.at[pl.ds(0, n_rows), :], vmem, sem).wait()


def _wait_row_scatter(n_rows, vmem, hbm, sem):
    pltpu.make_async_copy(vmem, hbm.at[pl.ds(0, n_rows), :], sem).wait()


def _experts_body(read_ref, write_ref, tgrp_ref, ntile_ref,
                  u2e_hbm, w1_ref, w3_ref, w2_ref, ys_hbm, xbuf0, xbuf1, obuf0, obuf1, w1b, w3b, w2b,
                  gsem, ssem):
    j = pl.program_id(0)
    n_tiles = ntile_ref[0]
    tr, d = obuf0.shape
    xbuf, obuf = (xbuf0, xbuf1), (obuf0, obuf1)

    def start_gather(tile, s):
        for i in range(tr):
            pltpu.make_async_copy(u2e_hbm.at[pl.ds(read_ref[tile * tr + i], 1), :],
                                  xbuf[s].at[pl.ds(i, 1), :], gsem.at[s]).start()

    def start_scatter(tile, s):
        for i in range(tr):
            pltpu.make_async_copy(obuf[s].at[pl.ds(i, 1), :],
                                  ys_hbm.at[pl.ds(write_ref[tile * tr + i], 1), :], ssem.at[s]).start()

    def load_tile(s):
        xs = xbuf[s][...]
        return xs[:, :d].astype(bf16), xs[:, d:]

    def compute(tile, s, xb, gslab):
        lane = lax.broadcasted_iota(jnp.int32, gslab.shape, 1)
        first_expert = tgrp_ref[tile] * E_PER_GROUP
        acc = None
        for e in range(E_PER_GROUP):
            ge = jnp.sum(jnp.where(lane == first_expert + e, gslab, 0.0), axis=1, keepdims=True)
            h1 = _dot(xb, w1b[e])
            h = h1 * _sigmoid(h1) * _dot(xb, w3b[e]) * ge
            part = _dot(h.astype(bf16), w2b[e])
            acc = part if acc is None else acc + part
        obuf[s][...] = acc

    nxt = jnp.minimum(j + 1, n_tiles - 1)

    last = tgrp_ref.shape[0] - 1
    @pl.when((j < n_tiles) & ((j == 0) | (tgrp_ref[jnp.minimum(j, last)] != tgrp_ref[jnp.clip(j - 1, 0, last)])))
    def _():
        w1b[...] = w1_ref[0].astype(bf16)
        w3b[...] = w3_ref[0].astype(bf16)
        w2b[...] = w2_ref[0].astype(bf16)

    @pl.when(j == 0)
    def _():
        start_gather(0, 0)
        start_gather(nxt, 1)
        obuf[1][...] = jnp.zeros_like(obuf[1])
        n_tok = ys_hbm.shape[0] - N_GROUPS * tr
        spare = [pltpu.make_async_copy(obuf[1], ys_hbm.at[pl.ds(n_tok + g * tr, tr), :], ssem.at[1])
                 for g in range(N_GROUPS)]
        for cp in spare:
            cp.start()
        for cp in spare:
            cp.wait()
        _wait_row_gather(tr, u2e_hbm, xbuf[0], gsem.at[0])
        compute(0, 0, *load_tile(0))

    for s in (0, 1):
        @pl.when((j > 0) & (j < n_tiles) & (j % 2 == s))
        def _(s=s):
            start_gather(nxt, 1 - s)
            start_scatter(j - 1, 1 - s)

        @pl.when((j > 0) & (j < n_tiles) & (j % 2 == s))
        def _(s=s):
            _wait_row_gather(tr, u2e_hbm, xbuf[s], gsem.at[s])
            compute(j, s, *load_tile(s))
            _wait_row_scatter(tr, obuf[1 - s], ys_hbm, ssem.at[1 - s])

        @pl.when((j == n_tiles) & (j % 2 == s))
        def _(s=s):
            _wait_row_gather(tr, u2e_hbm, xbuf[s], gsem.at[s])
            start_scatter(j - 1, 1 - s)
            _wait_row_scatter(tr, obuf[1 - s], ys_hbm, ssem.at[1 - s])


def _experts(read_tok, write_row, tile_grp, n_tiles, u2e, w1, w3, w2, layer):
    rows = read_tok.shape[0]
    tr = MOE_ROW_TILE
    max_tiles = tile_grp.shape[0]
    _, ne, d, fe = w1.shape
    grouped = lambda w: w.reshape(w.shape[0], N_GROUPS, E_PER_GROUP, *w.shape[2:])
    wspec = lambda a, b: pl.BlockSpec(
        (None, 1, E_PER_GROUP, a, b),
        lambda j, rd, wr, tg, nt: (layer, tg[jnp.minimum(j, max_tiles - 1)], 0, 0, 0))
    return pl.pallas_call(
        _experts_body,
        grid_spec=pltpu.PrefetchScalarGridSpec(
            num_scalar_prefetch=4, grid=(max_tiles + 1,),
            in_specs=[pl.BlockSpec(memory_space=pl.ANY), wspec(d, fe), wspec(d, fe), wspec(fe, d)],
            out_specs=pl.BlockSpec(memory_space=pl.ANY),
            scratch_shapes=[pltpu.VMEM((tr, u2e.shape[1]), f32), pltpu.VMEM((tr, u2e.shape[1]), f32),
                            pltpu.VMEM((tr, d), f32), pltpu.VMEM((tr, d), f32),
                            pltpu.VMEM((E_PER_GROUP, d, fe), bf16), pltpu.VMEM((E_PER_GROUP, d, fe), bf16),
                            pltpu.VMEM((E_PER_GROUP, fe, d), bf16),
                            pltpu.SemaphoreType.DMA((2,)), pltpu.SemaphoreType.DMA((2,))]),
        out_shape=jax.ShapeDtypeStruct((rows, d), f32),
        compiler_params=_cparams(("arbitrary",)),
        name="experts",
    )(read_tok, write_row, tile_grp, n_tiles, u2e, grouped(w1), grouped(w3), grouped(w2))


def _moe_combine_body(y_ref, x_ref, mod_ref, g_ref, b_ref, o_ref):
    r = DN_ALPHA * x_ref[...] + mod_ref[0, 5:6, :] * y_ref[...]
    o_ref[...] = _layer_norm(r) * g_ref[...] + b_ref[...]


def _moe_combine(ys, x1, mod, ln_g, ln_b, seq):
    t, d = x1.shape
    tm = 512
    tps = seq // tm
    row = pl.BlockSpec((tm, d), lambda i: (i, 0))
    const = lambda a: pl.BlockSpec(a.shape, lambda i: (0,) * a.ndim)
    return pl.pallas_call(
        _moe_combine_body,
        grid=(t // tm,),
        in_specs=[row, row, pl.BlockSpec((1, 6, d), lambda i: (i // tps, 0, 0)), const(ln_g), const(ln_b)],
        out_specs=row,
        out_shape=jax.ShapeDtypeStruct((t, d), f32),
        compiler_params=_cparams(("arbitrary",)),
        name="moe_combine",
    )(ys, x1, mod, ln_g, ln_b)


def _pad_cols(a, width):
    return jnp.pad(a, ((0, 0), (0, width - a.shape[1])))


def _pack_w_in(w):
    parts, acc = [], 0
    for sz in IN_SIZES:
        parts.append(w[:, acc:acc + sz])
        acc += sz
    (a_q, a_kv, a_kr, m_q, m_k, m_v, m_i, m_f, m_o, c_b, c_c, c_h, d_q, d_k, d_v) = parts
    gates = _pad_cols(jnp.concatenate([m_i, m_f], axis=1), LANE)
    packed = jnp.concatenate([
        _pad_cols(a_q, 256), a_kv, _pad_cols(a_kr, LANE),
        m_q, m_k, m_v, gates, m_o, c_b, c_c, c_h, d_q, d_k, d_v], axis=1)
    return packed.astype(bf16)


def _pack_mla(w_uq, w_ukv):
    half = MLA_ROPE // 2
    wq3 = w_uq.reshape(MLA_Q_LORA, MLA_HEADS, MLA_NOPE + MLA_ROPE)
    wkv3 = w_ukv.reshape(MLA_KV_LORA, MLA_HEADS, MLA_NOPE + MLA_V)
    wq, wqr, wk, wv = [], [], [], []
    for h in range(MLA_HEADS):
        nope, pe = wq3[:, h, :MLA_NOPE], wq3[:, h, MLA_NOPE:]
        rot = jnp.concatenate([-pe[:, half:], pe[:, :half]], axis=1)
        zq = jnp.zeros((MLA_Q_LORA, MLA_NOPE), f32)
        pad_rows = lambda a: jnp.pad(a, ((0, 256 - MLA_Q_LORA), (0, 0)))
        wq.append(pad_rows(_pad_cols(jnp.concatenate([nope, pe], axis=1), LANE)))
        wqr.append(pad_rows(_pad_cols(jnp.concatenate([zq, rot], axis=1), LANE)))
        wk.append(_pad_cols(wkv3[:, h, :MLA_NOPE], LANE))
        wv.append(wkv3[:, h, MLA_NOPE:])
    st = lambda xs: jnp.stack(xs).astype(bf16)
    return st(wq), st(wqr), st(wk), st(wv)


def _rope_tables(seq):
    half = DF_DK // 2
    inv = 1.0 / (ROPE_THETA ** (jnp.arange(0, DF_DK, 2, dtype=f32) / DF_DK))
    ang = jnp.arange(seq, dtype=f32)[:, None] * inv[None, :]
    cos, sin = jnp.cos(ang), jnp.sin(ang)
    cos_t = jnp.tile(jnp.concatenate([cos, cos], axis=1), (1, 256 // DF_DK))
    sin_t = jnp.tile(jnp.concatenate([-sin, sin], axis=1), (1, 256 // DF_DK))
    scale = (MLA_NOPE + MLA_ROPE) ** -0.5 * LOG2E
    ones = jnp.ones((seq, MLA_NOPE), f32)
    zeros = jnp.zeros((seq, MLA_NOPE), f32)
    cq = _pad_cols(jnp.concatenate([ones, cos, cos], axis=1) * scale, LANE)
    sq = _pad_cols(jnp.concatenate([zeros, sin, sin], axis=1) * scale, LANE)
    return cos_t, sin_t, cq, sq


def kernel(x, c, w_ada, b_ada, w_in, mla_g_q, mla_g_kv, mla_w_uq, mla_w_ukv, ml_b_i, ml_b_f, sc_w,
           df_lq1, df_lk1, df_lq2, df_lk2, df_g, w_out, ln1_g, ln1_b, w_router, b_router, w1, w3, w2,
           ln2_g, ln2_b):
    batch, seq, d = x.shape
    depth = w_in.shape[0]
    assert MLA_ROPE == DF_DK, "both rotary blocks share one table"
    cos_t, sin_t, cq, sq = _rope_tables(seq)
    place = jnp.zeros((LANE, LANE), f32).at[jnp.arange(MLA_ROPE), MLA_NOPE + jnp.arange(MLA_ROPE)].set(1.0).astype(bf16)
    wr_hi = w_router.astype(bf16)
    wr_split = jnp.concatenate([wr_hi, (w_router - wr_hi.astype(f32)).astype(bf16)], axis=1).T
    mod_all = _ada_mod(c, w_ada, b_ada).reshape(depth, batch, 6, d)
    xf = x.reshape(batch * seq, d)
    for l in range(depth):
        mod = mod_all[l]
        w_packed = _pack_w_in(w_in[l])
        gq = _pad_cols(mla_g_q[l][None, :], 256)
        gkv = mla_g_kv[l][None, :]
        gbias = _pad_cols(jnp.concatenate([ml_b_i[l], ml_b_f[l]])[None, :], LANE)
        wkt = w_packed[:, OFF_MK:OFF_MK + GROUP_W].T
        (qn, kvn, kpe, mq, mk, mv, mo, gcol, yc, dq, dk, dv, mkt) = _inproj(
            xf, mod, w_packed, wkt, cos_t, sin_t, gq, gkv, gbias, sc_w[l], seq)
        wq, wqr, wk, wv = _pack_mla(mla_w_uq[l], mla_w_ukv[l])
        ya = _mla_attention(qn, kvn, kpe, wq, wqr, wk, wv, place, cq, sq, batch, seq)
        yb = _mlstm(mq, mk, mkt, mv, mo, gcol, batch, seq)
        lambda_init = 0.8 - 0.6 * math.exp(-0.3 * l)
        yd = _diff_attention(dq, dk, dv, df_lq1[l][None, :], df_lk1[l][None, :], df_lq2[l][None, :],
                             df_lk2[l][None, :], _pad_cols(df_g[l][None, :], LANE),
                             batch, seq, lambda_init)
        x1, u2e, grp = _outproj_route(ya, yb, yc, yd, xf, mod, w_out[l].astype(bf16), ln1_g[l][None, :],
                                      ln1_b[l][None, :], wr_split, b_router[:, None], seq)
        plan = _dispatch_plan(grp[0], MOE_ROW_TILE)
        ys = _experts(*plan, u2e, w1, w3, w2, l)
        xf = _moe_combine(ys, x1, mod, ln2_g[l][None, :], ln2_b[l][None, :], seq)
    return xf.reshape(batch, seq, d)
```

```python
import functools
import math

import jax
import jax.numpy as jnp
from jax import lax
from jax.experimental import pallas as pl
from jax.experimental.pallas import tpu as pltpu

f32 = jnp.float32
bf16 = jnp.bfloat16
HIGHEST = lax.Precision.HIGHEST

GROUP_W = 256
MLA_HEADS, MLA_NOPE, MLA_ROPE, MLA_V = 4, 64, 32, 64
MLA_Q_LORA, MLA_KV_LORA = 192, 128
ML_HEADS, ML_DH, ML_CHUNK = 4, 64, 64
DF_HEADS, DF_DK, DF_V = 4, 32, 64
ROPE_THETA = 10000.0
N_EXPERTS, N_GROUPS, E_PER_GROUP, D_EXPERT = 16, 4, 4, 256
DEPTH = 2
DN_ALPHA = (2 * DEPTH) ** 0.25
LN_EPS = 1e-5
RMS_EPS = 1e-6
IN_SIZES = (192, 128, 32, 256, 256, 256, 4, 4, 256, 256, 256, 256, 256, 256, 256)

OFF_PQ, OFF_PKV, OFF_KR = 0, 256, 384
OFF_MQ, OFF_MK, OFF_MV, OFF_MG, OFF_MO = 512, 768, 1024, 1280, 1408
OFF_CB, OFF_CC, OFF_CH = 1664, 1920, 2176
OFF_DQ, OFF_DK, OFF_DV = 2432, 2688, 2944
IN_PACKED = 3200

LANE = 128
VMEM_LIMIT = 48 * 1024 * 1024
NEG = -1e30
LOG2E = math.log2(math.e)


def _cparams(sem):
    return pltpu.CompilerParams(dimension_semantics=sem, vmem_limit_bytes=VMEM_LIMIT)


def _sigmoid(x):
    return 1.0 / (1.0 + jnp.exp(-x))


def _layer_norm(x):
    mu = jnp.mean(x, axis=-1, keepdims=True)
    xc = x - mu
    var = jnp.mean(xc * xc, axis=-1, keepdims=True)
    return xc * lax.rsqrt(var + LN_EPS)


def _dot(a, b, **kw):
    return jnp.dot(a, b, preferred_element_type=f32, **kw)


def _dot_nt(a, b, **kw):
    return lax.dot_general(a, b, (((1,), (1,)), ((), ())), preferred_element_type=f32, **kw)


def _dot_tn(a, b, **kw):
    return lax.dot_general(a, b, (((0,), (0,)), ((), ())), preferred_element_type=f32, **kw)


def _expand_groups(cols, rows, width, group):
    lane = lax.broadcasted_iota(jnp.int32, (rows, width), 1)
    out = jnp.broadcast_to(cols[-1], (rows, width))
    for h in range(len(cols) - 2, -1, -1):
        out = jnp.where(lane < group * (h + 1), cols[h], out)
    return out


def _lane_group_mask(shape, lo, hi):
    lane = lax.broadcasted_iota(jnp.int32, shape, 1)
    return (lane >= lo) & (lane < hi)


def _ada_body(c_ref, w_ref, b_ref, o_ref):
    c = c_ref[...]
    ca = (c * _sigmoid(c)).astype(bf16)
    o_ref[0] = _dot(ca, w_ref[0].astype(bf16)) + b_ref[0]


def _ada_mod(c, w_ada, b_ada):
    depth, d, n = w_ada.shape
    b = c.shape[0]
    tn = 1536
    return pl.pallas_call(
        _ada_body,
        grid=(depth, n // tn),
        in_specs=[
            pl.BlockSpec((b, d), lambda l, j: (0, 0)),
            pl.BlockSpec((1, d, tn), lambda l, j: (l, 0, j)),
            pl.BlockSpec((1, 1, tn), lambda l, j: (l, 0, j)),
        ],
        out_specs=pl.BlockSpec((1, b, tn), lambda l, j: (l, 0, j)),
        out_shape=jax.ShapeDtypeStruct((depth, b, n), f32),
        compiler_params=_cparams(("arbitrary", "arbitrary")),
        name="ada_mod",
    )(c, w_ada, b_ada.reshape(depth, 1, n))


def _rope_lanes(x, cos, sin_signed):
    w = x.shape[1]
    lane = lax.broadcasted_iota(jnp.int32, x.shape, 1)
    rot = jnp.where(lane % 32 < 16, pltpu.roll(x, w - 16, axis=1), pltpu.roll(x, 16, axis=1))
    return x * cos + rot * sin_signed


def _inproj_body(x_ref, mod_ref, w_ref, wkt_ref, cos_ref, sin_ref, gq_ref, gkv_ref, gb_ref, cw_ref,
                 qn_ref, kvn_ref, kpe_ref, mq_ref, mk_ref, mv_ref, mo_ref, gcol_ref, yc_ref,
                 dq_ref, dk_ref, dv_ref, mkt_ref, carry_ref, *, tiles_per_seq):
    i = pl.program_id(0)
    tm = x_ref.shape[0]
    u = _layer_norm(x_ref[...]) * (1.0 + mod_ref[0, 1:2, :]) + mod_ref[0, 0:1, :]
    ub = u.astype(bf16)

    def seg(off, n):
        return _dot(ub, w_ref[:, off:off + n])

    cos = cos_ref[...]
    sin = sin_ref[...]

    pq = seg(OFF_PQ, 256)
    ms = jnp.sum(pq * pq, axis=-1, keepdims=True) * (1.0 / MLA_Q_LORA)
    qn_ref[...] = (pq * lax.rsqrt(ms + RMS_EPS) * gq_ref[...]).astype(bf16)
    pkv = seg(OFF_PKV, 128)
    ms = jnp.sum(pkv * pkv, axis=-1, keepdims=True) * (1.0 / MLA_KV_LORA)
    kvn_ref[...] = (pkv * lax.rsqrt(ms + RMS_EPS) * gkv_ref[...]).astype(bf16)
    kpe_ref[...] = _rope_lanes(seg(OFF_KR, 128), cos[:, :LANE], sin[:, :LANE]).astype(bf16)

    mq_ref[...] = (seg(OFF_MQ, 256) * (ML_DH ** -0.5)).astype(bf16)
    mk_ref[...] = seg(OFF_MK, 256).astype(bf16)
    mkt_ref[...] = _dot_nt(wkt_ref[...], ub).astype(bf16)
    mv_ref[...] = seg(OFF_MV, 256).astype(bf16)
    gcol_ref[...] = seg(OFF_MG, 128) + gb_ref[...]
    mo_ref[...] = seg(OFF_MO, 256).astype(bf16)

    uc = seg(OFF_CC, 256) * seg(OFF_CH, 256)

    @pl.when(i % tiles_per_seq == 0)
    def _():
        carry_ref[...] = jnp.zeros_like(carry_ref)

    prev = carry_ref[...]
    row =lax.broadcasted_iota(jnp.int32, uc.shape, 0)
    u1 = jnp.where(row == 0, prev[7:8, :], pltpu.roll(uc, 1, axis=0))
    u2 = jnp.where(row == 0, prev[6:7, :], jnp.where(row == 1, prev[7:8, :], pltpu.roll(uc, 2, axis=0)))
    carry_ref[...] = uc[tm - 8:, :]
    conv = cw_ref[0:1, :] * u2 + cw_ref[1:2, :] * u1 + cw_ref[2:3, :] * uc
    yc_ref[...] = (seg(OFF_CB, 256) * conv).astype(bf16)

    dq_ref[...] = (_rope_lanes(seg(OFF_DQ, 256), cos, sin) * (DF_DK ** -0.5 * LOG2E)).astype(bf16)
    dk_ref[...] = _rope_lanes(seg(OFF_DK, 256), cos, sin).astype(bf16)
    dv_ref[...] = seg(OFF_DV, 256).astype(bf16)


def _inproj(x2d, mod, w_packed, wkt, cos_t, sin_t, gq, gkv, gbias, conv_w, seq):
    t, d = x2d.shape
    tm = 512
    tps = seq // tm
    row = lambda w: pl.BlockSpec((tm, w), lambda i: (i, 0))
    const = lambda a: pl.BlockSpec(a.shape, lambda i: (0,) * a.ndim)
    widths = (256, 128, 128, 256, 256, 256, 256, 128, 256, 256, 256, 256)
    dtypes = (bf16, bf16, bf16, bf16, bf16, bf16, bf16, f32, bf16, bf16, bf16, bf16)
    return pl.pallas_call(
        functools.partial(_inproj_body, tiles_per_seq=tps),
        grid=(t // tm,),
        in_specs=[
            row(d),
            pl.BlockSpec((1, 6, d), lambda i: (i // tps, 0, 0)),
            const(w_packed), const(wkt),
            pl.BlockSpec((tm, 256), lambda i: (i % tps, 0)),
            pl.BlockSpec((tm, 256), lambda i: (i % tps, 0)),
            const(gq), const(gkv), const(gbias), const(conv_w),
        ],
        out_specs=[row(w) for w in widths] + [pl.BlockSpec((GROUP_W, tm), lambda i: (0, i))],
        out_shape=[jax.ShapeDtypeStruct((t, w), dt) for w, dt in zip(widths, dtypes)]
        + [jax.ShapeDtypeStruct((GROUP_W, t), bf16)],
        scratch_shapes=[pltpu.VMEM((8, 256), f32)],
        compiler_params=_cparams(("arbitrary",)),
        name="inproj",
    )(x2d, mod, w_packed, wkt, cos_t, sin_t, gq, gkv, gbias, conv_w)


def _causal_softmax_heads(q_heads, k_at, v_at, qi, s_scr, m_scr, acc_scr):
    n_heads = len(q_heads)
    tq = q_heads[0].shape[0]
    half = tq // 2
    rows = pl.ds(pl.multiple_of(qi * tq, tq), tq)

    def scores(g, j):
        return _dot_nt(q_heads[g], k_at(g, pl.multiple_of(j * tq, tq)))

    def fold_max(g, s):
        m_scr[g] = jnp.maximum(m_scr[g], jnp.maximum(s[:, :half], s[:, half:]))

    m_scr[...] = jnp.full(m_scr.shape, NEG, f32)
    for g in range(n_heads):
        acc_scr[g, rows, :] = jnp.zeros((tq, LANE), f32)

    def score_blocks(js):
        for g in range(n_heads):
            mx = None
            for j in js:
                s = scores(g, j)
                s_scr[g, j] = s
                s = jnp.maximum(s[:, :half], s[:, half:])
                mx = s if mx is None else jnp.maximum(mx, s)
            m_scr[g] = jnp.maximum(m_scr[g], mx)

    def prob_blocks(js):
        for g in range(n_heads):
            mb = m_scr[g]
            pv = None
            for j in js:
                sj = s_scr[g, j]
                p = jnp.concatenate([jnp.exp2(sj[:, :half] - mb), jnp.exp2(sj[:, half:] - mb)],
                                    axis=1).astype(bf16)
                d = _dot(p, v_at(g, pl.multiple_of(j * tq, tq)))
                pv = d if pv is None else pv + d
            acc_scr[g, rows, :] += pv

    def over_full_blocks(blocks_fn):
        @pl.loop(0, qi // 2)
        def _(jj):
            blocks_fn((2 * jj, 2 * jj + 1))

        @pl.when(qi % 2 == 1)
        def _():
            blocks_fn((qi - 1,))

    over_full_blocks(score_blocks)
    rowi = lax.broadcasted_iota(jnp.int32, (tq, tq), 0)
    coli = lax.broadcasted_iota(jnp.int32, (tq, tq), 1)
    for g in range(n_heads):
        s = jnp.where(coli <= rowi, scores(g, qi), NEG)
        s_scr[g, qi] = s
        fold_max(g, s)
        m_scr[g] = jnp.broadcast_to(jnp.max(m_scr[g], axis=1, keepdims=True), (tq, half))

    over_full_blocks(prob_blocks)
    prob_blocks((qi,))


def _softmax_scratch(n_heads, nq, tq):
    return [pltpu.VMEM((n_heads, nq, tq, tq), f32), pltpu.VMEM((n_heads, tq, tq // 2), f32),
            pltpu.VMEM((n_heads, nq * tq, LANE), f32)]


EPILOGUE_ROWS = 512


def _with_ones_lanes(v, dv):
    return jnp.concatenate([v.astype(bf16), jnp.ones((v.shape[0], LANE - dv), bf16)], axis=1)


def _softmax_normalise(acc, dv):
    lane = lax.broadcasted_iota(jnp.int32, acc.shape, 1)
    return jnp.where(lane < dv, acc / pltpu.roll(acc, LANE - dv, axis=1), 0.0)


def _pair_lanes(a, b, dv):
    lane = lax.broadcasted_iota(jnp.int32, a.shape, 1)
    return jnp.where(lane < dv, a, pltpu.roll(b, dv, axis=1))


def _mla_body(qn_ref, kvn_ref, kpe_ref, wq_ref, wqr_ref, wk_ref, wv_ref, place_ref, cq_ref, sq_ref,
              o_ref, k_scr, v_scr, s_scr, m_scr, acc_scr):
    qi = pl.program_id(1)
    tq = qn_ref.shape[0]

    @pl.when(qi == 0)
    def _():
        kvn = kvn_ref[...]
        kpe_placed = _dot(kpe_ref[...], place_ref[...])
        for h in range(MLA_HEADS):
            k_scr[h] = (_dot(kvn, wk_ref[h]) + kpe_placed).astype(bf16)
            v_scr[h] = _with_ones_lanes(_dot(kvn, wv_ref[h]), MLA_V)

    qn = qn_ref[...]
    cq = cq_ref[...]
    sq = sq_ref[...]
    q_heads = [(_dot(qn, wq_ref[h]) * cq + _dot(qn, wqr_ref[h]) * sq).astype(bf16)
               for h in range(MLA_HEADS)]
    _causal_softmax_heads(
        q_heads, lambda g, start: k_scr[g, pl.ds(start, tq), :],
        lambda g, start: v_scr[g, pl.ds(start, tq), :], qi, s_scr, m_scr, acc_scr)

    @pl.when(qi == pl.num_programs(1) - 1)
    def _():
        for r0 in range(0, o_ref.shape[0], EPILOGUE_ROWS):
            rows = pl.ds(r0, EPILOGUE_ROWS)
            o = [_softmax_normalise(acc_scr[h, rows, :], MLA_V) for h in range(MLA_HEADS)]
            o_ref[rows, :] = jnp.concatenate(
                [_pair_lanes(o[0], o[1], MLA_V), _pair_lanes(o[2], o[3], MLA_V)], axis=1).astype(bf16)


def _mla_attention(qn, kvn, kpe, wq, wqr, wk, wv, place, cq, sq, batch, seq):
    tq = 256
    nq = seq // tq
    const = lambda a: pl.BlockSpec(a.shape, lambda b, i: (0,) * a.ndim)
    return pl.pallas_call(
        _mla_body,
        grid=(batch, nq),
        in_specs=[
            pl.BlockSpec((tq, 256), lambda b, i: (b * nq + i, 0)),
            pl.BlockSpec((seq, 128), lambda b, i: (b, 0)),
            pl.BlockSpec((seq, 128), lambda b, i: (b, 0)),
            const(wq), const(wqr), const(wk), const(wv), const(place),
            pl.BlockSpec((tq, 128), lambda b, i: (i, 0)),
            pl.BlockSpec((tq, 128), lambda b, i: (i, 0)),
        ],
        out_specs=pl.BlockSpec((seq, GROUP_W), lambda b, i: (b, 0)),
        out_shape=jax.ShapeDtypeStruct((batch * seq, GROUP_W), bf16),
        scratch_shapes=[pltpu.VMEM((MLA_HEADS, seq, 128), bf16),
                        pltpu.VMEM((MLA_HEADS, seq, LANE), bf16)]
        + _softmax_scratch(MLA_HEADS, nq, tq),
        compiler_params=_cparams(("arbitrary", "arbitrary")),
        name="mla_attn",
    )(qn, kvn, kpe, wq, wqr, wk, wv, place, cq, sq)


def _diff_body(q_ref, k_ref, v_ref, lq1_ref, lk1_ref, lq2_ref, lk2_ref, g_ref,
               o_ref, v_scr, s_scr, m_scr, acc_scr, *, lambda_init):
    qi = pl.program_id(1)
    tq = q_ref.shape[0]

    @pl.when(qi == 0)
    def _():
        v = v_ref[...]
        for h in range(DF_HEADS):
            v_scr[h] = _with_ones_lanes(v[:, DF_V * h:DF_V * (h + 1)], DF_V)

    q = q_ref[...]
    q_heads = [jnp.where(_lane_group_mask(q.shape, DF_DK * g, DF_DK * (g + 1)), q, jnp.zeros_like(q))
               for g in range(2 * DF_HEADS)]
    _causal_softmax_heads(
        q_heads, lambda g, start: k_ref[pl.ds(start, tq), :],
        lambda g, start: v_scr[g // 2, pl.ds(start, tq), :], qi, s_scr, m_scr, acc_scr)

    @pl.when(qi == pl.num_programs(1) - 1)
    def _():
        lam = (jnp.exp(jnp.sum(lq1_ref[...] * lk1_ref[...], axis=1, keepdims=True))
               - jnp.exp(jnp.sum(lq2_ref[...] * lk2_ref[...], axis=1, keepdims=True)) + lambda_init)
        gain = g_ref[...] * (1.0 - lambda_init)
        for r0 in range(0, o_ref.shape[0], EPILOGUE_ROWS):
            rows = pl.ds(r0, EPILOGUE_ROWS)
            ys = []
            for h in range(DF_HEADS):
                o = (_softmax_normalise(acc_scr[2 * h, rows, :], DF_V)
                     - lam * _softmax_normalise(acc_scr[2 * h + 1, rows, :], DF_V))
                ms = jnp.sum(o * o, axis=1, keepdims=True) * (1.0 / DF_V)
                ys.append(o * lax.rsqrt(ms + RMS_EPS) * gain)
            o_ref[rows, :] = jnp.concatenate(
                [_pair_lanes(ys[0], ys[1], DF_V), _pair_lanes(ys[2], ys[3], DF_V)], axis=1).astype(bf16)


def _diff_attention(dq, dk, dv, lq1, lk1, lq2, lk2, g_sub, batch, seq, lambda_init):
    tq = 256
    nq = seq // tq
    const = lambda a: pl.BlockSpec(a.shape, lambda b, i: (0,) * a.ndim)
    return pl.pallas_call(
        functools.partial(_diff_body, lambda_init=lambda_init),
        grid=(batch, nq),
        in_specs=[
            pl.BlockSpec((tq, 256), lambda b, i: (b * nq + i, 0)),
            pl.BlockSpec((seq, 256), lambda b, i: (b, 0)),
            pl.BlockSpec((seq, 256), lambda b, i: (b, 0)),
            const(lq1), const(lk1), const(lq2), const(lk2), const(g_sub),
        ],
        out_specs=pl.BlockSpec((seq, GROUP_W), lambda b, i: (b, 0)),
        out_shape=jax.ShapeDtypeStruct((batch * seq, GROUP_W), bf16),
        scratch_shapes=[pltpu.VMEM((DF_HEADS, seq, LANE), bf16)]
        + _softmax_scratch(2 * DF_HEADS, nq, tq),
        compiler_params=_cparams(("arbitrary", "arbitrary")),
        name="diff_attn",
    )(dq, dk, dv, lq1, lk1, lq2, lk2, g_sub)


def _split3(a):
    hi = a.astype(bf16)
    r1 = a - hi.astype(f32)
    mid = r1.astype(bf16)
    return hi, mid, (r1 - mid.astype(f32)).astype(bf16)


def _dot_exact_rhs01(a, b01):
    hi, mid, lo = _split3(a)
    return _dot(hi, b01) + _dot(mid, b01) + _dot(lo, b01)


def _dot_exact_lhs01(a01, b):
    hi, mid, lo = _split3(b)
    return _dot(a01, hi) + _dot(a01, mid) + _dot(a01, lo)


def _mlstm_gate_terms(g, eif):
    R, W, L = g.shape[0], eif.shape[1] // 2, ML_CHUNK
    lf = jnp.minimum(g, 0.0) - jnp.log(1.0 + jnp.exp(-jnp.abs(g)))
    glane = lax.broadcasted_iota(jnp.int32, g.shape, 1)
    wide = _dot_exact_rhs01(jnp.where(glane < ML_HEADS, g, lf), eif)
    iw, lfw = wide[:, :W], wide[:, W:]
    rr = lax.broadcasted_iota(jnp.int32, (R, R), 0)
    cc = lax.broadcasted_iota(jnp.int32, (R, R), 1)
    bw = _dot_exact_lhs01(((rr >= cc) & (rr // L == cc // L)).astype(bf16), lfw)
    rw = iw - bw
    pos_in_chunk = lax.broadcasted_iota(jnp.int32, (R, W), 0) % L
    cm = rw
    for sh in (1, 2, 4, 8, 16, 32):
        cm = jnp.maximum(cm, jnp.where(pos_in_chunk >= sh, pltpu.roll(cm, sh, axis=0), NEG))
    return bw, rw, cm


def _mlstm_chunk(q, k, kt_bd, v, o, bw, rw, cm, c_prev, n_prev, m_prev):
    L, W = q.shape
    row = lax.broadcasted_iota(jnp.int32, (L, W), 0)
    pos = lax.broadcasted_iota(jnp.int32, (L, W), 1) % L
    mw = jnp.maximum(m_prev, cm)
    a_inter = jnp.exp(m_prev - mw)
    r_row = jnp.sum(jnp.where(pos == row, rw, 0.0), axis=0, keepdims=True)
    ew = jnp.exp(jnp.where(pos <= row, r_row - mw, NEG))
    pw = _dot(q, kt_bd.astype(bf16)) * ew
    head_eq = (lax.broadcasted_iota(jnp.int32, (W, W), 0) // ML_DH
               == lax.broadcasted_iota(jnp.int32, (W, W), 1) // ML_DH)
    v4 = jnp.concatenate([v] * ML_HEADS, axis=0)
    vbd = jnp.where(head_eq, v4, jnp.zeros_like(v4))
    obd = head_eq.astype(bf16)
    den = a_inter * _dot_exact_rhs01(q.astype(f32) * n_prev, obd) + _dot_exact_rhs01(pw, obd)
    hden = jnp.maximum(jnp.abs(den), jnp.exp(-(bw + mw)))
    num = a_inter * _dot(q, c_prev.astype(bf16)) + _dot(pw.astype(bf16), vbd)
    y = num / hden * _sigmoid(o.astype(f32))

    rmax, m_last = cm[L - 1:L, :], mw[L - 1:L, :]
    a_w = jnp.exp(m_prev - m_last)
    b_w = jnp.exp(rmax - m_last)
    c_loc = _dot((kt_bd * jnp.exp(r_row - rmax)).astype(bf16), vbd)
    n_loc = jnp.sum(k.astype(f32) * jnp.exp(rw - rmax), axis=0, keepdims=True)
    return y, c_prev * a_w + c_loc * b_w, a_w * n_prev + b_w * n_loc, bw[L - 1:L, :] + m_last


def _mlstm_body(q_ref, k_ref, kt_ref, v_ref, o_ref, g_ref, eif_ref, sel_ref, y_ref, c_scr, nm_scr):
    @pl.when(pl.program_id(1) == 0)
    def _():
        c_scr[...] = jnp.zeros_like(c_scr)
        nm_scr[...] = jnp.zeros_like(nm_scr)

    c, n, m = c_scr[...], nm_scr[0:1, :], nm_scr[1:2, :]
    kt = kt_ref[...]
    eif = eif_ref[...]
    w = kt.shape[0]
    head_eq = (lax.broadcasted_iota(jnp.int32, (w, w), 0) // ML_DH
               == lax.broadcasted_iota(jnp.int32, (w, w), 1) // ML_DH)
    bw, rw, cm = _mlstm_gate_terms(g_ref[...], eif)
    for i in range(q_ref.shape[0] // ML_CHUNK):
        rows = pl.ds(i * ML_CHUNK, ML_CHUNK)
        lo, hi = i * ML_CHUNK, (i + 1) * ML_CHUNK
        kt_bd = jnp.where(head_eq, _dot(kt, sel_ref[i]), 0.0)
        y, c, n, m = _mlstm_chunk(q_ref[rows, :], k_ref[rows, :], kt_bd, v_ref[rows, :], o_ref[rows, :],
                                  bw[lo:hi], rw[lo:hi], cm[lo:hi], c, n, m)
        y_ref[rows, :] = y.astype(y_ref.dtype)
    c_scr[...] = c
    nm_scr[0:1, :] = n
    nm_scr[1:2, :] = m


MLSTM_CHUNKS_PER_STEP = 4


def _mlstm(mq, mk, mkt, mv, mo, gcol, batch, seq):
    cps = MLSTM_CHUNKS_PER_STEP
    L = ML_CHUNK * cps
    nc = seq // L
    blk = lambda w: pl.BlockSpec((L, w), lambda b, c: (b * nc + c, 0))
    const = lambda a: pl.BlockSpec(a.shape, lambda b, c: (0,) * a.ndim)
    gl = jnp.arange(LANE)[:, None]
    col = jnp.arange(2 * GROUP_W)[None, :]
    eif = (gl == (col // GROUP_W) * ML_HEADS + (col % GROUP_W) // ML_DH).astype(bf16)
    p = jnp.arange(L)[None, :, None]
    cc = jnp.arange(GROUP_W)[None, None, :]
    sel = (p == jnp.arange(cps)[:, None, None] * ML_CHUNK + cc % ML_CHUNK).astype(bf16)
    return pl.pallas_call(
        _mlstm_body,
        grid=(batch, nc),
        in_specs=[blk(256), blk(256), pl.BlockSpec((GROUP_W, L), lambda b, c: (0, b * nc + c)),
                  blk(256), blk(256), blk(128), const(eif), const(sel)],
        out_specs=blk(GROUP_W),
        out_shape=jax.ShapeDtypeStruct((batch * seq, GROUP_W), bf16),
        scratch_shapes=[pltpu.VMEM((GROUP_W, GROUP_W), f32), pltpu.VMEM((8, GROUP_W), f32)],
        compiler_params=_cparams(("arbitrary", "arbitrary")),
        name="mlstm",
    )(mq, mk, mkt, mv, mo, gcol, eif, sel)


def _route(scores, sel):
    ne, tm = sel.shape
    eid = lax.broadcasted_iota(jnp.int32, (ne, tm), 0)
    grp = eid // E_PER_GROUP
    big = ne + 1

    def top2(vals):
        m1 = jnp.max(vals, axis=0, keepdims=True)
        i1 = jnp.min(jnp.where(vals == m1, eid, big), axis=0, keepdims=True)
        rest = jnp.where(eid == i1, NEG, vals)
        m2 = jnp.max(rest, axis=0, keepdims=True)
        i2 = jnp.min(jnp.where(rest == m2, eid, big), axis=0, keepdims=True)
        return m1, i1, m2, i2

    best_score, best_grp = None, None
    for gi in range(N_GROUPS):
        m1, _, m2, _ = top2(jnp.where(grp == gi, sel, NEG))
        sc = m1 + m2
        if best_score is None:
            best_score, best_grp = sc, jnp.zeros_like(sc, dtype=jnp.int32)
        else:
            better = sc > best_score
            best_grp = jnp.where(better, gi, best_grp)
            best_score = jnp.where(better, sc, best_score)
    _, i1, _, i2 = top2(jnp.where(grp == best_grp, sel, NEG))
    picked = jnp.where((eid == i1) | (eid == i2), scores, 0.0)
    return picked / jnp.sum(picked, axis=0, keepdims=True), best_grp


MOE_ROW_TILE = 256


def _outproj_route_body(ya_ref, yb_ref, yc_ref, yd_ref, x_ref, mod_ref, w_ref, g_ref, b_ref,
                        wr_ref, br_ref, x1_ref, u2e_ref, grp_ref):
    d = x_ref.shape[1]
    mix = None
    for j, y_ref in enumerate((ya_ref, yb_ref, yc_ref, yd_ref)):
        part = _dot(y_ref[...], w_ref[GROUP_W * j:GROUP_W * (j + 1), :])
        mix = part if mix is None else mix + part
    x1 = _layer_norm(DN_ALPHA * x_ref[...] + mod_ref[0, 2:3, :] * mix) * g_ref[...] + b_ref[...]
    x1_ref[...] = x1
    u = _layer_norm(x1) * (1.0 + mod_ref[0, 4:5, :]) + mod_ref[0, 3:4, :]
    u_hi = u.astype(bf16)
    u_lo = (u - u_hi.astype(f32)).astype(bf16)
    ne = br_ref.shape[0]
    part = _dot_nt(wr_ref[...], u_hi)
    logits = part[:ne] + part[ne:] + _dot_nt(wr_ref[:ne, :], u_lo)
    scores = _sigmoid(logits)
    gates_t, best_grp = _route(scores, scores + br_ref[...])
    u2e_ref[:, :d] = u_hi.astype(f32)
    u2e_ref[:, d:] = jnp.zeros((x1.shape[0], LANE), f32)
    u2e_ref[:, d:d + ne] = gates_t.T
    grp_ref[...] = best_grp


def _outproj_route(ya, yb, yc, yd, x2d, mod, w_out, ln_g, ln_b, wr_split, b_router, seq):
    t, d = x2d.shape
    tm = 512
    tps = seq // tm
    row = lambda w: pl.BlockSpec((tm, w), lambda i: (i, 0))
    const = lambda a: pl.BlockSpec(a.shape, lambda i: (0,) * a.ndim)
    return pl.pallas_call(
        _outproj_route_body,
        grid=(t // tm,),
        in_specs=[row(256), row(256), row(256), row(256), row(d),
                  pl.BlockSpec((1, 6, d), lambda i: (i // tps, 0, 0)),
                  const(w_out), const(ln_g), const(ln_b), const(wr_split), const(b_router)],
        out_specs=[row(d), row(d + LANE), pl.BlockSpec((1, tm), lambda i: (0, i))],
        out_shape=[jax.ShapeDtypeStruct((t, d), f32), jax.ShapeDtypeStruct((t, d + LANE), f32),
                   jax.ShapeDtypeStruct((1, t), jnp.int32)],
        compiler_params=_cparams(("arbitrary",)),
        name="outproj_route",
    )(ya, yb, yc, yd, x2d, mod, w_out, ln_g, ln_b, wr_split, b_router)


def _dispatch_plan(grp, tr):
    t = grp.shape[0]
    rows = t + N_GROUPS * tr
    order = jnp.sort(grp * t + jnp.arange(t, dtype=jnp.int32)) % t
    order = jnp.concatenate([order, jnp.zeros((rows - t,), order.dtype)])
    counts = jnp.sum((grp[:, None] == jnp.arange(N_GROUPS)[None, :]).astype(jnp.int32), axis=0)
    padded = ((counts + tr - 1) // tr) * tr
    row_end = jnp.cumsum(padded)
    r = jnp.arange(rows, dtype=jnp.int32)
    read_tok = jnp.zeros((rows,), jnp.int32)
    write_row = jnp.zeros((rows,), jnp.int32)
    for g in range(N_GROUPS):
        row_start = row_end[g] - padded[g]
        tok_start = jnp.sum(counts[:g])
        shifted = jnp.roll(order, row_start - tok_start)
        in_grp = (r >= row_start) & (r < row_end[g])
        valid = in_grp & (r < row_start + counts[g])
        read_tok = jnp.where(valid, shifted, jnp.where(in_grp, order[jnp.minimum(tok_start, t - 1)], read_tok))
        write_row = jnp.where(valid, shifted, jnp.where(in_grp, t + g * tr + r % tr, write_row))
    tile_start = jnp.arange(rows // tr, dtype=jnp.int32) * tr
    tile_grp = jnp.minimum(jnp.sum((tile_start[:, None] >= row_end[None, :]).astype(jnp.int32), axis=1),
                           N_GROUPS - 1)
    i32 = lambda a: a.astype(jnp.int32)
    return i32(read_tok), i32(write_row), i32(tile_grp), i32(row_end[-1:] // tr)


def _wait_row_gather(n_rows, hbm, vmem, sem):
    pltpu.make_async_copy(hbm.at[pl.ds(0, n_rows), :], vmem, sem).wait()


def _wait_row_scatter(n_rows, vmem, hbm, sem):
    pltpu.make_async_copy(vmem, hbm.at[pl.ds(0, n_rows), :], sem).wait()


def _experts_body(read_ref, write_ref, tgrp_ref, ntile_ref,
                  u2e_hbm, w1_ref, w3_ref, w2_ref, ys_hbm, xbuf0, xbuf1, obuf0, obuf1, w1b, w3b, w2b,
                  gsem, ssem):
    j = pl.program_id(0)
    n_tiles = ntile_ref[0]
    tr, d = obuf0.shape
    xbuf, obuf = (xbuf0, xbuf1), (obuf0, obuf1)

    def start_gather(tile, s):
        for i in range(tr):
            pltpu.make_async_copy(u2e_hbm.at[pl.ds(read_ref[tile * tr + i], 1), :],
                                  xbuf[s].at[pl.ds(i, 1), :], gsem.at[s]).start()

    def start_scatter(tile, s):
        for i in range(tr):
            pltpu.make_async_copy(obuf[s].at[pl.ds(i, 1), :],
                                  ys_hbm.at[pl.ds(write_ref[tile * tr + i], 1), :], ssem.at[s]).start()

    def load_tile(s):
        xs = xbuf[s][...]
        return xs[:, :d].astype(bf16), xs[:, d:]

    def compute(tile, s, xb, gslab):
        lane = lax.broadcasted_iota(jnp.int32, gslab.shape, 1)
        first_expert = tgrp_ref[tile] * E_PER_GROUP
        acc = None
        for e in range(E_PER_GROUP):
            ge = jnp.sum(jnp.where(lane == first_expert + e, gslab, 0.0), axis=1, keepdims=True)
            h1 = _dot(xb, w1b[e])
            h = h1 * _sigmoid(h1) * _dot(xb, w3b[e]) * ge
            part = _dot(h.astype(bf16), w2b[e])
            acc = part if acc is None else acc + part
        obuf[s][...] = acc

    nxt = jnp.minimum(j + 1, n_tiles - 1)

    last = tgrp_ref.shape[0] - 1
    @pl.when((j < n_tiles) & ((j == 0) | (tgrp_ref[jnp.minimum(j, last)] != tgrp_ref[jnp.clip(j - 1, 0, last)])))
    def _():
        w1b[...] = w1_ref[0].astype(bf16)
        w3b[...] = w3_ref[0].astype(bf16)
        w2b[...] = w2_ref[0].astype(bf16)

    @pl.when(j == 0)
    def _():
        start_gather(0, 0)
        start_gather(nxt, 1)
        obuf[1][...] = jnp.zeros_like(obuf[1])
        n_tok = ys_hbm.shape[0] - N_GROUPS * tr
        spare = [pltpu.make_async_copy(obuf[1], ys_hbm.at[pl.ds(n_tok + g * tr, tr), :], ssem.at[1])
                 for g in range(N_GROUPS)]
        for cp in spare:
            cp.start()
        for cp in spare:
            cp.wait()
        _wait_row_gather(tr, u2e_hbm, xbuf[0], gsem.at[0])
        compute(0, 0, *load_tile(0))

    for s in (0, 1):
        @pl.when((j > 0) & (j < n_tiles) & (j % 2 == s))
        def _(s=s):
            start_gather(nxt, 1 - s)
            start_scatter(j - 1, 1 - s)

        @pl.when((j > 0) & (j < n_tiles) & (j % 2 == s))
        def _(s=s):
            _wait_row_gather(tr, u2e_hbm, xbuf[s], gsem.at[s])
            compute(j, s, *load_tile(s))
            _wait_row_scatter(tr, obuf[1 - s], ys_hbm, ssem.at[1 - s])

        @pl.when((j == n_tiles) & (j % 2 == s))
        def _(s=s):
            _wait_row_gather(tr, u2e_hbm, xbuf[s], gsem.at[s])
            start_scatter(j - 1, 1 - s)
            _wait_row_scatter(tr, obuf[1 - s], ys_hbm, ssem.at[1 - s])


def _experts(read_tok, write_row, tile_grp, n_tiles, u2e, w1, w3, w2, layer):
    rows = read_tok.shape[0]
    tr = MOE_ROW_TILE
    max_tiles = tile_grp.shape[0]
    _, ne, d, fe = w1.shape
    grouped = lambda w: w.reshape(w.shape[0], N_GROUPS, E_PER_GROUP, *w.shape[2:])
    wspec = lambda a, b: pl.BlockSpec(
        (None, 1, E_PER_GROUP, a, b),
        lambda j, rd, wr, tg, nt: (layer, tg[jnp.minimum(j, max_tiles - 1)], 0, 0, 0))
    return pl.pallas_call(
        _experts_body,
        grid_spec=pltpu.PrefetchScalarGridSpec(
            num_scalar_prefetch=4, grid=(max_tiles + 1,),
            in_specs=[pl.BlockSpec(memory_space=pl.ANY), wspec(d, fe), wspec(d, fe), wspec(fe, d)],
            out_specs=pl.BlockSpec(memory_space=pl.ANY),
            scratch_shapes=[pltpu.VMEM((tr, u2e.shape[1]), f32), pltpu.VMEM((tr, u2e.shape[1]), f32),
                            pltpu.VMEM((tr, d), f32), pltpu.VMEM((tr, d), f32),
                            pltpu.VMEM((E_PER_GROUP, d, fe), bf16), pltpu.VMEM((E_PER_GROUP, d, fe), bf16),
                            pltpu.VMEM((E_PER_GROUP, fe, d), bf16),
                            pltpu.SemaphoreType.DMA((2,)), pltpu.SemaphoreType.DMA((2,))]),
        out_shape=jax.ShapeDtypeStruct((rows, d), f32),
        compiler_params=_cparams(("arbitrary",)),
        name="experts",
    )(read_tok, write_row, tile_grp, n_tiles, u2e, grouped(w1), grouped(w3), grouped(w2))


def _moe_combine_body(y_ref, x_ref, mod_ref, g_ref, b_ref, o_ref):
    r = DN_ALPHA * x_ref[...] + mod_ref[0, 5:6, :] * y_ref[...]
    o_ref[...] = _layer_norm(r) * g_ref[...] + b_ref[...]


def _moe_combine(ys, x1, mod, ln_g, ln_b, seq):
    t, d = x1.shape
    tm = 512
    tps = seq // tm
    row = pl.BlockSpec((tm, d), lambda i: (i, 0))
    const = lambda a: pl.BlockSpec(a.shape, lambda i: (0,) * a.ndim)
    return pl.pallas_call(
        _moe_combine_body,
        grid=(t // tm,),
        in_specs=[row, row, pl.BlockSpec((1, 6, d), lambda i: (i // tps, 0, 0)), const(ln_g), const(ln_b)],
        out_specs=row,
        out_shape=jax.ShapeDtypeStruct((t, d), f32),
        compiler_params=_cparams(("arbitrary",)),
        name="moe_combine",
    )(ys, x1, mod, ln_g, ln_b)


def _pad_cols(a, width):
    return jnp.pad(a, ((0, 0), (0, width - a.shape[1])))


def _pack_w_in(w):
    parts, acc = [], 0
    for sz in IN_SIZES:
        parts.append(w[:, acc:acc + sz])
        acc += sz
    (a_q, a_kv, a_kr, m_q, m_k, m_v, m_i, m_f, m_o, c_b, c_c, c_h, d_q, d_k, d_v) = parts
    gates = _pad_cols(jnp.concatenate([m_i, m_f], axis=1), LANE)
    packed = jnp.concatenate([
        _pad_cols(a_q, 256), a_kv, _pad_cols(a_kr, LANE),
        m_q, m_k, m_v, gates, m_o, c_b, c_c, c_h, d_q, d_k, d_v], axis=1)
    return packed.astype(bf16)


def _pack_mla(w_uq, w_ukv):
    half = MLA_ROPE // 2
    wq3 = w_uq.reshape(MLA_Q_LORA, MLA_HEADS, MLA_NOPE + MLA_ROPE)
    wkv3 = w_ukv.reshape(MLA_KV_LORA, MLA_HEADS, MLA_NOPE + MLA_V)
    wq, wqr, wk, wv = [], [], [], []
    for h in range(MLA_HEADS):
        nope, pe = wq3[:, h, :MLA_NOPE], wq3[:, h, MLA_NOPE:]
        rot = jnp.concatenate([-pe[:, half:], pe[:, :half]], axis=1)
        zq = jnp.zeros((MLA_Q_LORA, MLA_NOPE), f32)
        pad_rows = lambda a: jnp.pad(a, ((0, 256 - MLA_Q_LORA), (0, 0)))
        wq.append(pad_rows(_pad_cols(jnp.concatenate([nope, pe], axis=1), LANE)))
        wqr.append(pad_rows(_pad_cols(jnp.concatenate([zq, rot], axis=1), LANE)))
        wk.append(_pad_cols(wkv3[:, h, :MLA_NOPE], LANE))
        wv.append(wkv3[:, h, MLA_NOPE:])
    st = lambda xs: jnp.stack(xs).astype(bf16)
    return st(wq), st(wqr), st(wk), st(wv)


def _rope_tables(seq):
    half = DF_DK // 2
    inv = 1.0 / (ROPE_THETA ** (jnp.arange(0, DF_DK, 2, dtype=f32) / DF_DK))
    ang = jnp.arange(seq, dtype=f32)[:, None] * inv[None, :]
    cos, sin = jnp.cos(ang), jnp.sin(ang)
    cos_t = jnp.tile(jnp.concatenate([cos, cos], axis=1), (1, 256 // DF_DK))
    sin_t = jnp.tile(jnp.concatenate([-sin, sin], axis=1), (1, 256 // DF_DK))
    scale = (MLA_NOPE + MLA_ROPE) ** -0.5 * LOG2E
    ones = jnp.ones((seq, MLA_NOPE), f32)
    zeros = jnp.zeros((seq, MLA_NOPE), f32)
    cq = _pad_cols(jnp.concatenate([ones, cos, cos], axis=1) * scale, LANE)
    sq = _pad_cols(jnp.concatenate([zeros, sin, sin], axis=1) * scale, LANE)
    return cos_t, sin_t, cq, sq


def kernel(x, c, w_ada, b_ada, w_in, mla_g_q, mla_g_kv, mla_w_uq, mla_w_ukv, ml_b_i, ml_b_f, sc_w,
           df_lq1, df_lk1, df_lq2, df_lk2, df_g, w_out, ln1_g, ln1_b, w_router, b_router, w1, w3, w2,
           ln2_g, ln2_b):
    batch, seq, d = x.shape
    depth = w_in.shape[0]
    assert MLA_ROPE == DF_DK, "both rotary blocks share one table"
    cos_t, sin_t, cq, sq = _rope_tables(seq)
    place = jnp.zeros((LANE, LANE), f32).at[jnp.arange(MLA_ROPE), MLA_NOPE + jnp.arange(MLA_ROPE)].set(1.0).astype(bf16)
    wr_hi = w_router.astype(bf16)
    wr_split = jnp.concatenate([wr_hi, (w_router - wr_hi.astype(f32)).astype(bf16)], axis=1).T
    mod_all = _ada_mod(c, w_ada, b_ada).reshape(depth, batch, 6, d)
    xf = x.reshape(batch * seq, d)
    for l in range(depth):
        mod = mod_all[l]
        w_packed = _pack_w_in(w_in[l])
        gq = _pad_cols(mla_g_q[l][None, :], 256)
        gkv = mla_g_kv[l][None, :]
        gbias = _pad_cols(jnp.concatenate([ml_b_i[l], ml_b_f[l]])[None, :], LANE)
        wkt = w_packed[:, OFF_MK:OFF_MK + GROUP_W].T
        (qn, kvn, kpe, mq, mk, mv, mo, gcol, yc, dq, dk, dv, mkt) = _inproj(
            xf, mod, w_packed, wkt, cos_t, sin_t, gq, gkv, gbias, sc_w[l], seq)
        wq, wqr, wk, wv = _pack_mla(mla_w_uq[l], mla_w_ukv[l])
        ya = _mla_attention(qn, kvn, kpe, wq, wqr, wk, wv, place, cq, sq, batch, seq)
        yb = _mlstm(mq, mk, mkt, mv, mo, gcol, batch, seq)
        lambda_init = 0.8 - 0.6 * math.exp(-0.3 * l)
        yd = _diff_attention(dq, dk, dv, df_lq1[l][None, :], df_lk1[l][None, :], df_lq2[l][None, :],
                             df_lk2[l][None, :], _pad_cols(df_g[l][None, :], LANE),
                             batch, seq, lambda_init)
        x1, u2e, grp = _outproj_route(ya, yb, yc, yd, xf, mod, w_out[l].astype(bf16), ln1_g[l][None, :],
                                      ln1_b[l][None, :], wr_split, b_router[:, None], seq)
        plan = _dispatch_plan(grp[0], MOE_ROW_TILE)
        ys = _experts(*plan, u2e, w1, w3, w2, l)
        xf = _moe_combine(ys, x1, mod, ln2_g[l][None, :], ln2_b[l][None, :], seq)
    return xf.reshape(batch, seq, d)
```

```python
import functools
import math

import jax
import jax.numpy as jnp
from jax import lax
from jax.experimental import pallas as pl
from jax.experimental.pallas import tpu as pltpu

f32 = jnp.float32
bf16 = jnp.bfloat16
HIGHEST = lax.Precision.HIGHEST

GROUP_W = 256
MLA_HEADS, MLA_NOPE, MLA_ROPE, MLA_V = 4, 64, 32, 64
MLA_Q_LORA, MLA_KV_LORA = 192, 128
ML_HEADS, ML_DH, ML_CHUNK = 4, 64, 64
DF_HEADS, DF_DK, DF_V = 4, 32, 64
ROPE_THETA = 10000.0
N_EXPERTS, N_GROUPS, E_PER_GROUP, D_EXPERT = 16, 4, 4, 256
DEPTH = 2
DN_ALPHA = (2 * DEPTH) ** 0.25
LN_EPS = 1e-5
RMS_EPS = 1e-6
IN_SIZES = (192, 128, 32, 256, 256, 256, 4, 4, 256, 256, 256, 256, 256, 256, 256)

OFF_PQ, OFF_PKV, OFF_KR = 0, 256, 384
OFF_MQ, OFF_MK, OFF_MV, OFF_MG, OFF_MO = 512, 768, 1024, 1280, 1408
OFF_CB, OFF_CC, OFF_CH = 1664, 1920, 2176
OFF_DQ, OFF_DK, OFF_DV = 2432, 2688, 2944
IN_PACKED = 3200

LANE = 128
VMEM_LIMIT = 48 * 1024 * 1024
NEG = -1e30
LOG2E = math.log2(math.e)


def _cparams(sem):
    return pltpu.CompilerParams(dimension_semantics=sem, vmem_limit_bytes=VMEM_LIMIT)


def _sigmoid(x):
    return 1.0 / (1.0 + jnp.exp(-x))


def _layer_norm(x):
    mu = jnp.mean(x, axis=-1, keepdims=True)
    xc = x - mu
    var = jnp.mean(xc * xc, axis=-1, keepdims=True)
    return xc * lax.rsqrt(var + LN_EPS)


def _dot(a, b, **kw):
    return jnp.dot(a, b, preferred_element_type=f32, **kw)


def _dot_nt(a, b, **kw):
    return lax.dot_general(a, b, (((1,), (1,)), ((), ())), preferred_element_type=f32, **kw)


def _dot_tn(a, b, **kw):
    return lax.dot_general(a, b, (((0,), (0,)), ((), ())), preferred_element_type=f32, **kw)


def _expand_groups(cols, rows, width, group):
    lane = lax.broadcasted_iota(jnp.int32, (rows, width), 1)
    out = jnp.broadcast_to(cols[-1], (rows, width))
    for h in range(len(cols) - 2, -1, -1):
        out = jnp.where(lane < group * (h + 1), cols[h], out)
    return out


def _lane_group_mask(shape, lo, hi):
    lane = lax.broadcasted_iota(jnp.int32, shape, 1)
    return (lane >= lo) & (lane < hi)


def _ada_body(c_ref, w_ref, b_ref, o_ref):
    c = c_ref[...]
    ca = (c * _sigmoid(c)).astype(bf16)
    o_ref[0] = _dot(ca, w_ref[0].astype(bf16)) + b_ref[0]


def _ada_mod(c, w_ada, b_ada):
    depth, d, n = w_ada.shape
    b = c.shape[0]
    tn = 1536
    return pl.pallas_call(
        _ada_body,
        grid=(depth, n // tn),
        in_specs=[
            pl.BlockSpec((b, d), lambda l, j: (0, 0)),
            pl.BlockSpec((1, d, tn), lambda l, j: (l, 0, j)),
            pl.BlockSpec((1, 1, tn), lambda l, j: (l, 0, j)),
        ],
        out_specs=pl.BlockSpec((1, b, tn), lambda l, j: (l, 0, j)),
        out_shape=jax.ShapeDtypeStruct((depth, b, n), f32),
        compiler_params=_cparams(("arbitrary", "arbitrary")),
        name="ada_mod",
    )(c, w_ada, b_ada.reshape(depth, 1, n))


def _rope_lanes(x, cos, sin_signed):
    w = x.shape[1]
    lane = lax.broadcasted_iota(jnp.int32, x.shape, 1)
    rot = jnp.where(lane % 32 < 16, pltpu.roll(x, w - 16, axis=1), pltpu.roll(x, 16, axis=1))
    return x * cos + rot * sin_signed


def _inproj_body(x_ref, mod_ref, w_ref, wkt_ref, cos_ref, sin_ref, gq_ref, gkv_ref, gb_ref, cw_ref,
                 qn_ref, kvn_ref, kpe_ref, mq_ref, mk_ref, mv_ref, mo_ref, gcol_ref, yc_ref,
                 dq_ref, dk_ref, dv_ref, mkt_ref, carry_ref, *, tiles_per_seq):
    i = pl.program_id(0)
    tm = x_ref.shape[0]
    u = _layer_norm(x_ref[...]) * (1.0 + mod_ref[0, 1:2, :]) + mod_ref[0, 0:1, :]
    ub = u.astype(bf16)

    def seg(off, n):
        return _dot(ub, w_ref[:, off:off + n])

    cos = cos_ref[...]
    sin = sin_ref[...]

    pq = seg(OFF_PQ, 256)
    ms = jnp.sum(pq * pq, axis=-1, keepdims=True) * (1.0 / MLA_Q_LORA)
    qn_ref[...] = (pq * lax.rsqrt(ms + RMS_EPS) * gq_ref[...]).astype(bf16)
    pkv = seg(OFF_PKV, 128)
    ms = jnp.sum(pkv * pkv, axis=-1, keepdims=True) * (1.0 / MLA_KV_LORA)
    kvn_ref[...] = (pkv * lax.rsqrt(ms + RMS_EPS) * gkv_ref[...]).astype(bf16)
    kpe_ref[...] = _rope_lanes(seg(OFF_KR, 128), cos[:, :LANE], sin[:, :LANE]).astype(bf16)

    mq_ref[...] = (seg(OFF_MQ, 256) * (ML_DH ** -0.5)).astype(bf16)
    mk_ref[...] = seg(OFF_MK, 256).astype(bf16)
    mkt_ref[...] = _dot_nt(wkt_ref[...], ub).astype(bf16)
    mv_ref[...] = seg(OFF_MV, 256).astype(bf16)
    gcol_ref[...] = seg(OFF_MG, 128) + gb_ref[...]
    mo_ref[...] = seg(OFF_MO, 256).astype(bf16)

    uc = seg(OFF_CC, 256) * seg(OFF_CH, 256)

    @pl.when(i % tiles_per_seq == 0)
    def _():
        carry_ref[...] = jnp.zeros_like(carry_ref)

    prev = carry_ref[...]
    row =lax.broadcasted_iota(jnp.int32, uc.shape, 0)
    u1 = jnp.where(row == 0, prev[7:8, :], pltpu.roll(uc, 1, axis=0))
    u2 = jnp.where(row == 0, prev[6:7, :], jnp.where(row == 1, prev[7:8, :], pltpu.roll(uc, 2, axis=0)))
    carry_ref[...] = uc[tm - 8:, :]
    conv = cw_ref[0:1, :] * u2 + cw_ref[1:2, :] * u1 + cw_ref[2:3, :] * uc
    yc_ref[...] = (seg(OFF_CB, 256) * conv).astype(bf16)

    dq_ref[...] = (_rope_lanes(seg(OFF_DQ, 256), cos, sin) * (DF_DK ** -0.5 * LOG2E)).astype(bf16)
    dk_ref[...] = _rope_lanes(seg(OFF_DK, 256), cos, sin).astype(bf16)
    dv_ref[...] = seg(OFF_DV, 256).astype(bf16)


def _inproj(x2d, mod, w_packed, wkt, cos_t, sin_t, gq, gkv, gbias, conv_w, seq):
    t, d = x2d.shape
    tm = 512
    tps = seq // tm
    row = lambda w: pl.BlockSpec((tm, w), lambda i: (i, 0))
    const = lambda a: pl.BlockSpec(a.shape, lambda i: (0,) * a.ndim)
    widths = (256, 128, 128, 256, 256, 256, 256, 128, 256, 256, 256, 256)
    dtypes = (bf16, bf16, bf16, bf16, bf16, bf16, bf16, f32, bf16, bf16, bf16, bf16)
    return pl.pallas_call(
        functools.partial(_inproj_body, tiles_per_seq=tps),
        grid=(t // tm,),
        in_specs=[
            row(d),
            pl.BlockSpec((1, 6, d), lambda i: (i // tps, 0, 0)),
            const(w_packed), const(wkt),
            pl.BlockSpec((tm, 256), lambda i: (i % tps, 0)),
            pl.BlockSpec((tm, 256), lambda i: (i % tps, 0)),
            const(gq), const(gkv), const(gbias), const(conv_w),
        ],
        out_specs=[row(w) for w in widths] + [pl.BlockSpec((GROUP_W, tm), lambda i: (0, i))],
        out_shape=[jax.ShapeDtypeStruct((t, w), dt) for w, dt in zip(widths, dtypes)]
        + [jax.ShapeDtypeStruct((GROUP_W, t), bf16)],
        scratch_shapes=[pltpu.VMEM((8, 256), f32)],
        compiler_params=_cparams(("arbitrary",)),
        name="inproj",
    )(x2d, mod, w_packed, wkt, cos_t, sin_t, gq, gkv, gbias, conv_w)


def _causal_softmax_heads(q_heads, k_at, v_at, qi, s_scr, m_scr, acc_scr):
    n_heads = len(q_heads)
    tq = q_heads[0].shape[0]
    half = tq // 2
    rows = pl.ds(pl.multiple_of(qi * tq, tq), tq)

    def scores(g, j):
        return _dot_nt(q_heads[g], k_at(g, pl.multiple_of(j * tq, tq)))

    def fold_max(g, s):
        m_scr[g] = jnp.maximum(m_scr[g], jnp.maximum(s[:, :half], s[:, half:]))

    m_scr[...] = jnp.full(m_scr.shape, NEG, f32)
    for g in range(n_heads):
        acc_scr[g, rows, :] = jnp.zeros((tq, LANE), f32)

    def score_blocks(js):
        for g in range(n_heads):
            mx = None
            for j in js:
                s = scores(g, j)
                s_scr[g, j] = s
                s = jnp.maximum(s[:, :half], s[:, half:])
                mx = s if mx is None else jnp.maximum(mx, s)
            m_scr[g] = jnp.maximum(m_scr[g], mx)

    def prob_blocks(js):
        for g in range(n_heads):
            mb = m_scr[g]
            pv = None
            for j in js:
                sj = s_scr[g, j]
                p = jnp.concatenate([jnp.exp2(sj[:, :half] - mb), jnp.exp2(sj[:, half:] - mb)],
                                    axis=1).astype(bf16)
                d = _dot(p, v_at(g, pl.multiple_of(j * tq, tq)))
                pv = d if pv is None else pv + d
            acc_scr[g, rows, :] += pv

    def over_full_blocks(blocks_fn):
        @pl.loop(0, qi // 2)
        def _(jj):
            blocks_fn((2 * jj, 2 * jj + 1))

        @pl.when(qi % 2 == 1)
        def _():
            blocks_fn((qi - 1,))

    over_full_blocks(score_blocks)
    rowi = lax.broadcasted_iota(jnp.int32, (tq, tq), 0)
    coli = lax.broadcasted_iota(jnp.int32, (tq, tq), 1)
    for g in range(n_heads):
        s = jnp.where(coli <= rowi, scores(g, qi), NEG)
        s_scr[g, qi] = s
        fold_max(g, s)
        m_scr[g] = jnp.broadcast_to(jnp.max(m_scr[g], axis=1, keepdims=True), (tq, half))

    over_full_blocks(prob_blocks)
    prob_blocks((qi,))


def _softmax_scratch(n_heads, nq, tq):
    return [pltpu.VMEM((n_heads, nq, tq, tq), f32), pltpu.VMEM((n_heads, tq, tq // 2), f32),
            pltpu.VMEM((n_heads, nq * tq, LANE), f32)]


EPILOGUE_ROWS = 512


def _with_ones_lanes(v, dv):
    return jnp.concatenate([v.astype(bf16), jnp.ones((v.shape[0], LANE - dv), bf16)], axis=1)


def _softmax_normalise(acc, dv):
    lane = lax.broadcasted_iota(jnp.int32, acc.shape, 1)
    return jnp.where(lane < dv, acc / pltpu.roll(acc, LANE - dv, axis=1), 0.0)


def _pair_lanes(a, b, dv):
    lane = lax.broadcasted_iota(jnp.int32, a.shape, 1)
    return jnp.where(lane < dv, a, pltpu.roll(b, dv, axis=1))


def _mla_body(qn_ref, kvn_ref, kpe_ref, wq_ref, wqr_ref, wk_ref, wv_ref, place_ref, cq_ref, sq_ref,
              o_ref, q_scr, k_scr, v_scr, s_scr, m_scr, acc_scr):
    qi = pl.program_id(1)
    tq = s_scr.shape[2]

    @pl.when(qi == 0)
    def _():
        kvn = kvn_ref[...]
        kpe_placed = _dot(kpe_ref[...], place_ref[...])
        for h in range(MLA_HEADS):
            k_scr[h] = (_dot(kvn, wk_ref[h]) + kpe_placed).astype(bf16)
            v_scr[h] = _with_ones_lanes(_dot(kvn, wv_ref[h]), MLA_V)
            for r0 in range(0, qn_ref.shape[0], EPILOGUE_ROWS):
                rows = pl.ds(r0, EPILOGUE_ROWS)
                qn = qn_ref[rows, :]
                q_scr[h, rows, :] = (_dot(qn, wq_ref[h]) * cq_ref[rows, :]
                                     + _dot(qn, wqr_ref[h]) * sq_ref[rows, :]).astype(bf16)

    q_rows = pl.ds(pl.multiple_of(qi * tq, tq), tq)
    q_heads = [q_scr[h, q_rows, :] for h in range(MLA_HEADS)]
    _causal_softmax_heads(
        q_heads, lambda g, start: k_scr[g, pl.ds(start, tq), :],
        lambda g, start: v_scr[g, pl.ds(start, tq), :], qi, s_scr, m_scr, acc_scr)

    @pl.when(qi == pl.num_programs(1) - 1)
    def _():
        for r0 in range(0, o_ref.shape[0], EPILOGUE_ROWS):
            rows = pl.ds(r0, EPILOGUE_ROWS)
            o = [_softmax_normalise(acc_scr[h, rows, :], MLA_V) for h in range(MLA_HEADS)]
            o_ref[rows, :] = jnp.concatenate(
                [_pair_lanes(o[0], o[1], MLA_V), _pair_lanes(o[2], o[3], MLA_V)], axis=1).astype(bf16)


def _mla_attention(qn, kvn, kpe, wq, wqr, wk, wv, place, cq, sq, batch, seq):
    tq = 256
    nq = seq // tq
    const = lambda a: pl.BlockSpec(a.shape, lambda b, i: (0,) * a.ndim)
    return pl.pallas_call(
        _mla_body,
        grid=(batch, nq),
        in_specs=[
            pl.BlockSpec((seq, 256), lambda b, i: (b, 0)),
            pl.BlockSpec((seq, 128), lambda b, i: (b, 0)),
            pl.BlockSpec((seq, 128), lambda b, i: (b, 0)),
            const(wq), const(wqr), const(wk), const(wv), const(place), const(cq), const(sq),
        ],
        out_specs=pl.BlockSpec((seq, GROUP_W), lambda b, i: (b, 0)),
        out_shape=jax.ShapeDtypeStruct((batch * seq, GROUP_W), bf16),
        scratch_shapes=[pltpu.VMEM((MLA_HEADS, seq, 128), bf16),
                        pltpu.VMEM((MLA_HEADS, seq, 128), bf16),
                        pltpu.VMEM((MLA_HEADS, seq, LANE), bf16)]
        + _softmax_scratch(MLA_HEADS, nq, tq),
        compiler_params=_cparams(("arbitrary", "arbitrary")),
        name="mla_attn",
    )(qn, kvn, kpe, wq, wqr, wk, wv, place, cq, sq)


def _diff_body(q_ref, k_ref, v_ref, lq1_ref, lk1_ref, lq2_ref, lk2_ref, g_ref,
               o_ref, v_scr, s_scr, m_scr, acc_scr, *, lambda_init):
    qi = pl.program_id(1)
    tq = q_ref.shape[0]

    @pl.when(qi == 0)
    def _():
        v = v_ref[...]
        for h in range(DF_HEADS):
            v_scr[h] = _with_ones_lanes(v[:, DF_V * h:DF_V * (h + 1)], DF_V)

    q = q_ref[...]
    q_heads = [jnp.where(_lane_group_mask(q.shape, DF_DK * g, DF_DK * (g + 1)), q, jnp.zeros_like(q))
               for g in range(2 * DF_HEADS)]
    _causal_softmax_heads(
        q_heads, lambda g, start: k_ref[pl.ds(start, tq), :],
        lambda g, start: v_scr[g // 2, pl.ds(start, tq), :], qi, s_scr, m_scr, acc_scr)

    @pl.when(qi == pl.num_programs(1) - 1)
    def _():
        lam = (jnp.exp(jnp.sum(lq1_ref[...] * lk1_ref[...], axis=1, keepdims=True))
               - jnp.exp(jnp.sum(lq2_ref[...] * lk2_ref[...], axis=1, keepdims=True)) + lambda_init)
        gain = g_ref[...] * (1.0 - lambda_init)
        for r0 in range(0, o_ref.shape[0], EPILOGUE_ROWS):
            rows = pl.ds(r0, EPILOGUE_ROWS)
            ys = []
            for h in range(DF_HEADS):
                o = (_softmax_normalise(acc_scr[2 * h, rows, :], DF_V)
                     - lam * _softmax_normalise(acc_scr[2 * h + 1, rows, :], DF_V))
                ms = jnp.sum(o * o, axis=1, keepdims=True) * (1.0 / DF_V)
                ys.append(o * lax.rsqrt(ms + RMS_EPS) * gain)
            o_ref[rows, :] = jnp.concatenate(
                [_pair_lanes(ys[0], ys[1], DF_V), _pair_lanes(ys[2], ys[3], DF_V)], axis=1).astype(bf16)


def _diff_attention(dq, dk, dv, lq1, lk1, lq2, lk2, g_sub, batch, seq, lambda_init):
    tq = 256
    nq = seq // tq
    const = lambda a: pl.BlockSpec(a.shape, lambda b, i: (0,) * a.ndim)
    return pl.pallas_call(
        functools.partial(_diff_body, lambda_init=lambda_init),
        grid=(batch, nq),
        in_specs=[
            pl.BlockSpec((tq, 256), lambda b, i: (b * nq + i, 0)),
            pl.BlockSpec((seq, 256), lambda b, i: (b, 0)),
            pl.BlockSpec((seq, 256), lambda b, i: (b, 0)),
            const(lq1), const(lk1), const(lq2), const(lk2), const(g_sub),
        ],
        out_specs=pl.BlockSpec((seq, GROUP_W), lambda b, i: (b, 0)),
        out_shape=jax.ShapeDtypeStruct((batch * seq, GROUP_W), bf16),
        scratch_shapes=[pltpu.VMEM((DF_HEADS, seq, LANE), bf16)]
        + _softmax_scratch(2 * DF_HEADS, nq, tq),
        compiler_params=_cparams(("arbitrary", "arbitrary")),
        name="diff_attn",
    )(dq, dk, dv, lq1, lk1, lq2, lk2, g_sub)


def _split3(a):
    hi = a.astype(bf16)
    r1 = a - hi.astype(f32)
    mid = r1.astype(bf16)
    return hi, mid, (r1 - mid.astype(f32)).astype(bf16)


def _dot_exact_rhs01(a, b01):
    hi, mid, lo = _split3(a)
    return _dot(hi, b01) + _dot(mid, b01) + _dot(lo, b01)


def _dot_exact_lhs01(a01, b):
    hi, mid, lo = _split3(b)
    return _dot(a01, hi) + _dot(a01, mid) + _dot(a01, lo)


def _mlstm_gate_terms(g, eif):
    R, W, L = g.shape[0], eif.shape[1] // 2, ML_CHUNK
    lf = jnp.minimum(g, 0.0) - jnp.log(1.0 + jnp.exp(-jnp.abs(g)))
    glane = lax.broadcasted_iota(jnp.int32, g.shape, 1)
    wide = _dot_exact_rhs01(jnp.where(glane < ML_HEADS, g, lf), eif)
    iw, lfw = wide[:, :W], wide[:, W:]
    rr = lax.broadcasted_iota(jnp.int32, (R, R), 0)
    cc = lax.broadcasted_iota(jnp.int32, (R, R), 1)
    bw = _dot_exact_lhs01(((rr >= cc) & (rr // L == cc // L)).astype(bf16), lfw)
    rw = iw - bw
    pos_in_chunk = lax.broadcasted_iota(jnp.int32, (R, W), 0) % L
    cm = rw
    for sh in (1, 2, 4, 8, 16, 32):
        cm = jnp.maximum(cm, jnp.where(pos_in_chunk >= sh, pltpu.roll(cm, sh, axis=0), NEG))
    return bw, rw, cm


def _mlstm_chunk(q, k, kt_bd, v, o, bw, rw, cm, c_prev, n_prev, m_prev):
    L, W = q.shape
    row = lax.broadcasted_iota(jnp.int32, (L, W), 0)
    pos = lax.broadcasted_iota(jnp.int32, (L, W), 1) % L
    mw = jnp.maximum(m_prev, cm)
    a_inter = jnp.exp(m_prev - mw)
    r_row = jnp.sum(jnp.where(pos == row, rw, 0.0), axis=0, keepdims=True)
    ew = jnp.exp(jnp.where(pos <= row, r_row - mw, NEG))
    qk_qc = _dot(q, jnp.concatenate([kt_bd.astype(bf16), c_prev.astype(bf16)], axis=1))
    pw = qk_qc[:, :W] * ew
    head_eq = (lax.broadcasted_iota(jnp.int32, (W, W), 0) // ML_DH
               == lax.broadcasted_iota(jnp.int32, (W, W), 1) // ML_DH)
    v4 = jnp.concatenate([v] * ML_HEADS, axis=0)
    vbd = jnp.where(head_eq, v4, jnp.zeros_like(v4))
    sums = _dot_exact_rhs01(jnp.concatenate([q.astype(f32) * n_prev, pw], axis=0), head_eq.astype(bf16))
    den = a_inter * sums[:L] + sums[L:]
    hden = jnp.maximum(jnp.abs(den), jnp.exp(-(bw + mw)))
    num = a_inter * qk_qc[:, W:] + _dot(pw.astype(bf16), vbd)
    y = num / hden * _sigmoid(o.astype(f32))

    rmax, m_last = cm[L - 1:L, :], mw[L - 1:L, :]
    a_w = jnp.exp(m_prev - m_last)
    b_w = jnp.exp(rmax - m_last)
    c_loc = _dot((kt_bd * jnp.exp(r_row - rmax)).astype(bf16), vbd)
    n_loc = jnp.sum(k.astype(f32) * jnp.exp(rw - rmax), axis=0, keepdims=True)
    return y, c_prev * a_w + c_loc * b_w, a_w * n_prev + b_w * n_loc, bw[L - 1:L, :] + m_last


def _mlstm_body(q_ref, k_ref, kt_ref, v_ref, o_ref, g_ref, eif_ref, sel_ref, y_ref, c_scr, nm_scr):
    @pl.when(pl.program_id(1) == 0)
    def _():
        c_scr[...] = jnp.zeros_like(c_scr)
        nm_scr[...] = jnp.zeros_like(nm_scr)

    c, n, m = c_scr[...], nm_scr[0:1, :], nm_scr[1:2, :]
    kt = kt_ref[...]
    eif = eif_ref[...]
    w = kt.shape[0]
    head_eq = (lax.broadcasted_iota(jnp.int32, (w, w), 0) // ML_DH
               == lax.broadcasted_iota(jnp.int32, (w, w), 1) // ML_DH)
    bw, rw, cm = _mlstm_gate_terms(g_ref[...], eif)
    for i in range(q_ref.shape[0] // ML_CHUNK):
        rows = pl.ds(i * ML_CHUNK, ML_CHUNK)
        lo, hi = i * ML_CHUNK, (i + 1) * ML_CHUNK
        kt_bd = jnp.where(head_eq, _dot(kt, sel_ref[i]), 0.0)
        y, c, n, m = _mlstm_chunk(q_ref[rows, :], k_ref[rows, :], kt_bd, v_ref[rows, :], o_ref[rows, :],
                                  bw[lo:hi], rw[lo:hi], cm[lo:hi], c, n, m)
        y_ref[rows, :] = y.astype(y_ref.dtype)
    c_scr[...] = c
    nm_scr[0:1, :] = n
    nm_scr[1:2, :] = m


MLSTM_CHUNKS_PER_STEP = 4


def _mlstm(mq, mk, mkt, mv, mo, gcol, batch, seq):
    cps = MLSTM_CHUNKS_PER_STEP
    L = ML_CHUNK * cps
    nc = seq // L
    blk = lambda w: pl.BlockSpec((L, w), lambda b, c: (b * nc + c, 0))
    const = lambda a: pl.BlockSpec(a.shape, lambda b, c: (0,) * a.ndim)
    gl = jnp.arange(LANE)[:, None]
    col = jnp.arange(2 * GROUP_W)[None, :]
    eif = (gl == (col // GROUP_W) * ML_HEADS + (col % GROUP_W) // ML_DH).astype(bf16)
    p = jnp.arange(L)[None, :, None]
    cc = jnp.arange(GROUP_W)[None, None, :]
    sel = (p == jnp.arange(cps)[:, None, None] * ML_CHUNK + cc % ML_CHUNK).astype(bf16)
    return pl.pallas_call(
        _mlstm_body,
        grid=(batch, nc),
        in_specs=[blk(256), blk(256), pl.BlockSpec((GROUP_W, L), lambda b, c: (0, b * nc + c)),
                  blk(256), blk(256), blk(128), const(eif), const(sel)],
        out_specs=blk(GROUP_W),
        out_shape=jax.ShapeDtypeStruct((batch * seq, GROUP_W), bf16),
        scratch_shapes=[pltpu.VMEM((GROUP_W, GROUP_W), f32), pltpu.VMEM((8, GROUP_W), f32)],
        compiler_params=_cparams(("arbitrary", "arbitrary")),
        name="mlstm",
    )(mq, mk, mkt, mv, mo, gcol, eif, sel)


def _route(scores, sel):
    ne, tm = sel.shape
    eid = lax.broadcasted_iota(jnp.int32, (ne, tm), 0)
    grp = eid // E_PER_GROUP
    big = ne + 1

    def top2(vals):
        m1 = jnp.max(vals, axis=0, keepdims=True)
        i1 = jnp.min(jnp.where(vals == m1, eid, big), axis=0, keepdims=True)
        rest = jnp.where(eid == i1, NEG, vals)
        m2 = jnp.max(rest, axis=0, keepdims=True)
        i2 = jnp.min(jnp.where(rest == m2, eid, big), axis=0, keepdims=True)
        return m1, i1, m2, i2

    best_score, best_grp = None, None
    for gi in range(N_GROUPS):
        m1, _, m2, _ = top2(jnp.where(grp == gi, sel, NEG))
        sc = m1 + m2
        if best_score is None:
            best_score, best_grp = sc, jnp.zeros_like(sc, dtype=jnp.int32)
        else:
            better = sc > best_score
            best_grp = jnp.where(better, gi, best_grp)
            best_score = jnp.where(better, sc, best_score)
    _, i1, _, i2 = top2(jnp.where(grp == best_grp, sel, NEG))
    picked = jnp.where((eid == i1) | (eid == i2), scores, 0.0)
    return picked / jnp.sum(picked, axis=0, keepdims=True), best_grp


MOE_ROW_TILE = 256


def _outproj_route_body(ya_ref, yb_ref, yc_ref, yd_ref, x_ref, mod_ref, w_ref, g_ref, b_ref,
                        wr_ref, br_ref, x1_ref, u2e_ref, grp_ref):
    d = x_ref.shape[1]
    mix = None
    for j, y_ref in enumerate((ya_ref, yb_ref, yc_ref, yd_ref)):
        part = _dot(y_ref[...], w_ref[GROUP_W * j:GROUP_W * (j + 1), :])
        mix = part if mix is None else mix + part
    x1 = _layer_norm(DN_ALPHA * x_ref[...] + mod_ref[0, 2:3, :] * mix) * g_ref[...] + b_ref[...]
    x1_ref[...] = x1
    u = _layer_norm(x1) * (1.0 + mod_ref[0, 4:5, :]) + mod_ref[0, 3:4, :]
    u_hi = u.astype(bf16)
    u_lo = (u - u_hi.astype(f32)).astype(bf16)
    ne = br_ref.shape[0]
    part = _dot_nt(wr_ref[...], u_hi)
    logits = part[:ne] + part[ne:] + _dot_nt(wr_ref[:ne, :], u_lo)
    scores = _sigmoid(logits)
    gates_t, best_grp = _route(scores, scores + br_ref[...])
    u2e_ref[:, :d] = u_hi.astype(f32)
    u2e_ref[:, d:] = jnp.zeros((x1.shape[0], LANE), f32)
    u2e_ref[:, d:d + ne] = gates_t.T
    grp_ref[...] = best_grp


def _outproj_route(ya, yb, yc, yd, x2d, mod, w_out, ln_g, ln_b, wr_split, b_router, seq):
    t, d = x2d.shape
    tm = 512
    tps = seq // tm
    row = lambda w: pl.BlockSpec((tm, w), lambda i: (i, 0))
    const = lambda a: pl.BlockSpec(a.shape, lambda i: (0,) * a.ndim)
    return pl.pallas_call(
        _outproj_route_body,
        grid=(t // tm,),
        in_specs=[row(256), row(256), row(256), row(256), row(d),
                  pl.BlockSpec((1, 6, d), lambda i: (i // tps, 0, 0)),
                  const(w_out), const(ln_g), const(ln_b), const(wr_split), const(b_router)],
        out_specs=[row(d), row(d + LANE), pl.BlockSpec((1, tm), lambda i: (0, i))],
        out_shape=[jax.ShapeDtypeStruct((t, d), f32), jax.ShapeDtypeStruct((t, d + LANE), f32),
                   jax.ShapeDtypeStruct((1, t), jnp.int32)],
        compiler_params=_cparams(("arbitrary",)),
        name="outproj_route",
    )(ya, yb, yc, yd, x2d, mod, w_out, ln_g, ln_b, wr_split, b_router)


def _dispatch_plan(grp, tr):
    t = grp.shape[0]
    rows = t + N_GROUPS * tr
    order = jnp.sort(grp * t + jnp.arange(t, dtype=jnp.int32)) % t
    order = jnp.concatenate([order, jnp.zeros((rows - t,), order.dtype)])
    counts = jnp.sum((grp[:, None] == jnp.arange(N_GROUPS)[None, :]).astype(jnp.int32), axis=0)
    padded = ((counts + tr - 1) // tr) * tr
    row_end = jnp.cumsum(padded)
    r = jnp.arange(rows, dtype=jnp.int32)
    read_tok = jnp.zeros((rows,), jnp.int32)
    write_row = jnp.zeros((rows,), jnp.int32)
    for g in range(N_GROUPS):
        row_start = row_end[g] - padded[g]
        tok_start = jnp.sum(counts[:g])
        shifted = jnp.roll(order, row_start - tok_start)
        in_grp = (r >= row_start) & (r < row_end[g])
        valid = in_grp & (r < row_start + counts[g])
        read_tok = jnp.where(valid, shifted, jnp.where(in_grp, order[jnp.minimum(tok_start, t - 1)], read_tok))
        write_row = jnp.where(valid, shifted, jnp.where(in_grp, t + g * tr + r % tr, write_row))
    tile_start = jnp.arange(rows // tr, dtype=jnp.int32) * tr
    tile_grp = jnp.minimum(jnp.sum((tile_start[:, None] >= row_end[None, :]).astype(jnp.int32), axis=1),
                           N_GROUPS - 1)
    i32 = lambda a: a.astype(jnp.int32)
    return i32(read_tok), i32(write_row), i32(tile_grp), i32(row_end[-1:] // tr)


def _wait_row_gather(n_rows, hbm, vmem, sem):
    pltpu.make_async_copy(hbm.at[pl.ds(0, n_rows), :], vmem, sem).wait()


def _wait_row_scatter(n_rows, vmem, hbm, sem):
    pltpu.make_async_copy(vmem, hbm.at[pl.ds(0, n_rows), :], sem).wait()


def _experts_body(read_ref, write_ref, tgrp_ref, ntile_ref,
                  u2e_hbm, w1_ref, w3_ref, w2_ref, ys_hbm, xbuf0, xbuf1, obuf0, obuf1, w1b, w3b, w2b,
                  gsem, ssem):
    j = pl.program_id(0)
    n_tiles = ntile_ref[0]
    tr, d = obuf0.shape
    xbuf, obuf = (xbuf0, xbuf1), (obuf0, obuf1)

    def start_gather(tile, s):
        for i in range(tr):
            pltpu.make_async_copy(u2e_hbm.at[pl.ds(read_ref[tile * tr + i], 1), :],
                                  xbuf[s].at[pl.ds(i, 1), :], gsem.at[s]).start()

    def start_scatter(tile, s):
        for i in range(tr):
            pltpu.make_async_copy(obuf[s].at[pl.ds(i, 1), :],
                                  ys_hbm.at[pl.ds(write_ref[tile * tr + i], 1), :], ssem.at[s]).start()

    def load_tile(s):
        xs = xbuf[s][...]
        return xs[:, :d].astype(bf16), xs[:, d:]

    def compute(tile, s, xb, gslab):
        lane = lax.broadcasted_iota(jnp.int32, gslab.shape, 1)
        first_expert = tgrp_ref[tile] * E_PER_GROUP
        acc = None
        for e in range(E_PER_GROUP):
            ge = jnp.sum(jnp.where(lane == first_expert + e, gslab, 0.0), axis=1, keepdims=True)
            h1 = _dot(xb, w1b[e])
            h = h1 * _sigmoid(h1) * _dot(xb, w3b[e]) * ge
            part = _dot(h.astype(bf16), w2b[e])
            acc = part if acc is None else acc + part
        obuf[s][...] = acc

    nxt = jnp.minimum(j + 1, n_tiles - 1)

    last = tgrp_ref.shape[0] - 1
    @pl.when((j < n_tiles) & ((j == 0) | (tgrp_ref[jnp.minimum(j, last)] != tgrp_ref[jnp.clip(j - 1, 0, last)])))
    def _():
        w1b[...] = w1_ref[0].astype(bf16)
        w3b[...] = w3_ref[0].astype(bf16)
        w2b[...] = w2_ref[0].astype(bf16)

    @pl.when(j == 0)
    def _():
        start_gather(0, 0)
        start_gather(nxt, 1)
        obuf[1][...] = jnp.zeros_like(obuf[1])
        n_tok = ys_hbm.shape[0] - N_GROUPS * tr
        spare = [pltpu.make_async_copy(obuf[1], ys_hbm.at[pl.ds(n_tok + g * tr, tr), :], ssem.at[1])
                 for g in range(N_GROUPS)]
        for cp in spare:
            cp.start()
        for cp in spare:
            cp.wait()
        _wait_row_gather(tr, u2e_hbm, xbuf[0], gsem.at[0])
        compute(0, 0, *load_tile(0))

    for s in (0, 1):
        @pl.when((j > 0) & (j < n_tiles) & (j % 2 == s))
        def _(s=s):
            start_gather(nxt, 1 - s)
            start_scatter(j - 1, 1 - s)

        @pl.when((j > 0) & (j < n_tiles) & (j % 2 == s))
        def _(s=s):
            _wait_row_gather(tr, u2e_hbm, xbuf[s], gsem.at[s])
            compute(j, s, *load_tile(s))
            _wait_row_scatter(tr, obuf[1 - s], ys_hbm, ssem.at[1 - s])

        @pl.when((j == n_tiles) & (j % 2 == s))
        def _(s=s):
            _wait_row_gather(tr, u2e_hbm, xbuf[s], gsem.at[s])
            start_scatter(j - 1, 1 - s)
            _wait_row_scatter(tr, obuf[1 - s], ys_hbm, ssem.at[1 - s])


def _experts(read_tok, write_row, tile_grp, n_tiles, u2e, w1, w3, w2, layer):
    rows = read_tok.shape[0]
    tr = MOE_ROW_TILE
    max_tiles = tile_grp.shape[0]
    _, ne, d, fe = w1.shape
    grouped = lambda w: w.reshape(w.shape[0], N_GROUPS, E_PER_GROUP, *w.shape[2:])
    wspec = lambda a, b: pl.BlockSpec(
        (None, 1, E_PER_GROUP, a, b),
        lambda j, rd, wr, tg, nt: (layer, tg[jnp.minimum(j, max_tiles - 1)], 0, 0, 0))
    return pl.pallas_call(
        _experts_body,
        grid_spec=pltpu.PrefetchScalarGridSpec(
            num_scalar_prefetch=4, grid=(max_tiles + 1,),
            in_specs=[pl.BlockSpec(memory_space=pl.ANY), wspec(d, fe), wspec(d, fe), wspec(fe, d)],
            out_specs=pl.BlockSpec(memory_space=pl.ANY),
            scratch_shapes=[pltpu.VMEM((tr, u2e.shape[1]), f32), pltpu.VMEM((tr, u2e.shape[1]), f32),
                            pltpu.VMEM((tr, d), f32), pltpu.VMEM((tr, d), f32),
                            pltpu.VMEM((E_PER_GROUP, d, fe), bf16), pltpu.VMEM((E_PER_GROUP, d, fe), bf16),
                            pltpu.VMEM((E_PER_GROUP, fe, d), bf16),
                            pltpu.SemaphoreType.DMA((2,)), pltpu.SemaphoreType.DMA((2,))]),
        out_shape=jax.ShapeDtypeStruct((rows, d), f32),
        compiler_params=_cparams(("arbitrary",)),
        name="experts",
    )(read_tok, write_row, tile_grp, n_tiles, u2e, grouped(w1), grouped(w3), grouped(w2))


def _moe_combine_body(y_ref, x_ref, mod_ref, g_ref, b_ref, o_ref):
    r = DN_ALPHA * x_ref[...] + mod_ref[0, 5:6, :] * y_ref[...]
    o_ref[...] = _layer_norm(r) * g_ref[...] + b_ref[...]


def _moe_combine(ys, x1, mod, ln_g, ln_b, seq):
    t, d = x1.shape
    tm = 512
    tps = seq // tm
    row = pl.BlockSpec((tm, d), lambda i: (i, 0))
    const = lambda a: pl.BlockSpec(a.shape, lambda i: (0,) * a.ndim)
    return pl.pallas_call(
        _moe_combine_body,
        grid=(t // tm,),
        in_specs=[row, row, pl.BlockSpec((1, 6, d), lambda i: (i // tps, 0, 0)), const(ln_g), const(ln_b)],
        out_specs=row,
        out_shape=jax.ShapeDtypeStruct((t, d), f32),
        compiler_params=_cparams(("arbitrary",)),
        name="moe_combine",
    )(ys, x1, mod, ln_g, ln_b)


def _pad_cols(a, width):
    return jnp.pad(a, ((0, 0), (0, width - a.shape[1])))


def _pack_w_in(w):
    parts, acc = [], 0
    for sz in IN_SIZES:
        parts.append(w[:, acc:acc + sz])
        acc += sz
    (a_q, a_kv, a_kr, m_q, m_k, m_v, m_i, m_f, m_o, c_b, c_c, c_h, d_q, d_k, d_v) = parts
    gates = _pad_cols(jnp.concatenate([m_i, m_f], axis=1), LANE)
    packed = jnp.concatenate([
        _pad_cols(a_q, 256), a_kv, _pad_cols(a_kr, LANE),
        m_q, m_k, m_v, gates, m_o, c_b, c_c, c_h, d_q, d_k, d_v], axis=1)
    return packed.astype(bf16)


def _pack_mla(w_uq, w_ukv):
    half = MLA_ROPE // 2
    wq3 = w_uq.reshape(MLA_Q_LORA, MLA_HEADS, MLA_NOPE + MLA_ROPE).transpose(1, 0, 2)
    wkv3 = w_ukv.reshape(MLA_KV_LORA, MLA_HEADS, MLA_NOPE + MLA_V).transpose(1, 0, 2)
    nope, pe = wq3[:, :, :MLA_NOPE], wq3[:, :, MLA_NOPE:]
    rot = jnp.concatenate([-pe[:, :, half:], pe[:, :, :half]], axis=2)
    pad = lambda a: jnp.pad(a, ((0, 0), (0, 256 - a.shape[1]), (0, LANE - a.shape[2]))).astype(bf16)
    wq = pad(wq3)
    wqr = pad(jnp.concatenate([jnp.zeros_like(nope), rot], axis=2))
    wk = jnp.pad(wkv3[:, :, :MLA_NOPE], ((0, 0), (0, 0), (0, LANE - MLA_NOPE))).astype(bf16)
    return wq, wqr, wk, wkv3[:, :, MLA_NOPE:].astype(bf16)


def _rope_tables(seq):
    half = DF_DK // 2
    inv = 1.0 / (ROPE_THETA ** (jnp.arange(0, DF_DK, 2, dtype=f32) / DF_DK))
    ang = jnp.arange(seq, dtype=f32)[:, None] * inv[None, :]
    cos, sin = jnp.cos(ang), jnp.sin(ang)
    cos_t = jnp.tile(jnp.concatenate([cos, cos], axis=1), (1, 256 // DF_DK))
    sin_t = jnp.tile(jnp.concatenate([-sin, sin], axis=1), (1, 256 // DF_DK))
    scale = (MLA_NOPE + MLA_ROPE) ** -0.5 * LOG2E
    ones = jnp.ones((seq, MLA_NOPE), f32)
    zeros = jnp.zeros((seq, MLA_NOPE), f32)
    cq = _pad_cols(jnp.concatenate([ones, cos, cos], axis=1) * scale, LANE)
    sq = _pad_cols(jnp.concatenate([zeros, sin, sin], axis=1) * scale, LANE)
    return cos_t, sin_t, cq, sq


def kernel(x, c, w_ada, b_ada, w_in, mla_g_q, mla_g_kv, mla_w_uq, mla_w_ukv, ml_b_i, ml_b_f, sc_w,
           df_lq1, df_lk1, df_lq2, df_lk2, df_g, w_out, ln1_g, ln1_b, w_router, b_router, w1, w3, w2,
           ln2_g, ln2_b):
    batch, seq, d = x.shape
    depth = w_in.shape[0]
    assert MLA_ROPE == DF_DK, "both rotary blocks share one table"
    cos_t, sin_t, cq, sq = _rope_tables(seq)
    place = jnp.zeros((LANE, LANE), f32).at[jnp.arange(MLA_ROPE), MLA_NOPE + jnp.arange(MLA_ROPE)].set(1.0).astype(bf16)
    wr_hi = w_router.astype(bf16)
    wr_split = jnp.concatenate([wr_hi, (w_router - wr_hi.astype(f32)).astype(bf16)], axis=1).T
    mod_all = _ada_mod(c, w_ada, b_ada).reshape(depth, batch, 6, d)
    xf = x.reshape(batch * seq, d)
    for l in range(depth):
        mod = mod_all[l]
        w_packed = _pack_w_in(w_in[l])
        gq = _pad_cols(mla_g_q[l][None, :], 256)
        gkv = mla_g_kv[l][None, :]
        gbias = _pad_cols(jnp.concatenate([ml_b_i[l], ml_b_f[l]])[None, :], LANE)
        wkt = w_packed[:, OFF_MK:OFF_MK + GROUP_W].T
        (qn, kvn, kpe, mq, mk, mv, mo, gcol, yc, dq, dk, dv, mkt) = _inproj(
            xf, mod, w_packed, wkt, cos_t, sin_t, gq, gkv, gbias, sc_w[l], seq)
        wq, wqr, wk, wv = _pack_mla(mla_w_uq[l], mla_w_ukv[l])
        ya = _mla_attention(qn, kvn, kpe, wq, wqr, wk, wv, place, cq, sq, batch, seq)
        yb = _mlstm(mq, mk, mkt, mv, mo, gcol, batch, seq)
        lambda_init = 0.8 - 0.6 * math.exp(-0.3 * l)
        yd = _diff_attention(dq, dk, dv, df_lq1[l][None, :], df_lk1[l][None, :], df_lq2[l][None, :],
                             df_lk2[l][None, :], _pad_cols(df_g[l][None, :], LANE),
                             batch, seq, lambda_init)
        x1, u2e, grp = _outproj_route(ya, yb, yc, yd, xf, mod, w_out[l].astype(bf16), ln1_g[l][None, :],
                                      ln1_b[l][None, :], wr_split, b_router[:, None], seq)
        plan = _dispatch_plan(grp[0], MOE_ROW_TILE)
        ys = _experts(*plan, u2e, w1, w3, w2, l)
        xf = _moe_combine(ys, x1, mod, ln2_g[l][None, :], ln2_b[l][None, :], seq)
    return xf.reshape(batch, seq, d)
```

```python
import functools
import math

import jax
import jax.numpy as jnp
from jax import lax
from jax.experimental import pallas as pl
from jax.experimental.pallas import tpu as pltpu

f32 = jnp.float32
bf16 = jnp.bfloat16
HIGHEST = lax.Precision.HIGHEST

GROUP_W = 256
MLA_HEADS, MLA_NOPE, MLA_ROPE, MLA_V = 4, 64, 32, 64
MLA_Q_LORA, MLA_KV_LORA = 192, 128
ML_HEADS, ML_DH, ML_CHUNK = 4, 64, 64
DF_HEADS, DF_DK, DF_V = 4, 32, 64
ROPE_THETA = 10000.0
N_EXPERTS, N_GROUPS, E_PER_GROUP, D_EXPERT = 16, 4, 4, 256
DEPTH = 2
DN_ALPHA = (2 * DEPTH) ** 0.25
LN_EPS = 1e-5
RMS_EPS = 1e-6
IN_SIZES = (192, 128, 32, 256, 256, 256, 4, 4, 256, 256, 256, 256, 256, 256, 256)

OFF_PQ, OFF_PKV, OFF_KR = 0, 256, 384
OFF_MQ, OFF_MK, OFF_MV, OFF_MG, OFF_MO = 512, 768, 1024, 1280, 1408
OFF_CB, OFF_CC, OFF_CH = 1664, 1920, 2176
OFF_DQ, OFF_DK, OFF_DV = 2432, 2688, 2944
IN_PACKED = 3200

LANE = 128
VMEM_LIMIT = 48 * 1024 * 1024
NEG = -1e30
LOG2E = math.log2(math.e)


def _cparams(sem):
    return pltpu.CompilerParams(dimension_semantics=sem, vmem_limit_bytes=VMEM_LIMIT)


def _sigmoid(x):
    return 1.0 / (1.0 + jnp.exp(-x))


def _layer_norm(x):
    mu = jnp.mean(x, axis=-1, keepdims=True)
    xc = x - mu
    var = jnp.mean(xc * xc, axis=-1, keepdims=True)
    return xc * lax.rsqrt(var + LN_EPS)


def _dot(a, b, **kw):
    return jnp.dot(a, b, preferred_element_type=f32, **kw)


def _dot_nt(a, b, **kw):
    return lax.dot_general(a, b, (((1,), (1,)), ((), ())), preferred_element_type=f32, **kw)


def _dot_tn(a, b, **kw):
    return lax.dot_general(a, b, (((0,), (0,)), ((), ())), preferred_element_type=f32, **kw)


def _expand_groups(cols, rows, width, group):
    lane = lax.broadcasted_iota(jnp.int32, (rows, width), 1)
    out = jnp.broadcast_to(cols[-1], (rows, width))
    for h in range(len(cols) - 2, -1, -1):
        out = jnp.where(lane < group * (h + 1), cols[h], out)
    return out


def _lane_group_mask(shape, lo, hi):
    lane = lax.broadcasted_iota(jnp.int32, shape, 1)
    return (lane >= lo) & (lane < hi)


def _ada_body(c_ref, w_ref, b_ref, o_ref):
    c = c_ref[...]
    ca = (c * _sigmoid(c)).astype(bf16)
    o_ref[0] = _dot(ca, w_ref[0].astype(bf16)) + b_ref[0]


def _ada_mod(c, w_ada, b_ada):
    depth, d, n = w_ada.shape
    b = c.shape[0]
    tn = 1536
    return pl.pallas_call(
        _ada_body,
        grid=(depth, n // tn),
        in_specs=[
            pl.BlockSpec((b, d), lambda l, j: (0, 0)),
            pl.BlockSpec((1, d, tn), lambda l, j: (l, 0, j)),
            pl.BlockSpec((1, 1, tn), lambda l, j: (l, 0, j)),
        ],
        out_specs=pl.BlockSpec((1, b, tn), lambda l, j: (l, 0, j)),
        out_shape=jax.ShapeDtypeStruct((depth, b, n), f32),
        compiler_params=_cparams(("arbitrary", "arbitrary")),
        name="ada_mod",
    )(c, w_ada, b_ada.reshape(depth, 1, n))


def _rope_lanes(x, cos, sin_signed):
    w = x.shape[1]
    lane = lax.broadcasted_iota(jnp.int32, x.shape, 1)
    rot = jnp.where(lane % 32 < 16, pltpu.roll(x, w - 16, axis=1), pltpu.roll(x, 16, axis=1))
    return x * cos + rot * sin_signed


def _inproj_body(x_ref, mod_ref, w_ref, wkt_ref, cos_ref, sin_ref, gq_ref, gkv_ref, gb_ref, cw_ref,
                 qn_ref, kvn_ref, kpe_ref, mq_ref, mk_ref, mv_ref, mo_ref, gcol_ref, yc_ref,
                 dq_ref, dk_ref, dv_ref, mkt_ref, carry_ref, *, tiles_per_seq):
    i = pl.program_id(0)
    tm = x_ref.shape[0]
    u = _layer_norm(x_ref[...]) * (1.0 + mod_ref[0, 1:2, :]) + mod_ref[0, 0:1, :]
    ub = u.astype(bf16)

    def seg(off, n):
        return _dot(ub, w_ref[:, off:off + n])

    cos = cos_ref[...]
    sin = sin_ref[...]

    pq = seg(OFF_PQ, 256)
    ms = jnp.sum(pq * pq, axis=-1, keepdims=True) * (1.0 / MLA_Q_LORA)
    qn_ref[...] = (pq * lax.rsqrt(ms + RMS_EPS) * gq_ref[...]).astype(bf16)
    pkv = seg(OFF_PKV, 128)
    ms = jnp.sum(pkv * pkv, axis=-1, keepdims=True) * (1.0 / MLA_KV_LORA)
    kvn_ref[...] = (pkv * lax.rsqrt(ms + RMS_EPS) * gkv_ref[...]).astype(bf16)
    kpe_ref[...] = _rope_lanes(seg(OFF_KR, 128), cos[:, :LANE], sin[:, :LANE]).astype(bf16)

    mq_ref[...] = (seg(OFF_MQ, 256) * (ML_DH ** -0.5)).astype(bf16)
    mk_ref[...] = seg(OFF_MK, 256).astype(bf16)
    mkt_ref[...] = _dot_nt(wkt_ref[...], ub).astype(bf16)
    mv_ref[...] = seg(OFF_MV, 256).astype(bf16)
    gcol_ref[...] = seg(OFF_MG, 128) + gb_ref[...]
    mo_ref[...] = seg(OFF_MO, 256).astype(bf16)

    uc = seg(OFF_CC, 256) * seg(OFF_CH, 256)

    @pl.when(i % tiles_per_seq == 0)
    def _():
        carry_ref[...] = jnp.zeros_like(carry_ref)

    prev = carry_ref[...]
    row =lax.broadcasted_iota(jnp.int32, uc.shape, 0)
    u1 = jnp.where(row == 0, prev[7:8, :], pltpu.roll(uc, 1, axis=0))
    u2 = jnp.where(row == 0, prev[6:7, :], jnp.where(row == 1, prev[7:8, :], pltpu.roll(uc, 2, axis=0)))
    carry_ref[...] = uc[tm - 8:, :]
    conv = cw_ref[0:1, :] * u2 + cw_ref[1:2, :] * u1 + cw_ref[2:3, :] * uc
    yc_ref[...] = (seg(OFF_CB, 256) * conv).astype(bf16)

    dq_ref[...] = (_rope_lanes(seg(OFF_DQ, 256), cos, sin) * (DF_DK ** -0.5 * LOG2E)).astype(bf16)
    dk_ref[...] = _rope_lanes(seg(OFF_DK, 256), cos, sin).astype(bf16)
    dv_ref[...] = seg(OFF_DV, 256).astype(bf16)


def _inproj(x2d, mod, w_packed, wkt, cos_t, sin_t, gq, gkv, gbias, conv_w, seq):
    t, d = x2d.shape
    tm = 512
    tps = seq // tm
    row = lambda w: pl.BlockSpec((tm, w), lambda i: (i, 0))
    const = lambda a: pl.BlockSpec(a.shape, lambda i: (0,) * a.ndim)
    widths = (256, 128, 128, 256, 256, 256, 256, 128, 256, 256, 256, 256)
    dtypes = (bf16, bf16, bf16, bf16, bf16, bf16, bf16, f32, bf16, bf16, bf16, bf16)
    return pl.pallas_call(
        functools.partial(_inproj_body, tiles_per_seq=tps),
        grid=(t // tm,),
        in_specs=[
            row(d),
            pl.BlockSpec((1, 6, d), lambda i: (i // tps, 0, 0)),
            const(w_packed), const(wkt),
            pl.BlockSpec((tm, 256), lambda i: (i % tps, 0)),
            pl.BlockSpec((tm, 256), lambda i: (i % tps, 0)),
            const(gq), const(gkv), const(gbias), const(conv_w),
        ],
        out_specs=[row(w) for w in widths] + [pl.BlockSpec((GROUP_W, tm), lambda i: (0, i))],
        out_shape=[jax.ShapeDtypeStruct((t, w), dt) for w, dt in zip(widths, dtypes)]
        + [jax.ShapeDtypeStruct((GROUP_W, t), bf16)],
        scratch_shapes=[pltpu.VMEM((8, 256), f32)],
        compiler_params=_cparams(("arbitrary",)),
        name="inproj",
    )(x2d, mod, w_packed, wkt, cos_t, sin_t, gq, gkv, gbias, conv_w)


def _causal_softmax_heads(q_heads, k_at, v_at, qi, s_scr, m_scr, acc_scr):
    n_heads = len(q_heads)
    tq = q_heads[0].shape[0]
    half = tq // 2
    rows = pl.ds(pl.multiple_of(qi * tq, tq), tq)

    def scores(g, j):
        return _dot_nt(q_heads[g], k_at(g, pl.multiple_of(j * tq, tq)))

    def fold_max(g, s):
        m_scr[g] = jnp.maximum(m_scr[g], jnp.maximum(s[:, :half], s[:, half:]))

    m_scr[...] = jnp.full(m_scr.shape, NEG, f32)
    for g in range(n_heads):
        acc_scr[g, rows, :] = jnp.zeros((tq, LANE), f32)

    def score_blocks(js):
        for g in range(n_heads):
            mx = None
            for j in js:
                s = scores(g, j)
                s_scr[g, j] = s
                s = jnp.maximum(s[:, :half], s[:, half:])
                mx = s if mx is None else jnp.maximum(mx, s)
            m_scr[g] = jnp.maximum(m_scr[g], mx)

    def prob_blocks(js):
        for g in range(n_heads):
            mb = m_scr[g]
            pv = None
            for j in js:
                sj = s_scr[g, j]
                p = jnp.concatenate([jnp.exp2(sj[:, :half] - mb), jnp.exp2(sj[:, half:] - mb)],
                                    axis=1).astype(bf16)
                d = _dot(p, v_at(g, pl.multiple_of(j * tq, tq)))
                pv = d if pv is None else pv + d
            acc_scr[g, rows, :] += pv

    def over_full_blocks(blocks_fn):
        @pl.loop(0, qi // 2)
        def _(jj):
            blocks_fn((2 * jj, 2 * jj + 1))

        @pl.when(qi % 2 == 1)
        def _():
            blocks_fn((qi - 1,))

    over_full_blocks(score_blocks)
    rowi = lax.broadcasted_iota(jnp.int32, (tq, tq), 0)
    coli = lax.broadcasted_iota(jnp.int32, (tq, tq), 1)
    for g in range(n_heads):
        s = jnp.where(coli <= rowi, scores(g, qi), NEG)
        s_scr[g, qi] = s
        fold_max(g, s)
        m_scr[g] = jnp.broadcast_to(jnp.max(m_scr[g], axis=1, keepdims=True), (tq, half))

    over_full_blocks(prob_blocks)
    prob_blocks((qi,))


def _softmax_scratch(n_heads, nq, tq):
    return [pltpu.VMEM((n_heads, nq, tq, tq), f32), pltpu.VMEM((n_heads, tq, tq // 2), f32),
            pltpu.VMEM((n_heads, nq * tq, LANE), f32)]


EPILOGUE_ROWS = 512


def _with_ones_lanes(v, dv):
    return jnp.concatenate([v.astype(bf16), jnp.ones((v.shape[0], LANE - dv), bf16)], axis=1)


def _softmax_normalise(acc, dv):
    lane = lax.broadcasted_iota(jnp.int32, acc.shape, 1)
    return jnp.where(lane < dv, acc / pltpu.roll(acc, LANE - dv, axis=1), 0.0)


def _pair_lanes(a, b, dv):
    lane = lax.broadcasted_iota(jnp.int32, a.shape, 1)
    return jnp.where(lane < dv, a, pltpu.roll(b, dv, axis=1))


def _mla_body(qn_ref, kvn_ref, kpe_ref, wq_ref, wqr_ref, wk_ref, wv_ref, place_ref, cq_ref, sq_ref,
              o_ref, q_scr, k_scr, v_scr, s_scr, m_scr, acc_scr):
    qi = pl.program_id(1)
    tq = s_scr.shape[2]

    @pl.when(qi == 0)
    def _():
        kvn = kvn_ref[...]
        kpe_placed = _dot(kpe_ref[...], place_ref[...])
        for h in range(MLA_HEADS):
            k_scr[h] = (_dot(kvn, wk_ref[h]) + kpe_placed).astype(bf16)
            v_scr[h] = _with_ones_lanes(_dot(kvn, wv_ref[h]), MLA_V)
            for r0 in range(0, qn_ref.shape[0], EPILOGUE_ROWS):
                rows = pl.ds(r0, EPILOGUE_ROWS)
                qn = qn_ref[rows, :]
                q_scr[h, rows, :] = (_dot(qn, wq_ref[h]) * cq_ref[rows, :]
                                     + _dot(qn, wqr_ref[h]) * sq_ref[rows, :]).astype(bf16)

    q_rows = pl.ds(pl.multiple_of(qi * tq, tq), tq)
    q_heads = [q_scr[h, q_rows, :] for h in range(MLA_HEADS)]
    _causal_softmax_heads(
        q_heads, lambda g, start: k_scr[g, pl.ds(start, tq), :],
        lambda g, start: v_scr[g, pl.ds(start, tq), :], qi, s_scr, m_scr, acc_scr)

    @pl.when(qi == pl.num_programs(1) - 1)
    def _():
        for r0 in range(0, o_ref.shape[0], EPILOGUE_ROWS):
            rows = pl.ds(r0, EPILOGUE_ROWS)
            o = [_softmax_normalise(acc_scr[h, rows, :], MLA_V) for h in range(MLA_HEADS)]
            o_ref[rows, :] = jnp.concatenate(
                [_pair_lanes(o[0], o[1], MLA_V), _pair_lanes(o[2], o[3], MLA_V)], axis=1).astype(bf16)


def _mla_attention(qn, kvn, kpe, wq, wqr, wk, wv, place, cq, sq, batch, seq):
    tq = 256
    nq = seq // tq
    const = lambda a: pl.BlockSpec(a.shape, lambda b, i: (0,) * a.ndim)
    return pl.pallas_call(
        _mla_body,
        grid=(batch, nq),
        in_specs=[
            pl.BlockSpec((seq, 256), lambda b, i: (b, 0)),
            pl.BlockSpec((seq, 128), lambda b, i: (b, 0)),
            pl.BlockSpec((seq, 128), lambda b, i: (b, 0)),
            const(wq), const(wqr), const(wk), const(wv), const(place), const(cq), const(sq),
        ],
        out_specs=pl.BlockSpec((seq, GROUP_W), lambda b, i: (b, 0)),
        out_shape=jax.ShapeDtypeStruct((batch * seq, GROUP_W), bf16),
        scratch_shapes=[pltpu.VMEM((MLA_HEADS, seq, 128), bf16),
                        pltpu.VMEM((MLA_HEADS, seq, 128), bf16),
                        pltpu.VMEM((MLA_HEADS, seq, LANE), bf16)]
        + _softmax_scratch(MLA_HEADS, nq, tq),
        compiler_params=_cparams(("arbitrary", "arbitrary")),
        name="mla_attn",
    )(qn, kvn, kpe, wq, wqr, wk, wv, place, cq, sq)


def _diff_body(q_ref, k_ref, v_ref, lq1_ref, lk1_ref, lq2_ref, lk2_ref, g_ref,
               o_ref, v_scr, s_scr, m_scr, acc_scr, *, lambda_init):
    qi = pl.program_id(1)
    tq = q_ref.shape[0]

    @pl.when(qi == 0)
    def _():
        v = v_ref[...]
        for h in range(DF_HEADS):
            v_scr[h] = _with_ones_lanes(v[:, DF_V * h:DF_V * (h + 1)], DF_V)

    q = q_ref[...]
    q_heads = [jnp.where(_lane_group_mask(q.shape, DF_DK * g, DF_DK * (g + 1)), q, jnp.zeros_like(q))
               for g in range(2 * DF_HEADS)]
    _causal_softmax_heads(
        q_heads, lambda g, start: k_ref[pl.ds(start, tq), :],
        lambda g, start: v_scr[g // 2, pl.ds(start, tq), :], qi, s_scr, m_scr, acc_scr)

    @pl.when(qi == pl.num_programs(1) - 1)
    def _():
        lam = (jnp.exp(jnp.sum(lq1_ref[...] * lk1_ref[...], axis=1, keepdims=True))
               - jnp.exp(jnp.sum(lq2_ref[...] * lk2_ref[...], axis=1, keepdims=True)) + lambda_init)
        gain = g_ref[...] * (1.0 - lambda_init)
        for r0 in range(0, o_ref.shape[0], EPILOGUE_ROWS):
            rows = pl.ds(r0, EPILOGUE_ROWS)
            ys = []
            for h in range(DF_HEADS):
                o = (_softmax_normalise(acc_scr[2 * h, rows, :], DF_V)
                     - lam * _softmax_normalise(acc_scr[2 * h + 1, rows, :], DF_V))
                ms = jnp.sum(o * o, axis=1, keepdims=True) * (1.0 / DF_V)
                ys.append(o * lax.rsqrt(ms + RMS_EPS) * gain)
            o_ref[rows, :] = jnp.concatenate(
                [_pair_lanes(ys[0], ys[1], DF_V), _pair_lanes(ys[2], ys[3], DF_V)], axis=1).astype(bf16)


def _diff_attention(dq, dk, dv, lq1, lk1, lq2, lk2, g_sub, batch, seq, lambda_init):
    tq = 256
    nq = seq // tq
    const = lambda a: pl.BlockSpec(a.shape, lambda b, i: (0,) * a.ndim)
    return pl.pallas_call(
        functools.partial(_diff_body, lambda_init=lambda_init),
        grid=(batch, nq),
        in_specs=[
            pl.BlockSpec((tq, 256), lambda b, i: (b * nq + i, 0)),
            pl.BlockSpec((seq, 256), lambda b, i: (b, 0)),
            pl.BlockSpec((seq, 256), lambda b, i: (b, 0)),
            const(lq1), const(lk1), const(lq2), const(lk2), const(g_sub),
        ],
        out_specs=pl.BlockSpec((seq, GROUP_W), lambda b, i: (b, 0)),
        out_shape=jax.ShapeDtypeStruct((batch * seq, GROUP_W), bf16),
        scratch_shapes=[pltpu.VMEM((DF_HEADS, seq, LANE), bf16)]
        + _softmax_scratch(2 * DF_HEADS, nq, tq),
        compiler_params=_cparams(("arbitrary", "arbitrary")),
        name="diff_attn",
    )(dq, dk, dv, lq1, lk1, lq2, lk2, g_sub)


def _split3(a):
    hi = a.astype(bf16)
    r1 = a - hi.astype(f32)
    mid = r1.astype(bf16)
    return hi, mid, (r1 - mid.astype(f32)).astype(bf16)


def _dot_exact_rhs01(a, b01):
    hi, mid, lo = _split3(a)
    return _dot(hi, b01) + _dot(mid, b01) + _dot(lo, b01)


def _dot_exact_lhs01(a01, b):
    hi, mid, lo = _split3(b)
    return _dot(a01, hi) + _dot(a01, mid) + _dot(a01, lo)


def _mlstm_gate_terms(g, eif):
    R, W, L = g.shape[0], eif.shape[1] // 2, ML_CHUNK
    lf = jnp.minimum(g, 0.0) - jnp.log(1.0 + jnp.exp(-jnp.abs(g)))
    glane = lax.broadcasted_iota(jnp.int32, g.shape, 1)
    wide = _dot_exact_rhs01(jnp.where(glane < ML_HEADS, g, lf), eif)
    iw, lfw = wide[:, :W], wide[:, W:]
    rr = lax.broadcasted_iota(jnp.int32, (R, R), 0)
    cc = lax.broadcasted_iota(jnp.int32, (R, R), 1)
    bw = _dot_exact_lhs01(((rr >= cc) & (rr // L == cc // L)).astype(bf16), lfw)
    rw = iw - bw
    pos_in_chunk = lax.broadcasted_iota(jnp.int32, (R, W), 0) % L
    cm = rw
    for sh in (1, 2, 4, 8, 16, 32):
        cm = jnp.maximum(cm, jnp.where(pos_in_chunk >= sh, pltpu.roll(cm, sh, axis=0), NEG))
    return bw, rw, cm


def _mlstm_chunk_local(q, k, kt_bd, v, rw, cm):
    L, W = q.shape
    row = lax.broadcasted_iota(jnp.int32, (L, W), 0)
    pos = lax.broadcasted_iota(jnp.int32, (L, W), 1) % L
    head_eq = (lax.broadcasted_iota(jnp.int32, (W, W), 0) // ML_DH
               == lax.broadcasted_iota(jnp.int32, (W, W), 1) // ML_DH)
    r_row = jnp.sum(jnp.where(pos == row, rw, 0.0), axis=0, keepdims=True)
    v4 = jnp.concatenate([v] * ML_HEADS, axis=0)
    vbd = jnp.where(head_eq, v4, jnp.zeros_like(v4))
    rmax = cm[L - 1:L, :]
    c_loc = _dot((kt_bd * jnp.exp(r_row - rmax)).astype(bf16), vbd)
    n_loc = jnp.sum(k.astype(f32) * jnp.exp(rw - rmax), axis=0, keepdims=True)
    return _dot(q, kt_bd.astype(bf16)), vbd, r_row, c_loc, n_loc


def _mlstm_chunk_step(q, o, bw, cm, qk, vbd, r_row, c_loc, n_loc, c_prev, n_prev, m_prev):
    L, W = q.shape
    row = lax.broadcasted_iota(jnp.int32, (L, W), 0)
    pos = lax.broadcasted_iota(jnp.int32, (L, W), 1) % L
    head_eq = (lax.broadcasted_iota(jnp.int32, (W, W), 0) // ML_DH
               == lax.broadcasted_iota(jnp.int32, (W, W), 1) // ML_DH)
    mw = jnp.maximum(m_prev, cm)
    a_inter = jnp.exp(m_prev - mw)
    pw = qk * jnp.exp(jnp.where(pos <= row, r_row - mw, NEG))
    sums = _dot_exact_rhs01(jnp.concatenate([q.astype(f32) * n_prev, pw], axis=0), head_eq.astype(bf16))
    den = a_inter * sums[:L] + sums[L:]
    hden = jnp.maximum(jnp.abs(den), jnp.exp(-(bw + mw)))
    num = a_inter * _dot(q, c_prev.astype(bf16)) + _dot(pw.astype(bf16), vbd)
    y = num / hden * _sigmoid(o.astype(f32))
    rmax, m_last = cm[L - 1:L, :], mw[L - 1:L, :]
    a_w = jnp.exp(m_prev - m_last)
    b_w = jnp.exp(rmax - m_last)
    return y, c_prev * a_w + c_loc * b_w, a_w * n_prev + b_w * n_loc, bw[L - 1:L, :] + m_last


def _mlstm_body(q_ref, k_ref, kt_ref, v_ref, o_ref, g_ref, eif_ref, sel_ref, y_ref, c_scr, nm_scr):
    @pl.when(pl.program_id(1) == 0)
    def _():
        c_scr[...] = jnp.zeros_like(c_scr)
        nm_scr[...] = jnp.zeros_like(nm_scr)

    c, n, m = c_scr[...], nm_scr[0:1, :], nm_scr[1:2, :]
    kt = kt_ref[...]
    eif = eif_ref[...]
    w = kt.shape[0]
    head_eq = (lax.broadcasted_iota(jnp.int32, (w, w), 0) // ML_DH
               == lax.broadcasted_iota(jnp.int32, (w, w), 1) // ML_DH)
    bw, rw, cm = _mlstm_gate_terms(g_ref[...], eif)
    n_chunks = q_ref.shape[0] // ML_CHUNK
    local = []
    for i in range(n_chunks):
        rows = pl.ds(i * ML_CHUNK, ML_CHUNK)
        lo, hi = i * ML_CHUNK, (i + 1) * ML_CHUNK
        kt_bd = jnp.where(head_eq, _dot(kt, sel_ref[i]), 0.0)
        local.append(_mlstm_chunk_local(q_ref[rows, :], k_ref[rows, :], kt_bd, v_ref[rows, :],
                                        rw[lo:hi], cm[lo:hi]))
    for i in range(n_chunks):
        rows = pl.ds(i * ML_CHUNK, ML_CHUNK)
        lo, hi = i * ML_CHUNK, (i + 1) * ML_CHUNK
        y, c, n, m = _mlstm_chunk_step(q_ref[rows, :], o_ref[rows, :], bw[lo:hi], cm[lo:hi],
                                       *local[i], c, n, m)
        y_ref[rows, :] = y.astype(y_ref.dtype)
    c_scr[...] = c
    nm_scr[0:1, :] = n
    nm_scr[1:2, :] = m


MLSTM_CHUNKS_PER_STEP = 4


def _mlstm(mq, mk, mkt, mv, mo, gcol, batch, seq):
    cps = MLSTM_CHUNKS_PER_STEP
    L = ML_CHUNK * cps
    nc = seq // L
    blk = lambda w: pl.BlockSpec((L, w), lambda b, c: (b * nc + c, 0))
    const = lambda a: pl.BlockSpec(a.shape, lambda b, c: (0,) * a.ndim)
    gl = jnp.arange(LANE)[:, None]
    col = jnp.arange(2 * GROUP_W)[None, :]
    eif = (gl == (col // GROUP_W) * ML_HEADS + (col % GROUP_W) // ML_DH).astype(bf16)
    p = jnp.arange(L)[None, :, None]
    cc = jnp.arange(GROUP_W)[None, None, :]
    sel = (p == jnp.arange(cps)[:, None, None] * ML_CHUNK + cc % ML_CHUNK).astype(bf16)
    return pl.pallas_call(
        _mlstm_body,
        grid=(batch, nc),
        in_specs=[blk(256), blk(256), pl.BlockSpec((GROUP_W, L), lambda b, c: (0, b * nc + c)),
                  blk(256), blk(256), blk(128), const(eif), const(sel)],
        out_specs=blk(GROUP_W),
        out_shape=jax.ShapeDtypeStruct((batch * seq, GROUP_W), bf16),
        scratch_shapes=[pltpu.VMEM((GROUP_W, GROUP_W), f32), pltpu.VMEM((8, GROUP_W), f32)],
        compiler_params=_cparams(("arbitrary", "arbitrary")),
        name="mlstm",
    )(mq, mk, mkt, mv, mo, gcol, eif, sel)


def _route(scores, sel):
    ne, tm = sel.shape
    eid = lax.broadcasted_iota(jnp.int32, (ne, tm), 0)
    grp = eid // E_PER_GROUP
    big = ne + 1

    def top2(vals):
        m1 = jnp.max(vals, axis=0, keepdims=True)
        i1 = jnp.min(jnp.where(vals == m1, eid, big), axis=0, keepdims=True)
        rest = jnp.where(eid == i1, NEG, vals)
        m2 = jnp.max(rest, axis=0, keepdims=True)
        i2 = jnp.min(jnp.where(rest == m2, eid, big), axis=0, keepdims=True)
        return m1, i1, m2, i2

    best_score, best_grp = None, None
    for gi in range(N_GROUPS):
        m1, _, m2, _ = top2(jnp.where(grp == gi, sel, NEG))
        sc = m1 + m2
        if best_score is None:
            best_score, best_grp = sc, jnp.zeros_like(sc, dtype=jnp.int32)
        else:
            better = sc > best_score
            best_grp = jnp.where(better, gi, best_grp)
            best_score = jnp.where(better, sc, best_score)
    _, i1, _, i2 = top2(jnp.where(grp == best_grp, sel, NEG))
    picked = jnp.where((eid == i1) | (eid == i2), scores, 0.0)
    return picked / jnp.sum(picked, axis=0, keepdims=True), best_grp


MOE_ROW_TILE = 256


def _outproj_route_body(ya_ref, yb_ref, yc_ref, yd_ref, x_ref, mod_ref, w_ref, g_ref, b_ref,
                        wr_ref, br_ref, x1_ref, u2e_ref, grp_ref):
    d = x_ref.shape[1]
    mix = None
    for j, y_ref in enumerate((ya_ref, yb_ref, yc_ref, yd_ref)):
        part = _dot(y_ref[...], w_ref[GROUP_W * j:GROUP_W * (j + 1), :])
        mix = part if mix is None else mix + part
    x1 = _layer_norm(DN_ALPHA * x_ref[...] + mod_ref[0, 2:3, :] * mix) * g_ref[...] + b_ref[...]
    x1_ref[...] = x1
    u = _layer_norm(x1) * (1.0 + mod_ref[0, 4:5, :]) + mod_ref[0, 3:4, :]
    u_hi = u.astype(bf16)
    u_lo = (u - u_hi.astype(f32)).astype(bf16)
    ne = br_ref.shape[0]
    part = _dot_nt(wr_ref[...], u_hi)
    logits = part[:ne] + part[ne:] + _dot_nt(wr_ref[:ne, :], u_lo)
    scores = _sigmoid(logits)
    gates_t, best_grp = _route(scores, scores + br_ref[...])
    u2e_ref[:, :d] = u_hi.astype(f32)
    u2e_ref[:, d:] = jnp.zeros((x1.shape[0], LANE), f32)
    u2e_ref[:, d:d + ne] = gates_t.T
    grp_ref[...] = best_grp


def _outproj_route(ya, yb, yc, yd, x2d, mod, w_out, ln_g, ln_b, wr_split, b_router, seq):
    t, d = x2d.shape
    tm = 512
    tps = seq // tm
    row = lambda w: pl.BlockSpec((tm, w), lambda i: (i, 0))
    const = lambda a: pl.BlockSpec(a.shape, lambda i: (0,) * a.ndim)
    return pl.pallas_call(
        _outproj_route_body,
        grid=(t // tm,),
        in_specs=[row(256), row(256), row(256), row(256), row(d),
                  pl.BlockSpec((1, 6, d), lambda i: (i // tps, 0, 0)),
                  const(w_out), const(ln_g), const(ln_b), const(wr_split), const(b_router)],
        out_specs=[row(d), row(d + LANE), pl.BlockSpec((1, tm), lambda i: (0, i))],
        out_shape=[jax.ShapeDtypeStruct((t, d), f32), jax.ShapeDtypeStruct((t, d + LANE), f32),
                   jax.ShapeDtypeStruct((1, t), jnp.int32)],
        compiler_params=_cparams(("arbitrary",)),
        name="outproj_route",
    )(ya, yb, yc, yd, x2d, mod, w_out, ln_g, ln_b, wr_split, b_router)


def _dispatch_plan(grp, tr):
    t = grp.shape[0]
    rows = t + N_GROUPS * tr
    order = jnp.sort(grp * t + jnp.arange(t, dtype=jnp.int32)) % t
    order = jnp.concatenate([order, jnp.zeros((rows - t,), order.dtype)])
    counts = jnp.sum((grp[:, None] == jnp.arange(N_GROUPS)[None, :]).astype(jnp.int32), axis=0)
    padded = ((counts + tr - 1) // tr) * tr
    row_end = jnp.cumsum(padded)
    r = jnp.arange(rows, dtype=jnp.int32)
    read_tok = jnp.zeros((rows,), jnp.int32)
    write_row = jnp.zeros((rows,), jnp.int32)
    for g in range(N_GROUPS):
        row_start = row_end[g] - padded[g]
        tok_start = jnp.sum(counts[:g])
        shifted = jnp.roll(order, row_start - tok_start)
        in_grp = (r >= row_start) & (r < row_end[g])
        valid = in_grp & (r < row_start + counts[g])
        read_tok = jnp.where(valid, shifted, jnp.where(in_grp, order[jnp.minimum(tok_start, t - 1)], read_tok))
        write_row = jnp.where(valid, shifted, jnp.where(in_grp, t + g * tr + r % tr, write_row))
    tile_start = jnp.arange(rows // tr, dtype=jnp.int32) * tr
    tile_grp = jnp.minimum(jnp.sum((tile_start[:, None] >= row_end[None, :]).astype(jnp.int32), axis=1),
                           N_GROUPS - 1)
    i32 = lambda a: a.astype(jnp.int32)
    return i32(read_tok), i32(write_row), i32(tile_grp), i32(row_end[-1:] // tr)


def _wait_row_gather(n_rows, hbm, vmem, sem):
    pltpu.make_async_copy(hbm.at[pl.ds(0, n_rows), :], vmem, sem).wait()


def _wait_row_scatter(n_rows, vmem, hbm, sem):
    pltpu.make_async_copy(vmem, hbm.at[pl.ds(0, n_rows), :], sem).wait()


def _experts_body(read_ref, write_ref, tgrp_ref, ntile_ref,
                  u2e_hbm, w1_ref, w3_ref, w2_ref, ys_hbm, xbuf0, xbuf1, obuf0, obuf1, w1b, w3b, w2b,
                  gsem, ssem):
    j = pl.program_id(0)
    n_tiles = ntile_ref[0]
    tr, d = obuf0.shape
    xbuf, obuf = (xbuf0, xbuf1), (obuf0, obuf1)

    def start_gather(tile, s):
        for i in range(tr):
            pltpu.make_async_copy(u2e_hbm.at[pl.ds(read_ref[tile * tr + i], 1), :],
                                  xbuf[s].at[pl.ds(i, 1), :], gsem.at[s]).start()

    def start_scatter(tile, s):
        for i in range(tr):
            pltpu.make_async_copy(obuf[s].at[pl.ds(i, 1), :],
                                  ys_hbm.at[pl.ds(write_ref[tile * tr + i], 1), :], ssem.at[s]).start()

    def load_tile(s):
        xs = xbuf[s][...]
        return xs[:, :d].astype(bf16), xs[:, d:]

    def compute(tile, s, xb, gslab):
        lane = lax.broadcasted_iota(jnp.int32, gslab.shape, 1)
        first_expert = tgrp_ref[tile] * E_PER_GROUP
        acc = None
        for e in range(E_PER_GROUP):
            ge = jnp.sum(jnp.where(lane == first_expert + e, gslab, 0.0), axis=1, keepdims=True)
            h1 = _dot(xb, w1b[e])
            h = h1 * _sigmoid(h1) * _dot(xb, w3b[e]) * ge
            part = _dot(h.astype(bf16), w2b[e])
            acc = part if acc is None else acc + part
        obuf[s][...] = acc

    nxt = jnp.minimum(j + 1, n_tiles - 1)

    last = tgrp_ref.shape[0] - 1
    @pl.when((j < n_tiles) & ((j == 0) | (tgrp_ref[jnp.minimum(j, last)] != tgrp_ref[jnp.clip(j - 1, 0, last)])))
    def _():
        w1b[...] = w1_ref[0].astype(bf16)
        w3b[...] = w3_ref[0].astype(bf16)
        w2b[...] = w2_ref[0].astype(bf16)

    @pl.when(j == 0)
    def _():
        start_gather(0, 0)
        start_gather(nxt, 1)
        obuf[1][...] = jnp.zeros_like(obuf[1])
        n_tok = ys_hbm.shape[0] - N_GROUPS * tr
        spare = [pltpu.make_async_copy(obuf[1], ys_hbm.at[pl.ds(n_tok + g * tr, tr), :], ssem.at[1])
                 for g in range(N_GROUPS)]
        for cp in spare:
            cp.start()
        for cp in spare:
            cp.wait()
        _wait_row_gather(tr, u2e_hbm, xbuf[0], gsem.at[0])
        compute(0, 0, *load_tile(0))

    for s in (0, 1):
        @pl.when((j > 0) & (j < n_tiles) & (j % 2 == s))
        def _(s=s):
            start_gather(nxt, 1 - s)
            start_scatter(j - 1, 1 - s)

        @pl.when((j > 0) & (j < n_tiles) & (j % 2 == s))
        def _(s=s):
            _wait_row_gather(tr, u2e_hbm, xbuf[s], gsem.at[s])
            compute(j, s, *load_tile(s))
            _wait_row_scatter(tr, obuf[1 - s], ys_hbm, ssem.at[1 - s])

        @pl.when((j == n_tiles) & (j % 2 == s))
        def _(s=s):
            _wait_row_gather(tr, u2e_hbm, xbuf[s], gsem.at[s])
            start_scatter(j - 1, 1 - s)
            _wait_row_scatter(tr, obuf[1 - s], ys_hbm, ssem.at[1 - s])


def _experts(read_tok, write_row, tile_grp, n_tiles, u2e, w1, w3, w2, layer):
    rows = read_tok.shape[0]
    tr = MOE_ROW_TILE
    max_tiles = tile_grp.shape[0]
    _, ne, d, fe = w1.shape
    grouped = lambda w: w.reshape(w.shape[0], N_GROUPS, E_PER_GROUP, *w.shape[2:])
    wspec = lambda a, b: pl.BlockSpec(
        (None, 1, E_PER_GROUP, a, b),
        lambda j, rd, wr, tg, nt: (layer, tg[jnp.minimum(j, max_tiles - 1)], 0, 0, 0))
    return pl.pallas_call(
        _experts_body,
        grid_spec=pltpu.PrefetchScalarGridSpec(
            num_scalar_prefetch=4, grid=(max_tiles + 1,),
            in_specs=[pl.BlockSpec(memory_space=pl.ANY), wspec(d, fe), wspec(d, fe), wspec(fe, d)],
            out_specs=pl.BlockSpec(memory_space=pl.ANY),
            scratch_shapes=[pltpu.VMEM((tr, u2e.shape[1]), f32), pltpu.VMEM((tr, u2e.shape[1]), f32),
                            pltpu.VMEM((tr, d), f32), pltpu.VMEM((tr, d), f32),
                            pltpu.VMEM((E_PER_GROUP, d, fe), bf16), pltpu.VMEM((E_PER_GROUP, d, fe), bf16),
                            pltpu.VMEM((E_PER_GROUP, fe, d), bf16),
                            pltpu.SemaphoreType.DMA((2,)), pltpu.SemaphoreType.DMA((2,))]),
        out_shape=jax.ShapeDtypeStruct((rows, d), f32),
        compiler_params=_cparams(("arbitrary",)),
        name="experts",
    )(read_tok, write_row, tile_grp, n_tiles, u2e, grouped(w1), grouped(w3), grouped(w2))


def _moe_combine_body(y_ref, x_ref, mod_ref, g_ref, b_ref, o_ref):
    r = DN_ALPHA * x_ref[...] + mod_ref[0, 5:6, :] * y_ref[...]
    o_ref[...] = _layer_norm(r) * g_ref[...] + b_ref[...]


def _moe_combine(ys, x1, mod, ln_g, ln_b, seq):
    t, d = x1.shape
    tm = 512
    tps = seq // tm
    row = pl.BlockSpec((tm, d), lambda i: (i, 0))
    const = lambda a: pl.BlockSpec(a.shape, lambda i: (0,) * a.ndim)
    return pl.pallas_call(
        _moe_combine_body,
        grid=(t // tm,),
        in_specs=[row, row, pl.BlockSpec((1, 6, d), lambda i: (i // tps, 0, 0)), const(ln_g), const(ln_b)],
        out_specs=row,
        out_shape=jax.ShapeDtypeStruct((t, d), f32),
        compiler_params=_cparams(("arbitrary",)),
        name="moe_combine",
    )(ys, x1, mod, ln_g, ln_b)


def _pad_cols(a, width):
    return jnp.pad(a, ((0, 0), (0, width - a.shape[1])))


def _pack_w_in(w):
    parts, acc = [], 0
    for sz in IN_SIZES:
        parts.append(w[:, acc:acc + sz])
        acc += sz
    (a_q, a_kv, a_kr, m_q, m_k, m_v, m_i, m_f, m_o, c_b, c_c, c_h, d_q, d_k, d_v) = parts
    gates = _pad_cols(jnp.concatenate([m_i, m_f], axis=1), LANE)
    packed = jnp.concatenate([
        _pad_cols(a_q, 256), a_kv, _pad_cols(a_kr, LANE),
        m_q, m_k, m_v, gates, m_o, c_b, c_c, c_h, d_q, d_k, d_v], axis=1)
    return packed.astype(bf16)


def _pack_mla(w_uq, w_ukv):
    half = MLA_ROPE // 2
    wq3 = w_uq.reshape(MLA_Q_LORA, MLA_HEADS, MLA_NOPE + MLA_ROPE).transpose(1, 0, 2)
    wkv3 = w_ukv.reshape(MLA_KV_LORA, MLA_HEADS, MLA_NOPE + MLA_V).transpose(1, 0, 2)
    nope, pe = wq3[:, :, :MLA_NOPE], wq3[:, :, MLA_NOPE:]
    rot = jnp.concatenate([-pe[:, :, half:], pe[:, :, :half]], axis=2)
    pad = lambda a: jnp.pad(a, ((0, 0), (0, 256 - a.shape[1]), (0, LANE - a.shape[2]))).astype(bf16)
    wq = pad(wq3)
    wqr = pad(jnp.concatenate([jnp.zeros_like(nope), rot], axis=2))
    wk = jnp.pad(wkv3[:, :, :MLA_NOPE], ((0, 0), (0, 0), (0, LANE - MLA_NOPE))).astype(bf16)
    return wq, wqr, wk, wkv3[:, :, MLA_NOPE:].astype(bf16)


def _rope_tables(seq):
    half = DF_DK // 2
    inv = 1.0 / (ROPE_THETA ** (jnp.arange(0, DF_DK, 2, dtype=f32) / DF_DK))
    ang = jnp.arange(seq, dtype=f32)[:, None] * inv[None, :]
    cos, sin = jnp.cos(ang), jnp.sin(ang)
    cos_t = jnp.tile(jnp.concatenate([cos, cos], axis=1), (1, 256 // DF_DK))
    sin_t = jnp.tile(jnp.concatenate([-sin, sin], axis=1), (1, 256 // DF_DK))
    scale = (MLA_NOPE + MLA_ROPE) ** -0.5 * LOG2E
    ones = jnp.ones((seq, MLA_NOPE), f32)
    zeros = jnp.zeros((seq, MLA_NOPE), f32)
    cq = _pad_cols(jnp.concatenate([ones, cos, cos], axis=1) * scale, LANE)
    sq = _pad_cols(jnp.concatenate([zeros, sin, sin], axis=1) * scale, LANE)
    return cos_t, sin_t, cq, sq


def kernel(x, c, w_ada, b_ada, w_in, mla_g_q, mla_g_kv, mla_w_uq, mla_w_ukv, ml_b_i, ml_b_f, sc_w,
           df_lq1, df_lk1, df_lq2, df_lk2, df_g, w_out, ln1_g, ln1_b, w_router, b_router, w1, w3, w2,
           ln2_g, ln2_b):
    batch, seq, d = x.shape
    depth = w_in.shape[0]
    assert MLA_ROPE == DF_DK, "both rotary blocks share one table"
    cos_t, sin_t, cq, sq = _rope_tables(seq)
    place = jnp.zeros((LANE, LANE), f32).at[jnp.arange(MLA_ROPE), MLA_NOPE + jnp.arange(MLA_ROPE)].set(1.0).astype(bf16)
    wr_hi = w_router.astype(bf16)
    wr_split = jnp.concatenate([wr_hi, (w_router - wr_hi.astype(f32)).astype(bf16)], axis=1).T
    mod_all = _ada_mod(c, w_ada, b_ada).reshape(depth, batch, 6, d)
    xf = x.reshape(batch * seq, d)
    for l in range(depth):
        mod = mod_all[l]
        w_packed = _pack_w_in(w_in[l])
        gq = _pad_cols(mla_g_q[l][None, :], 256)
        gkv = mla_g_kv[l][None, :]
        gbias = _pad_cols(jnp.concatenate([ml_b_i[l], ml_b_f[l]])[None, :], LANE)
        wkt = w_packed[:, OFF_MK:OFF_MK + GROUP_W].T
        (qn, kvn, kpe, mq, mk, mv, mo, gcol, yc, dq, dk, dv, mkt) = _inproj(
            xf, mod, w_packed, wkt, cos_t, sin_t, gq, gkv, gbias, sc_w[l], seq)
        wq, wqr, wk, wv = _pack_mla(mla_w_uq[l], mla_w_ukv[l])
        ya = _mla_attention(qn, kvn, kpe, wq, wqr, wk, wv, place, cq, sq, batch, seq)
        yb = _mlstm(mq, mk, mkt, mv, mo, gcol, batch, seq)
        lambda_init = 0.8 - 0.6 * math.exp(-0.3 * l)
        yd = _diff_attention(dq, dk, dv, df_lq1[l][None, :], df_lk1[l][None, :], df_lq2[l][None, :],
                             df_lk2[l][None, :], _pad_cols(df_g[l][None, :], LANE),
                             batch, seq, lambda_init)
        x1, u2e, grp = _outproj_route(ya, yb, yc, yd, xf, mod, w_out[l].astype(bf16), ln1_g[l][None, :],
                                      ln1_b[l][None, :], wr_split, b_router[:, None], seq)
        plan = _dispatch_plan(grp[0], MOE_ROW_TILE)
        ys = _experts(*plan, u2e, w1, w3, w2, l)
        xf = _moe_combine(ys, x1, mod, ln2_g[l][None, :], ln2_b[l][None, :], seq)
    return xf.reshape(batch, seq, d)
```

```python
import functools
import math

import jax
import jax.numpy as jnp
from jax import lax
from jax.experimental import pallas as pl
from jax.experimental.pallas import tpu as pltpu

f32 = jnp.float32
bf16 = jnp.bfloat16
HIGHEST = lax.Precision.HIGHEST

GROUP_W = 256
MLA_HEADS, MLA_NOPE, MLA_ROPE, MLA_V = 4, 64, 32, 64
MLA_Q_LORA, MLA_KV_LORA = 192, 128
ML_HEADS, ML_DH, ML_CHUNK = 4, 64, 64
DF_HEADS, DF_DK, DF_V = 4, 32, 64
ROPE_THETA = 10000.0
N_EXPERTS, N_GROUPS, E_PER_GROUP, D_EXPERT = 16, 4, 4, 256
DEPTH = 2
DN_ALPHA = (2 * DEPTH) ** 0.25
LN_EPS = 1e-5
RMS_EPS = 1e-6
IN_SIZES = (192, 128, 32, 256, 256, 256, 4, 4, 256, 256, 256, 256, 256, 256, 256)

OFF_PQ, OFF_PKV, OFF_KR = 0, 256, 384
OFF_MQ, OFF_MK, OFF_MV, OFF_MG, OFF_MO = 512, 768, 1024, 1280, 1408
OFF_CB, OFF_CC, OFF_CH = 1664, 1920, 2176
OFF_DQ, OFF_DK, OFF_DV = 2432, 2688, 2944
IN_PACKED = 3200

LANE = 128
VMEM_LIMIT = 48 * 1024 * 1024
NEG = -1e30
LOG2E = math.log2(math.e)


def _cparams(sem):
    return pltpu.CompilerParams(dimension_semantics=sem, vmem_limit_bytes=VMEM_LIMIT)


def _sigmoid(x):
    return 1.0 / (1.0 + jnp.exp(-x))


def _layer_norm(x):
    mu = jnp.mean(x, axis=-1, keepdims=True)
    xc = x - mu
    var = jnp.mean(xc * xc, axis=-1, keepdims=True)
    return xc * lax.rsqrt(var + LN_EPS)


def _dot(a, b, **kw):
    return jnp.dot(a, b, preferred_element_type=f32, **kw)


def _dot_nt(a, b, **kw):
    return lax.dot_general(a, b, (((1,), (1,)), ((), ())), preferred_element_type=f32, **kw)


def _dot_tn(a, b, **kw):
    return lax.dot_general(a, b, (((0,), (0,)), ((), ())), preferred_element_type=f32, **kw)


def _expand_groups(cols, rows, width, group):
    lane = lax.broadcasted_iota(jnp.int32, (rows, width), 1)
    out = jnp.broadcast_to(cols[-1], (rows, width))
    for h in range(len(cols) - 2, -1, -1):
        out = jnp.where(lane < group * (h + 1), cols[h], out)
    return out


def _lane_group_mask(shape, lo, hi):
    lane = lax.broadcasted_iota(jnp.int32, shape, 1)
    return (lane >= lo) & (lane < hi)


def _ada_body(c_ref, w_ref, b_ref, o_ref):
    c = c_ref[...]
    ca = (c * _sigmoid(c)).astype(bf16)
    o_ref[0] = _dot(ca, w_ref[0].astype(bf16)) + b_ref[0]


def _ada_mod(c, w_ada, b_ada):
    depth, d, n = w_ada.shape
    b = c.shape[0]
    tn = 1536
    return pl.pallas_call(
        _ada_body,
        grid=(depth, n // tn),
        in_specs=[
            pl.BlockSpec((b, d), lambda l, j: (0, 0)),
            pl.BlockSpec((1, d, tn), lambda l, j: (l, 0, j)),
            pl.BlockSpec((1, 1, tn), lambda l, j: (l, 0, j)),
        ],
        out_specs=pl.BlockSpec((1, b, tn), lambda l, j: (l, 0, j)),
        out_shape=jax.ShapeDtypeStruct((depth, b, n), f32),
        compiler_params=_cparams(("arbitrary", "arbitrary")),
        name="ada_mod",
    )(c, w_ada, b_ada.reshape(depth, 1, n))


def _rope_lanes(x, cos, sin_signed):
    w = x.shape[1]
    lane = lax.broadcasted_iota(jnp.int32, x.shape, 1)
    rot = jnp.where(lane % 32 < 16, pltpu.roll(x, w - 16, axis=1), pltpu.roll(x, 16, axis=1))
    return x * cos + rot * sin_signed


def _inproj_body(x_ref, mod_ref, w_ref, wkt_ref, cos_ref, sin_ref, gq_ref, gkv_ref, gb_ref, cw_ref,
                 qn_ref, kvn_ref, kpe_ref, mq_ref, mk_ref, mv_ref, mo_ref, gcol_ref, yc_ref,
                 dq_ref, dk_ref, dv_ref, mkt_ref, carry_ref, *, tiles_per_seq):
    i = pl.program_id(0)
    tm = x_ref.shape[0]
    u = _layer_norm(x_ref[...]) * (1.0 + mod_ref[0, 1:2, :]) + mod_ref[0, 0:1, :]
    ub = u.astype(bf16)

    def seg(off, n):
        return _dot(ub, w_ref[:, off:off + n])

    cos = cos_ref[...]
    sin = sin_ref[...]

    pq = seg(OFF_PQ, 256)
    ms = jnp.sum(pq * pq, axis=-1, keepdims=True) * (1.0 / MLA_Q_LORA)
    qn_ref[...] = (pq * lax.rsqrt(ms + RMS_EPS) * gq_ref[...]).astype(bf16)
    pkv = seg(OFF_PKV, 128)
    ms = jnp.sum(pkv * pkv, axis=-1, keepdims=True) * (1.0 / MLA_KV_LORA)
    kvn_ref[...] = (pkv * lax.rsqrt(ms + RMS_EPS) * gkv_ref[...]).astype(bf16)
    kpe_ref[...] = _rope_lanes(seg(OFF_KR, 128), cos[:, :LANE], sin[:, :LANE]).astype(bf16)

    mq_ref[...] = (seg(OFF_MQ, 256) * (ML_DH ** -0.5)).astype(bf16)
    mk_ref[...] = seg(OFF_MK, 256).astype(bf16)
    mkt_ref[...] = _dot_nt(wkt_ref[...], ub).astype(bf16)
    mv_ref[...] = seg(OFF_MV, 256).astype(bf16)
    gcol_ref[...] = seg(OFF_MG, 128) + gb_ref[...]
    mo_ref[...] = seg(OFF_MO, 256).astype(bf16)

    uc = seg(OFF_CC, 256) * seg(OFF_CH, 256)

    @pl.when(i % tiles_per_seq == 0)
    def _():
        carry_ref[...] = jnp.zeros_like(carry_ref)

    prev = carry_ref[...]
    row =lax.broadcasted_iota(jnp.int32, uc.shape, 0)
    u1 = jnp.where(row == 0, prev[7:8, :], pltpu.roll(uc, 1, axis=0))
    u2 = jnp.where(row == 0, prev[6:7, :], jnp.where(row == 1, prev[7:8, :], pltpu.roll(uc, 2, axis=0)))
    carry_ref[...] = uc[tm - 8:, :]
    conv = cw_ref[0:1, :] * u2 + cw_ref[1:2, :] * u1 + cw_ref[2:3, :] * uc
    yc_ref[...] = (seg(OFF_CB, 256) * conv).astype(bf16)

    dq_ref[...] = (_rope_lanes(seg(OFF_DQ, 256), cos, sin) * (DF_DK ** -0.5 * LOG2E)).astype(bf16)
    dk_ref[...] = _rope_lanes(seg(OFF_DK, 256), cos, sin).astype(bf16)
    dv_ref[...] = seg(OFF_DV, 256).astype(bf16)


def _inproj(x2d, mod, w_packed, wkt, cos_t, sin_t, gq, gkv, gbias, conv_w, seq):
    t, d = x2d.shape
    tm = 512
    tps = seq // tm
    row = lambda w: pl.BlockSpec((tm, w), lambda i: (i, 0))
    const = lambda a: pl.BlockSpec(a.shape, lambda i: (0,) * a.ndim)
    widths = (256, 128, 128, 256, 256, 256, 256, 128, 256, 256, 256, 256)
    dtypes = (bf16, bf16, bf16, bf16, bf16, bf16, bf16, f32, bf16, bf16, bf16, bf16)
    return pl.pallas_call(
        functools.partial(_inproj_body, tiles_per_seq=tps),
        grid=(t // tm,),
        in_specs=[
            row(d),
            pl.BlockSpec((1, 6, d), lambda i: (i // tps, 0, 0)),
            const(w_packed), const(wkt),
            pl.BlockSpec((tm, 256), lambda i: (i % tps, 0)),
            pl.BlockSpec((tm, 256), lambda i: (i % tps, 0)),
            const(gq), const(gkv), const(gbias), const(conv_w),
        ],
        out_specs=[row(w) for w in widths] + [pl.BlockSpec((GROUP_W, tm), lambda i: (0, i))],
        out_shape=[jax.ShapeDtypeStruct((t, w), dt) for w, dt in zip(widths, dtypes)]
        + [jax.ShapeDtypeStruct((GROUP_W, t), bf16)],
        scratch_shapes=[pltpu.VMEM((8, 256), f32)],
        compiler_params=_cparams(("arbitrary",)),
        name="inproj",
    )(x2d, mod, w_packed, wkt, cos_t, sin_t, gq, gkv, gbias, conv_w)


def _causal_softmax_heads(q_heads, k_at, v_at, qi, s_scr, m_scr, acc_scr):
    n_heads = len(q_heads)
    tq = q_heads[0].shape[0]
    half = tq // 2
    rows = pl.ds(pl.multiple_of(qi * tq, tq), tq)

    def scores(g, j):
        return _dot_nt(q_heads[g], k_at(g, pl.multiple_of(j * tq, tq)))

    def fold_max(g, s):
        m_scr[g] = jnp.maximum(m_scr[g], jnp.maximum(s[:, :half], s[:, half:]))

    m_scr[...] = jnp.full(m_scr.shape, NEG, f32)
    for g in range(n_heads):
        acc_scr[g, rows, :] = jnp.zeros((tq, LANE), f32)

    def score_blocks(js):
        for g in range(n_heads):
            mx = None
            for j in js:
                s = scores(g, j)
                s_scr[g, j] = s
                s = jnp.maximum(s[:, :half], s[:, half:])
                mx = s if mx is None else jnp.maximum(mx, s)
            m_scr[g] = jnp.maximum(m_scr[g], mx)

    def prob_blocks(js):
        for g in range(n_heads):
            mb = m_scr[g]
            pv = None
            for j in js:
                sj = s_scr[g, j]
                p = jnp.concatenate([jnp.exp2(sj[:, :half] - mb), jnp.exp2(sj[:, half:] - mb)],
                                    axis=1).astype(bf16)
                d = _dot(p, v_at(g, pl.multiple_of(j * tq, tq)))
                pv = d if pv is None else pv + d
            acc_scr[g, rows, :] += pv

    def over_full_blocks(blocks_fn):
        @pl.loop(0, qi // 4)
        def _(jj):
            blocks_fn(tuple(4 * jj + u for u in range(4)))

        rest = (qi // 4) * 4

        @pl.when(qi % 4 >= 2)
        def _():
            blocks_fn((rest, rest + 1))

        @pl.when(qi % 2 == 1)
        def _():
            blocks_fn((qi - 1,))

    over_full_blocks(score_blocks)
    rowi = lax.broadcasted_iota(jnp.int32, (tq, tq), 0)
    coli = lax.broadcasted_iota(jnp.int32, (tq, tq), 1)
    for g in range(n_heads):
        s = jnp.where(coli <= rowi, scores(g, qi), NEG)
        s_scr[g, qi] = s
        fold_max(g, s)
        m_scr[g] = jnp.broadcast_to(jnp.max(m_scr[g], axis=1, keepdims=True), (tq, half))

    over_full_blocks(prob_blocks)
    prob_blocks((qi,))


def _softmax_scratch(n_heads, nq, tq):
    return [pltpu.VMEM((n_heads, nq, tq, tq), f32), pltpu.VMEM((n_heads, tq, tq // 2), f32),
            pltpu.VMEM((n_heads, nq * tq, LANE), f32)]


EPILOGUE_ROWS = 512


def _with_ones_lanes(v, dv):
    return jnp.concatenate([v.astype(bf16), jnp.ones((v.shape[0], LANE - dv), bf16)], axis=1)


def _softmax_normalise(acc, dv):
    lane = lax.broadcasted_iota(jnp.int32, acc.shape, 1)
    return jnp.where(lane < dv, acc / pltpu.roll(acc, LANE - dv, axis=1), 0.0)


def _pair_lanes(a, b, dv):
    lane = lax.broadcasted_iota(jnp.int32, a.shape, 1)
    return jnp.where(lane < dv, a, pltpu.roll(b, dv, axis=1))


def _mla_body(qn_ref, kvn_ref, kpe_ref, wq_ref, wqr_ref, wk_ref, wv_ref, place_ref, cq_ref, sq_ref,
              o_ref, q_scr, k_scr, v_scr, s_scr, m_scr, acc_scr):
    qi = pl.program_id(1)
    tq = s_scr.shape[2]

    @pl.when(qi == 0)
    def _():
        kvn = kvn_ref[...]
        kpe_placed = _dot(kpe_ref[...], place_ref[...])
        for h in range(MLA_HEADS):
            k_scr[h] = (_dot(kvn, wk_ref[h]) + kpe_placed).astype(bf16)
            v_scr[h] = _with_ones_lanes(_dot(kvn, wv_ref[h]), MLA_V)
            for r0 in range(0, qn_ref.shape[0], EPILOGUE_ROWS):
                rows = pl.ds(r0, EPILOGUE_ROWS)
                qn = qn_ref[rows, :]
                q_scr[h, rows, :] = (_dot(qn, wq_ref[h]) * cq_ref[rows, :]
                                     + _dot(qn, wqr_ref[h]) * sq_ref[rows, :]).astype(bf16)

    q_rows = pl.ds(pl.multiple_of(qi * tq, tq), tq)
    q_heads = [q_scr[h, q_rows, :] for h in range(MLA_HEADS)]
    _causal_softmax_heads(
        q_heads, lambda g, start: k_scr[g, pl.ds(start, tq), :],
        lambda g, start: v_scr[g, pl.ds(start, tq), :], qi, s_scr, m_scr, acc_scr)

    @pl.when(qi == pl.num_programs(1) - 1)
    def _():
        for r0 in range(0, o_ref.shape[0], EPILOGUE_ROWS):
            rows = pl.ds(r0, EPILOGUE_ROWS)
            o = [_softmax_normalise(acc_scr[h, rows, :], MLA_V) for h in range(MLA_HEADS)]
            o_ref[rows, :] = jnp.concatenate(
                [_pair_lanes(o[0], o[1], MLA_V), _pair_lanes(o[2], o[3], MLA_V)], axis=1).astype(bf16)


def _mla_attention(qn, kvn, kpe, wq, wqr, wk, wv, place, cq, sq, batch, seq):
    tq = 256
    nq = seq // tq
    const = lambda a: pl.BlockSpec(a.shape, lambda b, i: (0,) * a.ndim)
    return pl.pallas_call(
        _mla_body,
        grid=(batch, nq),
        in_specs=[
            pl.BlockSpec((seq, 256), lambda b, i: (b, 0)),
            pl.BlockSpec((seq, 128), lambda b, i: (b, 0)),
            pl.BlockSpec((seq, 128), lambda b, i: (b, 0)),
            const(wq), const(wqr), const(wk), const(wv), const(place), const(cq), const(sq),
        ],
        out_specs=pl.BlockSpec((seq, GROUP_W), lambda b, i: (b, 0)),
        out_shape=jax.ShapeDtypeStruct((batch * seq, GROUP_W), bf16),
        scratch_shapes=[pltpu.VMEM((MLA_HEADS, seq, 128), bf16),
                        pltpu.VMEM((MLA_HEADS, seq, 128), bf16),
                        pltpu.VMEM((MLA_HEADS, seq, LANE), bf16)]
        + _softmax_scratch(MLA_HEADS, nq, tq),
        compiler_params=_cparams(("arbitrary", "arbitrary")),
        name="mla_attn",
    )(qn, kvn, kpe, wq, wqr, wk, wv, place, cq, sq)


def _diff_body(q_ref, k_ref, v_ref, lq1_ref, lk1_ref, lq2_ref, lk2_ref, g_ref,
               o_ref, v_scr, s_scr, m_scr, acc_scr, *, lambda_init):
    qi = pl.program_id(1)
    tq = q_ref.shape[0]

    @pl.when(qi == 0)
    def _():
        v = v_ref[...]
        for h in range(DF_HEADS):
            v_scr[h] = _with_ones_lanes(v[:, DF_V * h:DF_V * (h + 1)], DF_V)

    q = q_ref[...]
    q_heads = [jnp.where(_lane_group_mask(q.shape, DF_DK * g, DF_DK * (g + 1)), q, jnp.zeros_like(q))
               for g in range(2 * DF_HEADS)]
    _causal_softmax_heads(
        q_heads, lambda g, start: k_ref[pl.ds(start, tq), :],
        lambda g, start: v_scr[g // 2, pl.ds(start, tq), :], qi, s_scr, m_scr, acc_scr)

    @pl.when(qi == pl.num_programs(1) - 1)
    def _():
        lam = (jnp.exp(jnp.sum(lq1_ref[...] * lk1_ref[...], axis=1, keepdims=True))
               - jnp.exp(jnp.sum(lq2_ref[...] * lk2_ref[...], axis=1, keepdims=True)) + lambda_init)
        gain = g_ref[...] * (1.0 - lambda_init)
        for r0 in range(0, o_ref.shape[0], EPILOGUE_ROWS):
            rows = pl.ds(r0, EPILOGUE_ROWS)
            ys = []
            for h in range(DF_HEADS):
                o = (_softmax_normalise(acc_scr[2 * h, rows, :], DF_V)
                     - lam * _softmax_normalise(acc_scr[2 * h + 1, rows, :], DF_V))
                ms = jnp.sum(o * o, axis=1, keepdims=True) * (1.0 / DF_V)
                ys.append(o * lax.rsqrt(ms + RMS_EPS) * gain)
            o_ref[rows, :] = jnp.concatenate(
                [_pair_lanes(ys[0], ys[1], DF_V), _pair_lanes(ys[2], ys[3], DF_V)], axis=1).astype(bf16)


def _diff_attention(dq, dk, dv, lq1, lk1, lq2, lk2, g_sub, batch, seq, lambda_init):
    tq = 256
    nq = seq // tq
    const = lambda a: pl.BlockSpec(a.shape, lambda b, i: (0,) * a.ndim)
    return pl.pallas_call(
        functools.partial(_diff_body, lambda_init=lambda_init),
        grid=(batch, nq),
        in_specs=[
            pl.BlockSpec((tq, 256), lambda b, i: (b * nq + i, 0)),
            pl.BlockSpec((seq, 256), lambda b, i: (b, 0)),
            pl.BlockSpec((seq, 256), lambda b, i: (b, 0)),
            const(lq1), const(lk1), const(lq2), const(lk2), const(g_sub),
        ],
        out_specs=pl.BlockSpec((seq, GROUP_W), lambda b, i: (b, 0)),
        out_shape=jax.ShapeDtypeStruct((batch * seq, GROUP_W), bf16),
        scratch_shapes=[pltpu.VMEM((DF_HEADS, seq, LANE), bf16)]
        + _softmax_scratch(2 * DF_HEADS, nq, tq),
        compiler_params=_cparams(("arbitrary", "arbitrary")),
        name="diff_attn",
    )(dq, dk, dv, lq1, lk1, lq2, lk2, g_sub)


def _split3(a):
    hi = a.astype(bf16)
    r1 = a - hi.astype(f32)
    mid = r1.astype(bf16)
    return hi, mid, (r1 - mid.astype(f32)).astype(bf16)


def _dot_exact_rhs01(a, b01):
    hi, mid, lo = _split3(a)
    return _dot(hi, b01) + _dot(mid, b01) + _dot(lo, b01)


def _dot_exact_lhs01(a01, b):
    hi, mid, lo = _split3(b)
    return _dot(a01, hi) + _dot(a01, mid) + _dot(a01, lo)


def _mlstm_gate_terms(g, eif):
    R, W, L = g.shape[0], eif.shape[1] // 2, ML_CHUNK
    lf = jnp.minimum(g, 0.0) - jnp.log(1.0 + jnp.exp(-jnp.abs(g)))
    glane = lax.broadcasted_iota(jnp.int32, g.shape, 1)
    wide = _dot_exact_rhs01(jnp.where(glane < ML_HEADS, g, lf), eif)
    iw, lfw = wide[:, :W], wide[:, W:]
    rr = lax.broadcasted_iota(jnp.int32, (R, R), 0)
    cc = lax.broadcasted_iota(jnp.int32, (R, R), 1)
    bw = _dot_exact_lhs01(((rr >= cc) & (rr // L == cc // L)).astype(bf16), lfw)
    rw = iw - bw
    pos_in_chunk = lax.broadcasted_iota(jnp.int32, (R, W), 0) % L
    cm = rw
    for sh in (1, 2, 4, 8, 16, 32):
        cm = jnp.maximum(cm, jnp.where(pos_in_chunk >= sh, pltpu.roll(cm, sh, axis=0), NEG))
    return bw, rw, cm


def _mlstm_chunk_local(q, k, kt_bd, v, rw, cm):
    L, W = q.shape
    row = lax.broadcasted_iota(jnp.int32, (L, W), 0)
    pos = lax.broadcasted_iota(jnp.int32, (L, W), 1) % L
    head_eq = (lax.broadcasted_iota(jnp.int32, (W, W), 0) // ML_DH
               == lax.broadcasted_iota(jnp.int32, (W, W), 1) // ML_DH)
    r_row = jnp.sum(jnp.where(pos == row, rw, 0.0), axis=0, keepdims=True)
    v4 = jnp.concatenate([v] * ML_HEADS, axis=0)
    vbd = jnp.where(head_eq, v4, jnp.zeros_like(v4))
    rmax = cm[L - 1:L, :]
    c_loc = _dot((kt_bd * jnp.exp(r_row - rmax)).astype(bf16), vbd)
    n_loc = jnp.sum(k.astype(f32) * jnp.exp(rw - rmax), axis=0, keepdims=True)
    return _dot(q, kt_bd.astype(bf16)), vbd, r_row, c_loc, n_loc


def _mlstm_chunk_step(q, o, bw, cm, qk, vbd, r_row, c_loc, n_loc, c_prev, n_prev, m_prev):
    L, W = q.shape
    row = lax.broadcasted_iota(jnp.int32, (L, W), 0)
    pos = lax.broadcasted_iota(jnp.int32, (L, W), 1) % L
    head_eq = (lax.broadcasted_iota(jnp.int32, (W, W), 0) // ML_DH
               == lax.broadcasted_iota(jnp.int32, (W, W), 1) // ML_DH)
    mw = jnp.maximum(m_prev, cm)
    a_inter = jnp.exp(m_prev - mw)
    pw = qk * jnp.exp(jnp.where(pos <= row, r_row - mw, NEG))
    sums = _dot_exact_rhs01(jnp.concatenate([q.astype(f32) * n_prev, pw], axis=0), head_eq.astype(bf16))
    den = a_inter * sums[:L] + sums[L:]
    hden = jnp.maximum(jnp.abs(den), jnp.exp(-(bw + mw)))
    num = a_inter * _dot(q, c_prev.astype(bf16)) + _dot(pw.astype(bf16), vbd)
    y = num / hden * _sigmoid(o.astype(f32))
    rmax, m_last = cm[L - 1:L, :], mw[L - 1:L, :]
    a_w = jnp.exp(m_prev - m_last)
    b_w = jnp.exp(rmax - m_last)
    return y, c_prev * a_w + c_loc * b_w, a_w * n_prev + b_w * n_loc, bw[L - 1:L, :] + m_last


def _mlstm_body(q_ref, k_ref, kt_ref, v_ref, o_ref, g_ref, eif_ref, sel_ref, y_ref, c_scr, nm_scr):
    @pl.when(pl.program_id(1) == 0)
    def _():
        c_scr[...] = jnp.zeros_like(c_scr)
        nm_scr[...] = jnp.zeros_like(nm_scr)

    c, n, m = c_scr[...], nm_scr[0:1, :], nm_scr[1:2, :]
    kt = kt_ref[...]
    eif = eif_ref[...]
    w = kt.shape[0]
    head_eq = (lax.broadcasted_iota(jnp.int32, (w, w), 0) // ML_DH
               == lax.broadcasted_iota(jnp.int32, (w, w), 1) // ML_DH)
    bw, rw, cm = _mlstm_gate_terms(g_ref[...], eif)
    n_chunks = q_ref.shape[0] // ML_CHUNK
    local = []
    for i in range(n_chunks):
        rows = pl.ds(i * ML_CHUNK, ML_CHUNK)
        lo, hi = i * ML_CHUNK, (i + 1) * ML_CHUNK
        kt_bd = jnp.where(head_eq, _dot(kt, sel_ref[i]), 0.0)
        local.append(_mlstm_chunk_local(q_ref[rows, :], k_ref[rows, :], kt_bd, v_ref[rows, :],
                                        rw[lo:hi], cm[lo:hi]))
    for i in range(n_chunks):
        rows = pl.ds(i * ML_CHUNK, ML_CHUNK)
        lo, hi = i * ML_CHUNK, (i + 1) * ML_CHUNK
        y, c, n, m = _mlstm_chunk_step(q_ref[rows, :], o_ref[rows, :], bw[lo:hi], cm[lo:hi],
                                       *local[i], c, n, m)
        y_ref[rows, :] = y.astype(y_ref.dtype)
    c_scr[...] = c
    nm_scr[0:1, :] = n
    nm_scr[1:2, :] = m


MLSTM_CHUNKS_PER_STEP = 4


def _mlstm(mq, mk, mkt, mv, mo, gcol, batch, seq):
    cps = MLSTM_CHUNKS_PER_STEP
    L = ML_CHUNK * cps
    nc = seq // L
    blk = lambda w: pl.BlockSpec((L, w), lambda b, c: (b * nc + c, 0))
    const = lambda a: pl.BlockSpec(a.shape, lambda b, c: (0,) * a.ndim)
    gl = jnp.arange(LANE)[:, None]
    col = jnp.arange(2 * GROUP_W)[None, :]
    eif = (gl == (col // GROUP_W) * ML_HEADS + (col % GROUP_W) // ML_DH).astype(bf16)
    p = jnp.arange(L)[None, :, None]
    cc = jnp.arange(GROUP_W)[None, None, :]
    sel = (p == jnp.arange(cps)[:, None, None] * ML_CHUNK + cc % ML_CHUNK).astype(bf16)
    return pl.pallas_call(
        _mlstm_body,
        grid=(batch, nc),
        in_specs=[blk(256), blk(256), pl.BlockSpec((GROUP_W, L), lambda b, c: (0, b * nc + c)),
                  blk(256), blk(256), blk(128), const(eif), const(sel)],
        out_specs=blk(GROUP_W),
        out_shape=jax.ShapeDtypeStruct((batch * seq, GROUP_W), bf16),
        scratch_shapes=[pltpu.VMEM((GROUP_W, GROUP_W), f32), pltpu.VMEM((8, GROUP_W), f32)],
        compiler_params=_cparams(("arbitrary", "arbitrary")),
        name="mlstm",
    )(mq, mk, mkt, mv, mo, gcol, eif, sel)


def _route(scores, sel):
    ne, tm = sel.shape
    eid = lax.broadcasted_iota(jnp.int32, (ne, tm), 0)
    grp = eid // E_PER_GROUP
    big = ne + 1

    def top2(vals):
        m1 = jnp.max(vals, axis=0, keepdims=True)
        i1 = jnp.min(jnp.where(vals == m1, eid, big), axis=0, keepdims=True)
        rest = jnp.where(eid == i1, NEG, vals)
        m2 = jnp.max(rest, axis=0, keepdims=True)
        i2 = jnp.min(jnp.where(rest == m2, eid, big), axis=0, keepdims=True)
        return m1, i1, m2, i2

    best_score, best_grp = None, None
    for gi in range(N_GROUPS):
        m1, _, m2, _ = top2(jnp.where(grp == gi, sel, NEG))
        sc = m1 + m2
        if best_score is None:
            best_score, best_grp = sc, jnp.zeros_like(sc, dtype=jnp.int32)
        else:
            better = sc > best_score
            best_grp = jnp.where(better, gi, best_grp)
            best_score = jnp.where(better, sc, best_score)
    _, i1, _, i2 = top2(jnp.where(grp == best_grp, sel, NEG))
    picked = jnp.where((eid == i1) | (eid == i2), scores, 0.0)
    return picked / jnp.sum(picked, axis=0, keepdims=True), best_grp


MOE_ROW_TILE = 256


def _outproj_route_body(ya_ref, yb_ref, yc_ref, yd_ref, x_ref, mod_ref, w_ref, g_ref, b_ref,
                        wr_ref, br_ref, x1_ref, u2e_ref, grp_ref):
    d = x_ref.shape[1]
    mix = None
    for j, y_ref in enumerate((ya_ref, yb_ref, yc_ref, yd_ref)):
        part = _dot(y_ref[...], w_ref[GROUP_W * j:GROUP_W * (j + 1), :])
        mix = part if mix is None else mix + part
    x1 = _layer_norm(DN_ALPHA * x_ref[...] + mod_ref[0, 2:3, :] * mix) * g_ref[...] + b_ref[...]
    x1_ref[...] = x1
    u = _layer_norm(x1) * (1.0 + mod_ref[0, 4:5, :]) + mod_ref[0, 3:4, :]
    u_hi = u.astype(bf16)
    u_lo = (u - u_hi.astype(f32)).astype(bf16)
    ne = br_ref.shape[0]
    part = _dot_nt(wr_ref[...], u_hi)
    logits = part[:ne] + part[ne:] + _dot_nt(wr_ref[:ne, :], u_lo)
    scores = _sigmoid(logits)
    gates_t, best_grp = _route(scores, scores + br_ref[...])
    u2e_ref[:, :d] = u_hi.astype(f32)
    u2e_ref[:, d:] = jnp.zeros((x1.shape[0], LANE), f32)
    u2e_ref[:, d:d + ne] = gates_t.T
    grp_ref[...] = best_grp


def _outproj_route(ya, yb, yc, yd, x2d, mod, w_out, ln_g, ln_b, wr_split, b_router, seq):
    t, d = x2d.shape
    tm = 512
    tps = seq // tm
    row = lambda w: pl.BlockSpec((tm, w), lambda i: (i, 0))
    const = lambda a: pl.BlockSpec(a.shape, lambda i: (0,) * a.ndim)
    return pl.pallas_call(
        _outproj_route_body,
        grid=(t // tm,),
        in_specs=[row(256), row(256), row(256), row(256), row(d),
                  pl.BlockSpec((1, 6, d), lambda i: (i // tps, 0, 0)),
                  const(w_out), const(ln_g), const(ln_b), const(wr_split), const(b_router)],
        out_specs=[row(d), row(d + LANE), pl.BlockSpec((1, tm), lambda i: (0, i))],
        out_shape=[jax.ShapeDtypeStruct((t, d), f32), jax.ShapeDtypeStruct((t, d + LANE), f32),
                   jax.ShapeDtypeStruct((1, t), jnp.int32)],
        compiler_params=_cparams(("arbitrary",)),
        name="outproj_route",
    )(ya, yb, yc, yd, x2d, mod, w_out, ln_g, ln_b, wr_split, b_router)


def _dispatch_plan(grp, tr):
    t = grp.shape[0]
    rows = t + N_GROUPS * tr
    order = jnp.sort(grp * t + jnp.arange(t, dtype=jnp.int32)) % t
    order = jnp.concatenate([order, jnp.zeros((rows - t,), order.dtype)])
    counts = jnp.sum((grp[:, None] == jnp.arange(N_GROUPS)[None, :]).astype(jnp.int32), axis=0)
    padded = ((counts + tr - 1) // tr) * tr
    row_end = jnp.cumsum(padded)
    r = jnp.arange(rows, dtype=jnp.int32)
    read_tok = jnp.zeros((rows,), jnp.int32)
    write_row = jnp.zeros((rows,), jnp.int32)
    for g in range(N_GROUPS):
        row_start = row_end[g] - padded[g]
        tok_start = jnp.sum(counts[:g])
        shifted = jnp.roll(order, row_start - tok_start)
        in_grp = (r >= row_start) & (r < row_end[g])
        valid = in_grp & (r < row_start + counts[g])
        read_tok = jnp.where(valid, shifted, jnp.where(in_grp, order[jnp.minimum(tok_start, t - 1)], read_tok))
        write_row = jnp.where(valid, shifted, jnp.where(in_grp, t + g * tr + r % tr, write_row))
    tile_start = jnp.arange(rows // tr, dtype=jnp.int32) * tr
    tile_grp = jnp.minimum(jnp.sum((tile_start[:, None] >= row_end[None, :]).astype(jnp.int32), axis=1),
                           N_GROUPS - 1)
    i32 = lambda a: a.astype(jnp.int32)
    return i32(read_tok), i32(write_row), i32(tile_grp), i32(row_end[-1:] // tr)


def _wait_row_gather(n_rows, hbm, vmem, sem):
    pltpu.make_async_copy(hbm.at[pl.ds(0, n_rows), :], vmem, sem).wait()


def _wait_row_scatter(n_rows, vmem, hbm, sem):
    pltpu.make_async_copy(vmem, hbm.at[pl.ds(0, n_rows), :], sem).wait()


def _experts_body(read_ref, write_ref, tgrp_ref, ntile_ref,
                  u2e_hbm, w1_ref, w3_ref, w2_ref, ys_hbm, xbuf0, xbuf1, obuf0, obuf1, w1b, w3b, w2b,
                  gsem, ssem):
    j = pl.program_id(0)
    n_tiles = ntile_ref[0]
    tr, d = obuf0.shape
    xbuf, obuf = (xbuf0, xbuf1), (obuf0, obuf1)

    def start_gather(tile, s):
        for i in range(tr):
            pltpu.make_async_copy(u2e_hbm.at[pl.ds(read_ref[tile * tr + i], 1), :],
                                  xbuf[s].at[pl.ds(i, 1), :], gsem.at[s]).start()

    def start_scatter(tile, s):
        for i in range(tr):
            pltpu.make_async_copy(obuf[s].at[pl.ds(i, 1), :],
                                  ys_hbm.at[pl.ds(write_ref[tile * tr + i], 1), :], ssem.at[s]).start()

    def load_tile(s):
        xs = xbuf[s][...]
        return xs[:, :d].astype(bf16), xs[:, d:]

    def compute(tile, s, xb, gslab):
        lane = lax.broadcasted_iota(jnp.int32, gslab.shape, 1)
        first_expert = tgrp_ref[tile] * E_PER_GROUP
        acc = None
        for e in range(E_PER_GROUP):
            ge = jnp.sum(jnp.where(lane == first_expert + e, gslab, 0.0), axis=1, keepdims=True)
            h1 = _dot(xb, w1b[e])
            h = h1 * _sigmoid(h1) * _dot(xb, w3b[e]) * ge
            part = _dot(h.astype(bf16), w2b[e])
            acc = part if acc is None else acc + part
        obuf[s][...] = acc

    nxt = jnp.minimum(j + 1, n_tiles - 1)

    last = tgrp_ref.shape[0] - 1
    @pl.when((j < n_tiles) & ((j == 0) | (tgrp_ref[jnp.minimum(j, last)] != tgrp_ref[jnp.clip(j - 1, 0, last)])))
    def _():
        w1b[...] = w1_ref[0].astype(bf16)
        w3b[...] = w3_ref[0].astype(bf16)
        w2b[...] = w2_ref[0].astype(bf16)

    @pl.when(j == 0)
    def _():
        start_gather(0, 0)
        start_gather(nxt, 1)
        obuf[1][...] = jnp.zeros_like(obuf[1])
        n_tok = ys_hbm.shape[0] - N_GROUPS * tr
        spare = [pltpu.make_async_copy(obuf[1], ys_hbm.at[pl.ds(n_tok + g * tr, tr), :], ssem.at[1])
                 for g in range(N_GROUPS)]
        for cp in spare:
            cp.start()
        for cp in spare:
            cp.wait()
        _wait_row_gather(tr, u2e_hbm, xbuf[0], gsem.at[0])
        compute(0, 0, *load_tile(0))

    for s in (0, 1):
        @pl.when((j > 0) & (j < n_tiles) & (j % 2 == s))
        def _(s=s):
            start_gather(nxt, 1 - s)
            start_scatter(j - 1, 1 - s)

        @pl.when((j > 0) & (j < n_tiles) & (j % 2 == s))
        def _(s=s):
            _wait_row_gather(tr, u2e_hbm, xbuf[s], gsem.at[s])
            compute(j, s, *load_tile(s))
            _wait_row_scatter(tr, obuf[1 - s], ys_hbm, ssem.at[1 - s])

        @pl.when((j == n_tiles) & (j % 2 == s))
        def _(s=s):
            _wait_row_gather(tr, u2e_hbm, xbuf[s], gsem.at[s])
            start_scatter(j - 1, 1 - s)
            _wait_row_scatter(tr, obuf[1 - s], ys_hbm, ssem.at[1 - s])


def _experts(read_tok, write_row, tile_grp, n_tiles, u2e, w1, w3, w2, layer):
    rows = read_tok.shape[0]
    tr = MOE_ROW_TILE
    max_tiles = tile_grp.shape[0]
    _, ne, d, fe = w1.shape
    grouped = lambda w: w.reshape(w.shape[0], N_GROUPS, E_PER_GROUP, *w.shape[2:])
    wspec = lambda a, b: pl.BlockSpec(
        (None, 1, E_PER_GROUP, a, b),
        lambda j, rd, wr, tg, nt: (layer, tg[jnp.minimum(j, max_tiles - 1)], 0, 0, 0))
    return pl.pallas_call(
        _experts_body,
        grid_spec=pltpu.PrefetchScalarGridSpec(
            num_scalar_prefetch=4, grid=(max_tiles + 1,),
            in_specs=[pl.BlockSpec(memory_space=pl.ANY), wspec(d, fe), wspec(d, fe), wspec(fe, d)],
            out_specs=pl.BlockSpec(memory_space=pl.ANY),
            scratch_shapes=[pltpu.VMEM((tr, u2e.shape[1]), f32), pltpu.VMEM((tr, u2e.shape[1]), f32),
                            pltpu.VMEM((tr, d), f32), pltpu.VMEM((tr, d), f32),
                            pltpu.VMEM((E_PER_GROUP, d, fe), bf16), pltpu.VMEM((E_PER_GROUP, d, fe), bf16),
                            pltpu.VMEM((E_PER_GROUP, fe, d), bf16),
                            pltpu.SemaphoreType.DMA((2,)), pltpu.SemaphoreType.DMA((2,))]),
        out_shape=jax.ShapeDtypeStruct((rows, d), f32),
        compiler_params=_cparams(("arbitrary",)),
        name="experts",
    )(read_tok, write_row, tile_grp, n_tiles, u2e, grouped(w1), grouped(w3), grouped(w2))


def _moe_combine_body(y_ref, x_ref, mod_ref, g_ref, b_ref, o_ref):
    r = DN_ALPHA * x_ref[...] + mod_ref[0, 5:6, :] * y_ref[...]
    o_ref[...] = _layer_norm(r) * g_ref[...] + b_ref[...]


def _moe_combine(ys, x1, mod, ln_g, ln_b, seq):
    t, d = x1.shape
    tm = 512
    tps = seq // tm
    row = pl.BlockSpec((tm, d), lambda i: (i, 0))
    const = lambda a: pl.BlockSpec(a.shape, lambda i: (0,) * a.ndim)
    return pl.pallas_call(
        _moe_combine_body,
        grid=(t // tm,),
        in_specs=[row, row, pl.BlockSpec((1, 6, d), lambda i: (i // tps, 0, 0)), const(ln_g), const(ln_b)],
        out_specs=row,
        out_shape=jax.ShapeDtypeStruct((t, d), f32),
        compiler_params=_cparams(("arbitrary",)),
        name="moe_combine",
    )(ys, x1, mod, ln_g, ln_b)


def _pad_cols(a, width):
    return jnp.pad(a, ((0, 0), (0, width - a.shape[1])))


def _pack_w_in(w):
    parts, acc = [], 0
    for sz in IN_SIZES:
        parts.append(w[:, acc:acc + sz])
        acc += sz
    (a_q, a_kv, a_kr, m_q, m_k, m_v, m_i, m_f, m_o, c_b, c_c, c_h, d_q, d_k, d_v) = parts
    gates = _pad_cols(jnp.concatenate([m_i, m_f], axis=1), LANE)
    packed = jnp.concatenate([
        _pad_cols(a_q, 256), a_kv, _pad_cols(a_kr, LANE),
        m_q, m_k, m_v, gates, m_o, c_b, c_c, c_h, d_q, d_k, d_v], axis=1)
    return packed.astype(bf16)


def _pack_mla(w_uq, w_ukv):
    half = MLA_ROPE // 2
    wq3 = w_uq.reshape(MLA_Q_LORA, MLA_HEADS, MLA_NOPE + MLA_ROPE).transpose(1, 0, 2)
    wkv3 = w_ukv.reshape(MLA_KV_LORA, MLA_HEADS, MLA_NOPE + MLA_V).transpose(1, 0, 2)
    nope, pe = wq3[:, :, :MLA_NOPE], wq3[:, :, MLA_NOPE:]
    rot = jnp.concatenate([-pe[:, :, half:], pe[:, :, :half]], axis=2)
    pad = lambda a: jnp.pad(a, ((0, 0), (0, 256 - a.shape[1]), (0, LANE - a.shape[2]))).astype(bf16)
    wq = pad(wq3)
    wqr = pad(jnp.concatenate([jnp.zeros_like(nope), rot], axis=2))
    wk = jnp.pad(wkv3[:, :, :MLA_NOPE], ((0, 0), (0, 0), (0, LANE - MLA_NOPE))).astype(bf16)
    return wq, wqr, wk, wkv3[:, :, MLA_NOPE:].astype(bf16)


def _rope_tables(seq):
    half = DF_DK // 2
    inv = 1.0 / (ROPE_THETA ** (jnp.arange(0, DF_DK, 2, dtype=f32) / DF_DK))
    ang = jnp.arange(seq, dtype=f32)[:, None] * inv[None, :]
    cos, sin = jnp.cos(ang), jnp.sin(ang)
    cos_t = jnp.tile(jnp.concatenate([cos, cos], axis=1), (1, 256 // DF_DK))
    sin_t = jnp.tile(jnp.concatenate([-sin, sin], axis=1), (1, 256 // DF_DK))
    scale = (MLA_NOPE + MLA_ROPE) ** -0.5 * LOG2E
    ones = jnp.ones((seq, MLA_NOPE), f32)
    zeros = jnp.zeros((seq, MLA_NOPE), f32)
    cq = _pad_cols(jnp.concatenate([ones, cos, cos], axis=1) * scale, LANE)
    sq = _pad_cols(jnp.concatenate([zeros, sin, sin], axis=1) * scale, LANE)
    return cos_t, sin_t, cq, sq


def kernel(x, c, w_ada, b_ada, w_in, mla_g_q, mla_g_kv, mla_w_uq, mla_w_ukv, ml_b_i, ml_b_f, sc_w,
           df_lq1, df_lk1, df_lq2, df_lk2, df_g, w_out, ln1_g, ln1_b, w_router, b_router, w1, w3, w2,
           ln2_g, ln2_b):
    batch, seq, d = x.shape
    depth = w_in.shape[0]
    assert MLA_ROPE == DF_DK, "both rotary blocks share one table"
    cos_t, sin_t, cq, sq = _rope_tables(seq)
    place = jnp.zeros((LANE, LANE), f32).at[jnp.arange(MLA_ROPE), MLA_NOPE + jnp.arange(MLA_ROPE)].set(1.0).astype(bf16)
    wr_hi = w_router.astype(bf16)
    wr_split = jnp.concatenate([wr_hi, (w_router - wr_hi.astype(f32)).astype(bf16)], axis=1).T
    mod_all = _ada_mod(c, w_ada, b_ada).reshape(depth, batch, 6, d)
    xf = x.reshape(batch * seq, d)
    for l in range(depth):
        mod = mod_all[l]
        w_packed = _pack_w_in(w_in[l])
        gq = _pad_cols(mla_g_q[l][None, :], 256)
        gkv = mla_g_kv[l][None, :]
        gbias = _pad_cols(jnp.concatenate([ml_b_i[l], ml_b_f[l]])[None, :], LANE)
        wkt = w_packed[:, OFF_MK:OFF_MK + GROUP_W].T
        (qn, kvn, kpe, mq, mk, mv, mo, gcol, yc, dq, dk, dv, mkt) = _inproj(
            xf, mod, w_packed, wkt, cos_t, sin_t, gq, gkv, gbias, sc_w[l], seq)
        wq, wqr, wk, wv = _pack_mla(mla_w_uq[l], mla_w_ukv[l])
        ya = _mla_attention(qn, kvn, kpe, wq, wqr, wk, wv, place, cq, sq, batch, seq)
        yb = _mlstm(mq, mk, mkt, mv, mo, gcol, batch, seq)
        lambda_init = 0.8 - 0.6 * math.exp(-0.3 * l)
        yd = _diff_attention(dq, dk, dv, df_lq1[l][None, :], df_lk1[l][None, :], df_lq2[l][None, :],
                             df_lk2[l][None, :], _pad_cols(df_g[l][None, :], LANE),
                             batch, seq, lambda_init)
        x1, u2e, grp = _outproj_route(ya, yb, yc, yd, xf, mod, w_out[l].astype(bf16), ln1_g[l][None, :],
                                      ln1_b[l][None, :], wr_split, b_router[:, None], seq)
        plan = _dispatch_plan(grp[0], MOE_ROW_TILE)
        ys = _experts(*plan, u2e, w1, w3, w2, l)
        xf = _moe_combine(ys, x1, mod, ln2_g[l][None, :], ln2_b[l][None, :], seq)
    return xf.reshape(batch, seq, d)
```

```python
import functools
import math

import jax
import jax.numpy as jnp
from jax import lax
from jax.experimental import pallas as pl
from jax.experimental.pallas import tpu as pltpu

f32 = jnp.float32
bf16 = jnp.bfloat16
HIGHEST = lax.Precision.HIGHEST

GROUP_W = 256
MLA_HEADS, MLA_NOPE, MLA_ROPE, MLA_V = 4, 64, 32, 64
MLA_Q_LORA, MLA_KV_LORA = 192, 128
ML_HEADS, ML_DH, ML_CHUNK = 4, 64, 64
DF_HEADS, DF_DK, DF_V = 4, 32, 64
ROPE_THETA = 10000.0
N_EXPERTS, N_GROUPS, E_PER_GROUP, D_EXPERT = 16, 4, 4, 256
DEPTH = 2
DN_ALPHA = (2 * DEPTH) ** 0.25
LN_EPS = 1e-5
RMS_EPS = 1e-6
IN_SIZES = (192, 128, 32, 256, 256, 256, 4, 4, 256, 256, 256, 256, 256, 256, 256)

OFF_PQ, OFF_PKV, OFF_KR = 0, 256, 384
OFF_MQ, OFF_MK, OFF_MV, OFF_MG, OFF_MO = 512, 768, 1024, 1280, 1408
OFF_CB, OFF_CC, OFF_CH = 1664, 1920, 2176
OFF_DQ, OFF_DK, OFF_DV = 2432, 2688, 2944
IN_PACKED = 3200

LANE = 128
VMEM_LIMIT = 48 * 1024 * 1024
NEG = -1e30
LOG2E = math.log2(math.e)


def _cparams(sem):
    return pltpu.CompilerParams(dimension_semantics=sem, vmem_limit_bytes=VMEM_LIMIT)


def _sigmoid(x):
    return 1.0 / (1.0 + jnp.exp(-x))


def _layer_norm(x):
    mu = jnp.mean(x, axis=-1, keepdims=True)
    xc = x - mu
    var = jnp.mean(xc * xc, axis=-1, keepdims=True)
    return xc * lax.rsqrt(var + LN_EPS)


def _dot(a, b, **kw):
    return jnp.dot(a, b, preferred_element_type=f32, **kw)


def _dot_nt(a, b, **kw):
    return lax.dot_general(a, b, (((1,), (1,)), ((), ())), preferred_element_type=f32, **kw)


def _dot_tn(a, b, **kw):
    return lax.dot_general(a, b, (((0,), (0,)), ((), ())), preferred_element_type=f32, **kw)


def _expand_groups(cols, rows, width, group):
    lane = lax.broadcasted_iota(jnp.int32, (rows, width), 1)
    out = jnp.broadcast_to(cols[-1], (rows, width))
    for h in range(len(cols) - 2, -1, -1):
        out = jnp.where(lane < group * (h + 1), cols[h], out)
    return out


def _lane_group_mask(shape, lo, hi):
    lane = lax.broadcasted_iota(jnp.int32, shape, 1)
    return (lane >= lo) & (lane < hi)


def _ada_body(c_ref, w_ref, b_ref, o_ref):
    c = c_ref[...]
    ca = (c * _sigmoid(c)).astype(bf16)
    o_ref[0] = _dot(ca, w_ref[0].astype(bf16)) + b_ref[0]


def _ada_mod(c, w_ada, b_ada):
    depth, d, n = w_ada.shape
    b = c.shape[0]
    tn = 1536
    return pl.pallas_call(
        _ada_body,
        grid=(depth, n // tn),
        in_specs=[
            pl.BlockSpec((b, d), lambda l, j: (0, 0)),
            pl.BlockSpec((1, d, tn), lambda l, j: (l, 0, j)),
            pl.BlockSpec((1, 1, tn), lambda l, j: (l, 0, j)),
        ],
        out_specs=pl.BlockSpec((1, b, tn), lambda l, j: (l, 0, j)),
        out_shape=jax.ShapeDtypeStruct((depth, b, n), f32),
        compiler_params=_cparams(("arbitrary", "arbitrary")),
        name="ada_mod",
    )(c, w_ada, b_ada.reshape(depth, 1, n))


def _rope_lanes(x, cos, sin_signed):
    w = x.shape[1]
    lane = lax.broadcasted_iota(jnp.int32, x.shape, 1)
    rot = jnp.where(lane % 32 < 16, pltpu.roll(x, w - 16, axis=1), pltpu.roll(x, 16, axis=1))
    return x * cos + rot * sin_signed


def _inproj_body(x_ref, mod_ref, w_ref, wkt_ref, cos_ref, sin_ref, gq_ref, gkv_ref, gb_ref, cw_ref,
                 qn_ref, kvn_ref, kpe_ref, mq_ref, mk_ref, mv_ref, mo_ref, gcol_ref, yc_ref,
                 dq_ref, dk_ref, dv_ref, mkt_ref, carry_ref, *, tiles_per_seq):
    i = pl.program_id(0)
    tm = x_ref.shape[0]
    u = _layer_norm(x_ref[...]) * (1.0 + mod_ref[0, 1:2, :]) + mod_ref[0, 0:1, :]
    ub = u.astype(bf16)

    def seg(off, n):
        return _dot(ub, w_ref[:, off:off + n])

    cos = cos_ref[...]
    sin = sin_ref[...]

    pq = seg(OFF_PQ, 256)
    ms = jnp.sum(pq * pq, axis=-1, keepdims=True) * (1.0 / MLA_Q_LORA)
    qn_ref[...] = (pq * lax.rsqrt(ms + RMS_EPS) * gq_ref[...]).astype(bf16)
    pkv = seg(OFF_PKV, 128)
    ms = jnp.sum(pkv * pkv, axis=-1, keepdims=True) * (1.0 / MLA_KV_LORA)
    kvn_ref[...] = (pkv * lax.rsqrt(ms + RMS_EPS) * gkv_ref[...]).astype(bf16)
    kpe_ref[...] = _rope_lanes(seg(OFF_KR, 128), cos[:, :LANE], sin[:, :LANE]).astype(bf16)

    mq_ref[...] = (seg(OFF_MQ, 256) * (ML_DH ** -0.5)).astype(bf16)
    mk_ref[...] = seg(OFF_MK, 256).astype(bf16)
    mkt_ref[...] = _dot_nt(wkt_ref[...], ub).astype(bf16)
    mv_ref[...] = seg(OFF_MV, 256).astype(bf16)
    gcol_ref[...] = seg(OFF_MG, 128) + gb_ref[...]
    mo_ref[...] = seg(OFF_MO, 256).astype(bf16)

    uc = seg(OFF_CC, 256) * seg(OFF_CH, 256)

    @pl.when(i % tiles_per_seq == 0)
    def _():
        carry_ref[...] = jnp.zeros_like(carry_ref)

    prev = carry_ref[...]
    row =lax.broadcasted_iota(jnp.int32, uc.shape, 0)
    u1 = jnp.where(row == 0, prev[7:8, :], pltpu.roll(uc, 1, axis=0))
    u2 = jnp.where(row == 0, prev[6:7, :], jnp.where(row == 1, prev[7:8, :], pltpu.roll(uc, 2, axis=0)))
    carry_ref[...] = uc[tm - 8:, :]
    conv = cw_ref[0:1, :] * u2 + cw_ref[1:2, :] * u1 + cw_ref[2:3, :] * uc
    yc_ref[...] = (seg(OFF_CB, 256) * conv).astype(bf16)

    dq_ref[...] = (_rope_lanes(seg(OFF_DQ, 256), cos, sin) * (DF_DK ** -0.5 * LOG2E)).astype(bf16)
    dk_ref[...] = _rope_lanes(seg(OFF_DK, 256), cos, sin).astype(bf16)
    dv_ref[...] = seg(OFF_DV, 256).astype(bf16)


def _inproj(x2d, mod, w_packed, wkt, cos_t, sin_t, gq, gkv, gbias, conv_w, seq):
    t, d = x2d.shape
    tm = 512
    tps = seq // tm
    row = lambda w: pl.BlockSpec((tm, w), lambda i: (i, 0))
    const = lambda a: pl.BlockSpec(a.shape, lambda i: (0,) * a.ndim)
    widths = (256, 128, 128, 256, 256, 256, 256, 128, 256, 256, 256, 256)
    dtypes = (bf16, bf16, bf16, bf16, bf16, bf16, bf16, f32, bf16, bf16, bf16, bf16)
    return pl.pallas_call(
        functools.partial(_inproj_body, tiles_per_seq=tps),
        grid=(t // tm,),
        in_specs=[
            row(d),
            pl.BlockSpec((1, 6, d), lambda i: (i // tps, 0, 0)),
            const(w_packed), const(wkt),
            pl.BlockSpec((tm, 256), lambda i: (i % tps, 0)),
            pl.BlockSpec((tm, 256), lambda i: (i % tps, 0)),
            const(gq), const(gkv), const(gbias), const(conv_w),
        ],
        out_specs=[row(w) for w in widths] + [pl.BlockSpec((GROUP_W, tm), lambda i: (0, i))],
        out_shape=[jax.ShapeDtypeStruct((t, w), dt) for w, dt in zip(widths, dtypes)]
        + [jax.ShapeDtypeStruct((GROUP_W, t), bf16)],
        scratch_shapes=[pltpu.VMEM((8, 256), f32)],
        compiler_params=_cparams(("arbitrary",)),
        name="inproj",
    )(x2d, mod, w_packed, wkt, cos_t, sin_t, gq, gkv, gbias, conv_w)


def _causal_softmax_heads(q_heads, k_at, v_at, qi, s_scr, m_scr, acc_scr):
    n_heads = len(q_heads)
    tq = q_heads[0].shape[0]
    half = tq // 2
    rows = pl.ds(pl.multiple_of(qi * tq, tq), tq)

    def scores(g, j):
        return _dot_nt(q_heads[g], k_at(g, pl.multiple_of(j * tq, tq)))

    def fold_max(g, s):
        m_scr[g] = jnp.maximum(m_scr[g], jnp.maximum(s[:, :half], s[:, half:]))

    m_scr[...] = jnp.full(m_scr.shape, NEG, f32)
    for g in range(n_heads):
        acc_scr[g, rows, :] = jnp.zeros((tq, LANE), f32)

    def score_blocks(js):
        for g in range(n_heads):
            mx = None
            for j in js:
                s = scores(g, j)
                s_scr[g, j] = s
                s = jnp.maximum(s[:, :half], s[:, half:])
                mx = s if mx is None else jnp.maximum(mx, s)
            m_scr[g] = jnp.maximum(m_scr[g], mx)

    def prob_blocks(js):
        for g in range(n_heads):
            mb = m_scr[g]
            pv = None
            for j in js:
                sj = s_scr[g, j]
                p = jnp.concatenate([jnp.exp2(sj[:, :half] - mb), jnp.exp2(sj[:, half:] - mb)],
                                    axis=1).astype(bf16)
                d = _dot(p, v_at(g, pl.multiple_of(j * tq, tq)))
                pv = d if pv is None else pv + d
            acc_scr[g, rows, :] += pv

    def over_blocks(n, blocks_fn):
        @pl.loop(0, n // 4)
        def _(jj):
            blocks_fn(tuple(4 * jj + u for u in range(4)))

        rest = (n // 4) * 4

        @pl.when(n % 4 >= 2)
        def _():
            blocks_fn((rest, rest + 1))

        @pl.when(n % 2 == 1)
        def _():
            blocks_fn((n - 1,))

    over_blocks(qi, score_blocks)
    rowi = lax.broadcasted_iota(jnp.int32, (tq, tq), 0)
    coli = lax.broadcasted_iota(jnp.int32, (tq, tq), 1)
    for g in range(n_heads):
        s = jnp.where(coli <= rowi, scores(g, qi), NEG)
        s_scr[g, qi] = s
        fold_max(g, s)
        m_scr[g] = jnp.broadcast_to(jnp.max(m_scr[g], axis=1, keepdims=True), (tq, half))

    over_blocks(qi + 1, prob_blocks)


def _softmax_scratch(n_heads, nq, tq):
    return [pltpu.VMEM((n_heads, nq, tq, tq), f32), pltpu.VMEM((n_heads, tq, tq // 2), f32),
            pltpu.VMEM((n_heads, nq * tq, LANE), f32)]


EPILOGUE_ROWS = 512


def _with_ones_lanes(v, dv):
    return jnp.concatenate([v.astype(bf16), jnp.ones((v.shape[0], LANE - dv), bf16)], axis=1)


def _softmax_normalise(acc, dv):
    lane = lax.broadcasted_iota(jnp.int32, acc.shape, 1)
    return jnp.where(lane < dv, acc / pltpu.roll(acc, LANE - dv, axis=1), 0.0)


def _pair_lanes(a, b, dv):
    lane = lax.broadcasted_iota(jnp.int32, a.shape, 1)
    return jnp.where(lane < dv, a, pltpu.roll(b, dv, axis=1))


def _mla_body(qn_ref, kvn_ref, kpe_ref, wq_ref, wqr_ref, wk_ref, wv_ref, place_ref, cq_ref, sq_ref,
              o_ref, q_scr, k_scr, v_scr, s_scr, m_scr, acc_scr):
    qi = pl.program_id(1)
    tq = s_scr.shape[2]

    @pl.when(qi == 0)
    def _():
        kvn = kvn_ref[...]
        kpe_placed = _dot(kpe_ref[...], place_ref[...])
        for h in range(MLA_HEADS):
            k_scr[h] = (_dot(kvn, wk_ref[h]) + kpe_placed).astype(bf16)
            v_scr[h] = _with_ones_lanes(_dot(kvn, wv_ref[h]), MLA_V)
            for r0 in range(0, qn_ref.shape[0], EPILOGUE_ROWS):
                rows = pl.ds(r0, EPILOGUE_ROWS)
                qn = qn_ref[rows, :]
                q_scr[h, rows, :] = (_dot(qn, wq_ref[h]) * cq_ref[rows, :]
                                     + _dot(qn, wqr_ref[h]) * sq_ref[rows, :]).astype(bf16)

    q_rows = pl.ds(pl.multiple_of(qi * tq, tq), tq)
    q_heads = [q_scr[h, q_rows, :] for h in range(MLA_HEADS)]
    _causal_softmax_heads(
        q_heads, lambda g, start: k_scr[g, pl.ds(start, tq), :],
        lambda g, start: v_scr[g, pl.ds(start, tq), :], qi, s_scr, m_scr, acc_scr)

    @pl.when(qi == pl.num_programs(1) - 1)
    def _():
        for r0 in range(0, o_ref.shape[0], EPILOGUE_ROWS):
            rows = pl.ds(r0, EPILOGUE_ROWS)
            o = [_softmax_normalise(acc_scr[h, rows, :], MLA_V) for h in range(MLA_HEADS)]
            o_ref[rows, :] = jnp.concatenate(
                [_pair_lanes(o[0], o[1], MLA_V), _pair_lanes(o[2], o[3], MLA_V)], axis=1).astype(bf16)


def _mla_attention(qn, kvn, kpe, wq, wqr, wk, wv, place, cq, sq, batch, seq):
    tq = 256
    nq = seq // tq
    const = lambda a: pl.BlockSpec(a.shape, lambda b, i: (0,) * a.ndim)
    return pl.pallas_call(
        _mla_body,
        grid=(batch, nq),
        in_specs=[
            pl.BlockSpec((seq, 256), lambda b, i: (b, 0)),
            pl.BlockSpec((seq, 128), lambda b, i: (b, 0)),
            pl.BlockSpec((seq, 128), lambda b, i: (b, 0)),
            const(wq), const(wqr), const(wk), const(wv), const(place), const(cq), const(sq),
        ],
        out_specs=pl.BlockSpec((seq, GROUP_W), lambda b, i: (b, 0)),
        out_shape=jax.ShapeDtypeStruct((batch * seq, GROUP_W), bf16),
        scratch_shapes=[pltpu.VMEM((MLA_HEADS, seq, 128), bf16),
                        pltpu.VMEM((MLA_HEADS, seq, 128), bf16),
                        pltpu.VMEM((MLA_HEADS, seq, LANE), bf16)]
        + _softmax_scratch(MLA_HEADS, nq, tq),
        compiler_params=_cparams(("arbitrary", "arbitrary")),
        name="mla_attn",
    )(qn, kvn, kpe, wq, wqr, wk, wv, place, cq, sq)


def _diff_body(q_ref, k_ref, v_ref, lq1_ref, lk1_ref, lq2_ref, lk2_ref, g_ref,
               o_ref, v_scr, s_scr, m_scr, acc_scr, *, lambda_init):
    qi = pl.program_id(1)
    tq = q_ref.shape[0]

    @pl.when(qi == 0)
    def _():
        v = v_ref[...]
        for h in range(DF_HEADS):
            v_scr[h] = _with_ones_lanes(v[:, DF_V * h:DF_V * (h + 1)], DF_V)

    q = q_ref[...]
    q_heads = [jnp.where(_lane_group_mask(q.shape, DF_DK * g, DF_DK * (g + 1)), q, jnp.zeros_like(q))
               for g in range(2 * DF_HEADS)]
    _causal_softmax_heads(
        q_heads, lambda g, start: k_ref[pl.ds(start, tq), :],
        lambda g, start: v_scr[g // 2, pl.ds(start, tq), :], qi, s_scr, m_scr, acc_scr)

    @pl.when(qi == pl.num_programs(1) - 1)
    def _():
        lam = (jnp.exp(jnp.sum(lq1_ref[...] * lk1_ref[...], axis=1, keepdims=True))
               - jnp.exp(jnp.sum(lq2_ref[...] * lk2_ref[...], axis=1, keepdims=True)) + lambda_init)
        gain = g_ref[...] * (1.0 - lambda_init)
        for r0 in range(0, o_ref.shape[0], EPILOGUE_ROWS):
            rows = pl.ds(r0, EPILOGUE_ROWS)
            ys = []
            for h in range(DF_HEADS):
                o = (_softmax_normalise(acc_scr[2 * h, rows, :], DF_V)
                     - lam * _softmax_normalise(acc_scr[2 * h + 1, rows, :], DF_V))
                ms = jnp.sum(o * o, axis=1, keepdims=True) * (1.0 / DF_V)
                ys.append(o * lax.rsqrt(ms + RMS_EPS) * gain)
            o_ref[rows, :] = jnp.concatenate(
                [_pair_lanes(ys[0], ys[1], DF_V), _pair_lanes(ys[2], ys[3], DF_V)], axis=1).astype(bf16)


def _diff_attention(dq, dk, dv, lq1, lk1, lq2, lk2, g_sub, batch, seq, lambda_init):
    tq = 256
    nq = seq // tq
    const = lambda a: pl.BlockSpec(a.shape, lambda b, i: (0,) * a.ndim)
    return pl.pallas_call(
        functools.partial(_diff_body, lambda_init=lambda_init),
        grid=(batch, nq),
        in_specs=[
            pl.BlockSpec((tq, 256), lambda b, i: (b * nq + i, 0)),
            pl.BlockSpec((seq, 256), lambda b, i: (b, 0)),
            pl.BlockSpec((seq, 256), lambda b, i: (b, 0)),
            const(lq1), const(lk1), const(lq2), const(lk2), const(g_sub),
        ],
        out_specs=pl.BlockSpec((seq, GROUP_W), lambda b, i: (b, 0)),
        out_shape=jax.ShapeDtypeStruct((batch * seq, GROUP_W), bf16),
        scratch_shapes=[pltpu.VMEM((DF_HEADS, seq, LANE), bf16)]
        + _softmax_scratch(2 * DF_HEADS, nq, tq),
        compiler_params=_cparams(("arbitrary", "arbitrary")),
        name="diff_attn",
    )(dq, dk, dv, lq1, lk1, lq2, lk2, g_sub)


def _split3(a):
    hi = a.astype(bf16)
    r1 = a - hi.astype(f32)
    mid = r1.astype(bf16)
    return hi, mid, (r1 - mid.astype(f32)).astype(bf16)


def _dot_exact_rhs01(a, b01):
    hi, mid, lo = _split3(a)
    return _dot(hi, b01) + _dot(mid, b01) + _dot(lo, b01)


def _dot_exact_lhs01(a01, b):
    hi, mid, lo = _split3(b)
    return _dot(a01, hi) + _dot(a01, mid) + _dot(a01, lo)


def _mlstm_gate_terms(g, eif):
    R, W, L = g.shape[0], eif.shape[1] // 2, ML_CHUNK
    lf = jnp.minimum(g, 0.0) - jnp.log(1.0 + jnp.exp(-jnp.abs(g)))
    glane = lax.broadcasted_iota(jnp.int32, g.shape, 1)
    wide = _dot_exact_rhs01(jnp.where(glane < ML_HEADS, g, lf), eif)
    iw, lfw = wide[:, :W], wide[:, W:]
    rr = lax.broadcasted_iota(jnp.int32, (R, R), 0)
    cc = lax.broadcasted_iota(jnp.int32, (R, R), 1)
    bw = _dot_exact_lhs01(((rr >= cc) & (rr // L == cc // L)).astype(bf16), lfw)
    rw = iw - bw
    pos_in_chunk = lax.broadcasted_iota(jnp.int32, (R, W), 0) % L
    cm = rw
    for sh in (1, 2, 4, 8, 16, 32):
        cm = jnp.maximum(cm, jnp.where(pos_in_chunk >= sh, pltpu.roll(cm, sh, axis=0), NEG))
    return bw, rw, cm


def _mlstm_chunk_local(q, k, kt_bd, v, rw, cm):
    L, W = q.shape
    row = lax.broadcasted_iota(jnp.int32, (L, W), 0)
    pos = lax.broadcasted_iota(jnp.int32, (L, W), 1) % L
    head_eq = (lax.broadcasted_iota(jnp.int32, (W, W), 0) // ML_DH
               == lax.broadcasted_iota(jnp.int32, (W, W), 1) // ML_DH)
    r_row = jnp.sum(jnp.where(pos == row, rw, 0.0), axis=0, keepdims=True)
    v4 = jnp.concatenate([v] * ML_HEADS, axis=0)
    vbd = jnp.where(head_eq, v4, jnp.zeros_like(v4))
    rmax = cm[L - 1:L, :]
    c_loc = _dot((kt_bd * jnp.exp(r_row - rmax)).astype(bf16), vbd)
    n_loc = jnp.sum(k.astype(f32) * jnp.exp(rw - rmax), axis=0, keepdims=True)
    return _dot(q, kt_bd.astype(bf16)), vbd, r_row, c_loc, n_loc


def _mlstm_chunk_step(q, o, bw, cm, qk, vbd, r_row, c_loc, n_loc, c_prev, n_prev, m_prev):
    L, W = q.shape
    row = lax.broadcasted_iota(jnp.int32, (L, W), 0)
    pos = lax.broadcasted_iota(jnp.int32, (L, W), 1) % L
    head_eq = (lax.broadcasted_iota(jnp.int32, (W, W), 0) // ML_DH
               == lax.broadcasted_iota(jnp.int32, (W, W), 1) // ML_DH)
    mw = jnp.maximum(m_prev, cm)
    a_inter = jnp.exp(m_prev - mw)
    pw = qk * jnp.exp(jnp.where(pos <= row, r_row - mw, NEG))
    sums = _dot_exact_rhs01(jnp.concatenate([q.astype(f32) * n_prev, pw], axis=0), head_eq.astype(bf16))
    den = a_inter * sums[:L] + sums[L:]
    hden = jnp.maximum(jnp.abs(den), jnp.exp(-(bw + mw)))
    num = a_inter * _dot(q, c_prev.astype(bf16)) + _dot(pw.astype(bf16), vbd)
    y = num / hden * _sigmoid(o.astype(f32))
    rmax, m_last = cm[L - 1:L, :], mw[L - 1:L, :]
    a_w = jnp.exp(m_prev - m_last)
    b_w = jnp.exp(rmax - m_last)
    return y, c_prev * a_w + c_loc * b_w, a_w * n_prev + b_w * n_loc, bw[L - 1:L, :] + m_last


def _mlstm_body(q_ref, k_ref, kt_ref, v_ref, o_ref, g_ref, eif_ref, sel_ref, y_ref, c_scr, nm_scr):
    @pl.when(pl.program_id(1) == 0)
    def _():
        c_scr[...] = jnp.zeros_like(c_scr)
        nm_scr[...] = jnp.zeros_like(nm_scr)

    c, n, m = c_scr[...], nm_scr[0:1, :], nm_scr[1:2, :]
    kt = kt_ref[...]
    eif = eif_ref[...]
    w = kt.shape[0]
    head_eq = (lax.broadcasted_iota(jnp.int32, (w, w), 0) // ML_DH
               == lax.broadcasted_iota(jnp.int32, (w, w), 1) // ML_DH)
    bw, rw, cm = _mlstm_gate_terms(g_ref[...], eif)
    n_chunks = q_ref.shape[0] // ML_CHUNK
    local = []
    for i in range(n_chunks):
        rows = pl.ds(i * ML_CHUNK, ML_CHUNK)
        lo, hi = i * ML_CHUNK, (i + 1) * ML_CHUNK
        kt_bd = jnp.where(head_eq, _dot(kt, sel_ref[i]), 0.0)
        local.append(_mlstm_chunk_local(q_ref[rows, :], k_ref[rows, :], kt_bd, v_ref[rows, :],
                                        rw[lo:hi], cm[lo:hi]))
    for i in range(n_chunks):
        rows = pl.ds(i * ML_CHUNK, ML_CHUNK)
        lo, hi = i * ML_CHUNK, (i + 1) * ML_CHUNK
        y, c, n, m = _mlstm_chunk_step(q_ref[rows, :], o_ref[rows, :], bw[lo:hi], cm[lo:hi],
                                       *local[i], c, n, m)
        y_ref[rows, :] = y.astype(y_ref.dtype)
    c_scr[...] = c
    nm_scr[0:1, :] = n
    nm_scr[1:2, :] = m


MLSTM_CHUNKS_PER_STEP = 4


def _mlstm(mq, mk, mkt, mv, mo, gcol, batch, seq):
    cps = MLSTM_CHUNKS_PER_STEP
    L = ML_CHUNK * cps
    nc = seq // L
    blk = lambda w: pl.BlockSpec((L, w), lambda b, c: (b * nc + c, 0))
    const = lambda a: pl.BlockSpec(a.shape, lambda b, c: (0,) * a.ndim)
    gl = jnp.arange(LANE)[:, None]
    col = jnp.arange(2 * GROUP_W)[None, :]
    eif = (gl == (col // GROUP_W) * ML_HEADS + (col % GROUP_W) // ML_DH).astype(bf16)
    p = jnp.arange(L)[None, :, None]
    cc = jnp.arange(GROUP_W)[None, None, :]
    sel = (p == jnp.arange(cps)[:, None, None] * ML_CHUNK + cc % ML_CHUNK).astype(bf16)
    return pl.pallas_call(
        _mlstm_body,
        grid=(batch, nc),
        in_specs=[blk(256), blk(256), pl.BlockSpec((GROUP_W, L), lambda b, c: (0, b * nc + c)),
                  blk(256), blk(256), blk(128), const(eif), const(sel)],
        out_specs=blk(GROUP_W),
        out_shape=jax.ShapeDtypeStruct((batch * seq, GROUP_W), bf16),
        scratch_shapes=[pltpu.VMEM((GROUP_W, GROUP_W), f32), pltpu.VMEM((8, GROUP_W), f32)],
        compiler_params=_cparams(("arbitrary", "arbitrary")),
        name="mlstm",
    )(mq, mk, mkt, mv, mo, gcol, eif, sel)


def _route(scores, sel):
    ne, tm = sel.shape
    eid = lax.broadcasted_iota(jnp.int32, (ne, tm), 0)
    grp = eid // E_PER_GROUP
    big = ne + 1

    def top2(vals):
        m1 = jnp.max(vals, axis=0, keepdims=True)
        i1 = jnp.min(jnp.where(vals == m1, eid, big), axis=0, keepdims=True)
        rest = jnp.where(eid == i1, NEG, vals)
        m2 = jnp.max(rest, axis=0, keepdims=True)
        i2 = jnp.min(jnp.where(rest == m2, eid, big), axis=0, keepdims=True)
        return m1, i1, m2, i2

    best_score, best_grp = None, None
    for gi in range(N_GROUPS):
        m1, _, m2, _ = top2(jnp.where(grp == gi, sel, NEG))
        sc = m1 + m2
        if best_score is None:
            best_score, best_grp = sc, jnp.zeros_like(sc, dtype=jnp.int32)
        else:
            better = sc > best_score
            best_grp = jnp.where(better, gi, best_grp)
            best_score = jnp.where(better, sc, best_score)
    _, i1, _, i2 = top2(jnp.where(grp == best_grp, sel, NEG))
    picked = jnp.where((eid == i1) | (eid == i2), scores, 0.0)
    return picked / jnp.sum(picked, axis=0, keepdims=True), best_grp


MOE_ROW_TILE = 256


def _outproj_route_body(ya_ref, yb_ref, yc_ref, yd_ref, x_ref, mod_ref, w_ref, g_ref, b_ref,
                        wr_ref, br_ref, x1_ref, u2e_ref, grp_ref):
    d = x_ref.shape[1]
    mix = None
    for j, y_ref in enumerate((ya_ref, yb_ref, yc_ref, yd_ref)):
        part = _dot(y_ref[...], w_ref[GROUP_W * j:GROUP_W * (j + 1), :])
        mix = part if mix is None else mix + part
    x1 = _layer_norm(DN_ALPHA * x_ref[...] + mod_ref[0, 2:3, :] * mix) * g_ref[...] + b_ref[...]
    x1_ref[...] = x1
    u = _layer_norm(x1) * (1.0 + mod_ref[0, 4:5, :]) + mod_ref[0, 3:4, :]
    u_hi = u.astype(bf16)
    u_lo = (u - u_hi.astype(f32)).astype(bf16)
    ne = br_ref.shape[0]
    part = _dot_nt(wr_ref[...], u_hi)
    logits = part[:ne] + part[ne:] + _dot_nt(wr_ref[:ne, :], u_lo)
    scores = _sigmoid(logits)
    gates_t, best_grp = _route(scores, scores + br_ref[...])
    u2e_ref[:, :d] = u_hi.astype(f32)
    u2e_ref[:, d:] = jnp.zeros((x1.shape[0], LANE), f32)
    u2e_ref[:, d:d + ne] = gates_t.T
    grp_ref[...] = best_grp


def _outproj_route(ya, yb, yc, yd, x2d, mod, w_out, ln_g, ln_b, wr_split, b_router, seq):
    t, d = x2d.shape
    tm = 512
    tps = seq // tm
    row = lambda w: pl.BlockSpec((tm, w), lambda i: (i, 0))
    const = lambda a: pl.BlockSpec(a.shape, lambda i: (0,) * a.ndim)
    return pl.pallas_call(
        _outproj_route_body,
        grid=(t // tm,),
        in_specs=[row(256), row(256), row(256), row(256), row(d),
                  pl.BlockSpec((1, 6, d), lambda i: (i // tps, 0, 0)),
                  const(w_out), const(ln_g), const(ln_b), const(wr_split), const(b_router)],
        out_specs=[row(d), row(d + LANE), pl.BlockSpec((1, tm), lambda i: (0, i))],
        out_shape=[jax.ShapeDtypeStruct((t, d), f32), jax.ShapeDtypeStruct((t, d + LANE), f32),
                   jax.ShapeDtypeStruct((1, t), jnp.int32)],
        compiler_params=_cparams(("arbitrary",)),
        name="outproj_route",
    )(ya, yb, yc, yd, x2d, mod, w_out, ln_g, ln_b, wr_split, b_router)


def _dispatch_plan(grp, tr):
    t = grp.shape[0]
    rows = t + N_GROUPS * tr
    order = jnp.sort(grp * t + jnp.arange(t, dtype=jnp.int32)) % t
    order = jnp.concatenate([order, jnp.zeros((rows - t,), order.dtype)])
    onehot = (grp[:, None] == jnp.arange(N_GROUPS)[None, :]).astype(jnp.int32)
    counts = jnp.sum(onehot, axis=0)
    padded = ((counts + tr - 1) // tr) * tr
    row_end = jnp.cumsum(padded)
    rank = jnp.sum((jnp.cumsum(onehot, axis=0) - 1) * onehot, axis=1)
    row_of_tok = (row_end - padded)[grp] + rank
    r = jnp.arange(rows, dtype=jnp.int32)
    tok_of_row = jnp.zeros((rows,), jnp.int32)
    for g in range(N_GROUPS):
        row_start = row_end[g] - padded[g]
        tok_start = jnp.sum(counts[:g])
        shifted = jnp.roll(order, row_start - tok_start)
        in_grp = (r >= row_start) & (r < row_end[g])
        valid = in_grp & (r < row_start + counts[g])
        tok_of_row = jnp.where(valid, shifted,
                               jnp.where(in_grp, order[jnp.minimum(tok_start, t - 1)], tok_of_row))
    tile_start = jnp.arange(rows // tr, dtype=jnp.int32) * tr
    tile_grp = jnp.minimum(jnp.sum((tile_start[:, None] >= row_end[None, :]).astype(jnp.int32), axis=1),
                           N_GROUPS - 1)
    i32 = lambda a: a.astype(jnp.int32)
    return i32(tok_of_row), i32(row_of_tok), i32(tile_grp), i32(row_end[-1:] // tr)


def _wait_row_gather(n_rows, hbm, vmem, sem):
    pltpu.make_async_copy(hbm.at[pl.ds(0, n_rows), :], vmem, sem).wait()


def _start_row_gather(idx_ref, first, n_rows, hbm, vmem, sem):
    for i in range(n_rows):
        pltpu.make_async_copy(hbm.at[pl.ds(idx_ref[first + i], 1), :], vmem.at[pl.ds(i, 1), :], sem).start()


def _experts_body(tok_ref, tgrp_ref, ntile_ref, u2e_hbm, w1_ref, w3_ref, w2_ref, o_ref,
                  xbuf0, xbuf1, w1b, w3b, w2b, gsem):
    j = pl.program_id(0)
    n_tiles = ntile_ref[0]
    tr, d = o_ref.shape
    xbuf = (xbuf0, xbuf1)

    def compute(s):
        xs = xbuf[s][...]
        xb = xs[:, :d].astype(bf16)
        gslab = xs[:, d:]
        lane = lax.broadcasted_iota(jnp.int32, gslab.shape, 1)
        first_expert = tgrp_ref[j] * E_PER_GROUP
        acc = None
        for e in range(E_PER_GROUP):
            ge = jnp.sum(jnp.where(lane == first_expert + e, gslab, 0.0), axis=1, keepdims=True)
            h1 = _dot(xb, w1b[e])
            h = h1 * _sigmoid(h1) * _dot(xb, w3b[e]) * ge
            part = _dot(h.astype(bf16), w2b[e])
            acc = part if acc is None else acc + part
        o_ref[...] = acc

    @pl.when((j < n_tiles) & ((j == 0) | (tgrp_ref[j] != tgrp_ref[jnp.maximum(j - 1, 0)])))
    def _():
        w1b[...] = w1_ref[0].astype(bf16)
        w3b[...] = w3_ref[0].astype(bf16)
        w2b[...] = w2_ref[0].astype(bf16)

    @pl.when(j == 0)
    def _():
        _start_row_gather(tok_ref, 0, tr, u2e_hbm, xbuf[0], gsem.at[0])

    for s in (0, 1):
        @pl.when((j + 1 < n_tiles) & (j % 2 == s))
        def _(s=s):
            _start_row_gather(tok_ref, (j + 1) * tr, tr, u2e_hbm, xbuf[1 - s], gsem.at[1 - s])

        @pl.when((j < n_tiles) & (j % 2 == s))
        def _(s=s):
            _wait_row_gather(tr, u2e_hbm, xbuf[s], gsem.at[s])
            compute(s)

    @pl.when(j >= n_tiles)
    def _():
        o_ref[...] = jnp.zeros_like(o_ref)


def _experts(tok_of_row, tile_grp, n_tiles, u2e, w1, w3, w2, layer):
    rows = tok_of_row.shape[0]
    tr = MOE_ROW_TILE
    _, ne, d, fe = w1.shape
    grouped = lambda w: w.reshape(w.shape[0], N_GROUPS, E_PER_GROUP, *w.shape[2:])
    wspec = lambda a, b: pl.BlockSpec((None, 1, E_PER_GROUP, a, b), lambda j, tok, tg, nt: (layer, tg[j], 0, 0, 0))
    return pl.pallas_call(
        _experts_body,
        grid_spec=pltpu.PrefetchScalarGridSpec(
            num_scalar_prefetch=3, grid=(rows // tr,),
            in_specs=[pl.BlockSpec(memory_space=pl.ANY), wspec(d, fe), wspec(d, fe), wspec(fe, d)],
            out_specs=pl.BlockSpec((tr, d), lambda j, tok, tg, nt: (j, 0)),
            scratch_shapes=[pltpu.VMEM((tr, u2e.shape[1]), f32), pltpu.VMEM((tr, u2e.shape[1]), f32),
                            pltpu.VMEM((E_PER_GROUP, d, fe), bf16), pltpu.VMEM((E_PER_GROUP, d, fe), bf16),
                            pltpu.VMEM((E_PER_GROUP, fe, d), bf16),
                            pltpu.SemaphoreType.DMA((2,))]),
        out_shape=jax.ShapeDtypeStruct((rows, d), f32),
        compiler_params=_cparams(("arbitrary",)),
        name="experts",
    )(tok_of_row, tile_grp, n_tiles, u2e, grouped(w1), grouped(w3), grouped(w2))


def _moe_combine_body(row_ref, ys_hbm, x_ref, mod_ref, g_ref, b_ref, o_ref, ybuf0, ybuf1, sem):
    i = pl.program_id(0)
    tm = x_ref.shape[0]
    ybuf = (ybuf0, ybuf1)

    @pl.when(i == 0)
    def _():
        _start_row_gather(row_ref, 0, tm, ys_hbm, ybuf[0], sem.at[0])

    for s in (0, 1):
        @pl.when((i + 1 < pl.num_programs(0)) & (i % 2 == s))
        def _(s=s):
            _start_row_gather(row_ref, (i + 1) * tm, tm, ys_hbm, ybuf[1 - s], sem.at[1 - s])

        @pl.when(i % 2 == s)
        def _(s=s):
            _wait_row_gather(tm, ys_hbm, ybuf[s], sem.at[s])
            r = DN_ALPHA * x_ref[...] + mod_ref[0, 5:6, :] * ybuf[s][...]
            o_ref[...] = _layer_norm(r) * g_ref[...] + b_ref[...]


def _moe_combine(row_of_tok, ys, x1, mod, ln_g, ln_b, seq):
    t, d = x1.shape
    tm = 256
    tps = seq // tm
    const = lambda a: pl.BlockSpec(a.shape, lambda i, rows: (0,) * a.ndim)
    return pl.pallas_call(
        _moe_combine_body,
        grid_spec=pltpu.PrefetchScalarGridSpec(
            num_scalar_prefetch=1, grid=(t // tm,),
            in_specs=[pl.BlockSpec(memory_space=pl.ANY),
                      pl.BlockSpec((tm, d), lambda i, rows: (i, 0)),
                      pl.BlockSpec((1, 6, d), lambda i, rows: (i // tps, 0, 0)),
                      const(ln_g), const(ln_b)],
            out_specs=pl.BlockSpec((tm, d), lambda i, rows: (i, 0)),
            scratch_shapes=[pltpu.VMEM((tm, d), f32), pltpu.VMEM((tm, d), f32),
                            pltpu.SemaphoreType.DMA((2,))]),
        out_shape=jax.ShapeDtypeStruct((t, d), f32),
        compiler_params=_cparams(("arbitrary",)),
        name="moe_combine",
    )(row_of_tok, ys, x1, mod, ln_g, ln_b)


def _pad_cols(a, width):
    return jnp.pad(a, ((0, 0), (0, width - a.shape[1])))


def _pack_w_in(w):
    parts, acc = [], 0
    for sz in IN_SIZES:
        parts.append(w[:, acc:acc + sz])
        acc += sz
    (a_q, a_kv, a_kr, m_q, m_k, m_v, m_i, m_f, m_o, c_b, c_c, c_h, d_q, d_k, d_v) = parts
    gates = _pad_cols(jnp.concatenate([m_i, m_f], axis=1), LANE)
    packed = jnp.concatenate([
        _pad_cols(a_q, 256), a_kv, _pad_cols(a_kr, LANE),
        m_q, m_k, m_v, gates, m_o, c_b, c_c, c_h, d_q, d_k, d_v], axis=1)
    return packed.astype(bf16)


def _pack_mla(w_uq, w_ukv):
    half = MLA_ROPE // 2
    wq3 = w_uq.reshape(MLA_Q_LORA, MLA_HEADS, MLA_NOPE + MLA_ROPE).transpose(1, 0, 2)
    wkv3 = w_ukv.reshape(MLA_KV_LORA, MLA_HEADS, MLA_NOPE + MLA_V).transpose(1, 0, 2)
    nope, pe = wq3[:, :, :MLA_NOPE], wq3[:, :, MLA_NOPE:]
    rot = jnp.concatenate([-pe[:, :, half:], pe[:, :, :half]], axis=2)
    pad = lambda a: jnp.pad(a, ((0, 0), (0, 256 - a.shape[1]), (0, LANE - a.shape[2]))).astype(bf16)
    wq = pad(wq3)
    wqr = pad(jnp.concatenate([jnp.zeros_like(nope), rot], axis=2))
    wk = jnp.pad(wkv3[:, :, :MLA_NOPE], ((0, 0), (0, 0), (0, LANE - MLA_NOPE))).astype(bf16)
    return wq, wqr, wk, wkv3[:, :, MLA_NOPE:].astype(bf16)


def _rope_tables(seq):
    half = DF_DK // 2
    inv = 1.0 / (ROPE_THETA ** (jnp.arange(0, DF_DK, 2, dtype=f32) / DF_DK))
    ang = jnp.arange(seq, dtype=f32)[:, None] * inv[None, :]
    cos, sin = jnp.cos(ang), jnp.sin(ang)
    cos_t = jnp.tile(jnp.concatenate([cos, cos], axis=1), (1, 256 // DF_DK))
    sin_t = jnp.tile(jnp.concatenate([-sin, sin], axis=1), (1, 256 // DF_DK))
    scale = (MLA_NOPE + MLA_ROPE) ** -0.5 * LOG2E
    ones = jnp.ones((seq, MLA_NOPE), f32)
    zeros = jnp.zeros((seq, MLA_NOPE), f32)
    cq = _pad_cols(jnp.concatenate([ones, cos, cos], axis=1) * scale, LANE)
    sq = _pad_cols(jnp.concatenate([zeros, sin, sin], axis=1) * scale, LANE)
    return cos_t, sin_t, cq, sq


def kernel(x, c, w_ada, b_ada, w_in, mla_g_q, mla_g_kv, mla_w_uq, mla_w_ukv, ml_b_i, ml_b_f, sc_w,
           df_lq1, df_lk1, df_lq2, df_lk2, df_g, w_out, ln1_g, ln1_b, w_router, b_router, w1, w3, w2,
           ln2_g, ln2_b):
    batch, seq, d = x.shape
    depth = w_in.shape[0]
    assert MLA_ROPE == DF_DK, "both rotary blocks share one table"
    cos_t, sin_t, cq, sq = _rope_tables(seq)
    place = jnp.zeros((LANE, LANE), f32).at[jnp.arange(MLA_ROPE), MLA_NOPE + jnp.arange(MLA_ROPE)].set(1.0).astype(bf16)
    wr_hi = w_router.astype(bf16)
    wr_split = jnp.concatenate([wr_hi, (w_router - wr_hi.astype(f32)).astype(bf16)], axis=1).T
    mod_all = _ada_mod(c, w_ada, b_ada).reshape(depth, batch, 6, d)
    xf = x.reshape(batch * seq, d)
    for l in range(depth):
        mod = mod_all[l]
        w_packed = _pack_w_in(w_in[l])
        gq = _pad_cols(mla_g_q[l][None, :], 256)
        gkv = mla_g_kv[l][None, :]
        gbias = _pad_cols(jnp.concatenate([ml_b_i[l], ml_b_f[l]])[None, :], LANE)
        wkt = w_packed[:, OFF_MK:OFF_MK + GROUP_W].T
        (qn, kvn, kpe, mq, mk, mv, mo, gcol, yc, dq, dk, dv, mkt) = _inproj(
            xf, mod, w_packed, wkt, cos_t, sin_t, gq, gkv, gbias, sc_w[l], seq)
        wq, wqr, wk, wv = _pack_mla(mla_w_uq[l], mla_w_ukv[l])
        ya = _mla_attention(qn, kvn, kpe, wq, wqr, wk, wv, place, cq, sq, batch, seq)
        yb = _mlstm(mq, mk, mkt, mv, mo, gcol, batch, seq)
        lambda_init = 0.8 - 0.6 * math.exp(-0.3 * l)
        yd = _diff_attention(dq, dk, dv, df_lq1[l][None, :], df_lk1[l][None, :], df_lq2[l][None, :],
                             df_lk2[l][None, :], _pad_cols(df_g[l][None, :], LANE),
                             batch, seq, lambda_init)
        x1, u2e, grp = _outproj_route(ya, yb, yc, yd, xf, mod, w_out[l].astype(bf16), ln1_g[l][None, :],
                                      ln1_b[l][None, :], wr_split, b_router[:, None], seq)
        tok_of_row, row_of_tok, tile_grp, n_tiles = _dispatch_plan(grp[0], MOE_ROW_TILE)
        ys = _experts(tok_of_row, tile_grp, n_tiles, u2e, w1, w3, w2, l)
        xf = _moe_combine(row_of_tok, ys, x1, mod, ln2_g[l][None, :], ln2_b[l][None, :], seq)
    return xf.reshape(batch, seq, d)
```

```python
import functools
import math

import jax
import jax.numpy as jnp
from jax import lax
from jax.experimental import pallas as pl
from jax.experimental.pallas import tpu as pltpu

f32 = jnp.float32
bf16 = jnp.bfloat16
HIGHEST = lax.Precision.HIGHEST

GROUP_W = 256
MLA_HEADS, MLA_NOPE, MLA_ROPE, MLA_V = 4, 64, 32, 64
MLA_Q_LORA, MLA_KV_LORA = 192, 128
ML_HEADS, ML_DH, ML_CHUNK = 4, 64, 64
DF_HEADS, DF_DK, DF_V = 4, 32, 64
ROPE_THETA = 10000.0
N_EXPERTS, N_GROUPS, E_PER_GROUP, D_EXPERT = 16, 4, 4, 256
DEPTH = 2
DN_ALPHA = (2 * DEPTH) ** 0.25
LN_EPS = 1e-5
RMS_EPS = 1e-6
IN_SIZES = (192, 128, 32, 256, 256, 256, 4, 4, 256, 256, 256, 256, 256, 256, 256)

OFF_PQ, OFF_PKV, OFF_KR = 0, 256, 384
OFF_MQ, OFF_MK, OFF_MV, OFF_MG, OFF_MO = 512, 768, 1024, 1280, 1408
OFF_CB, OFF_CC, OFF_CH = 1664, 1920, 2176
OFF_DQ, OFF_DK, OFF_DV = 2432, 2688, 2944
IN_PACKED = 3200

LANE = 128
VMEM_LIMIT = 48 * 1024 * 1024
NEG = -1e30
LOG2E = math.log2(math.e)


def _cparams(sem):
    return pltpu.CompilerParams(dimension_semantics=sem, vmem_limit_bytes=VMEM_LIMIT)


def _sigmoid(x):
    return 1.0 / (1.0 + jnp.exp(-x))


def _layer_norm(x):
    mu = jnp.mean(x, axis=-1, keepdims=True)
    xc = x - mu
    var = jnp.mean(xc * xc, axis=-1, keepdims=True)
    return xc * lax.rsqrt(var + LN_EPS)


def _dot(a, b, **kw):
    return jnp.dot(a, b, preferred_element_type=f32, **kw)


def _dot_nt(a, b, **kw):
    return lax.dot_general(a, b, (((1,), (1,)), ((), ())), preferred_element_type=f32, **kw)


def _dot_tn(a, b, **kw):
    return lax.dot_general(a, b, (((0,), (0,)), ((), ())), preferred_element_type=f32, **kw)


def _expand_groups(cols, rows, width, group):
    lane = lax.broadcasted_iota(jnp.int32, (rows, width), 1)
    out = jnp.broadcast_to(cols[-1], (rows, width))
    for h in range(len(cols) - 2, -1, -1):
        out = jnp.where(lane < group * (h + 1), cols[h], out)
    return out


def _lane_group_mask(shape, lo, hi):
    lane = lax.broadcasted_iota(jnp.int32, shape, 1)
    return (lane >= lo) & (lane < hi)


def _ada_body(c_ref, w_ref, b_ref, o_ref):
    c = c_ref[...]
    ca = (c * _sigmoid(c)).astype(bf16)
    o_ref[0] = _dot(ca, w_ref[0].astype(bf16)) + b_ref[0]


def _ada_mod(c, w_ada, b_ada):
    depth, d, n = w_ada.shape
    b = c.shape[0]
    tn = 1536
    return pl.pallas_call(
        _ada_body,
        grid=(depth, n // tn),
        in_specs=[
            pl.BlockSpec((b, d), lambda l, j: (0, 0)),
            pl.BlockSpec((1, d, tn), lambda l, j: (l, 0, j)),
            pl.BlockSpec((1, 1, tn), lambda l, j: (l, 0, j)),
        ],
        out_specs=pl.BlockSpec((1, b, tn), lambda l, j: (l, 0, j)),
        out_shape=jax.ShapeDtypeStruct((depth, b, n), f32),
        compiler_params=_cparams(("arbitrary", "arbitrary")),
        name="ada_mod",
    )(c, w_ada, b_ada.reshape(depth, 1, n))


def _rope_lanes(x, cos, sin_signed):
    w = x.shape[1]
    lane = lax.broadcasted_iota(jnp.int32, x.shape, 1)
    rot = jnp.where(lane % 32 < 16, pltpu.roll(x, w - 16, axis=1), pltpu.roll(x, 16, axis=1))
    return x * cos + rot * sin_signed


def _inproj_body(x_ref, mod_ref, w_ref, wkt_ref, cos_ref, sin_ref, gq_ref, gkv_ref, gb_ref, cw_ref,
                 qn_ref, kvn_ref, kpe_ref, mq_ref, mk_ref, mv_ref, mo_ref, gcol_ref, yc_ref,
                 dq_ref, dk_ref, dv_ref, mkt_ref, carry_ref, *, tiles_per_seq):
    i = pl.program_id(0)
    tm = x_ref.shape[0]
    u = _layer_norm(x_ref[...]) * (1.0 + mod_ref[0, 1:2, :]) + mod_ref[0, 0:1, :]
    ub = u.astype(bf16)

    def seg(off, n):
        return _dot(ub, w_ref[:, off:off + n])

    cos = cos_ref[...]
    sin = sin_ref[...]

    pq = seg(OFF_PQ, 256)
    ms = jnp.sum(pq * pq, axis=-1, keepdims=True) * (1.0 / MLA_Q_LORA)
    qn_ref[...] = (pq * lax.rsqrt(ms + RMS_EPS) * gq_ref[...]).astype(bf16)
    pkv = seg(OFF_PKV, 128)
    ms = jnp.sum(pkv * pkv, axis=-1, keepdims=True) * (1.0 / MLA_KV_LORA)
    kvn_ref[...] = (pkv * lax.rsqrt(ms + RMS_EPS) * gkv_ref[...]).astype(bf16)
    kpe_ref[...] = _rope_lanes(seg(OFF_KR, 128), cos[:, :LANE], sin[:, :LANE]).astype(bf16)

    mq_ref[...] = (seg(OFF_MQ, 256) * (ML_DH ** -0.5)).astype(bf16)
    mk_ref[...] = seg(OFF_MK, 256).astype(bf16)
    mkt_ref[...] = _dot_nt(wkt_ref[...], ub).astype(bf16)
    mv_ref[...] = seg(OFF_MV, 256).astype(bf16)
    gcol_ref[...] = seg(OFF_MG, 128) + gb_ref[...]
    mo_ref[...] = seg(OFF_MO, 256).astype(bf16)

    uc = seg(OFF_CC, 256) * seg(OFF_CH, 256)

    @pl.when(i % tiles_per_seq == 0)
    def _():
        carry_ref[...] = jnp.zeros_like(carry_ref)

    prev = carry_ref[...]
    row =lax.broadcasted_iota(jnp.int32, uc.shape, 0)
    u1 = jnp.where(row == 0, prev[7:8, :], pltpu.roll(uc, 1, axis=0))
    u2 = jnp.where(row == 0, prev[6:7, :], jnp.where(row == 1, prev[7:8, :], pltpu.roll(uc, 2, axis=0)))
    carry_ref[...] = uc[tm - 8:, :]
    conv = cw_ref[0:1, :] * u2 + cw_ref[1:2, :] * u1 + cw_ref[2:3, :] * uc
    yc_ref[...] = (seg(OFF_CB, 256) * conv).astype(bf16)

    dq_ref[...] = (_rope_lanes(seg(OFF_DQ, 256), cos, sin) * (DF_DK ** -0.5 * LOG2E)).astype(bf16)
    dk_ref[...] = _rope_lanes(seg(OFF_DK, 256), cos, sin).astype(bf16)
    dv_ref[...] = seg(OFF_DV, 256).astype(bf16)


def _inproj(x2d, mod, w_packed, wkt, cos_t, sin_t, gq, gkv, gbias, conv_w, seq):
    t, d = x2d.shape
    tm = 512
    tps = seq // tm
    row = lambda w: pl.BlockSpec((tm, w), lambda i: (i, 0))
    const = lambda a: pl.BlockSpec(a.shape, lambda i: (0,) * a.ndim)
    widths = (256, 128, 128, 256, 256, 256, 256, 128, 256, 256, 256, 256)
    dtypes = (bf16, bf16, bf16, bf16, bf16, bf16, bf16, f32, bf16, bf16, bf16, bf16)
    return pl.pallas_call(
        functools.partial(_inproj_body, tiles_per_seq=tps),
        grid=(t // tm,),
        in_specs=[
            row(d),
            pl.BlockSpec((1, 6, d), lambda i: (i // tps, 0, 0)),
            const(w_packed), const(wkt),
            pl.BlockSpec((tm, 256), lambda i: (i % tps, 0)),
            pl.BlockSpec((tm, 256), lambda i: (i % tps, 0)),
            const(gq), const(gkv), const(gbias), const(conv_w),
        ],
        out_specs=[row(w) for w in widths] + [pl.BlockSpec((GROUP_W, tm), lambda i: (0, i))],
        out_shape=[jax.ShapeDtypeStruct((t, w), dt) for w, dt in zip(widths, dtypes)]
        + [jax.ShapeDtypeStruct((GROUP_W, t), bf16)],
        scratch_shapes=[pltpu.VMEM((8, 256), f32)],
        compiler_params=_cparams(("arbitrary",)),
        name="inproj",
    )(x2d, mod, w_packed, wkt, cos_t, sin_t, gq, gkv, gbias, conv_w)


def _causal_softmax_heads(q_heads, k_at, v_at, qi, s_scr, m_scr, acc_scr):
    n_heads = len(q_heads)
    tq = q_heads[0].shape[0]
    half = tq // 2
    rows = pl.ds(pl.multiple_of(qi * tq, tq), tq)

    def scores(g, j):
        return _dot_nt(q_heads[g], k_at(g, pl.multiple_of(j * tq, tq)))

    m_scr[...] = jnp.full(m_scr.shape, NEG, f32)
    for g in range(n_heads):
        acc_scr[g, rows, :] = jnp.zeros((tq, LANE), f32)

    col_minus_row = (lax.broadcasted_iota(jnp.int32, (tq, tq), 1)
                     - lax.broadcasted_iota(jnp.int32, (tq, tq), 0))

    def score_blocks(js):
        for g in range(n_heads):
            mx = None
            for u, j in enumerate(js):
                s = scores(g, j)
                if u == len(js) - 1:
                    s = jnp.where(col_minus_row <= (qi - j) * tq, s, NEG)
                s_scr[g, j] = s
                s = jnp.maximum(s[:, :half], s[:, half:])
                mx = s if mx is None else jnp.maximum(mx, s)
            m_scr[g] = jnp.maximum(m_scr[g], mx)

    def prob_blocks(js):
        for g in range(n_heads):
            mb = m_scr[g]
            pv = None
            for j in js:
                sj = s_scr[g, j]
                p = jnp.concatenate([jnp.exp2(sj[:, :half] - mb), jnp.exp2(sj[:, half:] - mb)],
                                    axis=1).astype(bf16)
                d = _dot(p, v_at(g, pl.multiple_of(j * tq, tq)))
                pv = d if pv is None else pv + d
            acc_scr[g, rows, :] += pv

    def over_blocks(n, blocks_fn):
        @pl.loop(0, n // 4)
        def _(jj):
            blocks_fn(tuple(4 * jj + u for u in range(4)))

        rest = (n // 4) * 4

        @pl.when(n % 4 >= 2)
        def _():
            blocks_fn((rest, rest + 1))

        @pl.when(n % 2 == 1)
        def _():
            blocks_fn((n - 1,))

    over_blocks(qi + 1, score_blocks)
    for g in range(n_heads):
        m_scr[g] = jnp.broadcast_to(jnp.max(m_scr[g], axis=1, keepdims=True), (tq, half))

    over_blocks(qi + 1, prob_blocks)


def _softmax_scratch(n_heads, nq, tq):
    return [pltpu.VMEM((n_heads, nq, tq, tq), f32), pltpu.VMEM((n_heads, tq, tq // 2), f32),
            pltpu.VMEM((n_heads, nq * tq, LANE), f32)]


EPILOGUE_ROWS = 512


def _with_ones_lanes(v, dv):
    return jnp.concatenate([v.astype(bf16), jnp.ones((v.shape[0], LANE - dv), bf16)], axis=1)


def _softmax_normalise(acc, dv):
    lane = lax.broadcasted_iota(jnp.int32, acc.shape, 1)
    return jnp.where(lane < dv, acc / pltpu.roll(acc, LANE - dv, axis=1), 0.0)


def _pair_lanes(a, b, dv):
    lane = lax.broadcasted_iota(jnp.int32, a.shape, 1)
    return jnp.where(lane < dv, a, pltpu.roll(b, dv, axis=1))


def _mla_body(qn_ref, kvn_ref, kpe_ref, wq_ref, wqr_ref, wk_ref, wv_ref, place_ref, cq_ref, sq_ref,
              o_ref, q_scr, k_scr, v_scr, s_scr, m_scr, acc_scr):
    qi = pl.program_id(1)
    tq = s_scr.shape[2]

    @pl.when(qi == 0)
    def _():
        kvn = kvn_ref[...]
        kpe_placed = _dot(kpe_ref[...], place_ref[...])
        for h in range(MLA_HEADS):
            k_scr[h] = (_dot(kvn, wk_ref[h]) + kpe_placed).astype(bf16)
            v_scr[h] = _with_ones_lanes(_dot(kvn, wv_ref[h]), MLA_V)
            for r0 in range(0, qn_ref.shape[0], EPILOGUE_ROWS):
                rows = pl.ds(r0, EPILOGUE_ROWS)
                qn = qn_ref[rows, :]
                q_scr[h, rows, :] = (_dot(qn, wq_ref[h]) * cq_ref[rows, :]
                                     + _dot(qn, wqr_ref[h]) * sq_ref[rows, :]).astype(bf16)

    q_rows = pl.ds(pl.multiple_of(qi * tq, tq), tq)
    q_heads = [q_scr[h, q_rows, :] for h in range(MLA_HEADS)]
    _causal_softmax_heads(
        q_heads, lambda g, start: k_scr[g, pl.ds(start, tq), :],
        lambda g, start: v_scr[g, pl.ds(start, tq), :], qi, s_scr, m_scr, acc_scr)

    @pl.when(qi == pl.num_programs(1) - 1)
    def _():
        for r0 in range(0, o_ref.shape[0], EPILOGUE_ROWS):
            rows = pl.ds(r0, EPILOGUE_ROWS)
            o = [_softmax_normalise(acc_scr[h, rows, :], MLA_V) for h in range(MLA_HEADS)]
            o_ref[rows, :] = jnp.concatenate(
                [_pair_lanes(o[0], o[1], MLA_V), _pair_lanes(o[2], o[3], MLA_V)], axis=1).astype(bf16)


def _mla_attention(qn, kvn, kpe, wq, wqr, wk, wv, place, cq, sq, batch, seq):
    tq = 256
    nq = seq // tq
    const = lambda a: pl.BlockSpec(a.shape, lambda b, i: (0,) * a.ndim)
    return pl.pallas_call(
        _mla_body,
        grid=(batch, nq),
        in_specs=[
            pl.BlockSpec((seq, 256), lambda b, i: (b, 0)),
            pl.BlockSpec((seq, 128), lambda b, i: (b, 0)),
            pl.BlockSpec((seq, 128), lambda b, i: (b, 0)),
            const(wq), const(wqr), const(wk), const(wv), const(place), const(cq), const(sq),
        ],
        out_specs=pl.BlockSpec((seq, GROUP_W), lambda b, i: (b, 0)),
        out_shape=jax.ShapeDtypeStruct((batch * seq, GROUP_W), bf16),
        scratch_shapes=[pltpu.VMEM((MLA_HEADS, seq, 128), bf16),
                        pltpu.VMEM((MLA_HEADS, seq, 128), bf16),
                        pltpu.VMEM((MLA_HEADS, seq, LANE), bf16)]
        + _softmax_scratch(MLA_HEADS, nq, tq),
        compiler_params=_cparams(("arbitrary", "arbitrary")),
        name="mla_attn",
    )(qn, kvn, kpe, wq, wqr, wk, wv, place, cq, sq)


def _diff_body(q_ref, k_ref, v_ref, lq1_ref, lk1_ref, lq2_ref, lk2_ref, g_ref,
               o_ref, v_scr, s_scr, m_scr, acc_scr, *, lambda_init):
    qi = pl.program_id(1)
    tq = q_ref.shape[0]

    @pl.when(qi == 0)
    def _():
        v = v_ref[...]
        for h in range(DF_HEADS):
            v_scr[h] = _with_ones_lanes(v[:, DF_V * h:DF_V * (h + 1)], DF_V)

    q = q_ref[...]
    q_heads = [jnp.where(_lane_group_mask(q.shape, DF_DK * g, DF_DK * (g + 1)), q, jnp.zeros_like(q))
               for g in range(2 * DF_HEADS)]
    _causal_softmax_heads(
        q_heads, lambda g, start: k_ref[pl.ds(start, tq), :],
        lambda g, start: v_scr[g // 2, pl.ds(start, tq), :], qi, s_scr, m_scr, acc_scr)

    @pl.when(qi == pl.num_programs(1) - 1)
    def _():
        lam = (jnp.exp(jnp.sum(lq1_ref[...] * lk1_ref[...], axis=1, keepdims=True))
               - jnp.exp(jnp.sum(lq2_ref[...] * lk2_ref[...], axis=1, keepdims=True)) + lambda_init)
        gain = g_ref[...] * (1.0 - lambda_init)
        for r0 in range(0, o_ref.shape[0], EPILOGUE_ROWS):
            rows = pl.ds(r0, EPILOGUE_ROWS)
            ys = []
            for h in range(DF_HEADS):
                o = (_softmax_normalise(acc_scr[2 * h, rows, :], DF_V)
                     - lam * _softmax_normalise(acc_scr[2 * h + 1, rows, :], DF_V))
                ms = jnp.sum(o * o, axis=1, keepdims=True) * (1.0 / DF_V)
                ys.append(o * lax.rsqrt(ms + RMS_EPS) * gain)
            o_ref[rows, :] = jnp.concatenate(
                [_pair_lanes(ys[0], ys[1], DF_V), _pair_lanes(ys[2], ys[3], DF_V)], axis=1).astype(bf16)


def _diff_attention(dq, dk, dv, lq1, lk1, lq2, lk2, g_sub, batch, seq, lambda_init):
    tq = 256
    nq = seq // tq
    const = lambda a: pl.BlockSpec(a.shape, lambda b, i: (0,) * a.ndim)
    return pl.pallas_call(
        functools.partial(_diff_body, lambda_init=lambda_init),
        grid=(batch, nq),
        in_specs=[
            pl.BlockSpec((tq, 256), lambda b, i: (b * nq + i, 0)),
            pl.BlockSpec((seq, 256), lambda b, i: (b, 0)),
            pl.BlockSpec((seq, 256), lambda b, i: (b, 0)),
            const(lq1), const(lk1), const(lq2), const(lk2), const(g_sub),
        ],
        out_specs=pl.BlockSpec((seq, GROUP_W), lambda b, i: (b, 0)),
        out_shape=jax.ShapeDtypeStruct((batch * seq, GROUP_W), bf16),
        scratch_shapes=[pltpu.VMEM((DF_HEADS, seq, LANE), bf16)]
        + _softmax_scratch(2 * DF_HEADS, nq, tq),
        compiler_params=_cparams(("arbitrary", "arbitrary")),
        name="diff_attn",
    )(dq, dk, dv, lq1, lk1, lq2, lk2, g_sub)


def _split3(a):
    hi = a.astype(bf16)
    r1 = a - hi.astype(f32)
    mid = r1.astype(bf16)
    return hi, mid, (r1 - mid.astype(f32)).astype(bf16)


def _dot_exact_rhs01(a, b01):
    hi, mid, lo = _split3(a)
    return _dot(hi, b01) + _dot(mid, b01) + _dot(lo, b01)


def _dot_exact_lhs01(a01, b):
    hi, mid, lo = _split3(b)
    return _dot(a01, hi) + _dot(a01, mid) + _dot(a01, lo)


def _mlstm_gate_terms(g, eif):
    R, W, L = g.shape[0], eif.shape[1] // 2, ML_CHUNK
    lf = jnp.minimum(g, 0.0) - jnp.log(1.0 + jnp.exp(-jnp.abs(g)))
    glane = lax.broadcasted_iota(jnp.int32, g.shape, 1)
    wide = _dot_exact_rhs01(jnp.where(glane < ML_HEADS, g, lf), eif)
    iw, lfw = wide[:, :W], wide[:, W:]
    rr = lax.broadcasted_iota(jnp.int32, (R, R), 0)
    cc = lax.broadcasted_iota(jnp.int32, (R, R), 1)
    bw = _dot_exact_lhs01(((rr >= cc) & (rr // L == cc // L)).astype(bf16), lfw)
    rw = iw - bw
    pos_in_chunk = lax.broadcasted_iota(jnp.int32, (R, W), 0) % L
    cm = rw
    for sh in (1, 2, 4, 8, 16, 32):
        cm = jnp.maximum(cm, jnp.where(pos_in_chunk >= sh, pltpu.roll(cm, sh, axis=0), NEG))
    return bw, rw, cm


def _mlstm_chunk_local(q, k, kt_bd, v, rw, cm):
    L, W = q.shape
    row = lax.broadcasted_iota(jnp.int32, (L, W), 0)
    pos = lax.broadcasted_iota(jnp.int32, (L, W), 1) % L
    head_eq = (lax.broadcasted_iota(jnp.int32, (W, W), 0) // ML_DH
               == lax.broadcasted_iota(jnp.int32, (W, W), 1) // ML_DH)
    r_row = jnp.sum(jnp.where(pos == row, rw, 0.0), axis=0, keepdims=True)
    v4 = jnp.concatenate([v] * ML_HEADS, axis=0)
    vbd = jnp.where(head_eq, v4, jnp.zeros_like(v4))
    rmax = cm[L - 1:L, :]
    c_loc = _dot((kt_bd * jnp.exp(r_row - rmax)).astype(bf16), vbd)
    n_loc = jnp.sum(k.astype(f32) * jnp.exp(rw - rmax), axis=0, keepdims=True)
    return _dot(q, kt_bd.astype(bf16)), vbd, r_row, c_loc, n_loc


def _mlstm_chunk_step(q, o, bw, cm, qk, vbd, r_row, c_loc, n_loc, c_prev, n_prev, m_prev):
    L, W = q.shape
    row = lax.broadcasted_iota(jnp.int32, (L, W), 0)
    pos = lax.broadcasted_iota(jnp.int32, (L, W), 1) % L
    head_eq = (lax.broadcasted_iota(jnp.int32, (W, W), 0) // ML_DH
               == lax.broadcasted_iota(jnp.int32, (W, W), 1) // ML_DH)
    mw = jnp.maximum(m_prev, cm)
    a_inter = jnp.exp(m_prev - mw)
    pw = qk * jnp.exp(jnp.where(pos <= row, r_row - mw, NEG))
    sums = _dot_exact_rhs01(jnp.concatenate([q.astype(f32) * n_prev, pw], axis=0), head_eq.astype(bf16))
    den = a_inter * sums[:L] + sums[L:]
    hden = jnp.maximum(jnp.abs(den), jnp.exp(-(bw + mw)))
    num = a_inter * _dot(q, c_prev.astype(bf16)) + _dot(pw.astype(bf16), vbd)
    y = num / hden * _sigmoid(o.astype(f32))
    rmax, m_last = cm[L - 1:L, :], mw[L - 1:L, :]
    a_w = jnp.exp(m_prev - m_last)
    b_w = jnp.exp(rmax - m_last)
    return y, c_prev * a_w + c_loc * b_w, a_w * n_prev + b_w * n_loc, bw[L - 1:L, :] + m_last


def _mlstm_body(q_ref, k_ref, kt_ref, v_ref, o_ref, g_ref, eif_ref, sel_ref, y_ref, c_scr, nm_scr):
    @pl.when(pl.program_id(1) == 0)
    def _():
        c_scr[...] = jnp.zeros_like(c_scr)
        nm_scr[...] = jnp.zeros_like(nm_scr)

    c, n, m = c_scr[...], nm_scr[0:1, :], nm_scr[1:2, :]
    kt = kt_ref[...]
    eif = eif_ref[...]
    w = kt.shape[0]
    head_eq = (lax.broadcasted_iota(jnp.int32, (w, w), 0) // ML_DH
               == lax.broadcasted_iota(jnp.int32, (w, w), 1) // ML_DH)
    bw, rw, cm = _mlstm_gate_terms(g_ref[...], eif)
    n_chunks = q_ref.shape[0] // ML_CHUNK
    local = []
    for i in range(n_chunks):
        rows = pl.ds(i * ML_CHUNK, ML_CHUNK)
        lo, hi = i * ML_CHUNK, (i + 1) * ML_CHUNK
        kt_bd = jnp.where(head_eq, _dot(kt, sel_ref[i]), 0.0)
        local.append(_mlstm_chunk_local(q_ref[rows, :], k_ref[rows, :], kt_bd, v_ref[rows, :],
                                        rw[lo:hi], cm[lo:hi]))
    for i in range(n_chunks):
        rows = pl.ds(i * ML_CHUNK, ML_CHUNK)
        lo, hi = i * ML_CHUNK, (i + 1) * ML_CHUNK
        y, c, n, m = _mlstm_chunk_step(q_ref[rows, :], o_ref[rows, :], bw[lo:hi], cm[lo:hi],
                                       *local[i], c, n, m)
        y_ref[rows, :] = y.astype(y_ref.dtype)
    c_scr[...] = c
    nm_scr[0:1, :] = n
    nm_scr[1:2, :] = m


MLSTM_CHUNKS_PER_STEP = 4


def _mlstm(mq, mk, mkt, mv, mo, gcol, batch, seq):
    cps = MLSTM_CHUNKS_PER_STEP
    L = ML_CHUNK * cps
    nc = seq // L
    blk = lambda w: pl.BlockSpec((L, w), lambda b, c: (b * nc + c, 0))
    const = lambda a: pl.BlockSpec(a.shape, lambda b, c: (0,) * a.ndim)
    gl = jnp.arange(LANE)[:, None]
    col = jnp.arange(2 * GROUP_W)[None, :]
    eif = (gl == (col // GROUP_W) * ML_HEADS + (col % GROUP_W) // ML_DH).astype(bf16)
    p = jnp.arange(L)[None, :, None]
    cc = jnp.arange(GROUP_W)[None, None, :]
    sel = (p == jnp.arange(cps)[:, None, None] * ML_CHUNK + cc % ML_CHUNK).astype(bf16)
    return pl.pallas_call(
        _mlstm_body,
        grid=(batch, nc),
        in_specs=[blk(256), blk(256), pl.BlockSpec((GROUP_W, L), lambda b, c: (0, b * nc + c)),
                  blk(256), blk(256), blk(128), const(eif), const(sel)],
        out_specs=blk(GROUP_W),
        out_shape=jax.ShapeDtypeStruct((batch * seq, GROUP_W), bf16),
        scratch_shapes=[pltpu.VMEM((GROUP_W, GROUP_W), f32), pltpu.VMEM((8, GROUP_W), f32)],
        compiler_params=_cparams(("arbitrary", "arbitrary")),
        name="mlstm",
    )(mq, mk, mkt, mv, mo, gcol, eif, sel)


def _route(scores, sel):
    ne, tm = sel.shape
    eid = lax.broadcasted_iota(jnp.int32, (ne, tm), 0)
    grp = eid // E_PER_GROUP
    big = ne + 1

    def top2(vals):
        m1 = jnp.max(vals, axis=0, keepdims=True)
        i1 = jnp.min(jnp.where(vals == m1, eid, big), axis=0, keepdims=True)
        rest = jnp.where(eid == i1, NEG, vals)
        m2 = jnp.max(rest, axis=0, keepdims=True)
        i2 = jnp.min(jnp.where(rest == m2, eid, big), axis=0, keepdims=True)
        return m1, i1, m2, i2

    best_score, best_grp = None, None
    for gi in range(N_GROUPS):
        m1, _, m2, _ = top2(jnp.where(grp == gi, sel, NEG))
        sc = m1 + m2
        if best_score is None:
            best_score, best_grp = sc, jnp.zeros_like(sc, dtype=jnp.int32)
        else:
            better = sc > best_score
            best_grp = jnp.where(better, gi, best_grp)
            best_score = jnp.where(better, sc, best_score)
    _, i1, _, i2 = top2(jnp.where(grp == best_grp, sel, NEG))
    picked = jnp.where((eid == i1) | (eid == i2), scores, 0.0)
    return picked / jnp.sum(picked, axis=0, keepdims=True), best_grp


MOE_ROW_TILE = 256


def _outproj_route_body(ya_ref, yb_ref, yc_ref, yd_ref, x_ref, mod_ref, w_ref, g_ref, b_ref,
                        wr_ref, br_ref, x1_ref, u2e_ref, grp_ref):
    d = x_ref.shape[1]
    mix = None
    for j, y_ref in enumerate((ya_ref, yb_ref, yc_ref, yd_ref)):
        part = _dot(y_ref[...], w_ref[GROUP_W * j:GROUP_W * (j + 1), :])
        mix = part if mix is None else mix + part
    x1 = _layer_norm(DN_ALPHA * x_ref[...] + mod_ref[0, 2:3, :] * mix) * g_ref[...] + b_ref[...]
    x1_ref[...] = x1
    u = _layer_norm(x1) * (1.0 + mod_ref[0, 4:5, :]) + mod_ref[0, 3:4, :]
    u_hi = u.astype(bf16)
    u_lo = (u - u_hi.astype(f32)).astype(bf16)
    ne = br_ref.shape[0]
    part = _dot_nt(wr_ref[...], u_hi)
    logits = part[:ne] + part[ne:] + _dot_nt(wr_ref[:ne, :], u_lo)
    scores = _sigmoid(logits)
    gates_t, best_grp = _route(scores, scores + br_ref[...])
    u2e_ref[:, :d] = u_hi.astype(f32)
    u2e_ref[:, d:] = jnp.zeros((x1.shape[0], LANE), f32)
    u2e_ref[:, d:d + ne] = gates_t.T
    grp_ref[...] = best_grp


def _outproj_route(ya, yb, yc, yd, x2d, mod, w_out, ln_g, ln_b, wr_split, b_router, seq):
    t, d = x2d.shape
    tm = 512
    tps = seq // tm
    row = lambda w: pl.BlockSpec((tm, w), lambda i: (i, 0))
    const = lambda a: pl.BlockSpec(a.shape, lambda i: (0,) * a.ndim)
    return pl.pallas_call(
        _outproj_route_body,
        grid=(t // tm,),
        in_specs=[row(256), row(256), row(256), row(256), row(d),
                  pl.BlockSpec((1, 6, d), lambda i: (i // tps, 0, 0)),
                  const(w_out), const(ln_g), const(ln_b), const(wr_split), const(b_router)],
        out_specs=[row(d), row(d + LANE), pl.BlockSpec((1, tm), lambda i: (0, i))],
        out_shape=[jax.ShapeDtypeStruct((t, d), f32), jax.ShapeDtypeStruct((t, d + LANE), f32),
                   jax.ShapeDtypeStruct((1, t), jnp.int32)],
        compiler_params=_cparams(("arbitrary",)),
        name="outproj_route",
    )(ya, yb, yc, yd, x2d, mod, w_out, ln_g, ln_b, wr_split, b_router)


def _dispatch_plan(grp, tr):
    t = grp.shape[0]
    rows = t + N_GROUPS * tr
    order = jnp.sort(grp * t + jnp.arange(t, dtype=jnp.int32)) % t
    order = jnp.concatenate([order, jnp.zeros((rows - t,), order.dtype)])
    onehot = (grp[:, None] == jnp.arange(N_GROUPS)[None, :]).astype(jnp.int32)
    counts = jnp.sum(onehot, axis=0)
    padded = ((counts + tr - 1) // tr) * tr
    row_end = jnp.cumsum(padded)
    rank = jnp.sum((jnp.cumsum(onehot, axis=0) - 1) * onehot, axis=1)
    row_of_tok = (row_end - padded)[grp] + rank
    r = jnp.arange(rows, dtype=jnp.int32)
    tok_of_row = jnp.zeros((rows,), jnp.int32)
    for g in range(N_GROUPS):
        row_start = row_end[g] - padded[g]
        tok_start = jnp.sum(counts[:g])
        shifted = jnp.roll(order, row_start - tok_start)
        in_grp = (r >= row_start) & (r < row_end[g])
        valid = in_grp & (r < row_start + counts[g])
        tok_of_row = jnp.where(valid, shifted,
                               jnp.where(in_grp, order[jnp.minimum(tok_start, t - 1)], tok_of_row))
    tile_start = jnp.arange(rows // tr, dtype=jnp.int32) * tr
    tile_grp = jnp.minimum(jnp.sum((tile_start[:, None] >= row_end[None, :]).astype(jnp.int32), axis=1),
                           N_GROUPS - 1)
    i32 = lambda a: a.astype(jnp.int32)
    return i32(tok_of_row), i32(row_of_tok), i32(tile_grp), i32(row_end[-1:] // tr)


def _wait_row_gather(n_rows, hbm, vmem, sem):
    pltpu.make_async_copy(hbm.at[pl.ds(0, n_rows), :], vmem, sem).wait()


def _start_row_gather(idx_ref, first, n_rows, hbm, vmem, sem):
    for i in range(n_rows):
        pltpu.make_async_copy(hbm.at[pl.ds(idx_ref[first + i], 1), :], vmem.at[pl.ds(i, 1), :], sem).start()


def _experts_body(tok_ref, tgrp_ref, ntile_ref, u2e_hbm, w1_ref, w3_ref, w2_ref, o_ref,
                  xbuf0, xbuf1, w1b, w3b, w2b, gsem):
    j = pl.program_id(0)
    n_tiles = ntile_ref[0]
    tr, d = o_ref.shape
    xbuf = (xbuf0, xbuf1)

    def compute(s):
        xs = xbuf[s][...]
        xb = xs[:, :d].astype(bf16)
        gslab = xs[:, d:]
        lane = lax.broadcasted_iota(jnp.int32, gslab.shape, 1)
        first_expert = tgrp_ref[j] * E_PER_GROUP
        acc = None
        for e in range(E_PER_GROUP):
            ge = jnp.sum(jnp.where(lane == first_expert + e, gslab, 0.0), axis=1, keepdims=True)
            h1 = _dot(xb, w1b[e])
            h = h1 * _sigmoid(h1) * _dot(xb, w3b[e]) * ge
            part = _dot(h.astype(bf16), w2b[e])
            acc = part if acc is None else acc + part
        o_ref[...] = acc

    @pl.when((j < n_tiles) & ((j == 0) | (tgrp_ref[j] != tgrp_ref[jnp.maximum(j - 1, 0)])))
    def _():
        w1b[...] = w1_ref[0].astype(bf16)
        w3b[...] = w3_ref[0].astype(bf16)
        w2b[...] = w2_ref[0].astype(bf16)

    @pl.when(j == 0)
    def _():
        _start_row_gather(tok_ref, 0, tr, u2e_hbm, xbuf[0], gsem.at[0])

    for s in (0, 1):
        @pl.when((j + 1 < n_tiles) & (j % 2 == s))
        def _(s=s):
            _start_row_gather(tok_ref, (j + 1) * tr, tr, u2e_hbm, xbuf[1 - s], gsem.at[1 - s])

        @pl.when((j < n_tiles) & (j % 2 == s))
        def _(s=s):
            _wait_row_gather(tr, u2e_hbm, xbuf[s], gsem.at[s])
            compute(s)

    @pl.when(j >= n_tiles)
    def _():
        o_ref[...] = jnp.zeros_like(o_ref)


def _experts(tok_of_row, tile_grp, n_tiles, u2e, w1, w3, w2, layer):
    rows = tok_of_row.shape[0]
    tr = MOE_ROW_TILE
    _, ne, d, fe = w1.shape
    grouped = lambda w: w.reshape(w.shape[0], N_GROUPS, E_PER_GROUP, *w.shape[2:])
    wspec = lambda a, b: pl.BlockSpec((None, 1, E_PER_GROUP, a, b), lambda j, tok, tg, nt: (layer, tg[j], 0, 0, 0))
    return pl.pallas_call(
        _experts_body,
        grid_spec=pltpu.PrefetchScalarGridSpec(
            num_scalar_prefetch=3, grid=(rows // tr,),
            in_specs=[pl.BlockSpec(memory_space=pl.ANY), wspec(d, fe), wspec(d, fe), wspec(fe, d)],
            out_specs=pl.BlockSpec((tr, d), lambda j, tok, tg, nt: (j, 0)),
            scratch_shapes=[pltpu.VMEM((tr, u2e.shape[1]), f32), pltpu.VMEM((tr, u2e.shape[1]), f32),
                            pltpu.VMEM((E_PER_GROUP, d, fe), bf16), pltpu.VMEM((E_PER_GROUP, d, fe), bf16),
                            pltpu.VMEM((E_PER_GROUP, fe, d), bf16),
                            pltpu.SemaphoreType.DMA((2,))]),
        out_shape=jax.ShapeDtypeStruct((rows, d), f32),
        compiler_params=_cparams(("arbitrary",)),
        name="experts",
    )(tok_of_row, tile_grp, n_tiles, u2e, grouped(w1), grouped(w3), grouped(w2))


def _moe_combine_body(row_ref, ys_hbm, x_ref, mod_ref, g_ref, b_ref, o_ref, ybuf0, ybuf1, sem):
    i = pl.program_id(0)
    tm = x_ref.shape[0]
    ybuf = (ybuf0, ybuf1)

    @pl.when(i == 0)
    def _():
        _start_row_gather(row_ref, 0, tm, ys_hbm, ybuf[0], sem.at[0])

    for s in (0, 1):
        @pl.when((i + 1 < pl.num_programs(0)) & (i % 2 == s))
        def _(s=s):
            _start_row_gather(row_ref, (i + 1) * tm, tm, ys_hbm, ybuf[1 - s], sem.at[1 - s])

        @pl.when(i % 2 == s)
        def _(s=s):
            _wait_row_gather(tm, ys_hbm, ybuf[s], sem.at[s])
            r = DN_ALPHA * x_ref[...] + mod_ref[0, 5:6, :] * ybuf[s][...]
            o_ref[...] = _layer_norm(r) * g_ref[...] + b_ref[...]


def _moe_combine(row_of_tok, ys, x1, mod, ln_g, ln_b, seq):
    t, d = x1.shape
    tm = 256
    tps = seq // tm
    const = lambda a: pl.BlockSpec(a.shape, lambda i, rows: (0,) * a.ndim)
    return pl.pallas_call(
        _moe_combine_body,
        grid_spec=pltpu.PrefetchScalarGridSpec(
            num_scalar_prefetch=1, grid=(t // tm,),
            in_specs=[pl.BlockSpec(memory_space=pl.ANY),
                      pl.BlockSpec((tm, d), lambda i, rows: (i, 0)),
                      pl.BlockSpec((1, 6, d), lambda i, rows: (i // tps, 0, 0)),
                      const(ln_g), const(ln_b)],
            out_specs=pl.BlockSpec((tm, d), lambda i, rows: (i, 0)),
            scratch_shapes=[pltpu.VMEM((tm, d), f32), pltpu.VMEM((tm, d), f32),
                            pltpu.SemaphoreType.DMA((2,))]),
        out_shape=jax.ShapeDtypeStruct((t, d), f32),
        compiler_params=_cparams(("arbitrary",)),
        name="moe_combine",
    )(row_of_tok, ys, x1, mod, ln_g, ln_b)


def _pad_cols(a, width):
    return jnp.pad(a, ((0, 0), (0, width - a.shape[1])))


def _pack_w_in(w):
    parts, acc = [], 0
    for sz in IN_SIZES:
        parts.append(w[:, acc:acc + sz])
        acc += sz
    (a_q, a_kv, a_kr, m_q, m_k, m_v, m_i, m_f, m_o, c_b, c_c, c_h, d_q, d_k, d_v) = parts
    gates = _pad_cols(jnp.concatenate([m_i, m_f], axis=1), LANE)
    packed = jnp.concatenate([
        _pad_cols(a_q, 256), a_kv, _pad_cols(a_kr, LANE),
        m_q, m_k, m_v, gates, m_o, c_b, c_c, c_h, d_q, d_k, d_v], axis=1)
    return packed.astype(bf16)


def _pack_mla(w_uq, w_ukv):
    half = MLA_ROPE // 2
    wq3 = w_uq.reshape(MLA_Q_LORA, MLA_HEADS, MLA_NOPE + MLA_ROPE).transpose(1, 0, 2)
    wkv3 = w_ukv.reshape(MLA_KV_LORA, MLA_HEADS, MLA_NOPE + MLA_V).transpose(1, 0, 2)
    nope, pe = wq3[:, :, :MLA_NOPE], wq3[:, :, MLA_NOPE:]
    rot = jnp.concatenate([-pe[:, :, half:], pe[:, :, :half]], axis=2)
    pad = lambda a: jnp.pad(a, ((0, 0), (0, 256 - a.shape[1]), (0, LANE - a.shape[2]))).astype(bf16)
    wq = pad(wq3)
    wqr = pad(jnp.concatenate([jnp.zeros_like(nope), rot], axis=2))
    wk = jnp.pad(wkv3[:, :, :MLA_NOPE], ((0, 0), (0, 0), (0, LANE - MLA_NOPE))).astype(bf16)
    return wq, wqr, wk, wkv3[:, :, MLA_NOPE:].astype(bf16)


def _rope_tables(seq):
    half = DF_DK // 2
    inv = 1.0 / (ROPE_THETA ** (jnp.arange(0, DF_DK, 2, dtype=f32) / DF_DK))
    ang = jnp.arange(seq, dtype=f32)[:, None] * inv[None, :]
    cos, sin = jnp.cos(ang), jnp.sin(ang)
    cos_t = jnp.tile(jnp.concatenate([cos, cos], axis=1), (1, 256 // DF_DK))
    sin_t = jnp.tile(jnp.concatenate([-sin, sin], axis=1), (1, 256 // DF_DK))
    scale = (MLA_NOPE + MLA_ROPE) ** -0.5 * LOG2E
    ones = jnp.ones((seq, MLA_NOPE), f32)
    zeros = jnp.zeros((seq, MLA_NOPE), f32)
    cq = _pad_cols(jnp.concatenate([ones, cos, cos], axis=1) * scale, LANE)
    sq = _pad_cols(jnp.concatenate([zeros, sin, sin], axis=1) * scale, LANE)
    return cos_t, sin_t, cq, sq


def kernel(x, c, w_ada, b_ada, w_in, mla_g_q, mla_g_kv, mla_w_uq, mla_w_ukv, ml_b_i, ml_b_f, sc_w,
           df_lq1, df_lk1, df_lq2, df_lk2, df_g, w_out, ln1_g, ln1_b, w_router, b_router, w1, w3, w2,
           ln2_g, ln2_b):
    batch, seq, d = x.shape
    depth = w_in.shape[0]
    assert MLA_ROPE == DF_DK, "both rotary blocks share one table"
    cos_t, sin_t, cq, sq = _rope_tables(seq)
    place = jnp.zeros((LANE, LANE), f32).at[jnp.arange(MLA_ROPE), MLA_NOPE + jnp.arange(MLA_ROPE)].set(1.0).astype(bf16)
    wr_hi = w_router.astype(bf16)
    wr_split = jnp.concatenate([wr_hi, (w_router - wr_hi.astype(f32)).astype(bf16)], axis=1).T
    mod_all = _ada_mod(c, w_ada, b_ada).reshape(depth, batch, 6, d)
    xf = x.reshape(batch * seq, d)
    for l in range(depth):
        mod = mod_all[l]
        w_packed = _pack_w_in(w_in[l])
        gq = _pad_cols(mla_g_q[l][None, :], 256)
        gkv = mla_g_kv[l][None, :]
        gbias = _pad_cols(jnp.concatenate([ml_b_i[l], ml_b_f[l]])[None, :], LANE)
        wkt = w_packed[:, OFF_MK:OFF_MK + GROUP_W].T
        (qn, kvn, kpe, mq, mk, mv, mo, gcol, yc, dq, dk, dv, mkt) = _inproj(
            xf, mod, w_packed, wkt, cos_t, sin_t, gq, gkv, gbias, sc_w[l], seq)
        wq, wqr, wk, wv = _pack_mla(mla_w_uq[l], mla_w_ukv[l])
        ya = _mla_attention(qn, kvn, kpe, wq, wqr, wk, wv, place, cq, sq, batch, seq)
        yb = _mlstm(mq, mk, mkt, mv, mo, gcol, batch, seq)
        lambda_init = 0.8 - 0.6 * math.exp(-0.3 * l)
        yd = _diff_attention(dq, dk, dv, df_lq1[l][None, :], df_lk1[l][None, :], df_lq2[l][None, :],
                             df_lk2[l][None, :], _pad_cols(df_g[l][None, :], LANE),
                             batch, seq, lambda_init)
        x1, u2e, grp = _outproj_route(ya, yb, yc, yd, xf, mod, w_out[l].astype(bf16), ln1_g[l][None, :],
                                      ln1_b[l][None, :], wr_split, b_router[:, None], seq)
        tok_of_row, row_of_tok, tile_grp, n_tiles = _dispatch_plan(grp[0], MOE_ROW_TILE)
        ys = _experts(tok_of_row, tile_grp, n_tiles, u2e, w1, w3, w2, l)
        xf = _moe_combine(row_of_tok, ys, x1, mod, ln2_g[l][None, :], ln2_b[l][None, :], seq)
    return xf.reshape(batch, seq, d)
```

```python
import functools
import math

import jax
import jax.numpy as jnp
from jax import lax
from jax.experimental import pallas as pl
from jax.experimental.pallas import tpu as pltpu

f32 = jnp.float32
bf16 = jnp.bfloat16
HIGHEST = lax.Precision.HIGHEST

GROUP_W = 256
MLA_HEADS, MLA_NOPE, MLA_ROPE, MLA_V = 4, 64, 32, 64
MLA_Q_LORA, MLA_KV_LORA = 192, 128
ML_HEADS, ML_DH, ML_CHUNK = 4, 64, 64
DF_HEADS, DF_DK, DF_V = 4, 32, 64
ROPE_THETA = 10000.0
N_EXPERTS, N_GROUPS, E_PER_GROUP, D_EXPERT = 16, 4, 4, 256
DEPTH = 2
DN_ALPHA = (2 * DEPTH) ** 0.25
LN_EPS = 1e-5
RMS_EPS = 1e-6
IN_SIZES = (192, 128, 32, 256, 256, 256, 4, 4, 256, 256, 256, 256, 256, 256, 256)

OFF_PQ, OFF_PKV, OFF_KR = 0, 256, 384
OFF_MQ, OFF_MK, OFF_MV, OFF_MG, OFF_MO = 512, 768, 1024, 1280, 1408
OFF_CB, OFF_CC, OFF_CH = 1664, 1920, 2176
OFF_DQ, OFF_DK, OFF_DV = 2432, 2688, 2944
IN_PACKED = 3200

LANE = 128
VMEM_LIMIT = 48 * 1024 * 1024
NEG = -1e30
LOG2E = math.log2(math.e)


def _cparams(sem):
    return pltpu.CompilerParams(dimension_semantics=sem, vmem_limit_bytes=VMEM_LIMIT)


def _sigmoid(x):
    return 1.0 / (1.0 + jnp.exp(-x))


def _layer_norm(x):
    mu = jnp.mean(x, axis=-1, keepdims=True)
    xc = x - mu
    var = jnp.mean(xc * xc, axis=-1, keepdims=True)
    return xc * lax.rsqrt(var + LN_EPS)


def _dot(a, b, **kw):
    return jnp.dot(a, b, preferred_element_type=f32, **kw)


def _dot_nt(a, b, **kw):
    return lax.dot_general(a, b, (((1,), (1,)), ((), ())), preferred_element_type=f32, **kw)


def _dot_tn(a, b, **kw):
    return lax.dot_general(a, b, (((0,), (0,)), ((), ())), preferred_element_type=f32, **kw)


def _expand_groups(cols, rows, width, group):
    lane = lax.broadcasted_iota(jnp.int32, (rows, width), 1)
    out = jnp.broadcast_to(cols[-1], (rows, width))
    for h in range(len(cols) - 2, -1, -1):
        out = jnp.where(lane < group * (h + 1), cols[h], out)
    return out


def _lane_group_mask(shape, lo, hi):
    lane = lax.broadcasted_iota(jnp.int32, shape, 1)
    return (lane >= lo) & (lane < hi)


def _ada_body(c_ref, w_ref, b_ref, o_ref):
    c = c_ref[...]
    ca = (c * _sigmoid(c)).astype(bf16)
    o_ref[0] = _dot(ca, w_ref[0].astype(bf16)) + b_ref[0]


def _ada_mod(c, w_ada, b_ada):
    depth, d, n = w_ada.shape
    b = c.shape[0]
    tn = 1536
    return pl.pallas_call(
        _ada_body,
        grid=(depth, n // tn),
        in_specs=[
            pl.BlockSpec((b, d), lambda l, j: (0, 0)),
            pl.BlockSpec((1, d, tn), lambda l, j: (l, 0, j)),
            pl.BlockSpec((1, 1, tn), lambda l, j: (l, 0, j)),
        ],
        out_specs=pl.BlockSpec((1, b, tn), lambda l, j: (l, 0, j)),
        out_shape=jax.ShapeDtypeStruct((depth, b, n), f32),
        compiler_params=_cparams(("arbitrary", "arbitrary")),
        name="ada_mod",
    )(c, w_ada, b_ada.reshape(depth, 1, n))


def _rope_lanes(x, cos, sin_signed):
    w = x.shape[1]
    lane = lax.broadcasted_iota(jnp.int32, x.shape, 1)
    rot = jnp.where(lane % 32 < 16, pltpu.roll(x, w - 16, axis=1), pltpu.roll(x, 16, axis=1))
    return x * cos + rot * sin_signed


def _inproj_body(x_ref, mod_ref, w_ref, wkt_ref, cos_ref, sin_ref, gq_ref, gkv_ref, gb_ref, cw_ref,
                 qn_ref, kvn_ref, kpe_ref, mq_ref, mk_ref, mv_ref, mo_ref, gcol_ref, yc_ref,
                 dq_ref, dk_ref, dv_ref, mkt_ref, carry_ref, *, tiles_per_seq):
    i = pl.program_id(0)
    tm = x_ref.shape[0]
    u = _layer_norm(x_ref[...]) * (1.0 + mod_ref[0, 1:2, :]) + mod_ref[0, 0:1, :]
    ub = u.astype(bf16)

    def seg(off, n):
        return _dot(ub, w_ref[:, off:off + n])

    cos = cos_ref[...]
    sin = sin_ref[...]

    pq = seg(OFF_PQ, 256)
    ms = jnp.sum(pq * pq, axis=-1, keepdims=True) * (1.0 / MLA_Q_LORA)
    qn_ref[...] = (pq * lax.rsqrt(ms + RMS_EPS) * gq_ref[...]).astype(bf16)
    pkv = seg(OFF_PKV, 128)
    ms = jnp.sum(pkv * pkv, axis=-1, keepdims=True) * (1.0 / MLA_KV_LORA)
    kvn_ref[...] = (pkv * lax.rsqrt(ms + RMS_EPS) * gkv_ref[...]).astype(bf16)
    kpe_ref[...] = _rope_lanes(seg(OFF_KR, 128), cos[:, :LANE], sin[:, :LANE]).astype(bf16)

    mq_ref[...] = (seg(OFF_MQ, 256) * (ML_DH ** -0.5)).astype(bf16)
    mk_ref[...] = seg(OFF_MK, 256).astype(bf16)
    mkt_ref[...] = _dot_nt(wkt_ref[...], ub).astype(bf16)
    mv_ref[...] = seg(OFF_MV, 256).astype(bf16)
    gcol_ref[...] = seg(OFF_MG, 128) + gb_ref[...]
    mo_ref[...] = seg(OFF_MO, 256).astype(bf16)

    uc = seg(OFF_CC, 256) * seg(OFF_CH, 256)

    @pl.when(i % tiles_per_seq == 0)
    def _():
        carry_ref[...] = jnp.zeros_like(carry_ref)

    prev = carry_ref[...]
    row =lax.broadcasted_iota(jnp.int32, uc.shape, 0)
    u1 = jnp.where(row == 0, prev[7:8, :], pltpu.roll(uc, 1, axis=0))
    u2 = jnp.where(row == 0, prev[6:7, :], jnp.where(row == 1, prev[7:8, :], pltpu.roll(uc, 2, axis=0)))
    carry_ref[...] = uc[tm - 8:, :]
    conv = cw_ref[0:1, :] * u2 + cw_ref[1:2, :] * u1 + cw_ref[2:3, :] * uc
    yc_ref[...] = (seg(OFF_CB, 256) * conv).astype(bf16)

    dq_ref[...] = (_rope_lanes(seg(OFF_DQ, 256), cos, sin) * (DF_DK ** -0.5 * LOG2E)).astype(bf16)
    dk_ref[...] = _rope_lanes(seg(OFF_DK, 256), cos, sin).astype(bf16)
    dv_ref[...] = seg(OFF_DV, 256).astype(bf16)


def _inproj(x2d, mod, w_packed, wkt, cos_t, sin_t, gq, gkv, gbias, conv_w, seq):
    t, d = x2d.shape
    tm = 1024
    tps = seq // tm
    row = lambda w: pl.BlockSpec((tm, w), lambda i: (i, 0))
    const = lambda a: pl.BlockSpec(a.shape, lambda i: (0,) * a.ndim)
    widths = (256, 128, 128, 256, 256, 256, 256, 128, 256, 256, 256, 256)
    dtypes = (bf16, bf16, bf16, bf16, bf16, bf16, bf16, f32, bf16, bf16, bf16, bf16)
    return pl.pallas_call(
        functools.partial(_inproj_body, tiles_per_seq=tps),
        grid=(t // tm,),
        in_specs=[
            row(d),
            pl.BlockSpec((1, 6, d), lambda i: (i // tps, 0, 0)),
            const(w_packed), const(wkt),
            pl.BlockSpec((tm, 256), lambda i: (i % tps, 0)),
            pl.BlockSpec((tm, 256), lambda i: (i % tps, 0)),
            const(gq), const(gkv), const(gbias), const(conv_w),
        ],
        out_specs=[row(w) for w in widths] + [pl.BlockSpec((GROUP_W, tm), lambda i: (0, i))],
        out_shape=[jax.ShapeDtypeStruct((t, w), dt) for w, dt in zip(widths, dtypes)]
        + [jax.ShapeDtypeStruct((GROUP_W, t), bf16)],
        scratch_shapes=[pltpu.VMEM((8, 256), f32)],
        compiler_params=_cparams(("arbitrary",)),
        name="inproj",
    )(x2d, mod, w_packed, wkt, cos_t, sin_t, gq, gkv, gbias, conv_w)


def _causal_softmax_heads(q_heads, k_at, v_at, qi, s_scr, m_scr, acc_scr):
    n_heads = len(q_heads)
    tq = q_heads[0].shape[0]
    half = tq // 2
    rows = pl.ds(pl.multiple_of(qi * tq, tq), tq)

    def scores(g, j):
        return _dot_nt(q_heads[g], k_at(g, pl.multiple_of(j * tq, tq)))

    def fold_max(g, s):
        m_scr[g] = jnp.maximum(m_scr[g], jnp.maximum(s[:, :half], s[:, half:]))

    m_scr[...] = jnp.full(m_scr.shape, NEG, f32)
    for g in range(n_heads):
        acc_scr[g, rows, :] = jnp.zeros((tq, LANE), f32)

    def score_blocks(js):
        for g in range(n_heads):
            mx = None
            for j in js:
                s = scores(g, j)
                s_scr[g, j] = s
                s = jnp.maximum(s[:, :half], s[:, half:])
                mx = s if mx is None else jnp.maximum(mx, s)
            m_scr[g] = jnp.maximum(m_scr[g], mx)

    def prob_blocks(js):
        for g in range(n_heads):
            mb = m_scr[g]
            pv = None
            for j in js:
                sj = s_scr[g, j]
                p = jnp.concatenate([jnp.exp2(sj[:, :half] - mb), jnp.exp2(sj[:, half:] - mb)],
                                    axis=1).astype(bf16)
                d = _dot(p, v_at(g, pl.multiple_of(j * tq, tq)))
                pv = d if pv is None else pv + d
            acc_scr[g, rows, :] += pv

    def over_blocks(n, blocks_fn):
        @pl.loop(0, n // 4)
        def _(jj):
            blocks_fn(tuple(4 * jj + u for u in range(4)))

        rest = (n // 4) * 4

        @pl.when(n % 4 >= 2)
        def _():
            blocks_fn((rest, rest + 1))

        @pl.when(n % 2 == 1)
        def _():
            blocks_fn((n - 1,))

    over_blocks(qi, score_blocks)
    rowi = lax.broadcasted_iota(jnp.int32, (tq, tq), 0)
    coli = lax.broadcasted_iota(jnp.int32, (tq, tq), 1)
    for g in range(n_heads):
        s = jnp.where(coli <= rowi, scores(g, qi), NEG)
        s_scr[g, qi] = s
        fold_max(g, s)
        m_scr[g] = jnp.broadcast_to(jnp.max(m_scr[g], axis=1, keepdims=True), (tq, half))

    over_blocks(qi + 1, prob_blocks)


def _softmax_scratch(n_heads, nq, tq):
    return [pltpu.VMEM((n_heads, nq, tq, tq), f32), pltpu.VMEM((n_heads, tq, tq // 2), f32),
            pltpu.VMEM((n_heads, nq * tq, LANE), f32)]


EPILOGUE_ROWS = 512


def _with_ones_lanes(v, dv):
    return jnp.concatenate([v.astype(bf16), jnp.ones((v.shape[0], LANE - dv), bf16)], axis=1)


def _softmax_normalise(acc, dv):
    lane = lax.broadcasted_iota(jnp.int32, acc.shape, 1)
    return jnp.where(lane < dv, acc / pltpu.roll(acc, LANE - dv, axis=1), 0.0)


def _pair_lanes(a, b, dv):
    lane = lax.broadcasted_iota(jnp.int32, a.shape, 1)
    return jnp.where(lane < dv, a, pltpu.roll(b, dv, axis=1))


def _mla_body(qn_ref, kvn_ref, kpe_ref, wq_ref, wqr_ref, wk_ref, wv_ref, place_ref, cq_ref, sq_ref,
              o_ref, q_scr, k_scr, v_scr, s_scr, m_scr, acc_scr):
    qi = pl.program_id(1)
    tq = s_scr.shape[2]

    @pl.when(qi == 0)
    def _():
        kvn = kvn_ref[...]
        kpe_placed = _dot(kpe_ref[...], place_ref[...])
        for h in range(MLA_HEADS):
            k_scr[h] = (_dot(kvn, wk_ref[h]) + kpe_placed).astype(bf16)
            v_scr[h] = _with_ones_lanes(_dot(kvn, wv_ref[h]), MLA_V)
            for r0 in range(0, qn_ref.shape[0], EPILOGUE_ROWS):
                rows = pl.ds(r0, EPILOGUE_ROWS)
                qn = qn_ref[rows, :]
                q_scr[h, rows, :] = (_dot(qn, wq_ref[h]) * cq_ref[rows, :]
                                     + _dot(qn, wqr_ref[h]) * sq_ref[rows, :]).astype(bf16)

    q_rows = pl.ds(pl.multiple_of(qi * tq, tq), tq)
    q_heads = [q_scr[h, q_rows, :] for h in range(MLA_HEADS)]
    _causal_softmax_heads(
        q_heads, lambda g, start: k_scr[g, pl.ds(start, tq), :],
        lambda g, start: v_scr[g, pl.ds(start, tq), :], qi, s_scr, m_scr, acc_scr)

    @pl.when(qi == pl.num_programs(1) - 1)
    def _():
        for r0 in range(0, o_ref.shape[0], EPILOGUE_ROWS):
            rows = pl.ds(r0, EPILOGUE_ROWS)
            o = [_softmax_normalise(acc_scr[h, rows, :], MLA_V) for h in range(MLA_HEADS)]
            o_ref[rows, :] = jnp.concatenate(
                [_pair_lanes(o[0], o[1], MLA_V), _pair_lanes(o[2], o[3], MLA_V)], axis=1).astype(bf16)


def _mla_attention(qn, kvn, kpe, wq, wqr, wk, wv, place, cq, sq, batch, seq):
    tq = 256
    nq = seq // tq
    const = lambda a: pl.BlockSpec(a.shape, lambda b, i: (0,) * a.ndim)
    return pl.pallas_call(
        _mla_body,
        grid=(batch, nq),
        in_specs=[
            pl.BlockSpec((seq, 256), lambda b, i: (b, 0)),
            pl.BlockSpec((seq, 128), lambda b, i: (b, 0)),
            pl.BlockSpec((seq, 128), lambda b, i: (b, 0)),
            const(wq), const(wqr), const(wk), const(wv), const(place), const(cq), const(sq),
        ],
        out_specs=pl.BlockSpec((seq, GROUP_W), lambda b, i: (b, 0)),
        out_shape=jax.ShapeDtypeStruct((batch * seq, GROUP_W), bf16),
        scratch_shapes=[pltpu.VMEM((MLA_HEADS, seq, 128), bf16),
                        pltpu.VMEM((MLA_HEADS, seq, 128), bf16),
                        pltpu.VMEM((MLA_HEADS, seq, LANE), bf16)]
        + _softmax_scratch(MLA_HEADS, nq, tq),
        compiler_params=_cparams(("arbitrary", "arbitrary")),
        name="mla_attn",
    )(qn, kvn, kpe, wq, wqr, wk, wv, place, cq, sq)


def _diff_body(q_ref, k_ref, v_ref, lq1_ref, lk1_ref, lq2_ref, lk2_ref, g_ref,
               o_ref, v_scr, s_scr, m_scr, acc_scr, *, lambda_init):
    qi = pl.program_id(1)
    tq = q_ref.shape[0]

    @pl.when(qi == 0)
    def _():
        v = v_ref[...]
        for h in range(DF_HEADS):
            v_scr[h] = _with_ones_lanes(v[:, DF_V * h:DF_V * (h + 1)], DF_V)

    q = q_ref[...]
    q_heads = [jnp.where(_lane_group_mask(q.shape, DF_DK * g, DF_DK * (g + 1)), q, jnp.zeros_like(q))
               for g in range(2 * DF_HEADS)]
    _causal_softmax_heads(
        q_heads, lambda g, start: k_ref[pl.ds(start, tq), :],
        lambda g, start: v_scr[g // 2, pl.ds(start, tq), :], qi, s_scr, m_scr, acc_scr)

    @pl.when(qi == pl.num_programs(1) - 1)
    def _():
        lam = (jnp.exp(jnp.sum(lq1_ref[...] * lk1_ref[...], axis=1, keepdims=True))
               - jnp.exp(jnp.sum(lq2_ref[...] * lk2_ref[...], axis=1, keepdims=True)) + lambda_init)
        gain = g_ref[...] * (1.0 - lambda_init)
        for r0 in range(0, o_ref.shape[0], EPILOGUE_ROWS):
            rows = pl.ds(r0, EPILOGUE_ROWS)
            ys = []
            for h in range(DF_HEADS):
                o = (_softmax_normalise(acc_scr[2 * h, rows, :], DF_V)
                     - lam * _softmax_normalise(acc_scr[2 * h + 1, rows, :], DF_V))
                ms = jnp.sum(o * o, axis=1, keepdims=True) * (1.0 / DF_V)
                ys.append(o * lax.rsqrt(ms + RMS_EPS) * gain)
            o_ref[rows, :] = jnp.concatenate(
                [_pair_lanes(ys[0], ys[1], DF_V), _pair_lanes(ys[2], ys[3], DF_V)], axis=1).astype(bf16)


def _diff_attention(dq, dk, dv, lq1, lk1, lq2, lk2, g_sub, batch, seq, lambda_init):
    tq = 256
    nq = seq // tq
    const = lambda a: pl.BlockSpec(a.shape, lambda b, i: (0,) * a.ndim)
    return pl.pallas_call(
        functools.partial(_diff_body, lambda_init=lambda_init),
        grid=(batch, nq),
        in_specs=[
            pl.BlockSpec((tq, 256), lambda b, i: (b * nq + i, 0)),
            pl.BlockSpec((seq, 256), lambda b, i: (b, 0)),
            pl.BlockSpec((seq, 256), lambda b, i: (b, 0)),
            const(lq1), const(lk1), const(lq2), const(lk2), const(g_sub),
        ],
        out_specs=pl.BlockSpec((seq, GROUP_W), lambda b, i: (b, 0)),
        out_shape=jax.ShapeDtypeStruct((batch * seq, GROUP_W), bf16),
        scratch_shapes=[pltpu.VMEM((DF_HEADS, seq, LANE), bf16)]
        + _softmax_scratch(2 * DF_HEADS, nq, tq),
        compiler_params=_cparams(("arbitrary", "arbitrary")),
        name="diff_attn",
    )(dq, dk, dv, lq1, lk1, lq2, lk2, g_sub)


def _split3(a):
    hi = a.astype(bf16)
    r1 = a - hi.astype(f32)
    mid = r1.astype(bf16)
    return hi, mid, (r1 - mid.astype(f32)).astype(bf16)


def _dot_exact_rhs01(a, b01):
    hi, mid, lo = _split3(a)
    return _dot(hi, b01) + _dot(mid, b01) + _dot(lo, b01)


def _dot_exact_lhs01(a01, b):
    hi, mid, lo = _split3(b)
    return _dot(a01, hi) + _dot(a01, mid) + _dot(a01, lo)


def _mlstm_gate_terms(g, eif):
    R, W, L = g.shape[0], eif.shape[1] // 2, ML_CHUNK
    lf = jnp.minimum(g, 0.0) - jnp.log(1.0 + jnp.exp(-jnp.abs(g)))
    glane = lax.broadcasted_iota(jnp.int32, g.shape, 1)
    wide = _dot_exact_rhs01(jnp.where(glane < ML_HEADS, g, lf), eif)
    iw, lfw = wide[:, :W], wide[:, W:]
    rr = lax.broadcasted_iota(jnp.int32, (R, R), 0)
    cc = lax.broadcasted_iota(jnp.int32, (R, R), 1)
    bw = _dot_exact_lhs01(((rr >= cc) & (rr // L == cc // L)).astype(bf16), lfw)
    rw = iw - bw
    pos_in_chunk = lax.broadcasted_iota(jnp.int32, (R, W), 0) % L
    cm = rw
    for sh in (1, 2, 4, 8, 16, 32):
        cm = jnp.maximum(cm, jnp.where(pos_in_chunk >= sh, pltpu.roll(cm, sh, axis=0), NEG))
    return bw, rw, cm


def _mlstm_chunk_local(q, k, kt_bd, v, rw, cm):
    L, W = q.shape
    row = lax.broadcasted_iota(jnp.int32, (L, W), 0)
    pos = lax.broadcasted_iota(jnp.int32, (L, W), 1) % L
    head_eq = (lax.broadcasted_iota(jnp.int32, (W, W), 0) // ML_DH
               == lax.broadcasted_iota(jnp.int32, (W, W), 1) // ML_DH)
    r_row = jnp.sum(jnp.where(pos == row, rw, 0.0), axis=0, keepdims=True)
    v4 = jnp.concatenate([v] * ML_HEADS, axis=0)
    vbd = jnp.where(head_eq, v4, jnp.zeros_like(v4))
    rmax = cm[L - 1:L, :]
    c_loc = _dot((kt_bd * jnp.exp(r_row - rmax)).astype(bf16), vbd)
    n_loc = jnp.sum(k.astype(f32) * jnp.exp(rw - rmax), axis=0, keepdims=True)
    return _dot(q, kt_bd.astype(bf16)), vbd, r_row, c_loc, n_loc


def _mlstm_chunk_step(q, o, bw, cm, qk, vbd, r_row, c_loc, n_loc, c_prev, n_prev, m_prev):
    L, W = q.shape
    row = lax.broadcasted_iota(jnp.int32, (L, W), 0)
    pos = lax.broadcasted_iota(jnp.int32, (L, W), 1) % L
    head_eq = (lax.broadcasted_iota(jnp.int32, (W, W), 0) // ML_DH
               == lax.broadcasted_iota(jnp.int32, (W, W), 1) // ML_DH)
    mw = jnp.maximum(m_prev, cm)
    a_inter = jnp.exp(m_prev - mw)
    pw = qk * jnp.exp(jnp.where(pos <= row, r_row - mw, NEG))
    sums = _dot_exact_rhs01(jnp.concatenate([q.astype(f32) * n_prev, pw], axis=0), head_eq.astype(bf16))
    den = a_inter * sums[:L] + sums[L:]
    hden = jnp.maximum(jnp.abs(den), jnp.exp(-(bw + mw)))
    num = a_inter * _dot(q, c_prev.astype(bf16)) + _dot(pw.astype(bf16), vbd)
    y = num / hden * _sigmoid(o.astype(f32))
    rmax, m_last = cm[L - 1:L, :], mw[L - 1:L, :]
    a_w = jnp.exp(m_prev - m_last)
    b_w = jnp.exp(rmax - m_last)
    return y, c_prev * a_w + c_loc * b_w, a_w * n_prev + b_w * n_loc, bw[L - 1:L, :] + m_last


def _mlstm_body(q_ref, k_ref, kt_ref, v_ref, o_ref, g_ref, eif_ref, sel_ref, y_ref, c_scr, nm_scr):
    @pl.when(pl.program_id(1) == 0)
    def _():
        c_scr[...] = jnp.zeros_like(c_scr)
        nm_scr[...] = jnp.zeros_like(nm_scr)

    c, n, m = c_scr[...], nm_scr[0:1, :], nm_scr[1:2, :]
    kt = kt_ref[...]
    eif = eif_ref[...]
    w = kt.shape[0]
    head_eq = (lax.broadcasted_iota(jnp.int32, (w, w), 0) // ML_DH
               == lax.broadcasted_iota(jnp.int32, (w, w), 1) // ML_DH)
    bw, rw, cm = _mlstm_gate_terms(g_ref[...], eif)
    n_chunks = q_ref.shape[0] // ML_CHUNK
    local = []
    for i in range(n_chunks):
        rows = pl.ds(i * ML_CHUNK, ML_CHUNK)
        lo, hi = i * ML_CHUNK, (i + 1) * ML_CHUNK
        kt_bd = jnp.where(head_eq, _dot(kt, sel_ref[i]), 0.0)
        local.append(_mlstm_chunk_local(q_ref[rows, :], k_ref[rows, :], kt_bd, v_ref[rows, :],
                                        rw[lo:hi], cm[lo:hi]))
    for i in range(n_chunks):
        rows = pl.ds(i * ML_CHUNK, ML_CHUNK)
        lo, hi = i * ML_CHUNK, (i + 1) * ML_CHUNK
        y, c, n, m = _mlstm_chunk_step(q_ref[rows, :], o_ref[rows, :], bw[lo:hi], cm[lo:hi],
                                       *local[i], c, n, m)
        y_ref[rows, :] = y.astype(y_ref.dtype)
    c_scr[...] = c
    nm_scr[0:1, :] = n
    nm_scr[1:2, :] = m


MLSTM_CHUNKS_PER_STEP = 4


def _mlstm(mq, mk, mkt, mv, mo, gcol, batch, seq):
    cps = MLSTM_CHUNKS_PER_STEP
    L = ML_CHUNK * cps
    nc = seq // L
    blk = lambda w: pl.BlockSpec((L, w), lambda b, c: (b * nc + c, 0))
    const = lambda a: pl.BlockSpec(a.shape, lambda b, c: (0,) * a.ndim)
    gl = jnp.arange(LANE)[:, None]
    col = jnp.arange(2 * GROUP_W)[None, :]
    eif = (gl == (col // GROUP_W) * ML_HEADS + (col % GROUP_W) // ML_DH).astype(bf16)
    p = jnp.arange(L)[None, :, None]
    cc = jnp.arange(GROUP_W)[None, None, :]
    sel = (p == jnp.arange(cps)[:, None, None] * ML_CHUNK + cc % ML_CHUNK).astype(bf16)
    return pl.pallas_call(
        _mlstm_body,
        grid=(batch, nc),
        in_specs=[blk(256), blk(256), pl.BlockSpec((GROUP_W, L), lambda b, c: (0, b * nc + c)),
                  blk(256), blk(256), blk(128), const(eif), const(sel)],
        out_specs=blk(GROUP_W),
        out_shape=jax.ShapeDtypeStruct((batch * seq, GROUP_W), bf16),
        scratch_shapes=[pltpu.VMEM((GROUP_W, GROUP_W), f32), pltpu.VMEM((8, GROUP_W), f32)],
        compiler_params=_cparams(("arbitrary", "arbitrary")),
        name="mlstm",
    )(mq, mk, mkt, mv, mo, gcol, eif, sel)


def _route(scores, sel):
    ne, tm = sel.shape
    eid = lax.broadcasted_iota(jnp.int32, (ne, tm), 0)
    grp = eid // E_PER_GROUP
    big = ne + 1

    def top2(vals):
        m1 = jnp.max(vals, axis=0, keepdims=True)
        i1 = jnp.min(jnp.where(vals == m1, eid, big), axis=0, keepdims=True)
        rest = jnp.where(eid == i1, NEG, vals)
        m2 = jnp.max(rest, axis=0, keepdims=True)
        i2 = jnp.min(jnp.where(rest == m2, eid, big), axis=0, keepdims=True)
        return m1, i1, m2, i2

    best_score, best_grp = None, None
    for gi in range(N_GROUPS):
        m1, _, m2, _ = top2(jnp.where(grp == gi, sel, NEG))
        sc = m1 + m2
        if best_score is None:
            best_score, best_grp = sc, jnp.zeros_like(sc, dtype=jnp.int32)
        else:
            better = sc > best_score
            best_grp = jnp.where(better, gi, best_grp)
            best_score = jnp.where(better, sc, best_score)
    _, i1, _, i2 = top2(jnp.where(grp == best_grp, sel, NEG))
    picked = jnp.where((eid == i1) | (eid == i2), scores, 0.0)
    return picked / jnp.sum(picked, axis=0, keepdims=True), best_grp


MOE_ROW_TILE = 256


def _outproj_route_body(ya_ref, yb_ref, yc_ref, yd_ref, x_ref, mod_ref, w_ref, g_ref, b_ref,
                        wr_ref, br_ref, x1_ref, u2e_ref, grp_ref):
    d = x_ref.shape[1]
    mix = None
    for j, y_ref in enumerate((ya_ref, yb_ref, yc_ref, yd_ref)):
        part = _dot(y_ref[...], w_ref[GROUP_W * j:GROUP_W * (j + 1), :])
        mix = part if mix is None else mix + part
    x1 = _layer_norm(DN_ALPHA * x_ref[...] + mod_ref[0, 2:3, :] * mix) * g_ref[...] + b_ref[...]
    x1_ref[...] = x1
    u = _layer_norm(x1) * (1.0 + mod_ref[0, 4:5, :]) + mod_ref[0, 3:4, :]
    u_hi = u.astype(bf16)
    u_lo = (u - u_hi.astype(f32)).astype(bf16)
    ne = br_ref.shape[0]
    part = _dot_nt(wr_ref[...], u_hi)
    logits = part[:ne] + part[ne:] + _dot_nt(wr_ref[:ne, :], u_lo)
    scores = _sigmoid(logits)
    gates_t, best_grp = _route(scores, scores + br_ref[...])
    u2e_ref[:, :d] = u_hi.astype(f32)
    u2e_ref[:, d:] = jnp.zeros((x1.shape[0], LANE), f32)
    u2e_ref[:, d:d + ne] = gates_t.T
    grp_ref[...] = best_grp


def _outproj_route(ya, yb, yc, yd, x2d, mod, w_out, ln_g, ln_b, wr_split, b_router, seq):
    t, d = x2d.shape
    tm = 1024
    tps = seq // tm
    row = lambda w: pl.BlockSpec((tm, w), lambda i: (i, 0))
    const = lambda a: pl.BlockSpec(a.shape, lambda i: (0,) * a.ndim)
    return pl.pallas_call(
        _outproj_route_body,
        grid=(t // tm,),
        in_specs=[row(256), row(256), row(256), row(256), row(d),
                  pl.BlockSpec((1, 6, d), lambda i: (i // tps, 0, 0)),
                  const(w_out), const(ln_g), const(ln_b), const(wr_split), const(b_router)],
        out_specs=[row(d), row(d + LANE), pl.BlockSpec((1, tm), lambda i: (0, i))],
        out_shape=[jax.ShapeDtypeStruct((t, d), f32), jax.ShapeDtypeStruct((t, d + LANE), f32),
                   jax.ShapeDtypeStruct((1, t), jnp.int32)],
        compiler_params=_cparams(("arbitrary",)),
        name="outproj_route",
    )(ya, yb, yc, yd, x2d, mod, w_out, ln_g, ln_b, wr_split, b_router)


def _dispatch_plan(grp, tr):
    t = grp.shape[0]
    rows = t + N_GROUPS * tr
    order = jnp.sort(grp * t + jnp.arange(t, dtype=jnp.int32)) % t
    order = jnp.concatenate([order, jnp.zeros((rows - t,), order.dtype)])
    onehot = (grp[:, None] == jnp.arange(N_GROUPS)[None, :]).astype(jnp.int32)
    counts = jnp.sum(onehot, axis=0)
    padded = ((counts + tr - 1) // tr) * tr
    row_end = jnp.cumsum(padded)
    rank = jnp.sum((jnp.cumsum(onehot, axis=0) - 1) * onehot, axis=1)
    row_of_tok = (row_end - padded)[grp] + rank
    r = jnp.arange(rows, dtype=jnp.int32)
    tok_of_row = jnp.zeros((rows,), jnp.int32)
    for g in range(N_GROUPS):
        row_start = row_end[g] - padded[g]
        tok_start = jnp.sum(counts[:g])
        shifted = jnp.roll(order, row_start - tok_start)
        in_grp = (r >= row_start) & (r < row_end[g])
        valid = in_grp & (r < row_start + counts[g])
        tok_of_row = jnp.where(valid, shifted,
                               jnp.where(in_grp, order[jnp.minimum(tok_start, t - 1)], tok_of_row))
    tile_start = jnp.arange(rows // tr, dtype=jnp.int32) * tr
    tile_grp = jnp.minimum(jnp.sum((tile_start[:, None] >= row_end[None, :]).astype(jnp.int32), axis=1),
                           N_GROUPS - 1)
    i32 = lambda a: a.astype(jnp.int32)
    return i32(tok_of_row), i32(row_of_tok), i32(tile_grp), i32(row_end[-1:] // tr)


def _wait_row_gather(n_rows, hbm, vmem, sem):
    pltpu.make_async_copy(hbm.at[pl.ds(0, n_rows), :], vmem, sem).wait()


def _start_row_gather(idx_ref, first, n_rows, hbm, vmem, sem):
    for i in range(n_rows):
        pltpu.make_async_copy(hbm.at[pl.ds(idx_ref[first + i], 1), :], vmem.at[pl.ds(i, 1), :], sem).start()


def _experts_body(tok_ref, tgrp_ref, ntile_ref, u2e_hbm, w1_ref, w3_ref, w2_ref, o_ref,
                  xbuf0, xbuf1, w1b, w3b, w2b, gsem):
    j = pl.program_id(0)
    n_tiles = ntile_ref[0]
    tr, d = o_ref.shape
    xbuf = (xbuf0, xbuf1)

    def compute(s):
        xs = xbuf[s][...]
        xb = xs[:, :d].astype(bf16)
        gslab = xs[:, d:]
        lane = lax.broadcasted_iota(jnp.int32, gslab.shape, 1)
        first_expert = tgrp_ref[j] * E_PER_GROUP
        acc = None
        for e in range(E_PER_GROUP):
            ge = jnp.sum(jnp.where(lane == first_expert + e, gslab, 0.0), axis=1, keepdims=True)
            h1 = _dot(xb, w1b[e])
            h = h1 * _sigmoid(h1) * _dot(xb, w3b[e]) * ge
            part = _dot(h.astype(bf16), w2b[e])
            acc = part if acc is None else acc + part
        o_ref[...] = acc

    @pl.when((j < n_tiles) & ((j == 0) | (tgrp_ref[j] != tgrp_ref[jnp.maximum(j - 1, 0)])))
    def _():
        w1b[...] = w1_ref[0].astype(bf16)
        w3b[...] = w3_ref[0].astype(bf16)
        w2b[...] = w2_ref[0].astype(bf16)

    @pl.when(j == 0)
    def _():
        _start_row_gather(tok_ref, 0, tr, u2e_hbm, xbuf[0], gsem.at[0])

    for s in (0, 1):
        @pl.when((j + 1 < n_tiles) & (j % 2 == s))
        def _(s=s):
            _start_row_gather(tok_ref, (j + 1) * tr, tr, u2e_hbm, xbuf[1 - s], gsem.at[1 - s])

        @pl.when((j < n_tiles) & (j % 2 == s))
        def _(s=s):
            _wait_row_gather(tr, u2e_hbm, xbuf[s], gsem.at[s])
            compute(s)

    @pl.when(j >= n_tiles)
    def _():
        o_ref[...] = jnp.zeros_like(o_ref)


def _experts(tok_of_row, tile_grp, n_tiles, u2e, w1, w3, w2, layer):
    rows = tok_of_row.shape[0]
    tr = MOE_ROW_TILE
    _, ne, d, fe = w1.shape
    grouped = lambda w: w.reshape(w.shape[0], N_GROUPS, E_PER_GROUP, *w.shape[2:])
    wspec = lambda a, b: pl.BlockSpec((None, 1, E_PER_GROUP, a, b), lambda j, tok, tg, nt: (layer, tg[j], 0, 0, 0))
    return pl.pallas_call(
        _experts_body,
        grid_spec=pltpu.PrefetchScalarGridSpec(
            num_scalar_prefetch=3, grid=(rows // tr,),
            in_specs=[pl.BlockSpec(memory_space=pl.ANY), wspec(d, fe), wspec(d, fe), wspec(fe, d)],
            out_specs=pl.BlockSpec((tr, d), lambda j, tok, tg, nt: (j, 0)),
            scratch_shapes=[pltpu.VMEM((tr, u2e.shape[1]), f32), pltpu.VMEM((tr, u2e.shape[1]), f32),
                            pltpu.VMEM((E_PER_GROUP, d, fe), bf16), pltpu.VMEM((E_PER_GROUP, d, fe), bf16),
                            pltpu.VMEM((E_PER_GROUP, fe, d), bf16),
                            pltpu.SemaphoreType.DMA((2,))]),
        out_shape=jax.ShapeDtypeStruct((rows, d), f32),
        compiler_params=_cparams(("arbitrary",)),
        name="experts",
    )(tok_of_row, tile_grp, n_tiles, u2e, grouped(w1), grouped(w3), grouped(w2))


def _moe_combine_body(row_ref, ys_hbm, x_ref, mod_ref, g_ref, b_ref, o_ref, ybuf0, ybuf1, sem):
    i = pl.program_id(0)
    tm = x_ref.shape[0]
    ybuf = (ybuf0, ybuf1)

    @pl.when(i == 0)
    def _():
        _start_row_gather(row_ref, 0, tm, ys_hbm, ybuf[0], sem.at[0])

    for s in (0, 1):
        @pl.when((i + 1 < pl.num_programs(0)) & (i % 2 == s))
        def _(s=s):
            _start_row_gather(row_ref, (i + 1) * tm, tm, ys_hbm, ybuf[1 - s], sem.at[1 - s])

        @pl.when(i % 2 == s)
        def _(s=s):
            _wait_row_gather(tm, ys_hbm, ybuf[s], sem.at[s])
            r = DN_ALPHA * x_ref[...] + mod_ref[0, 5:6, :] * ybuf[s][...]
            o_ref[...] = _layer_norm(r) * g_ref[...] + b_ref[...]


def _moe_combine(row_of_tok, ys, x1, mod, ln_g, ln_b, seq):
    t, d = x1.shape
    tm = 256
    tps = seq // tm
    const = lambda a: pl.BlockSpec(a.shape, lambda i, rows: (0,) * a.ndim)
    return pl.pallas_call(
        _moe_combine_body,
        grid_spec=pltpu.PrefetchScalarGridSpec(
            num_scalar_prefetch=1, grid=(t // tm,),
            in_specs=[pl.BlockSpec(memory_space=pl.ANY),
                      pl.BlockSpec((tm, d), lambda i, rows: (i, 0)),
                      pl.BlockSpec((1, 6, d), lambda i, rows: (i // tps, 0, 0)),
                      const(ln_g), const(ln_b)],
            out_specs=pl.BlockSpec((tm, d), lambda i, rows: (i, 0)),
            scratch_shapes=[pltpu.VMEM((tm, d), f32), pltpu.VMEM((tm, d), f32),
                            pltpu.SemaphoreType.DMA((2,))]),
        out_shape=jax.ShapeDtypeStruct((t, d), f32),
        compiler_params=_cparams(("arbitrary",)),
        name="moe_combine",
    )(row_of_tok, ys, x1, mod, ln_g, ln_b)


def _pad_cols(a, width):
    return jnp.pad(a, ((0, 0), (0, width - a.shape[1])))


def _pack_w_in(w):
    parts, acc = [], 0
    for sz in IN_SIZES:
        parts.append(w[:, acc:acc + sz])
        acc += sz
    (a_q, a_kv, a_kr, m_q, m_k, m_v, m_i, m_f, m_o, c_b, c_c, c_h, d_q, d_k, d_v) = parts
    gates = _pad_cols(jnp.concatenate([m_i, m_f], axis=1), LANE)
    packed = jnp.concatenate([
        _pad_cols(a_q, 256), a_kv, _pad_cols(a_kr, LANE),
        m_q, m_k, m_v, gates, m_o, c_b, c_c, c_h, d_q, d_k, d_v], axis=1)
    return packed.astype(bf16)


def _pack_mla(w_uq, w_ukv):
    half = MLA_ROPE // 2
    wq3 = w_uq.reshape(MLA_Q_LORA, MLA_HEADS, MLA_NOPE + MLA_ROPE).transpose(1, 0, 2)
    wkv3 = w_ukv.reshape(MLA_KV_LORA, MLA_HEADS, MLA_NOPE + MLA_V).transpose(1, 0, 2)
    nope, pe = wq3[:, :, :MLA_NOPE], wq3[:, :, MLA_NOPE:]
    rot = jnp.concatenate([-pe[:, :, half:], pe[:, :, :half]], axis=2)
    pad = lambda a: jnp.pad(a, ((0, 0), (0, 256 - a.shape[1]), (0, LANE - a.shape[2]))).astype(bf16)
    wq = pad(wq3)
    wqr = pad(jnp.concatenate([jnp.zeros_like(nope), rot], axis=2))
    wk = jnp.pad(wkv3[:, :, :MLA_NOPE], ((0, 0), (0, 0), (0, LANE - MLA_NOPE))).astype(bf16)
    return wq, wqr, wk, wkv3[:, :, MLA_NOPE:].astype(bf16)


def _rope_tables(seq):
    half = DF_DK // 2
    inv = 1.0 / (ROPE_THETA ** (jnp.arange(0, DF_DK, 2, dtype=f32) / DF_DK))
    ang = jnp.arange(seq, dtype=f32)[:, None] * inv[None, :]
    cos, sin = jnp.cos(ang), jnp.sin(ang)
    cos_t = jnp.tile(jnp.concatenate([cos, cos], axis=1), (1, 256 // DF_DK))
    sin_t = jnp.tile(jnp.concatenate([-sin, sin], axis=1), (1, 256 // DF_DK))
    scale = (MLA_NOPE + MLA_ROPE) ** -0.5 * LOG2E
    ones = jnp.ones((seq, MLA_NOPE), f32)
    zeros = jnp.zeros((seq, MLA_NOPE), f32)
    cq = _pad_cols(jnp.concatenate([ones, cos, cos], axis=1) * scale, LANE)
    sq = _pad_cols(jnp.concatenate([zeros, sin, sin], axis=1) * scale, LANE)
    return cos_t, sin_t, cq, sq


def kernel(x, c, w_ada, b_ada, w_in, mla_g_q, mla_g_kv, mla_w_uq, mla_w_ukv, ml_b_i, ml_b_f, sc_w,
           df_lq1, df_lk1, df_lq2, df_lk2, df_g, w_out, ln1_g, ln1_b, w_router, b_router, w1, w3, w2,
           ln2_g, ln2_b):
    batch, seq, d = x.shape
    depth = w_in.shape[0]
    assert MLA_ROPE == DF_DK, "both rotary blocks share one table"
    cos_t, sin_t, cq, sq = _rope_tables(seq)
    place = jnp.zeros((LANE, LANE), f32).at[jnp.arange(MLA_ROPE), MLA_NOPE + jnp.arange(MLA_ROPE)].set(1.0).astype(bf16)
    wr_hi = w_router.astype(bf16)
    wr_split = jnp.concatenate([wr_hi, (w_router - wr_hi.astype(f32)).astype(bf16)], axis=1).T
    mod_all = _ada_mod(c, w_ada, b_ada).reshape(depth, batch, 6, d)
    xf = x.reshape(batch * seq, d)
    for l in range(depth):
        mod = mod_all[l]
        w_packed = _pack_w_in(w_in[l])
        gq = _pad_cols(mla_g_q[l][None, :], 256)
        gkv = mla_g_kv[l][None, :]
        gbias = _pad_cols(jnp.concatenate([ml_b_i[l], ml_b_f[l]])[None, :], LANE)
        wkt = w_packed[:, OFF_MK:OFF_MK + GROUP_W].T
        (qn, kvn, kpe, mq, mk, mv, mo, gcol, yc, dq, dk, dv, mkt) = _inproj(
            xf, mod, w_packed, wkt, cos_t, sin_t, gq, gkv, gbias, sc_w[l], seq)
        wq, wqr, wk, wv = _pack_mla(mla_w_uq[l], mla_w_ukv[l])
        ya = _mla_attention(qn, kvn, kpe, wq, wqr, wk, wv, place, cq, sq, batch, seq)
        yb = _mlstm(mq, mk, mkt, mv, mo, gcol, batch, seq)
        lambda_init = 0.8 - 0.6 * math.exp(-0.3 * l)
        yd = _diff_attention(dq, dk, dv, df_lq1[l][None, :], df_lk1[l][None, :], df_lq2[l][None, :],
                             df_lk2[l][None, :], _pad_cols(df_g[l][None, :], LANE),
                             batch, seq, lambda_init)
        x1, u2e, grp = _outproj_route(ya, yb, yc, yd, xf, mod, w_out[l].astype(bf16), ln1_g[l][None, :],
                                      ln1_b[l][None, :], wr_split, b_router[:, None], seq)
        tok_of_row, row_of_tok, tile_grp, n_tiles = _dispatch_plan(grp[0], MOE_ROW_TILE)
        ys = _experts(tok_of_row, tile_grp, n_tiles, u2e, w1, w3, w2, l)
        xf = _moe_combine(row_of_tok, ys, x1, mod, ln2_g[l][None, :], ln2_b[l][None, :], seq)
    return xf.reshape(batch, seq, d)
```

```python
import functools
import math

import jax
import jax.numpy as jnp
from jax import lax
from jax.experimental import pallas as pl
from jax.experimental.pallas import tpu as pltpu

f32 = jnp.float32
bf16 = jnp.bfloat16
HIGHEST = lax.Precision.HIGHEST

GROUP_W = 256
MLA_HEADS, MLA_NOPE, MLA_ROPE, MLA_V = 4, 64, 32, 64
MLA_Q_LORA, MLA_KV_LORA = 192, 128
ML_HEADS, ML_DH, ML_CHUNK = 4, 64, 64
DF_HEADS, DF_DK, DF_V = 4, 32, 64
ROPE_THETA = 10000.0
N_EXPERTS, N_GROUPS, E_PER_GROUP, D_EXPERT = 16, 4, 4, 256
DEPTH = 2
DN_ALPHA = (2 * DEPTH) ** 0.25
LN_EPS = 1e-5
RMS_EPS = 1e-6
IN_SIZES = (192, 128, 32, 256, 256, 256, 4, 4, 256, 256, 256, 256, 256, 256, 256)

OFF_PQ, OFF_PKV, OFF_KR = 0, 256, 384
OFF_MQ, OFF_MK, OFF_MV, OFF_MG, OFF_MO = 512, 768, 1024, 1280, 1408
OFF_CB, OFF_CC, OFF_CH = 1664, 1920, 2176
OFF_DQ, OFF_DK, OFF_DV = 2432, 2688, 2944
IN_PACKED = 3200

LANE = 128
VMEM_LIMIT = 48 * 1024 * 1024
NEG = -1e30
LOG2E = math.log2(math.e)


def _cparams(sem):
    return pltpu.CompilerParams(dimension_semantics=sem, vmem_limit_bytes=VMEM_LIMIT)


def _sigmoid(x):
    return 1.0 / (1.0 + jnp.exp(-x))


def _layer_norm(x):
    mu = jnp.mean(x, axis=-1, keepdims=True)
    xc = x - mu
    var = jnp.mean(xc * xc, axis=-1, keepdims=True)
    return xc * lax.rsqrt(var + LN_EPS)


def _dot(a, b, **kw):
    return jnp.dot(a, b, preferred_element_type=f32, **kw)


def _dot_nt(a, b, **kw):
    return lax.dot_general(a, b, (((1,), (1,)), ((), ())), preferred_element_type=f32, **kw)


def _dot_tn(a, b, **kw):
    return lax.dot_general(a, b, (((0,), (0,)), ((), ())), preferred_element_type=f32, **kw)


def _expand_groups(cols, rows, width, group):
    lane = lax.broadcasted_iota(jnp.int32, (rows, width), 1)
    out = jnp.broadcast_to(cols[-1], (rows, width))
    for h in range(len(cols) - 2, -1, -1):
        out = jnp.where(lane < group * (h + 1), cols[h], out)
    return out


def _lane_group_mask(shape, lo, hi):
    lane = lax.broadcasted_iota(jnp.int32, shape, 1)
    return (lane >= lo) & (lane < hi)


def _ada_body(c_ref, w_ref, b_ref, o_ref):
    c = c_ref[...]
    ca = (c * _sigmoid(c)).astype(bf16)
    o_ref[0] = _dot(ca, w_ref[0].astype(bf16)) + b_ref[0]


def _ada_mod(c, w_ada, b_ada):
    depth, d, n = w_ada.shape
    b = c.shape[0]
    tn = 1536
    return pl.pallas_call(
        _ada_body,
        grid=(depth, n // tn),
        in_specs=[
            pl.BlockSpec((b, d), lambda l, j: (0, 0)),
            pl.BlockSpec((1, d, tn), lambda l, j: (l, 0, j)),
            pl.BlockSpec((1, 1, tn), lambda l, j: (l, 0, j)),
        ],
        out_specs=pl.BlockSpec((1, b, tn), lambda l, j: (l, 0, j)),
        out_shape=jax.ShapeDtypeStruct((depth, b, n), f32),
        compiler_params=_cparams(("arbitrary", "arbitrary")),
        name="ada_mod",
    )(c, w_ada, b_ada.reshape(depth, 1, n))


def _rope_lanes(x, cos, sin_signed):
    w = x.shape[1]
    lane = lax.broadcasted_iota(jnp.int32, x.shape, 1)
    rot = jnp.where(lane % 32 < 16, pltpu.roll(x, w - 16, axis=1), pltpu.roll(x, 16, axis=1))
    return x * cos + rot * sin_signed


def _inproj_body(x_ref, mod_ref, w_ref, wkt_ref, cos_ref, sin_ref, gq_ref, gkv_ref, gb_ref, cw_ref,
                 qn_ref, kvn_ref, kpe_ref, mq_ref, mk_ref, mv_ref, mo_ref, gcol_ref, yc_ref,
                 dq_ref, dk_ref, dv_ref, mkt_ref, carry_ref, *, tiles_per_seq):
    i = pl.program_id(0)
    tm = x_ref.shape[0]
    u = _layer_norm(x_ref[...]) * (1.0 + mod_ref[0, 1:2, :]) + mod_ref[0, 0:1, :]
    ub = u.astype(bf16)

    def seg(off, n):
        return _dot(ub, w_ref[:, off:off + n])

    cos = cos_ref[...]
    sin = sin_ref[...]

    pq = seg(OFF_PQ, 256)
    ms = jnp.sum(pq * pq, axis=-1, keepdims=True) * (1.0 / MLA_Q_LORA)
    qn_ref[...] = (pq * lax.rsqrt(ms + RMS_EPS) * gq_ref[...]).astype(bf16)
    pkv = seg(OFF_PKV, 128)
    ms = jnp.sum(pkv * pkv, axis=-1, keepdims=True) * (1.0 / MLA_KV_LORA)
    kvn_ref[...] = (pkv * lax.rsqrt(ms + RMS_EPS) * gkv_ref[...]).astype(bf16)
    kpe_ref[...] = _rope_lanes(seg(OFF_KR, 128), cos[:, :LANE], sin[:, :LANE]).astype(bf16)

    mq_ref[...] = (seg(OFF_MQ, 256) * (ML_DH ** -0.5)).astype(bf16)
    mk_ref[...] = seg(OFF_MK, 256).astype(bf16)
    mkt_ref[...] = _dot_nt(wkt_ref[...], ub).astype(bf16)
    mv_ref[...] = seg(OFF_MV, 256).astype(bf16)
    gcol_ref[...] = seg(OFF_MG, 128) + gb_ref[...]
    mo_ref[...] = seg(OFF_MO, 256).astype(bf16)

    uc = seg(OFF_CC, 256) * seg(OFF_CH, 256)

    @pl.when(i % tiles_per_seq == 0)
    def _():
        carry_ref[...] = jnp.zeros_like(carry_ref)

    prev = carry_ref[...]
    row =lax.broadcasted_iota(jnp.int32, uc.shape, 0)
    u1 = jnp.where(row == 0, prev[7:8, :], pltpu.roll(uc, 1, axis=0))
    u2 = jnp.where(row == 0, prev[6:7, :], jnp.where(row == 1, prev[7:8, :], pltpu.roll(uc, 2, axis=0)))
    carry_ref[...] = uc[tm - 8:, :]
    conv = cw_ref[0:1, :] * u2 + cw_ref[1:2, :] * u1 + cw_ref[2:3, :] * uc
    yc_ref[...] = (seg(OFF_CB, 256) * conv).astype(bf16)

    dq_ref[...] = (_rope_lanes(seg(OFF_DQ, 256), cos, sin) * (DF_DK ** -0.5 * LOG2E)).astype(bf16)
    dk_ref[...] = _rope_lanes(seg(OFF_DK, 256), cos, sin).astype(bf16)
    dv_ref[...] = seg(OFF_DV, 256).astype(bf16)


def _inproj(x2d, mod, w_packed, wkt, cos_t, sin_t, gq, gkv, gbias, conv_w, seq):
    t, d = x2d.shape
    tm = 1024
    tps = seq // tm
    row = lambda w: pl.BlockSpec((tm, w), lambda i: (i, 0))
    const = lambda a: pl.BlockSpec(a.shape, lambda i: (0,) * a.ndim)
    widths = (256, 128, 128, 256, 256, 256, 256, 128, 256, 256, 256, 256)
    dtypes = (bf16, bf16, bf16, bf16, bf16, bf16, bf16, f32, bf16, bf16, bf16, bf16)
    return pl.pallas_call(
        functools.partial(_inproj_body, tiles_per_seq=tps),
        grid=(t // tm,),
        in_specs=[
            row(d),
            pl.BlockSpec((1, 6, d), lambda i: (i // tps, 0, 0)),
            const(w_packed), const(wkt),
            pl.BlockSpec((tm, 256), lambda i: (i % tps, 0)),
            pl.BlockSpec((tm, 256), lambda i: (i % tps, 0)),
            const(gq), const(gkv), const(gbias), const(conv_w),
        ],
        out_specs=[row(w) for w in widths] + [pl.BlockSpec((GROUP_W, tm), lambda i: (0, i))],
        out_shape=[jax.ShapeDtypeStruct((t, w), dt) for w, dt in zip(widths, dtypes)]
        + [jax.ShapeDtypeStruct((GROUP_W, t), bf16)],
        scratch_shapes=[pltpu.VMEM((8, 256), f32)],
        compiler_params=_cparams(("arbitrary",)),
        name="inproj",
    )(x2d, mod, w_packed, wkt, cos_t, sin_t, gq, gkv, gbias, conv_w)


def _causal_softmax_heads(q_heads, k_at, v_at, qi, s_scr, m_scr, acc_scr):
    n_heads = len(q_heads)
    tq = q_heads[0].shape[0]
    half = tq // 2
    rows = pl.ds(pl.multiple_of(qi * tq, tq), tq)

    def scores(g, j):
        return _dot_nt(q_heads[g], k_at(g, pl.multiple_of(j * tq, tq)))

    def fold_max(g, s):
        m_scr[g] = jnp.maximum(m_scr[g], jnp.maximum(s[:, :half], s[:, half:]))

    m_scr[...] = jnp.full(m_scr.shape, NEG, f32)
    for g in range(n_heads):
        acc_scr[g, rows, :] = jnp.zeros((tq, LANE), f32)

    def score_blocks(js):
        for g in range(n_heads):
            mx = None
            for j in js:
                s = scores(g, j)
                s_scr[g, j] = s
                s = jnp.maximum(s[:, :half], s[:, half:])
                mx = s if mx is None else jnp.maximum(mx, s)
            m_scr[g] = jnp.maximum(m_scr[g], mx)

    def prob_blocks(js):
        for g in range(n_heads):
            mb = m_scr[g]
            pv = None
            for j in js:
                sj = s_scr[g, j]
                p = jnp.concatenate([jnp.exp2(sj[:, :half] - mb), jnp.exp2(sj[:, half:] - mb)],
                                    axis=1).astype(bf16)
                d = _dot(p, v_at(g, pl.multiple_of(j * tq, tq)))
                pv = d if pv is None else pv + d
            acc_scr[g, rows, :] += pv

    def over_blocks(n, blocks_fn):
        @pl.loop(0, n // 4)
        def _(jj):
            blocks_fn(tuple(4 * jj + u for u in range(4)))

        rest = (n // 4) * 4

        @pl.when(n % 4 >= 2)
        def _():
            blocks_fn((rest, rest + 1))

        @pl.when(n % 2 == 1)
        def _():
            blocks_fn((n - 1,))

    over_blocks(qi, score_blocks)
    rowi = lax.broadcasted_iota(jnp.int32, (tq, tq), 0)
    coli = lax.broadcasted_iota(jnp.int32, (tq, tq), 1)
    for g in range(n_heads):
        s = jnp.where(coli <= rowi, scores(g, qi), NEG)
        s_scr[g, qi] = s
        fold_max(g, s)
        m_scr[g] = jnp.broadcast_to(jnp.max(m_scr[g], axis=1, keepdims=True), (tq, half))

    over_blocks(qi + 1, prob_blocks)


def _softmax_scratch(n_heads, nq, tq):
    return [pltpu.VMEM((n_heads, nq, tq, tq), f32), pltpu.VMEM((n_heads, tq, tq // 2), f32),
            pltpu.VMEM((n_heads, nq * tq, LANE), f32)]


EPILOGUE_ROWS = 512


def _with_ones_lanes(v, dv):
    return jnp.concatenate([v.astype(bf16), jnp.ones((v.shape[0], LANE - dv), bf16)], axis=1)


def _softmax_normalise(acc, dv):
    lane = lax.broadcasted_iota(jnp.int32, acc.shape, 1)
    return jnp.where(lane < dv, acc / pltpu.roll(acc, LANE - dv, axis=1), 0.0)


def _pair_lanes(a, b, dv):
    lane = lax.broadcasted_iota(jnp.int32, a.shape, 1)
    return jnp.where(lane < dv, a, pltpu.roll(b, dv, axis=1))


def _mla_body(qn_ref, kvn_ref, kpe_ref, wq_ref, wqr_ref, wk_ref, wv_ref, place_ref, cq_ref, sq_ref,
              o_ref, q_scr, k_scr, v_scr, s_scr, m_scr, acc_scr):
    qi = pl.program_id(1)
    tq = s_scr.shape[2]

    @pl.when(qi == 0)
    def _():
        kvn = kvn_ref[...]
        kpe_placed = _dot(kpe_ref[...], place_ref[...])
        for h in range(MLA_HEADS):
            k_scr[h] = (_dot(kvn, wk_ref[h]) + kpe_placed).astype(bf16)
            v_scr[h] = _with_ones_lanes(_dot(kvn, wv_ref[h]), MLA_V)
            for r0 in range(0, qn_ref.shape[0], EPILOGUE_ROWS):
                rows = pl.ds(r0, EPILOGUE_ROWS)
                qn = qn_ref[rows, :]
                q_scr[h, rows, :] = (_dot(qn, wq_ref[h]) * cq_ref[rows, :]
                                     + _dot(qn, wqr_ref[h]) * sq_ref[rows, :]).astype(bf16)

    q_rows = pl.ds(pl.multiple_of(qi * tq, tq), tq)
    q_heads = [q_scr[h, q_rows, :] for h in range(MLA_HEADS)]
    _causal_softmax_heads(
        q_heads, lambda g, start: k_scr[g, pl.ds(start, tq), :],
        lambda g, start: v_scr[g, pl.ds(start, tq), :], qi, s_scr, m_scr, acc_scr)

    @pl.when(qi == pl.num_programs(1) - 1)
    def _():
        for r0 in range(0, o_ref.shape[0], EPILOGUE_ROWS):
            rows = pl.ds(r0, EPILOGUE_ROWS)
            o = [_softmax_normalise(acc_scr[h, rows, :], MLA_V) for h in range(MLA_HEADS)]
            o_ref[rows, :] = jnp.concatenate(
                [_pair_lanes(o[0], o[1], MLA_V), _pair_lanes(o[2], o[3], MLA_V)], axis=1).astype(bf16)


def _mla_attention(qn, kvn, kpe, wq, wqr, wk, wv, place, cq, sq, batch, seq):
    tq = 256
    nq = seq // tq
    const = lambda a: pl.BlockSpec(a.shape, lambda b, i: (0,) * a.ndim)
    return pl.pallas_call(
        _mla_body,
        grid=(batch, nq),
        in_specs=[
            pl.BlockSpec((seq, 256), lambda b, i: (b, 0)),
            pl.BlockSpec((seq, 128), lambda b, i: (b, 0)),
            pl.BlockSpec((seq, 128), lambda b, i: (b, 0)),
            const(wq), const(wqr), const(wk), const(wv), const(place), const(cq), const(sq),
        ],
        out_specs=pl.BlockSpec((seq, GROUP_W), lambda b, i: (b, 0)),
        out_shape=jax.ShapeDtypeStruct((batch * seq, GROUP_W), bf16),
        scratch_shapes=[pltpu.VMEM((MLA_HEADS, seq, 128), bf16),
                        pltpu.VMEM((MLA_HEADS, seq, 128), bf16),
                        pltpu.VMEM((MLA_HEADS, seq, LANE), bf16)]
        + _softmax_scratch(MLA_HEADS, nq, tq),
        compiler_params=_cparams(("arbitrary", "arbitrary")),
        name="mla_attn",
    )(qn, kvn, kpe, wq, wqr, wk, wv, place, cq, sq)


def _diff_body(q_ref, k_ref, v_ref, lq1_ref, lk1_ref, lq2_ref, lk2_ref, g_ref,
               o_ref, v_scr, s_scr, m_scr, acc_scr, *, lambda_init):
    qi = pl.program_id(1)
    tq = q_ref.shape[0]

    @pl.when(qi == 0)
    def _():
        v = v_ref[...]
        for h in range(DF_HEADS):
            v_scr[h] = _with_ones_lanes(v[:, DF_V * h:DF_V * (h + 1)], DF_V)

    q = q_ref[...]
    q_heads = [jnp.where(_lane_group_mask(q.shape, DF_DK * g, DF_DK * (g + 1)), q, jnp.zeros_like(q))
               for g in range(2 * DF_HEADS)]
    _causal_softmax_heads(
        q_heads, lambda g, start: k_ref[pl.ds(start, tq), :],
        lambda g, start: v_scr[g // 2, pl.ds(start, tq), :], qi, s_scr, m_scr, acc_scr)

    @pl.when(qi == pl.num_programs(1) - 1)
    def _():
        lam = (jnp.exp(jnp.sum(lq1_ref[...] * lk1_ref[...], axis=1, keepdims=True))
               - jnp.exp(jnp.sum(lq2_ref[...] * lk2_ref[...], axis=1, keepdims=True)) + lambda_init)
        gain = g_ref[...] * (1.0 - lambda_init)
        for r0 in range(0, o_ref.shape[0], EPILOGUE_ROWS):
            rows = pl.ds(r0, EPILOGUE_ROWS)
            ys = []
            for h in range(DF_HEADS):
                o = (_softmax_normalise(acc_scr[2 * h, rows, :], DF_V)
                     - lam * _softmax_normalise(acc_scr[2 * h + 1, rows, :], DF_V))
                ms = jnp.sum(o * o, axis=1, keepdims=True) * (1.0 / DF_V)
                ys.append(o * lax.rsqrt(ms + RMS_EPS) * gain)
            o_ref[rows, :] = jnp.concatenate(
                [_pair_lanes(ys[0], ys[1], DF_V), _pair_lanes(ys[2], ys[3], DF_V)], axis=1).astype(bf16)


def _diff_attention(dq, dk, dv, lq1, lk1, lq2, lk2, g_sub, batch, seq, lambda_init):
    tq = 256
    nq = seq // tq
    const = lambda a: pl.BlockSpec(a.shape, lambda b, i: (0,) * a.ndim)
    return pl.pallas_call(
        functools.partial(_diff_body, lambda_init=lambda_init),
        grid=(batch, nq),
        in_specs=[
            pl.BlockSpec((tq, 256), lambda b, i: (b * nq + i, 0)),
            pl.BlockSpec((seq, 256), lambda b, i: (b, 0)),
            pl.BlockSpec((seq, 256), lambda b, i: (b, 0)),
            const(lq1), const(lk1), const(lq2), const(lk2), const(g_sub),
        ],
        out_specs=pl.BlockSpec((seq, GROUP_W), lambda b, i: (b, 0)),
        out_shape=jax.ShapeDtypeStruct((batch * seq, GROUP_W), bf16),
        scratch_shapes=[pltpu.VMEM((DF_HEADS, seq, LANE), bf16)]
        + _softmax_scratch(2 * DF_HEADS, nq, tq),
        compiler_params=_cparams(("arbitrary", "arbitrary")),
        name="diff_attn",
    )(dq, dk, dv, lq1, lk1, lq2, lk2, g_sub)


def _split3(a):
    hi = a.astype(bf16)
    r1 = a - hi.astype(f32)
    mid = r1.astype(bf16)
    return hi, mid, (r1 - mid.astype(f32)).astype(bf16)


def _dot_exact_rhs01(a, b01):
    hi, mid, lo = _split3(a)
    return _dot(hi, b01) + _dot(mid, b01) + _dot(lo, b01)


def _dot_exact_lhs01(a01, b):
    hi, mid, lo = _split3(b)
    return _dot(a01, hi) + _dot(a01, mid) + _dot(a01, lo)


def _mlstm_gate_terms(g, eif):
    R, W, L = g.shape[0], eif.shape[1] // 2, ML_CHUNK
    lf = jnp.minimum(g, 0.0) - jnp.log(1.0 + jnp.exp(-jnp.abs(g)))
    glane = lax.broadcasted_iota(jnp.int32, g.shape, 1)
    wide = _dot_exact_rhs01(jnp.where(glane < ML_HEADS, g, lf), eif)
    iw, lfw = wide[:, :W], wide[:, W:]
    rr = lax.broadcasted_iota(jnp.int32, (R, R), 0)
    cc = lax.broadcasted_iota(jnp.int32, (R, R), 1)
    bw = _dot_exact_lhs01(((rr >= cc) & (rr // L == cc // L)).astype(bf16), lfw)
    rw = iw - bw
    pos_in_chunk = lax.broadcasted_iota(jnp.int32, (R, W), 0) % L
    cm = rw
    for sh in (1, 2, 4, 8, 16, 32):
        cm = jnp.maximum(cm, jnp.where(pos_in_chunk >= sh, pltpu.roll(cm, sh, axis=0), NEG))
    return bw, rw, cm


def _mlstm_chunk_local(q, k, kt_bd, v, rw, cm):
    L, W = q.shape
    row = lax.broadcasted_iota(jnp.int32, (L, W), 0)
    pos = lax.broadcasted_iota(jnp.int32, (L, W), 1) % L
    head_eq = (lax.broadcasted_iota(jnp.int32, (W, W), 0) // ML_DH
               == lax.broadcasted_iota(jnp.int32, (W, W), 1) // ML_DH)
    r_row = jnp.sum(jnp.where(pos == row, rw, 0.0), axis=0, keepdims=True)
    v4 = jnp.concatenate([v] * ML_HEADS, axis=0)
    vbd = jnp.where(head_eq, v4, jnp.zeros_like(v4))
    rmax = cm[L - 1:L, :]
    c_loc = _dot((kt_bd * jnp.exp(r_row - rmax)).astype(bf16), vbd)
    n_loc = jnp.sum(k.astype(f32) * jnp.exp(rw - rmax), axis=0, keepdims=True)
    return _dot(q, kt_bd.astype(bf16)), vbd, r_row, c_loc, n_loc


def _mlstm_chunk_step(q, o, bw, cm, qk, vbd, r_row, c_loc, n_loc, c_prev, n_prev, m_prev):
    L, W = q.shape
    row = lax.broadcasted_iota(jnp.int32, (L, W), 0)
    pos = lax.broadcasted_iota(jnp.int32, (L, W), 1) % L
    head_eq = (lax.broadcasted_iota(jnp.int32, (W, W), 0) // ML_DH
               == lax.broadcasted_iota(jnp.int32, (W, W), 1) // ML_DH)
    mw = jnp.maximum(m_prev, cm)
    a_inter = jnp.exp(m_prev - mw)
    pw = qk * jnp.exp(jnp.where(pos <= row, r_row - mw, NEG))
    sums = _dot_exact_rhs01(jnp.concatenate([q.astype(f32) * n_prev, pw], axis=0), head_eq.astype(bf16))
    den = a_inter * sums[:L] + sums[L:]
    hden = jnp.maximum(jnp.abs(den), jnp.exp(-(bw + mw)))
    num = a_inter * _dot(q, c_prev.astype(bf16)) + _dot(pw.astype(bf16), vbd)
    y = num / hden * _sigmoid(o.astype(f32))
    rmax, m_last = cm[L - 1:L, :], mw[L - 1:L, :]
    a_w = jnp.exp(m_prev - m_last)
    b_w = jnp.exp(rmax - m_last)
    return y, c_prev * a_w + c_loc * b_w, a_w * n_prev + b_w * n_loc, bw[L - 1:L, :] + m_last


def _mlstm_body(q_ref, k_ref, kt_ref, v_ref, o_ref, g_ref, eif_ref, sel_ref, y_ref, c_scr, nm_scr):
    @pl.when(pl.program_id(1) == 0)
    def _():
        c_scr[...] = jnp.zeros_like(c_scr)
        nm_scr[...] = jnp.zeros_like(nm_scr)

    c, n, m = c_scr[...], nm_scr[0:1, :], nm_scr[1:2, :]
    kt = kt_ref[...]
    eif = eif_ref[...]
    w = kt.shape[0]
    head_eq = (lax.broadcasted_iota(jnp.int32, (w, w), 0) // ML_DH
               == lax.broadcasted_iota(jnp.int32, (w, w), 1) // ML_DH)
    bw, rw, cm = _mlstm_gate_terms(g_ref[...], eif)
    n_chunks = q_ref.shape[0] // ML_CHUNK
    local = []
    for i in range(n_chunks):
        rows = pl.ds(i * ML_CHUNK, ML_CHUNK)
        lo, hi = i * ML_CHUNK, (i + 1) * ML_CHUNK
        kt_bd = jnp.where(head_eq, _dot(kt, sel_ref[i]), 0.0)
        local.append(_mlstm_chunk_local(q_ref[rows, :], k_ref[rows, :], kt_bd, v_ref[rows, :],
                                        rw[lo:hi], cm[lo:hi]))
    for i in range(n_chunks):
        rows = pl.ds(i * ML_CHUNK, ML_CHUNK)
        lo, hi = i * ML_CHUNK, (i + 1) * ML_CHUNK
        y, c, n, m = _mlstm_chunk_step(q_ref[rows, :], o_ref[rows, :], bw[lo:hi], cm[lo:hi],
                                       *local[i], c, n, m)
        y_ref[rows, :] = y.astype(y_ref.dtype)
    c_scr[...] = c
    nm_scr[0:1, :] = n
    nm_scr[1:2, :] = m


MLSTM_CHUNKS_PER_STEP = 4


def _mlstm(mq, mk, mkt, mv, mo, gcol, batch, seq):
    cps = MLSTM_CHUNKS_PER_STEP
    L = ML_CHUNK * cps
    nc = seq // L
    blk = lambda w: pl.BlockSpec((L, w), lambda b, c: (b * nc + c, 0))
    const = lambda a: pl.BlockSpec(a.shape, lambda b, c: (0,) * a.ndim)
    gl = jnp.arange(LANE)[:, None]
    col = jnp.arange(2 * GROUP_W)[None, :]
    eif = (gl == (col // GROUP_W) * ML_HEADS + (col % GROUP_W) // ML_DH).astype(bf16)
    p = jnp.arange(L)[None, :, None]
    cc = jnp.arange(GROUP_W)[None, None, :]
    sel = (p == jnp.arange(cps)[:, None, None] * ML_CHUNK + cc % ML_CHUNK).astype(bf16)
    return pl.pallas_call(
        _mlstm_body,
        grid=(batch, nc),
        in_specs=[blk(256), blk(256), pl.BlockSpec((GROUP_W, L), lambda b, c: (0, b * nc + c)),
                  blk(256), blk(256), blk(128), const(eif), const(sel)],
        out_specs=blk(GROUP_W),
        out_shape=jax.ShapeDtypeStruct((batch * seq, GROUP_W), bf16),
        scratch_shapes=[pltpu.VMEM((GROUP_W, GROUP_W), f32), pltpu.VMEM((8, GROUP_W), f32)],
        compiler_params=_cparams(("arbitrary", "arbitrary")),
        name="mlstm",
    )(mq, mk, mkt, mv, mo, gcol, eif, sel)


def _route(scores, sel):
    ne, tm = sel.shape
    eid = lax.broadcasted_iota(jnp.int32, (ne, tm), 0)
    grp = eid // E_PER_GROUP
    big = ne + 1

    def top2(vals):
        m1 = jnp.max(vals, axis=0, keepdims=True)
        i1 = jnp.min(jnp.where(vals == m1, eid, big), axis=0, keepdims=True)
        rest = jnp.where(eid == i1, NEG, vals)
        m2 = jnp.max(rest, axis=0, keepdims=True)
        i2 = jnp.min(jnp.where(rest == m2, eid, big), axis=0, keepdims=True)
        return m1, i1, m2, i2

    best_score, best_grp = None, None
    for gi in range(N_GROUPS):
        m1, _, m2, _ = top2(jnp.where(grp == gi, sel, NEG))
        sc = m1 + m2
        if best_score is None:
            best_score, best_grp = sc, jnp.zeros_like(sc, dtype=jnp.int32)
        else:
            better = sc > best_score
            best_grp = jnp.where(better, gi, best_grp)
            best_score = jnp.where(better, sc, best_score)
    _, i1, _, i2 = top2(jnp.where(grp == best_grp, sel, NEG))
    picked = jnp.where((eid == i1) | (eid == i2), scores, 0.0)
    return picked / jnp.sum(picked, axis=0, keepdims=True), best_grp


MOE_ROW_TILE = 512


def _outproj_route_body(ya_ref, yb_ref, yc_ref, yd_ref, x_ref, mod_ref, w_ref, g_ref, b_ref,
                        wr_ref, br_ref, x1_ref, u2e_ref, grp_ref):
    d = x_ref.shape[1]
    mix = None
    for j, y_ref in enumerate((ya_ref, yb_ref, yc_ref, yd_ref)):
        part = _dot(y_ref[...], w_ref[GROUP_W * j:GROUP_W * (j + 1), :])
        mix = part if mix is None else mix + part
    x1 = _layer_norm(DN_ALPHA * x_ref[...] + mod_ref[0, 2:3, :] * mix) * g_ref[...] + b_ref[...]
    x1_ref[...] = x1
    u = _layer_norm(x1) * (1.0 + mod_ref[0, 4:5, :]) + mod_ref[0, 3:4, :]
    u_hi = u.astype(bf16)
    u_lo = (u - u_hi.astype(f32)).astype(bf16)
    ne = br_ref.shape[0]
    part = _dot_nt(wr_ref[...], u_hi)
    logits = part[:ne] + part[ne:] + _dot_nt(wr_ref[:ne, :], u_lo)
    scores = _sigmoid(logits)
    gates_t, best_grp = _route(scores, scores + br_ref[...])
    u2e_ref[:, :d] = u_hi.astype(f32)
    u2e_ref[:, d:] = jnp.zeros((x1.shape[0], LANE), f32)
    u2e_ref[:, d:d + ne] = gates_t.T
    grp_ref[...] = best_grp


def _outproj_route(ya, yb, yc, yd, x2d, mod, w_out, ln_g, ln_b, wr_split, b_router, seq):
    t, d = x2d.shape
    tm = 1024
    tps = seq // tm
    row = lambda w: pl.BlockSpec((tm, w), lambda i: (i, 0))
    const = lambda a: pl.BlockSpec(a.shape, lambda i: (0,) * a.ndim)
    return pl.pallas_call(
        _outproj_route_body,
        grid=(t // tm,),
        in_specs=[row(256), row(256), row(256), row(256), row(d),
                  pl.BlockSpec((1, 6, d), lambda i: (i // tps, 0, 0)),
                  const(w_out), const(ln_g), const(ln_b), const(wr_split), const(b_router)],
        out_specs=[row(d), row(d + LANE), pl.BlockSpec((1, tm), lambda i: (0, i))],
        out_shape=[jax.ShapeDtypeStruct((t, d), f32), jax.ShapeDtypeStruct((t, d + LANE), f32),
                   jax.ShapeDtypeStruct((1, t), jnp.int32)],
        compiler_params=_cparams(("arbitrary",)),
        name="outproj_route",
    )(ya, yb, yc, yd, x2d, mod, w_out, ln_g, ln_b, wr_split, b_router)


def _dispatch_plan(grp, tr):
    t = grp.shape[0]
    rows = t + N_GROUPS * tr
    order = jnp.sort(grp * t + jnp.arange(t, dtype=jnp.int32)) % t
    order = jnp.concatenate([order, jnp.zeros((rows - t,), order.dtype)])
    onehot = (grp[:, None] == jnp.arange(N_GROUPS)[None, :]).astype(jnp.int32)
    counts = jnp.sum(onehot, axis=0)
    padded = ((counts + tr - 1) // tr) * tr
    row_end = jnp.cumsum(padded)
    rank = jnp.sum((jnp.cumsum(onehot, axis=0) - 1) * onehot, axis=1)
    row_of_tok = (row_end - padded)[grp] + rank
    r = jnp.arange(rows, dtype=jnp.int32)
    tok_of_row = jnp.zeros((rows,), jnp.int32)
    for g in range(N_GROUPS):
        row_start = row_end[g] - padded[g]
        tok_start = jnp.sum(counts[:g])
        shifted = jnp.roll(order, row_start - tok_start)
        in_grp = (r >= row_start) & (r < row_end[g])
        valid = in_grp & (r < row_start + counts[g])
        tok_of_row = jnp.where(valid, shifted,
                               jnp.where(in_grp, order[jnp.minimum(tok_start, t - 1)], tok_of_row))
    tile_start = jnp.arange(rows // tr, dtype=jnp.int32) * tr
    tile_grp = jnp.minimum(jnp.sum((tile_start[:, None] >= row_end[None, :]).astype(jnp.int32), axis=1),
                           N_GROUPS - 1)
    i32 = lambda a: a.astype(jnp.int32)
    return i32(tok_of_row), i32(row_of_tok), i32(tile_grp), i32(row_end[-1:] // tr)


def _wait_row_gather(n_rows, hbm, vmem, sem):
    pltpu.make_async_copy(hbm.at[pl.ds(0, n_rows), :], vmem, sem).wait()


def _start_row_gather(idx_ref, first, n_rows, hbm, vmem, sem):
    for i in range(n_rows):
        pltpu.make_async_copy(hbm.at[pl.ds(idx_ref[first + i], 1), :], vmem.at[pl.ds(i, 1), :], sem).start()


def _experts_body(tok_ref, tgrp_ref, ntile_ref, u2e_hbm, w1_ref, w3_ref, w2_ref, o_ref,
                  xbuf0, xbuf1, w1b, w3b, w2b, gsem):
    j = pl.program_id(0)
    n_tiles = ntile_ref[0]
    tr, d = o_ref.shape
    xbuf = (xbuf0, xbuf1)

    def compute(s):
        xs = xbuf[s][...]
        xb = xs[:, :d].astype(bf16)
        gslab = xs[:, d:]
        lane = lax.broadcasted_iota(jnp.int32, gslab.shape, 1)
        first_expert = tgrp_ref[j] * E_PER_GROUP
        acc = None
        for e in range(E_PER_GROUP):
            ge = jnp.sum(jnp.where(lane == first_expert + e, gslab, 0.0), axis=1, keepdims=True)
            h1 = _dot(xb, w1b[e])
            h = h1 * _sigmoid(h1) * _dot(xb, w3b[e]) * ge
            part = _dot(h.astype(bf16), w2b[e])
            acc = part if acc is None else acc + part
        o_ref[...] = acc

    @pl.when((j < n_tiles) & ((j == 0) | (tgrp_ref[j] != tgrp_ref[jnp.maximum(j - 1, 0)])))
    def _():
        w1b[...] = w1_ref[0].astype(bf16)
        w3b[...] = w3_ref[0].astype(bf16)
        w2b[...] = w2_ref[0].astype(bf16)

    @pl.when(j == 0)
    def _():
        _start_row_gather(tok_ref, 0, tr, u2e_hbm, xbuf[0], gsem.at[0])

    for s in (0, 1):
        @pl.when((j + 1 < n_tiles) & (j % 2 == s))
        def _(s=s):
            _start_row_gather(tok_ref, (j + 1) * tr, tr, u2e_hbm, xbuf[1 - s], gsem.at[1 - s])

        @pl.when((j < n_tiles) & (j % 2 == s))
        def _(s=s):
            _wait_row_gather(tr, u2e_hbm, xbuf[s], gsem.at[s])
            compute(s)

    @pl.when(j >= n_tiles)
    def _():
        o_ref[...] = jnp.zeros_like(o_ref)


def _experts(tok_of_row, tile_grp, n_tiles, u2e, w1, w3, w2, layer):
    rows = tok_of_row.shape[0]
    tr = MOE_ROW_TILE
    _, ne, d, fe = w1.shape
    grouped = lambda w: w.reshape(w.shape[0], N_GROUPS, E_PER_GROUP, *w.shape[2:])
    wspec = lambda a, b: pl.BlockSpec((None, 1, E_PER_GROUP, a, b), lambda j, tok, tg, nt: (layer, tg[j], 0, 0, 0))
    return pl.pallas_call(
        _experts_body,
        grid_spec=pltpu.PrefetchScalarGridSpec(
            num_scalar_prefetch=3, grid=(rows // tr,),
            in_specs=[pl.BlockSpec(memory_space=pl.ANY), wspec(d, fe), wspec(d, fe), wspec(fe, d)],
            out_specs=pl.BlockSpec((tr, d), lambda j, tok, tg, nt: (j, 0)),
            scratch_shapes=[pltpu.VMEM((tr, u2e.shape[1]), f32), pltpu.VMEM((tr, u2e.shape[1]), f32),
                            pltpu.VMEM((E_PER_GROUP, d, fe), bf16), pltpu.VMEM((E_PER_GROUP, d, fe), bf16),
                            pltpu.VMEM((E_PER_GROUP, fe, d), bf16),
                            pltpu.SemaphoreType.DMA((2,))]),
        out_shape=jax.ShapeDtypeStruct((rows, d), f32),
        compiler_params=_cparams(("arbitrary",)),
        name="experts",
    )(tok_of_row, tile_grp, n_tiles, u2e, grouped(w1), grouped(w3), grouped(w2))


def _moe_combine_body(row_ref, ys_hbm, x_ref, mod_ref, g_ref, b_ref, o_ref, ybuf0, ybuf1, sem):
    i = pl.program_id(0)
    tm = x_ref.shape[0]
    ybuf = (ybuf0, ybuf1)

    @pl.when(i == 0)
    def _():
        _start_row_gather(row_ref, 0, tm, ys_hbm, ybuf[0], sem.at[0])

    for s in (0, 1):
        @pl.when((i + 1 < pl.num_programs(0)) & (i % 2 == s))
        def _(s=s):
            _start_row_gather(row_ref, (i + 1) * tm, tm, ys_hbm, ybuf[1 - s], sem.at[1 - s])

        @pl.when(i % 2 == s)
        def _(s=s):
            _wait_row_gather(tm, ys_hbm, ybuf[s], sem.at[s])
            r = DN_ALPHA * x_ref[...] + mod_ref[0, 5:6, :] * ybuf[s][...]
            o_ref[...] = _layer_norm(r) * g_ref[...] + b_ref[...]


def _moe_combine(row_of_tok, ys, x1, mod, ln_g, ln_b, seq):
    t, d = x1.shape
    tm = 256
    tps = seq // tm
    const = lambda a: pl.BlockSpec(a.shape, lambda i, rows: (0,) * a.ndim)
    return pl.pallas_call(
        _moe_combine_body,
        grid_spec=pltpu.PrefetchScalarGridSpec(
            num_scalar_prefetch=1, grid=(t // tm,),
            in_specs=[pl.BlockSpec(memory_space=pl.ANY),
                      pl.BlockSpec((tm, d), lambda i, rows: (i, 0)),
                      pl.BlockSpec((1, 6, d), lambda i, rows: (i // tps, 0, 0)),
                      const(ln_g), const(ln_b)],
            out_specs=pl.BlockSpec((tm, d), lambda i, rows: (i, 0)),
            scratch_shapes=[pltpu.VMEM((tm, d), f32), pltpu.VMEM((tm, d), f32),
                            pltpu.SemaphoreType.DMA((2,))]),
        out_shape=jax.ShapeDtypeStruct((t, d), f32),
        compiler_params=_cparams(("arbitrary",)),
        name="moe_combine",
    )(row_of_tok, ys, x1, mod, ln_g, ln_b)


def _pad_cols(a, width):
    return jnp.pad(a, ((0, 0), (0, width - a.shape[1])))


def _pack_w_in(w):
    parts, acc = [], 0
    for sz in IN_SIZES:
        parts.append(w[:, acc:acc + sz])
        acc += sz
    (a_q, a_kv, a_kr, m_q, m_k, m_v, m_i, m_f, m_o, c_b, c_c, c_h, d_q, d_k, d_v) = parts
    gates = _pad_cols(jnp.concatenate([m_i, m_f], axis=1), LANE)
    packed = jnp.concatenate([
        _pad_cols(a_q, 256), a_kv, _pad_cols(a_kr, LANE),
        m_q, m_k, m_v, gates, m_o, c_b, c_c, c_h, d_q, d_k, d_v], axis=1)
    return packed.astype(bf16)


def _pack_mla(w_uq, w_ukv):
    half = MLA_ROPE // 2
    wq3 = w_uq.reshape(MLA_Q_LORA, MLA_HEADS, MLA_NOPE + MLA_ROPE).transpose(1, 0, 2)
    wkv3 = w_ukv.reshape(MLA_KV_LORA, MLA_HEADS, MLA_NOPE + MLA_V).transpose(1, 0, 2)
    nope, pe = wq3[:, :, :MLA_NOPE], wq3[:, :, MLA_NOPE:]
    rot = jnp.concatenate([-pe[:, :, half:], pe[:, :, :half]], axis=2)
    pad = lambda a: jnp.pad(a, ((0, 0), (0, 256 - a.shape[1]), (0, LANE - a.shape[2]))).astype(bf16)
    wq = pad(wq3)
    wqr = pad(jnp.concatenate([jnp.zeros_like(nope), rot], axis=2))
    wk = jnp.pad(wkv3[:, :, :MLA_NOPE], ((0, 0), (0, 0), (0, LANE - MLA_NOPE))).astype(bf16)
    return wq, wqr, wk, wkv3[:, :, MLA_NOPE:].astype(bf16)


def _rope_tables(seq):
    half = DF_DK // 2
    inv = 1.0 / (ROPE_THETA ** (jnp.arange(0, DF_DK, 2, dtype=f32) / DF_DK))
    ang = jnp.arange(seq, dtype=f32)[:, None] * inv[None, :]
    cos, sin = jnp.cos(ang), jnp.sin(ang)
    cos_t = jnp.tile(jnp.concatenate([cos, cos], axis=1), (1, 256 // DF_DK))
    sin_t = jnp.tile(jnp.concatenate([-sin, sin], axis=1), (1, 256 // DF_DK))
    scale = (MLA_NOPE + MLA_ROPE) ** -0.5 * LOG2E
    ones = jnp.ones((seq, MLA_NOPE), f32)
    zeros = jnp.zeros((seq, MLA_NOPE), f32)
    cq = _pad_cols(jnp.concatenate([ones, cos, cos], axis=1) * scale, LANE)
    sq = _pad_cols(jnp.concatenate([zeros, sin, sin], axis=1) * scale, LANE)
    return cos_t, sin_t, cq, sq


def kernel(x, c, w_ada, b_ada, w_in, mla_g_q, mla_g_kv, mla_w_uq, mla_w_ukv, ml_b_i, ml_b_f, sc_w,
           df_lq1, df_lk1, df_lq2, df_lk2, df_g, w_out, ln1_g, ln1_b, w_router, b_router, w1, w3, w2,
           ln2_g, ln2_b):
    batch, seq, d = x.shape
    depth = w_in.shape[0]
    assert MLA_ROPE == DF_DK, "both rotary blocks share one table"
    cos_t, sin_t, cq, sq = _rope_tables(seq)
    place = jnp.zeros((LANE, LANE), f32).at[jnp.arange(MLA_ROPE), MLA_NOPE + jnp.arange(MLA_ROPE)].set(1.0).astype(bf16)
    wr_hi = w_router.astype(bf16)
    wr_split = jnp.concatenate([wr_hi, (w_router - wr_hi.astype(f32)).astype(bf16)], axis=1).T
    mod_all = _ada_mod(c, w_ada, b_ada).reshape(depth, batch, 6, d)
    xf = x.reshape(batch * seq, d)
    for l in range(depth):
        mod = mod_all[l]
        w_packed = _pack_w_in(w_in[l])
        gq = _pad_cols(mla_g_q[l][None, :], 256)
        gkv = mla_g_kv[l][None, :]
        gbias = _pad_cols(jnp.concatenate([ml_b_i[l], ml_b_f[l]])[None, :], LANE)
        wkt = w_packed[:, OFF_MK:OFF_MK + GROUP_W].T
        (qn, kvn, kpe, mq, mk, mv, mo, gcol, yc, dq, dk, dv, mkt) = _inproj(
            xf, mod, w_packed, wkt, cos_t, sin_t, gq, gkv, gbias, sc_w[l], seq)
        wq, wqr, wk, wv = _pack_mla(mla_w_uq[l], mla_w_ukv[l])
        ya = _mla_attention(qn, kvn, kpe, wq, wqr, wk, wv, place, cq, sq, batch, seq)
        yb = _mlstm(mq, mk, mkt, mv, mo, gcol, batch, seq)
        lambda_init = 0.8 - 0.6 * math.exp(-0.3 * l)
        yd = _diff_attention(dq, dk, dv, df_lq1[l][None, :], df_lk1[l][None, :], df_lq2[l][None, :],
                             df_lk2[l][None, :], _pad_cols(df_g[l][None, :], LANE),
                             batch, seq, lambda_init)
        x1, u2e, grp = _outproj_route(ya, yb, yc, yd, xf, mod, w_out[l].astype(bf16), ln1_g[l][None, :],
                                      ln1_b[l][None, :], wr_split, b_router[:, None], seq)
        tok_of_row, row_of_tok, tile_grp, n_tiles = _dispatch_plan(grp[0], MOE_ROW_TILE)
        ys = _experts(tok_of_row, tile_grp, n_tiles, u2e, w1, w3, w2, l)
        xf = _moe_combine(row_of_tok, ys, x1, mod, ln2_g[l][None, :], ln2_b[l][None, :], seq)
    return xf.reshape(batch, seq, d)
```

```python
import functools
import math

import jax
import jax.numpy as jnp
from jax import lax
from jax.experimental import pallas as pl
from jax.experimental.pallas import tpu as pltpu

f32 = jnp.float32
bf16 = jnp.bfloat16
HIGHEST = lax.Precision.HIGHEST

GROUP_W = 256
MLA_HEADS, MLA_NOPE, MLA_ROPE, MLA_V = 4, 64, 32, 64
MLA_Q_LORA, MLA_KV_LORA = 192, 128
ML_HEADS, ML_DH, ML_CHUNK = 4, 64, 64
DF_HEADS, DF_DK, DF_V = 4, 32, 64
ROPE_THETA = 10000.0
N_EXPERTS, N_GROUPS, E_PER_GROUP, D_EXPERT = 16, 4, 4, 256
DEPTH = 2
DN_ALPHA = (2 * DEPTH) ** 0.25
LN_EPS = 1e-5
RMS_EPS = 1e-6
IN_SIZES = (192, 128, 32, 256, 256, 256, 4, 4, 256, 256, 256, 256, 256, 256, 256)

OFF_PQ, OFF_PKV, OFF_KR = 0, 256, 384
OFF_MQ, OFF_MK, OFF_MV, OFF_MG, OFF_MO = 512, 768, 1024, 1280, 1408
OFF_CB, OFF_CC, OFF_CH = 1664, 1920, 2176
OFF_DQ, OFF_DK, OFF_DV = 2432, 2688, 2944
IN_PACKED = 3200

LANE = 128
VMEM_LIMIT = 48 * 1024 * 1024
NEG = -1e30
LOG2E = math.log2(math.e)


def _cparams(sem):
    return pltpu.CompilerParams(dimension_semantics=sem, vmem_limit_bytes=VMEM_LIMIT)


def _sigmoid(x):
    return 1.0 / (1.0 + jnp.exp(-x))


def _layer_norm(x):
    mu = jnp.mean(x, axis=-1, keepdims=True)
    xc = x - mu
    var = jnp.mean(xc * xc, axis=-1, keepdims=True)
    return xc * lax.rsqrt(var + LN_EPS)


def _dot(a, b, **kw):
    return jnp.dot(a, b, preferred_element_type=f32, **kw)


def _dot_nt(a, b, **kw):
    return lax.dot_general(a, b, (((1,), (1,)), ((), ())), preferred_element_type=f32, **kw)


def _dot_tn(a, b, **kw):
    return lax.dot_general(a, b, (((0,), (0,)), ((), ())), preferred_element_type=f32, **kw)


def _expand_groups(cols, rows, width, group):
    lane = lax.broadcasted_iota(jnp.int32, (rows, width), 1)
    out = jnp.broadcast_to(cols[-1], (rows, width))
    for h in range(len(cols) - 2, -1, -1):
        out = jnp.where(lane < group * (h + 1), cols[h], out)
    return out


def _lane_group_mask(shape, lo, hi):
    lane = lax.broadcasted_iota(jnp.int32, shape, 1)
    return (lane >= lo) & (lane < hi)


def _ada_body(c_ref, w_ref, b_ref, o_ref):
    c = c_ref[...]
    ca = (c * _sigmoid(c)).astype(bf16)
    o_ref[0] = _dot(ca, w_ref[0].astype(bf16)) + b_ref[0]


def _ada_mod(c, w_ada, b_ada):
    depth, d, n = w_ada.shape
    b = c.shape[0]
    tn = 1536
    return pl.pallas_call(
        _ada_body,
        grid=(depth, n // tn),
        in_specs=[
            pl.BlockSpec((b, d), lambda l, j: (0, 0)),
            pl.BlockSpec((1, d, tn), lambda l, j: (l, 0, j)),
            pl.BlockSpec((1, 1, tn), lambda l, j: (l, 0, j)),
        ],
        out_specs=pl.BlockSpec((1, b, tn), lambda l, j: (l, 0, j)),
        out_shape=jax.ShapeDtypeStruct((depth, b, n), f32),
        compiler_params=_cparams(("arbitrary", "arbitrary")),
        name="ada_mod",
    )(c, w_ada, b_ada.reshape(depth, 1, n))


def _rope_lanes(x, cos, sin_signed):
    w = x.shape[1]
    lane = lax.broadcasted_iota(jnp.int32, x.shape, 1)
    rot = jnp.where(lane % 32 < 16, pltpu.roll(x, w - 16, axis=1), pltpu.roll(x, 16, axis=1))
    return x * cos + rot * sin_signed


def _inproj_body(x_ref, mod_ref, w_ref, wkt_ref, cos_ref, sin_ref, gq_ref, gkv_ref, gb_ref, cw_ref,
                 qn_ref, kvn_ref, kpe_ref, mq_ref, mk_ref, mv_ref, mo_ref, gcol_ref, yc_ref,
                 dq_ref, dk_ref, dv_ref, mkt_ref, carry_ref, *, tiles_per_seq):
    i = pl.program_id(0)
    tm = x_ref.shape[0]
    u = _layer_norm(x_ref[...]) * (1.0 + mod_ref[0, 1:2, :]) + mod_ref[0, 0:1, :]
    ub = u.astype(bf16)

    def seg(off, n):
        return _dot(ub, w_ref[:, off:off + n])

    cos = cos_ref[...]
    sin = sin_ref[...]

    pq = seg(OFF_PQ, 256)
    ms = jnp.sum(pq * pq, axis=-1, keepdims=True) * (1.0 / MLA_Q_LORA)
    qn_ref[...] = (pq * lax.rsqrt(ms + RMS_EPS) * gq_ref[...]).astype(bf16)
    pkv = seg(OFF_PKV, 128)
    ms = jnp.sum(pkv * pkv, axis=-1, keepdims=True) * (1.0 / MLA_KV_LORA)
    kvn_ref[...] = (pkv * lax.rsqrt(ms + RMS_EPS) * gkv_ref[...]).astype(bf16)
    kpe_ref[...] = _rope_lanes(seg(OFF_KR, 128), cos[:, :LANE], sin[:, :LANE]).astype(bf16)

    mq_ref[...] = (seg(OFF_MQ, 256) * (ML_DH ** -0.5)).astype(bf16)
    mk_ref[...] = seg(OFF_MK, 256).astype(bf16)
    mkt_ref[...] = _dot_nt(wkt_ref[...], ub).astype(bf16)
    mv_ref[...] = seg(OFF_MV, 256).astype(bf16)
    gcol_ref[...] = seg(OFF_MG, 128) + gb_ref[...]
    mo_ref[...] = seg(OFF_MO, 256).astype(bf16)

    uc = seg(OFF_CC, 256) * seg(OFF_CH, 256)

    @pl.when(i % tiles_per_seq == 0)
    def _():
        carry_ref[...] = jnp.zeros_like(carry_ref)

    prev = carry_ref[...]
    row =lax.broadcasted_iota(jnp.int32, uc.shape, 0)
    u1 = jnp.where(row == 0, prev[7:8, :], pltpu.roll(uc, 1, axis=0))
    u2 = jnp.where(row == 0, prev[6:7, :], jnp.where(row == 1, prev[7:8, :], pltpu.roll(uc, 2, axis=0)))
    carry_ref[...] = uc[tm - 8:, :]
    conv = cw_ref[0:1, :] * u2 + cw_ref[1:2, :] * u1 + cw_ref[2:3, :] * uc
    yc_ref[...] = (seg(OFF_CB, 256) * conv).astype(bf16)

    dq_ref[...] = (_rope_lanes(seg(OFF_DQ, 256), cos, sin) * (DF_DK ** -0.5 * LOG2E)).astype(bf16)
    dk_ref[...] = _rope_lanes(seg(OFF_DK, 256), cos, sin).astype(bf16)
    dv_ref[...] = seg(OFF_DV, 256).astype(bf16)


def _inproj(x2d, mod, w_packed, wkt, cos_t, sin_t, gq, gkv, gbias, conv_w, seq):
    t, d = x2d.shape
    tm = 1024
    tps = seq // tm
    row = lambda w: pl.BlockSpec((tm, w), lambda i: (i, 0))
    const = lambda a: pl.BlockSpec(a.shape, lambda i: (0,) * a.ndim)
    widths = (256, 128, 128, 256, 256, 256, 256, 128, 256, 256, 256, 256)
    dtypes = (bf16, bf16, bf16, bf16, bf16, bf16, bf16, f32, bf16, bf16, bf16, bf16)
    return pl.pallas_call(
        functools.partial(_inproj_body, tiles_per_seq=tps),
        grid=(t // tm,),
        in_specs=[
            row(d),
            pl.BlockSpec((1, 6, d), lambda i: (i // tps, 0, 0)),
            const(w_packed), const(wkt),
            pl.BlockSpec((tm, 256), lambda i: (i % tps, 0)),
            pl.BlockSpec((tm, 256), lambda i: (i % tps, 0)),
            const(gq), const(gkv), const(gbias), const(conv_w),
        ],
        out_specs=[row(w) for w in widths] + [pl.BlockSpec((GROUP_W, tm), lambda i: (0, i))],
        out_shape=[jax.ShapeDtypeStruct((t, w), dt) for w, dt in zip(widths, dtypes)]
        + [jax.ShapeDtypeStruct((GROUP_W, t), bf16)],
        scratch_shapes=[pltpu.VMEM((8, 256), f32)],
        compiler_params=_cparams(("arbitrary",)),
        name="inproj",
    )(x2d, mod, w_packed, wkt, cos_t, sin_t, gq, gkv, gbias, conv_w)


def _causal_softmax_heads(q_heads, k_at, v_at, qi, s_scr, m_scr, acc_scr):
    n_heads = len(q_heads)
    tq = q_heads[0].shape[0]
    half = tq // 2
    rows = pl.ds(pl.multiple_of(qi * tq, tq), tq)

    def scores(g, j):
        return _dot_nt(q_heads[g], k_at(g, pl.multiple_of(j * tq, tq)))

    def fold_max(g, s):
        m_scr[g] = jnp.maximum(m_scr[g], jnp.maximum(s[:, :half], s[:, half:]))

    m_scr[...] = jnp.full(m_scr.shape, NEG, f32)
    for g in range(n_heads):
        acc_scr[g, rows, :] = jnp.zeros((tq, LANE), f32)

    def score_blocks(js):
        for g in range(n_heads):
            mx = None
            for j in js:
                s = scores(g, j)
                s_scr[g, j] = s
                s = jnp.maximum(s[:, :half], s[:, half:])
                mx = s if mx is None else jnp.maximum(mx, s)
            m_scr[g] = jnp.maximum(m_scr[g], mx)

    def prob_blocks(js):
        for g in range(n_heads):
            mb = m_scr[g]
            pv = None
            for j in js:
                sj = s_scr[g, j]
                p = jnp.concatenate([jnp.exp2(sj[:, :half] - mb), jnp.exp2(sj[:, half:] - mb)],
                                    axis=1).astype(bf16)
                d = _dot(p, v_at(g, pl.multiple_of(j * tq, tq)))
                pv = d if pv is None else pv + d
            acc_scr[g, rows, :] += pv

    def over_blocks(n, blocks_fn):
        @pl.loop(0, n // 4)
        def _(jj):
            blocks_fn(tuple(4 * jj + u for u in range(4)))

        rest = (n // 4) * 4

        @pl.when(n % 4 >= 2)
        def _():
            blocks_fn((rest, rest + 1))

        @pl.when(n % 2 == 1)
        def _():
            blocks_fn((n - 1,))

    over_blocks(qi, score_blocks)
    rowi = lax.broadcasted_iota(jnp.int32, (tq, tq), 0)
    coli = lax.broadcasted_iota(jnp.int32, (tq, tq), 1)
    for g in range(n_heads):
        s = jnp.where(coli <= rowi, scores(g, qi), NEG)
        s_scr[g, qi] = s
        fold_max(g, s)
        m_scr[g] = jnp.broadcast_to(jnp.max(m_scr[g], axis=1, keepdims=True), (tq, half))

    over_blocks(qi + 1, prob_blocks)


def _softmax_scratch(n_heads, nq, tq):
    return [pltpu.VMEM((n_heads, nq, tq, tq), f32), pltpu.VMEM((n_heads, tq, tq // 2), f32),
            pltpu.VMEM((n_heads, nq * tq, LANE), f32)]


EPILOGUE_ROWS = 512


def _with_ones_lanes(v, dv):
    return jnp.concatenate([v.astype(bf16), jnp.ones((v.shape[0], LANE - dv), bf16)], axis=1)


def _softmax_normalise(acc, dv):
    lane = lax.broadcasted_iota(jnp.int32, acc.shape, 1)
    return jnp.where(lane < dv, acc / pltpu.roll(acc, LANE - dv, axis=1), 0.0)


def _pair_lanes(a, b, dv):
    lane = lax.broadcasted_iota(jnp.int32, a.shape, 1)
    return jnp.where(lane < dv, a, pltpu.roll(b, dv, axis=1))


def _mla_body(qn_ref, kvn_ref, kpe_ref, wq_ref, wqr_ref, wk_ref, wv_ref, place_ref, cq_ref, sq_ref,
              o_ref, q_scr, k_scr, v_scr, s_scr, m_scr, acc_scr):
    qi = pl.program_id(1)
    tq = s_scr.shape[2]

    @pl.when(qi == 0)
    def _():
        kvn = kvn_ref[...]
        kpe_placed = _dot(kpe_ref[...], place_ref[...])
        for h in range(MLA_HEADS):
            k_scr[h] = (_dot(kvn, wk_ref[h]) + kpe_placed).astype(bf16)
            v_scr[h] = _with_ones_lanes(_dot(kvn, wv_ref[h]), MLA_V)
            for r0 in range(0, qn_ref.shape[0], EPILOGUE_ROWS):
                rows = pl.ds(r0, EPILOGUE_ROWS)
                qn = qn_ref[rows, :]
                q_scr[h, rows, :] = (_dot(qn, wq_ref[h]) * cq_ref[rows, :]
                                     + _dot(qn, wqr_ref[h]) * sq_ref[rows, :]).astype(bf16)

    q_rows = pl.ds(pl.multiple_of(qi * tq, tq), tq)
    q_heads = [q_scr[h, q_rows, :] for h in range(MLA_HEADS)]
    _causal_softmax_heads(
        q_heads, lambda g, start: k_scr[g, pl.ds(start, tq), :],
        lambda g, start: v_scr[g, pl.ds(start, tq), :], qi, s_scr, m_scr, acc_scr)

    @pl.when(qi == pl.num_programs(1) - 1)
    def _():
        for r0 in range(0, o_ref.shape[0], EPILOGUE_ROWS):
            rows = pl.ds(r0, EPILOGUE_ROWS)
            o = [_softmax_normalise(acc_scr[h, rows, :], MLA_V) for h in range(MLA_HEADS)]
            o_ref[rows, :] = jnp.concatenate(
                [_pair_lanes(o[0], o[1], MLA_V), _pair_lanes(o[2], o[3], MLA_V)], axis=1).astype(bf16)


def _mla_attention(qn, kvn, kpe, wq, wqr, wk, wv, place, cq, sq, batch, seq):
    tq = 256
    nq = seq // tq
    const = lambda a: pl.BlockSpec(a.shape, lambda b, i: (0,) * a.ndim)
    return pl.pallas_call(
        _mla_body,
        grid=(batch, nq),
        in_specs=[
            pl.BlockSpec((seq, 256), lambda b, i: (b, 0)),
            pl.BlockSpec((seq, 128), lambda b, i: (b, 0)),
            pl.BlockSpec((seq, 128), lambda b, i: (b, 0)),
            const(wq), const(wqr), const(wk), const(wv), const(place), const(cq), const(sq),
        ],
        out_specs=pl.BlockSpec((seq, GROUP_W), lambda b, i: (b, 0)),
        out_shape=jax.ShapeDtypeStruct((batch * seq, GROUP_W), bf16),
        scratch_shapes=[pltpu.VMEM((MLA_HEADS, seq, 128), bf16),
                        pltpu.VMEM((MLA_HEADS, seq, 128), bf16),
                        pltpu.VMEM((MLA_HEADS, seq, LANE), bf16)]
        + _softmax_scratch(MLA_HEADS, nq, tq),
        compiler_params=_cparams(("arbitrary", "arbitrary")),
        name="mla_attn",
    )(qn, kvn, kpe, wq, wqr, wk, wv, place, cq, sq)


def _diff_body(q_ref, k_ref, v_ref, lq1_ref, lk1_ref, lq2_ref, lk2_ref, g_ref,
               o_ref, v_scr, s_scr, m_scr, acc_scr, *, lambda_init):
    qi = pl.program_id(1)
    tq = q_ref.shape[0]

    @pl.when(qi == 0)
    def _():
        v = v_ref[...]
        for h in range(DF_HEADS):
            v_scr[h] = _with_ones_lanes(v[:, DF_V * h:DF_V * (h + 1)], DF_V)

    q = q_ref[...]
    q_heads = [jnp.where(_lane_group_mask(q.shape, DF_DK * g, DF_DK * (g + 1)), q, jnp.zeros_like(q))
               for g in range(2 * DF_HEADS)]
    _causal_softmax_heads(
        q_heads, lambda g, start: k_ref[pl.ds(start, tq), :],
        lambda g, start: v_scr[g // 2, pl.ds(start, tq), :], qi, s_scr, m_scr, acc_scr)

    @pl.when(qi == pl.num_programs(1) - 1)
    def _():
        lam = (jnp.exp(jnp.sum(lq1_ref[...] * lk1_ref[...], axis=1, keepdims=True))
               - jnp.exp(jnp.sum(lq2_ref[...] * lk2_ref[...], axis=1, keepdims=True)) + lambda_init)
        gain = g_ref[...] * (1.0 - lambda_init)
        for r0 in range(0, o_ref.shape[0], EPILOGUE_ROWS):
            rows = pl.ds(r0, EPILOGUE_ROWS)
            ys = []
            for h in range(DF_HEADS):
                o = (_softmax_normalise(acc_scr[2 * h, rows, :], DF_V)
                     - lam * _softmax_normalise(acc_scr[2 * h + 1, rows, :], DF_V))
                ms = jnp.sum(o * o, axis=1, keepdims=True) * (1.0 / DF_V)
                ys.append(o * lax.rsqrt(ms + RMS_EPS) * gain)
            o_ref[rows, :] = jnp.concatenate(
                [_pair_lanes(ys[0], ys[1], DF_V), _pair_lanes(ys[2], ys[3], DF_V)], axis=1).astype(bf16)


def _diff_attention(dq, dk, dv, lq1, lk1, lq2, lk2, g_sub, batch, seq, lambda_init):
    tq = 256
    nq = seq // tq
    const = lambda a: pl.BlockSpec(a.shape, lambda b, i: (0,) * a.ndim)
    return pl.pallas_call(
        functools.partial(_diff_body, lambda_init=lambda_init),
        grid=(batch, nq),
        in_specs=[
            pl.BlockSpec((tq, 256), lambda b, i: (b * nq + i, 0)),
            pl.BlockSpec((seq, 256), lambda b, i: (b, 0)),
            pl.BlockSpec((seq, 256), lambda b, i: (b, 0)),
            const(lq1), const(lk1), const(lq2), const(lk2), const(g_sub),
        ],
        out_specs=pl.BlockSpec((seq, GROUP_W), lambda b, i: (b, 0)),
        out_shape=jax.ShapeDtypeStruct((batch * seq, GROUP_W), bf16),
        scratch_shapes=[pltpu.VMEM((DF_HEADS, seq, LANE), bf16)]
        + _softmax_scratch(2 * DF_HEADS, nq, tq),
        compiler_params=_cparams(("arbitrary", "arbitrary")),
        name="diff_attn",
    )(dq, dk, dv, lq1, lk1, lq2, lk2, g_sub)


def _split3(a):
    hi = a.astype(bf16)
    r1 = a - hi.astype(f32)
    mid = r1.astype(bf16)
    return hi, mid, (r1 - mid.astype(f32)).astype(bf16)


def _dot_exact_rhs01(a, b01):
    hi, mid, lo = _split3(a)
    return _dot(hi, b01) + _dot(mid, b01) + _dot(lo, b01)


def _dot_exact_lhs01(a01, b):
    hi, mid, lo = _split3(b)
    return _dot(a01, hi) + _dot(a01, mid) + _dot(a01, lo)


def _mlstm_gate_terms(g, eif):
    R, W, L = g.shape[0], eif.shape[1] // 2, ML_CHUNK
    lf = jnp.minimum(g, 0.0) - jnp.log(1.0 + jnp.exp(-jnp.abs(g)))
    glane = lax.broadcasted_iota(jnp.int32, g.shape, 1)
    wide = _dot_exact_rhs01(jnp.where(glane < ML_HEADS, g, lf), eif)
    iw, lfw = wide[:, :W], wide[:, W:]
    rr = lax.broadcasted_iota(jnp.int32, (R, R), 0)
    cc = lax.broadcasted_iota(jnp.int32, (R, R), 1)
    bw = _dot_exact_lhs01(((rr >= cc) & (rr // L == cc // L)).astype(bf16), lfw)
    rw = iw - bw
    pos_in_chunk = lax.broadcasted_iota(jnp.int32, (R, W), 0) % L
    cm = rw
    for sh in (1, 2, 4, 8, 16, 32):
        cm = jnp.maximum(cm, jnp.where(pos_in_chunk >= sh, pltpu.roll(cm, sh, axis=0), NEG))
    return bw, rw, cm


def _mlstm_chunk_local(q, k, kt_bd, v, rw, cm):
    L, W = q.shape
    row = lax.broadcasted_iota(jnp.int32, (L, W), 0)
    pos = lax.broadcasted_iota(jnp.int32, (L, W), 1) % L
    head_eq = (lax.broadcasted_iota(jnp.int32, (W, W), 0) // ML_DH
               == lax.broadcasted_iota(jnp.int32, (W, W), 1) // ML_DH)
    r_row = jnp.sum(jnp.where(pos == row, rw, 0.0), axis=0, keepdims=True)
    v4 = jnp.concatenate([v] * ML_HEADS, axis=0)
    vbd = jnp.where(head_eq, v4, jnp.zeros_like(v4))
    rmax = cm[L - 1:L, :]
    c_loc = _dot((kt_bd * jnp.exp(r_row - rmax)).astype(bf16), vbd)
    n_loc = jnp.sum(k.astype(f32) * jnp.exp(rw - rmax), axis=0, keepdims=True)
    return _dot(q, kt_bd.astype(bf16)), vbd, r_row, c_loc, n_loc


def _mlstm_chunk_step(q, o, bw, cm, qk, vbd, r_row, c_loc, n_loc, c_prev, n_prev, m_prev):
    L, W = q.shape
    row = lax.broadcasted_iota(jnp.int32, (L, W), 0)
    pos = lax.broadcasted_iota(jnp.int32, (L, W), 1) % L
    head_eq = (lax.broadcasted_iota(jnp.int32, (W, W), 0) // ML_DH
               == lax.broadcasted_iota(jnp.int32, (W, W), 1) // ML_DH)
    mw = jnp.maximum(m_prev, cm)
    a_inter = jnp.exp(m_prev - mw)
    pw = qk * jnp.exp(jnp.where(pos <= row, r_row - mw, NEG))
    sums = _dot_exact_rhs01(jnp.concatenate([q.astype(f32) * n_prev, pw], axis=0), head_eq.astype(bf16))
    den = a_inter * sums[:L] + sums[L:]
    hden = jnp.maximum(jnp.abs(den), jnp.exp(-(bw + mw)))
    num = a_inter * _dot(q, c_prev.astype(bf16)) + _dot(pw.astype(bf16), vbd)
    y = num / hden * _sigmoid(o.astype(f32))
    rmax, m_last = cm[L - 1:L, :], mw[L - 1:L, :]
    a_w = jnp.exp(m_prev - m_last)
    b_w = jnp.exp(rmax - m_last)
    return y, c_prev * a_w + c_loc * b_w, a_w * n_prev + b_w * n_loc, bw[L - 1:L, :] + m_last


def _mlstm_body(q_ref, k_ref, kt_ref, v_ref, o_ref, g_ref, eif_ref, sel_ref, y_ref, c_scr, nm_scr):
    @pl.when(pl.program_id(1) == 0)
    def _():
        c_scr[...] = jnp.zeros_like(c_scr)
        nm_scr[...] = jnp.zeros_like(nm_scr)

    c, n, m = c_scr[...], nm_scr[0:1, :], nm_scr[1:2, :]
    kt = kt_ref[...]
    eif = eif_ref[...]
    w = kt.shape[0]
    head_eq = (lax.broadcasted_iota(jnp.int32, (w, w), 0) // ML_DH
               == lax.broadcasted_iota(jnp.int32, (w, w), 1) // ML_DH)
    bw, rw, cm = _mlstm_gate_terms(g_ref[...], eif)
    n_chunks = q_ref.shape[0] // ML_CHUNK
    local = []
    for i in range(n_chunks):
        rows = pl.ds(i * ML_CHUNK, ML_CHUNK)
        lo, hi = i * ML_CHUNK, (i + 1) * ML_CHUNK
        kt_bd = jnp.where(head_eq, _dot(kt, sel_ref[i]), 0.0)
        local.append(_mlstm_chunk_local(q_ref[rows, :], k_ref[rows, :], kt_bd, v_ref[rows, :],
                                        rw[lo:hi], cm[lo:hi]))
    for i in range(n_chunks):
        rows = pl.ds(i * ML_CHUNK, ML_CHUNK)
        lo, hi = i * ML_CHUNK, (i + 1) * ML_CHUNK
        y, c, n, m = _mlstm_chunk_step(q_ref[rows, :], o_ref[rows, :], bw[lo:hi], cm[lo:hi],
                                       *local[i], c, n, m)
        y_ref[rows, :] = y.astype(y_ref.dtype)
    c_scr[...] = c
    nm_scr[0:1, :] = n
    nm_scr[1:2, :] = m


MLSTM_CHUNKS_PER_STEP = 4


def _mlstm(mq, mk, mkt, mv, mo, gcol, batch, seq):
    cps = MLSTM_CHUNKS_PER_STEP
    L = ML_CHUNK * cps
    nc = seq // L
    blk = lambda w: pl.BlockSpec((L, w), lambda b, c: (b * nc + c, 0))
    const = lambda a: pl.BlockSpec(a.shape, lambda b, c: (0,) * a.ndim)
    gl = jnp.arange(LANE)[:, None]
    col = jnp.arange(2 * GROUP_W)[None, :]
    eif = (gl == (col // GROUP_W) * ML_HEADS + (col % GROUP_W) // ML_DH).astype(bf16)
    p = jnp.arange(L)[None, :, None]
    cc = jnp.arange(GROUP_W)[None, None, :]
    sel = (p == jnp.arange(cps)[:, None, None] * ML_CHUNK + cc % ML_CHUNK).astype(bf16)
    return pl.pallas_call(
        _mlstm_body,
        grid=(batch, nc),
        in_specs=[blk(256), blk(256), pl.BlockSpec((GROUP_W, L), lambda b, c: (0, b * nc + c)),
                  blk(256), blk(256), blk(128), const(eif), const(sel)],
        out_specs=blk(GROUP_W),
        out_shape=jax.ShapeDtypeStruct((batch * seq, GROUP_W), bf16),
        scratch_shapes=[pltpu.VMEM((GROUP_W, GROUP_W), f32), pltpu.VMEM((8, GROUP_W), f32)],
        compiler_params=_cparams(("arbitrary", "arbitrary")),
        name="mlstm",
    )(mq, mk, mkt, mv, mo, gcol, eif, sel)


def _route(scores, sel):
    ne, tm = sel.shape
    eid = lax.broadcasted_iota(jnp.int32, (ne, tm), 0)
    grp = eid // E_PER_GROUP
    big = ne + 1

    def top2(vals):
        m1 = jnp.max(vals, axis=0, keepdims=True)
        i1 = jnp.min(jnp.where(vals == m1, eid, big), axis=0, keepdims=True)
        rest = jnp.where(eid == i1, NEG, vals)
        m2 = jnp.max(rest, axis=0, keepdims=True)
        i2 = jnp.min(jnp.where(rest == m2, eid, big), axis=0, keepdims=True)
        return m1, i1, m2, i2

    best_score, best_grp = None, None
    for gi in range(N_GROUPS):
        m1, _, m2, _ = top2(jnp.where(grp == gi, sel, NEG))
        sc = m1 + m2
        if best_score is None:
            best_score, best_grp = sc, jnp.zeros_like(sc, dtype=jnp.int32)
        else:
            better = sc > best_score
            best_grp = jnp.where(better, gi, best_grp)
            best_score = jnp.where(better, sc, best_score)
    _, i1, _, i2 = top2(jnp.where(grp == best_grp, sel, NEG))
    picked = jnp.where((eid == i1) | (eid == i2), scores, 0.0)
    return picked / jnp.sum(picked, axis=0, keepdims=True), best_grp


MOE_ROW_TILE = 256


def _outproj_route_body(ya_ref, yb_ref, yc_ref, yd_ref, x_ref, mod_ref, w_ref, g_ref, b_ref,
                        wr_ref, br_ref, x1_ref, u2e_ref, grp_ref):
    d = x_ref.shape[1]
    mix = None
    for j, y_ref in enumerate((ya_ref, yb_ref, yc_ref, yd_ref)):
        part = _dot(y_ref[...], w_ref[GROUP_W * j:GROUP_W * (j + 1), :])
        mix = part if mix is None else mix + part
    x1 = _layer_norm(DN_ALPHA * x_ref[...] + mod_ref[0, 2:3, :] * mix) * g_ref[...] + b_ref[...]
    x1_ref[...] = x1
    u = _layer_norm(x1) * (1.0 + mod_ref[0, 4:5, :]) + mod_ref[0, 3:4, :]
    u_hi = u.astype(bf16)
    u_lo = (u - u_hi.astype(f32)).astype(bf16)
    ne = br_ref.shape[0]
    part = _dot_nt(wr_ref[...], u_hi)
    logits = part[:ne] + part[ne:] + _dot_nt(wr_ref[:ne, :], u_lo)
    scores = _sigmoid(logits)
    gates_t, best_grp = _route(scores, scores + br_ref[...])
    u2e_ref[:, :d] = u_hi.astype(f32)
    u2e_ref[:, d:] = jnp.zeros((x1.shape[0], LANE), f32)
    u2e_ref[:, d:d + ne] = gates_t.T
    grp_ref[...] = best_grp


def _outproj_route(ya, yb, yc, yd, x2d, mod, w_out, ln_g, ln_b, wr_split, b_router, seq):
    t, d = x2d.shape
    tm = 1024
    tps = seq // tm
    row = lambda w: pl.BlockSpec((tm, w), lambda i: (i, 0))
    const = lambda a: pl.BlockSpec(a.shape, lambda i: (0,) * a.ndim)
    return pl.pallas_call(
        _outproj_route_body,
        grid=(t // tm,),
        in_specs=[row(256), row(256), row(256), row(256), row(d),
                  pl.BlockSpec((1, 6, d), lambda i: (i // tps, 0, 0)),
                  const(w_out), const(ln_g), const(ln_b), const(wr_split), const(b_router)],
        out_specs=[row(d), row(d + LANE), pl.BlockSpec((1, tm), lambda i: (0, i))],
        out_shape=[jax.ShapeDtypeStruct((t, d), f32), jax.ShapeDtypeStruct((t, d + LANE), f32),
                   jax.ShapeDtypeStruct((1, t), jnp.int32)],
        compiler_params=_cparams(("arbitrary",)),
        name="outproj_route",
    )(ya, yb, yc, yd, x2d, mod, w_out, ln_g, ln_b, wr_split, b_router)


def _dispatch_plan(grp, tr):
    t = grp.shape[0]
    rows = t + N_GROUPS * tr
    order = jnp.sort(grp * t + jnp.arange(t, dtype=jnp.int32)) % t
    order = jnp.concatenate([order, jnp.zeros((rows - t,), order.dtype)])
    onehot = (grp[:, None] == jnp.arange(N_GROUPS)[None, :]).astype(jnp.int32)
    counts = jnp.sum(onehot, axis=0)
    padded = ((counts + tr - 1) // tr) * tr
    row_end = jnp.cumsum(padded)
    rank = jnp.sum((jnp.cumsum(onehot, axis=0) - 1) * onehot, axis=1)
    row_of_tok = (row_end - padded)[grp] + rank
    r = jnp.arange(rows, dtype=jnp.int32)
    tok_of_row = jnp.zeros((rows,), jnp.int32)
    for g in range(N_GROUPS):
        row_start = row_end[g] - padded[g]
        tok_start = jnp.sum(counts[:g])
        shifted = jnp.roll(order, row_start - tok_start)
        in_grp = (r >= row_start) & (r < row_end[g])
        valid = in_grp & (r < row_start + counts[g])
        tok_of_row = jnp.where(valid, shifted,
                               jnp.where(in_grp, order[jnp.minimum(tok_start, t - 1)], tok_of_row))
    tile_start = jnp.arange(rows // tr, dtype=jnp.int32) * tr
    tile_grp = jnp.minimum(jnp.sum((tile_start[:, None] >= row_end[None, :]).astype(jnp.int32), axis=1),
                           N_GROUPS - 1)
    i32 = lambda a: a.astype(jnp.int32)
    return i32(tok_of_row), i32(row_of_tok), i32(tile_grp), i32(row_end[-1:] // tr)


def _wait_row_gather(n_rows, hbm, vmem, sem):
    pltpu.make_async_copy(hbm.at[pl.ds(0, n_rows), :], vmem, sem).wait()


def _start_row_gather(idx_ref, first, n_rows, hbm, vmem, sem):
    for i in range(n_rows):
        pltpu.make_async_copy(hbm.at[pl.ds(idx_ref[first + i], 1), :], vmem.at[pl.ds(i, 1), :],
                              sem).start(priority=i % 2)


def _experts_body(tok_ref, tgrp_ref, ntile_ref, u2e_hbm, w1_ref, w3_ref, w2_ref, o_ref,
                  xbuf0, xbuf1, w1b, w3b, w2b, gsem):
    j = pl.program_id(0)
    n_tiles = ntile_ref[0]
    tr, d = o_ref.shape
    xbuf = (xbuf0, xbuf1)

    def compute(s):
        xs = xbuf[s][...]
        xb = xs[:, :d].astype(bf16)
        gslab = xs[:, d:]
        lane = lax.broadcasted_iota(jnp.int32, gslab.shape, 1)
        first_expert = tgrp_ref[j] * E_PER_GROUP
        acc = None
        for e in range(E_PER_GROUP):
            ge = jnp.sum(jnp.where(lane == first_expert + e, gslab, 0.0), axis=1, keepdims=True)
            h1 = _dot(xb, w1b[e])
            h = h1 * _sigmoid(h1) * _dot(xb, w3b[e]) * ge
            part = _dot(h.astype(bf16), w2b[e])
            acc = part if acc is None else acc + part
        o_ref[...] = acc

    @pl.when((j < n_tiles) & ((j == 0) | (tgrp_ref[j] != tgrp_ref[jnp.maximum(j - 1, 0)])))
    def _():
        w1b[...] = w1_ref[0].astype(bf16)
        w3b[...] = w3_ref[0].astype(bf16)
        w2b[...] = w2_ref[0].astype(bf16)

    @pl.when(j == 0)
    def _():
        _start_row_gather(tok_ref, 0, tr, u2e_hbm, xbuf[0], gsem.at[0])

    for s in (0, 1):
        @pl.when((j + 1 < n_tiles) & (j % 2 == s))
        def _(s=s):
            _start_row_gather(tok_ref, (j + 1) * tr, tr, u2e_hbm, xbuf[1 - s], gsem.at[1 - s])

        @pl.when((j < n_tiles) & (j % 2 == s))
        def _(s=s):
            _wait_row_gather(tr, u2e_hbm, xbuf[s], gsem.at[s])
            compute(s)

    @pl.when(j >= n_tiles)
    def _():
        o_ref[...] = jnp.zeros_like(o_ref)


def _experts(tok_of_row, tile_grp, n_tiles, u2e, w1, w3, w2, layer):
    rows = tok_of_row.shape[0]
    tr = MOE_ROW_TILE
    _, ne, d, fe = w1.shape
    grouped = lambda w: w.reshape(w.shape[0], N_GROUPS, E_PER_GROUP, *w.shape[2:])
    wspec = lambda a, b: pl.BlockSpec((None, 1, E_PER_GROUP, a, b), lambda j, tok, tg, nt: (layer, tg[j], 0, 0, 0))
    return pl.pallas_call(
        _experts_body,
        grid_spec=pltpu.PrefetchScalarGridSpec(
            num_scalar_prefetch=3, grid=(rows // tr,),
            in_specs=[pl.BlockSpec(memory_space=pl.ANY), wspec(d, fe), wspec(d, fe), wspec(fe, d)],
            out_specs=pl.BlockSpec((tr, d), lambda j, tok, tg, nt: (j, 0)),
            scratch_shapes=[pltpu.VMEM((tr, u2e.shape[1]), f32), pltpu.VMEM((tr, u2e.shape[1]), f32),
                            pltpu.VMEM((E_PER_GROUP, d, fe), bf16), pltpu.VMEM((E_PER_GROUP, d, fe), bf16),
                            pltpu.VMEM((E_PER_GROUP, fe, d), bf16),
                            pltpu.SemaphoreType.DMA((2,))]),
        out_shape=jax.ShapeDtypeStruct((rows, d), f32),
        compiler_params=_cparams(("arbitrary",)),
        name="experts",
    )(tok_of_row, tile_grp, n_tiles, u2e, grouped(w1), grouped(w3), grouped(w2))


def _moe_combine_body(row_ref, ys_hbm, x_ref, mod_ref, g_ref, b_ref, o_ref, ybuf0, ybuf1, sem):
    i = pl.program_id(0)
    tm = x_ref.shape[0]
    ybuf = (ybuf0, ybuf1)

    @pl.when(i == 0)
    def _():
        _start_row_gather(row_ref, 0, tm, ys_hbm, ybuf[0], sem.at[0])

    for s in (0, 1):
        @pl.when((i + 1 < pl.num_programs(0)) & (i % 2 == s))
        def _(s=s):
            _start_row_gather(row_ref, (i + 1) * tm, tm, ys_hbm, ybuf[1 - s], sem.at[1 - s])

        @pl.when(i % 2 == s)
        def _(s=s):
            _wait_row_gather(tm, ys_hbm, ybuf[s], sem.at[s])
            r = DN_ALPHA * x_ref[...] + mod_ref[0, 5:6, :] * ybuf[s][...]
            o_ref[...] = _layer_norm(r) * g_ref[...] + b_ref[...]


def _moe_combine(row_of_tok, ys, x1, mod, ln_g, ln_b, seq):
    t, d = x1.shape
    tm = 256
    tps = seq // tm
    const = lambda a: pl.BlockSpec(a.shape, lambda i, rows: (0,) * a.ndim)
    return pl.pallas_call(
        _moe_combine_body,
        grid_spec=pltpu.PrefetchScalarGridSpec(
            num_scalar_prefetch=1, grid=(t // tm,),
            in_specs=[pl.BlockSpec(memory_space=pl.ANY),
                      pl.BlockSpec((tm, d), lambda i, rows: (i, 0)),
                      pl.BlockSpec((1, 6, d), lambda i, rows: (i // tps, 0, 0)),
                      const(ln_g), const(ln_b)],
            out_specs=pl.BlockSpec((tm, d), lambda i, rows: (i, 0)),
            scratch_shapes=[pltpu.VMEM((tm, d), f32), pltpu.VMEM((tm, d), f32),
                            pltpu.SemaphoreType.DMA((2,))]),
        out_shape=jax.ShapeDtypeStruct((t, d), f32),
        compiler_params=_cparams(("arbitrary",)),
        name="moe_combine",
    )(row_of_tok, ys, x1, mod, ln_g, ln_b)


def _pad_cols(a, width):
    return jnp.pad(a, ((0, 0), (0, width - a.shape[1])))


def _pack_w_in(w):
    parts, acc = [], 0
    for sz in IN_SIZES:
        parts.append(w[:, acc:acc + sz])
        acc += sz
    (a_q, a_kv, a_kr, m_q, m_k, m_v, m_i, m_f, m_o, c_b, c_c, c_h, d_q, d_k, d_v) = parts
    gates = _pad_cols(jnp.concatenate([m_i, m_f], axis=1), LANE)
    packed = jnp.concatenate([
        _pad_cols(a_q, 256), a_kv, _pad_cols(a_kr, LANE),
        m_q, m_k, m_v, gates, m_o, c_b, c_c, c_h, d_q, d_k, d_v], axis=1)
    return packed.astype(bf16)


def _pack_mla(w_uq, w_ukv):
    half = MLA_ROPE // 2
    wq3 = w_uq.reshape(MLA_Q_LORA, MLA_HEADS, MLA_NOPE + MLA_ROPE).transpose(1, 0, 2)
    wkv3 = w_ukv.reshape(MLA_KV_LORA, MLA_HEADS, MLA_NOPE + MLA_V).transpose(1, 0, 2)
    nope, pe = wq3[:, :, :MLA_NOPE], wq3[:, :, MLA_NOPE:]
    rot = jnp.concatenate([-pe[:, :, half:], pe[:, :, :half]], axis=2)
    pad = lambda a: jnp.pad(a, ((0, 0), (0, 256 - a.shape[1]), (0, LANE - a.shape[2]))).astype(bf16)
    wq = pad(wq3)
    wqr = pad(jnp.concatenate([jnp.zeros_like(nope), rot], axis=2))
    wk = jnp.pad(wkv3[:, :, :MLA_NOPE], ((0, 0), (0, 0), (0, LANE - MLA_NOPE))).astype(bf16)
    return wq, wqr, wk, wkv3[:, :, MLA_NOPE:].astype(bf16)


def _rope_tables(seq):
    half = DF_DK // 2
    inv = 1.0 / (ROPE_THETA ** (jnp.arange(0, DF_DK, 2, dtype=f32) / DF_DK))
    ang = jnp.arange(seq, dtype=f32)[:, None] * inv[None, :]
    cos, sin = jnp.cos(ang), jnp.sin(ang)
    cos_t = jnp.tile(jnp.concatenate([cos, cos], axis=1), (1, 256 // DF_DK))
    sin_t = jnp.tile(jnp.concatenate([-sin, sin], axis=1), (1, 256 // DF_DK))
    scale = (MLA_NOPE + MLA_ROPE) ** -0.5 * LOG2E
    ones = jnp.ones((seq, MLA_NOPE), f32)
    zeros = jnp.zeros((seq, MLA_NOPE), f32)
    cq = _pad_cols(jnp.concatenate([ones, cos, cos], axis=1) * scale, LANE)
    sq = _pad_cols(jnp.concatenate([zeros, sin, sin], axis=1) * scale, LANE)
    return cos_t, sin_t, cq, sq


def kernel(x, c, w_ada, b_ada, w_in, mla_g_q, mla_g_kv, mla_w_uq, mla_w_ukv, ml_b_i, ml_b_f, sc_w,
           df_lq1, df_lk1, df_lq2, df_lk2, df_g, w_out, ln1_g, ln1_b, w_router, b_router, w1, w3, w2,
           ln2_g, ln2_b):
    batch, seq, d = x.shape
    depth = w_in.shape[0]
    assert MLA_ROPE == DF_DK, "both rotary blocks share one table"
    cos_t, sin_t, cq, sq = _rope_tables(seq)
    place = jnp.zeros((LANE, LANE), f32).at[jnp.arange(MLA_ROPE), MLA_NOPE + jnp.arange(MLA_ROPE)].set(1.0).astype(bf16)
    wr_hi = w_router.astype(bf16)
    wr_split = jnp.concatenate([wr_hi, (w_router - wr_hi.astype(f32)).astype(bf16)], axis=1).T
    mod_all = _ada_mod(c, w_ada, b_ada).reshape(depth, batch, 6, d)
    xf = x.reshape(batch * seq, d)
    for l in range(depth):
        mod = mod_all[l]
        w_packed = _pack_w_in(w_in[l])
        gq = _pad_cols(mla_g_q[l][None, :], 256)
        gkv = mla_g_kv[l][None, :]
        gbias = _pad_cols(jnp.concatenate([ml_b_i[l], ml_b_f[l]])[None, :], LANE)
        wkt = w_packed[:, OFF_MK:OFF_MK + GROUP_W].T
        (qn, kvn, kpe, mq, mk, mv, mo, gcol, yc, dq, dk, dv, mkt) = _inproj(
            xf, mod, w_packed, wkt, cos_t, sin_t, gq, gkv, gbias, sc_w[l], seq)
        wq, wqr, wk, wv = _pack_mla(mla_w_uq[l], mla_w_ukv[l])
        ya = _mla_attention(qn, kvn, kpe, wq, wqr, wk, wv, place, cq, sq, batch, seq)
        yb = _mlstm(mq, mk, mkt, mv, mo, gcol, batch, seq)
        lambda_init = 0.8 - 0.6 * math.exp(-0.3 * l)
        yd = _diff_attention(dq, dk, dv, df_lq1[l][None, :], df_lk1[l][None, :], df_lq2[l][None, :],
                             df_lk2[l][None, :], _pad_cols(df_g[l][None, :], LANE),
                             batch, seq, lambda_init)
        x1, u2e, grp = _outproj_route(ya, yb, yc, yd, xf, mod, w_out[l].astype(bf16), ln1_g[l][None, :],
                                      ln1_b[l][None, :], wr_split, b_router[:, None], seq)
        tok_of_row, row_of_tok, tile_grp, n_tiles = _dispatch_plan(grp[0], MOE_ROW_TILE)
        ys = _experts(tok_of_row, tile_grp, n_tiles, u2e, w1, w3, w2, l)
        xf = _moe_combine(row_of_tok, ys, x1, mod, ln2_g[l][None, :], ln2_b[l][None, :], seq)
    return xf.reshape(batch, seq, d)
```

```python
import functools
import math

import jax
import jax.numpy as jnp
from jax import lax
from jax.experimental import pallas as pl
from jax.experimental.pallas import tpu as pltpu

f32 = jnp.float32
bf16 = jnp.bfloat16
HIGHEST = lax.Precision.HIGHEST

GROUP_W = 256
MLA_HEADS, MLA_NOPE, MLA_ROPE, MLA_V = 4, 64, 32, 64
MLA_Q_LORA, MLA_KV_LORA = 192, 128
ML_HEADS, ML_DH, ML_CHUNK = 4, 64, 64
DF_HEADS, DF_DK, DF_V = 4, 32, 64
ROPE_THETA = 10000.0
N_EXPERTS, N_GROUPS, E_PER_GROUP, D_EXPERT = 16, 4, 4, 256
DEPTH = 2
DN_ALPHA = (2 * DEPTH) ** 0.25
LN_EPS = 1e-5
RMS_EPS = 1e-6
IN_SIZES = (192, 128, 32, 256, 256, 256, 4, 4, 256, 256, 256, 256, 256, 256, 256)

OFF_PQ, OFF_PKV, OFF_KR = 0, 256, 384
OFF_MQ, OFF_MK, OFF_MV, OFF_MG, OFF_MO = 512, 768, 1024, 1280, 1408
OFF_CB, OFF_CC, OFF_CH = 1664, 1920, 2176
OFF_DQ, OFF_DK, OFF_DV = 2432, 2688, 2944
IN_PACKED = 3200

LANE = 128
VMEM_LIMIT = 48 * 1024 * 1024
NEG = -1e30
LOG2E = math.log2(math.e)


def _cparams(sem):
    return pltpu.CompilerParams(dimension_semantics=sem, vmem_limit_bytes=VMEM_LIMIT)


def _sigmoid(x):
    return 1.0 / (1.0 + jnp.exp(-x))


def _layer_norm(x):
    mu = jnp.mean(x, axis=-1, keepdims=True)
    xc = x - mu
    var = jnp.mean(xc * xc, axis=-1, keepdims=True)
    return xc * lax.rsqrt(var + LN_EPS)


def _dot(a, b, **kw):
    return jnp.dot(a, b, preferred_element_type=f32, **kw)


def _dot_nt(a, b, **kw):
    return lax.dot_general(a, b, (((1,), (1,)), ((), ())), preferred_element_type=f32, **kw)


def _dot_tn(a, b, **kw):
    return lax.dot_general(a, b, (((0,), (0,)), ((), ())), preferred_element_type=f32, **kw)


def _expand_groups(cols, rows, width, group):
    lane = lax.broadcasted_iota(jnp.int32, (rows, width), 1)
    out = jnp.broadcast_to(cols[-1], (rows, width))
    for h in range(len(cols) - 2, -1, -1):
        out = jnp.where(lane < group * (h + 1), cols[h], out)
    return out


def _lane_group_mask(shape, lo, hi):
    lane = lax.broadcasted_iota(jnp.int32, shape, 1)
    return (lane >= lo) & (lane < hi)


def _ada_body(c_ref, w_ref, b_ref, o_ref):
    c = c_ref[...]
    ca = (c * _sigmoid(c)).astype(bf16)
    o_ref[0] = _dot(ca, w_ref[0].astype(bf16)) + b_ref[0]


def _ada_mod(c, w_ada, b_ada):
    depth, d, n = w_ada.shape
    b = c.shape[0]
    tn = 1536
    return pl.pallas_call(
        _ada_body,
        grid=(depth, n // tn),
        in_specs=[
            pl.BlockSpec((b, d), lambda l, j: (0, 0)),
            pl.BlockSpec((1, d, tn), lambda l, j: (l, 0, j)),
            pl.BlockSpec((1, 1, tn), lambda l, j: (l, 0, j)),
        ],
        out_specs=pl.BlockSpec((1, b, tn), lambda l, j: (l, 0, j)),
        out_shape=jax.ShapeDtypeStruct((depth, b, n), f32),
        compiler_params=_cparams(("arbitrary", "arbitrary")),
        name="ada_mod",
    )(c, w_ada, b_ada.reshape(depth, 1, n))


def _rope_lanes(x, cos, sin_signed):
    w = x.shape[1]
    lane = lax.broadcasted_iota(jnp.int32, x.shape, 1)
    rot = jnp.where(lane % 32 < 16, pltpu.roll(x, w - 16, axis=1), pltpu.roll(x, 16, axis=1))
    return x * cos + rot * sin_signed


def _inproj_body(x_ref, mod_ref, w_ref, wkt_ref, cos_ref, sin_ref, gq_ref, gkv_ref, gb_ref, cw_ref,
                 qn_ref, kvn_ref, kpe_ref, mq_ref, mk_ref, mv_ref, mo_ref, gcol_ref, yc_ref,
                 dq_ref, dk_ref, dv_ref, mkt_ref, carry_ref, *, tiles_per_seq):
    i = pl.program_id(0)
    tm = x_ref.shape[0]
    u = _layer_norm(x_ref[...]) * (1.0 + mod_ref[0, 1:2, :]) + mod_ref[0, 0:1, :]
    ub = u.astype(bf16)

    def seg(off, n):
        return _dot(ub, w_ref[:, off:off + n])

    cos = cos_ref[...]
    sin = sin_ref[...]

    pq = seg(OFF_PQ, 256)
    ms = jnp.sum(pq * pq, axis=-1, keepdims=True) * (1.0 / MLA_Q_LORA)
    qn_ref[...] = (pq * lax.rsqrt(ms + RMS_EPS) * gq_ref[...]).astype(bf16)
    pkv = seg(OFF_PKV, 128)
    ms = jnp.sum(pkv * pkv, axis=-1, keepdims=True) * (1.0 / MLA_KV_LORA)
    kvn_ref[...] = (pkv * lax.rsqrt(ms + RMS_EPS) * gkv_ref[...]).astype(bf16)
    kpe_ref[...] = _rope_lanes(seg(OFF_KR, 128), cos[:, :LANE], sin[:, :LANE]).astype(bf16)

    mq_ref[...] = (seg(OFF_MQ, 256) * (ML_DH ** -0.5)).astype(bf16)
    mk_ref[...] = seg(OFF_MK, 256).astype(bf16)
    mkt_ref[...] = _dot_nt(wkt_ref[...], ub).astype(bf16)
    mv_ref[...] = seg(OFF_MV, 256).astype(bf16)
    gcol_ref[...] = seg(OFF_MG, 128) + gb_ref[...]
    mo_ref[...] = seg(OFF_MO, 256).astype(bf16)

    uc = seg(OFF_CC, 256) * seg(OFF_CH, 256)

    @pl.when(i % tiles_per_seq == 0)
    def _():
        carry_ref[...] = jnp.zeros_like(carry_ref)

    prev = carry_ref[...]
    row =lax.broadcasted_iota(jnp.int32, uc.shape, 0)
    u1 = jnp.where(row == 0, prev[7:8, :], pltpu.roll(uc, 1, axis=0))
    u2 = jnp.where(row == 0, prev[6:7, :], jnp.where(row == 1, prev[7:8, :], pltpu.roll(uc, 2, axis=0)))
    carry_ref[...] = uc[tm - 8:, :]
    conv = cw_ref[0:1, :] * u2 + cw_ref[1:2, :] * u1 + cw_ref[2:3, :] * uc
    yc_ref[...] = (seg(OFF_CB, 256) * conv).astype(bf16)

    dq_ref[...] = (_rope_lanes(seg(OFF_DQ, 256), cos, sin) * (DF_DK ** -0.5 * LOG2E)).astype(bf16)
    dk_ref[...] = _rope_lanes(seg(OFF_DK, 256), cos, sin).astype(bf16)
    dv_ref[...] = seg(OFF_DV, 256).astype(bf16)


def _inproj(x2d, mod, w_packed, wkt, cos_t, sin_t, gq, gkv, gbias, conv_w, seq):
    t, d = x2d.shape
    tm = 1024
    tps = seq // tm
    row = lambda w: pl.BlockSpec((tm, w), lambda i: (i, 0))
    const = lambda a: pl.BlockSpec(a.shape, lambda i: (0,) * a.ndim)
    widths = (256, 128, 128, 256, 256, 256, 256, 128, 256, 256, 256, 256)
    dtypes = (bf16, bf16, bf16, bf16, bf16, bf16, bf16, f32, bf16, bf16, bf16, bf16)
    return pl.pallas_call(
        functools.partial(_inproj_body, tiles_per_seq=tps),
        grid=(t // tm,),
        in_specs=[
            row(d),
            pl.BlockSpec((1, 6, d), lambda i: (i // tps, 0, 0)),
            const(w_packed), const(wkt),
            pl.BlockSpec((tm, 256), lambda i: (i % tps, 0)),
            pl.BlockSpec((tm, 256), lambda i: (i % tps, 0)),
            const(gq), const(gkv), const(gbias), const(conv_w),
        ],
        out_specs=[row(w) for w in widths] + [pl.BlockSpec((GROUP_W, tm), lambda i: (0, i))],
        out_shape=[jax.ShapeDtypeStruct((t, w), dt) for w, dt in zip(widths, dtypes)]
        + [jax.ShapeDtypeStruct((GROUP_W, t), bf16)],
        scratch_shapes=[pltpu.VMEM((8, 256), f32)],
        compiler_params=pltpu.CompilerParams(dimension_semantics=("arbitrary",), vmem_limit_bytes=VMEM_LIMIT,
                                             allow_input_fusion=[i in (2, 3) for i in range(11)]),
        name="inproj",
    )(x2d, mod, w_packed, wkt, cos_t, sin_t, gq, gkv, gbias, conv_w)


def _causal_softmax_heads(q_heads, k_at, v_at, qi, s_scr, m_scr, acc_scr):
    n_heads = len(q_heads)
    tq = q_heads[0].shape[0]
    half = tq // 2
    rows = pl.ds(pl.multiple_of(qi * tq, tq), tq)

    def scores(g, j):
        return _dot_nt(q_heads[g], k_at(g, pl.multiple_of(j * tq, tq)))

    def fold_max(g, s):
        m_scr[g] = jnp.maximum(m_scr[g], jnp.maximum(s[:, :half], s[:, half:]))

    m_scr[...] = jnp.full(m_scr.shape, NEG, f32)
    for g in range(n_heads):
        acc_scr[g, rows, :] = jnp.zeros((tq, LANE), f32)

    def score_blocks(js):
        for g in range(n_heads):
            mx = None
            for j in js:
                s = scores(g, j)
                s_scr[g, j] = s
                s = jnp.maximum(s[:, :half], s[:, half:])
                mx = s if mx is None else jnp.maximum(mx, s)
            m_scr[g] = jnp.maximum(m_scr[g], mx)

    def prob_blocks(js):
        for g in range(n_heads):
            mb = m_scr[g]
            pv = None
            for j in js:
                sj = s_scr[g, j]
                p = jnp.concatenate([jnp.exp2(sj[:, :half] - mb), jnp.exp2(sj[:, half:] - mb)],
                                    axis=1).astype(bf16)
                d = _dot(p, v_at(g, pl.multiple_of(j * tq, tq)))
                pv = d if pv is None else pv + d
            acc_scr[g, rows, :] += pv

    def over_blocks(n, blocks_fn):
        @pl.loop(0, n // 4)
        def _(jj):
            blocks_fn(tuple(4 * jj + u for u in range(4)))

        rest = (n // 4) * 4

        @pl.when(n % 4 >= 2)
        def _():
            blocks_fn((rest, rest + 1))

        @pl.when(n % 2 == 1)
        def _():
            blocks_fn((n - 1,))

    over_blocks(qi, score_blocks)
    rowi = lax.broadcasted_iota(jnp.int32, (tq, tq), 0)
    coli = lax.broadcasted_iota(jnp.int32, (tq, tq), 1)
    for g in range(n_heads):
        s = jnp.where(coli <= rowi, scores(g, qi), NEG)
        s_scr[g, qi] = s
        fold_max(g, s)
        m_scr[g] = jnp.broadcast_to(jnp.max(m_scr[g], axis=1, keepdims=True), (tq, half))

    over_blocks(qi + 1, prob_blocks)


def _softmax_scratch(n_heads, nq, tq):
    return [pltpu.VMEM((n_heads, nq, tq, tq), f32), pltpu.VMEM((n_heads, tq, tq // 2), f32),
            pltpu.VMEM((n_heads, nq * tq, LANE), f32)]


EPILOGUE_ROWS = 512


def _with_ones_lanes(v, dv):
    return jnp.concatenate([v.astype(bf16), jnp.ones((v.shape[0], LANE - dv), bf16)], axis=1)


def _softmax_normalise(acc, dv):
    lane = lax.broadcasted_iota(jnp.int32, acc.shape, 1)
    return jnp.where(lane < dv, acc / pltpu.roll(acc, LANE - dv, axis=1), 0.0)


def _pair_lanes(a, b, dv):
    lane = lax.broadcasted_iota(jnp.int32, a.shape, 1)
    return jnp.where(lane < dv, a, pltpu.roll(b, dv, axis=1))


def _mla_body(qn_ref, kvn_ref, kpe_ref, wq_ref, wqr_ref, wk_ref, wv_ref, place_ref, cq_ref, sq_ref,
              o_ref, q_scr, k_scr, v_scr, s_scr, m_scr, acc_scr):
    qi = pl.program_id(1)
    tq = s_scr.shape[2]

    @pl.when(qi == 0)
    def _():
        kvn = kvn_ref[...]
        kpe_placed = _dot(kpe_ref[...], place_ref[...])
        for h in range(MLA_HEADS):
            k_scr[h] = (_dot(kvn, wk_ref[h]) + kpe_placed).astype(bf16)
            v_scr[h] = _with_ones_lanes(_dot(kvn, wv_ref[h]), MLA_V)
            for r0 in range(0, qn_ref.shape[0], EPILOGUE_ROWS):
                rows = pl.ds(r0, EPILOGUE_ROWS)
                qn = qn_ref[rows, :]
                q_scr[h, rows, :] = (_dot(qn, wq_ref[h]) * cq_ref[rows, :]
                                     + _dot(qn, wqr_ref[h]) * sq_ref[rows, :]).astype(bf16)

    q_rows = pl.ds(pl.multiple_of(qi * tq, tq), tq)
    q_heads = [q_scr[h, q_rows, :] for h in range(MLA_HEADS)]
    _causal_softmax_heads(
        q_heads, lambda g, start: k_scr[g, pl.ds(start, tq), :],
        lambda g, start: v_scr[g, pl.ds(start, tq), :], qi, s_scr, m_scr, acc_scr)

    @pl.when(qi == pl.num_programs(1) - 1)
    def _():
        for r0 in range(0, o_ref.shape[0], EPILOGUE_ROWS):
            rows = pl.ds(r0, EPILOGUE_ROWS)
            o = [_softmax_normalise(acc_scr[h, rows, :], MLA_V) for h in range(MLA_HEADS)]
            o_ref[rows, :] = jnp.concatenate(
                [_pair_lanes(o[0], o[1], MLA_V), _pair_lanes(o[2], o[3], MLA_V)], axis=1).astype(bf16)


def _mla_attention(qn, kvn, kpe, wq, wqr, wk, wv, place, cq, sq, batch, seq):
    tq = 256
    nq = seq // tq
    const = lambda a: pl.BlockSpec(a.shape, lambda b, i: (0,) * a.ndim)
    return pl.pallas_call(
        _mla_body,
        grid=(batch, nq),
        in_specs=[
            pl.BlockSpec((seq, 256), lambda b, i: (b, 0)),
            pl.BlockSpec((seq, 128), lambda b, i: (b, 0)),
            pl.BlockSpec((seq, 128), lambda b, i: (b, 0)),
            const(wq), const(wqr), const(wk), const(wv), const(place), const(cq), const(sq),
        ],
        out_specs=pl.BlockSpec((seq, GROUP_W), lambda b, i: (b, 0)),
        out_shape=jax.ShapeDtypeStruct((batch * seq, GROUP_W), bf16),
        scratch_shapes=[pltpu.VMEM((MLA_HEADS, seq, 128), bf16),
                        pltpu.VMEM((MLA_HEADS, seq, 128), bf16),
                        pltpu.VMEM((MLA_HEADS, seq, LANE), bf16)]
        + _softmax_scratch(MLA_HEADS, nq, tq),
        compiler_params=_cparams(("arbitrary", "arbitrary")),
        name="mla_attn",
    )(qn, kvn, kpe, wq, wqr, wk, wv, place, cq, sq)


def _diff_body(q_ref, k_ref, v_ref, lq1_ref, lk1_ref, lq2_ref, lk2_ref, g_ref,
               o_ref, v_scr, s_scr, m_scr, acc_scr, *, lambda_init):
    qi = pl.program_id(1)
    tq = q_ref.shape[0]

    @pl.when(qi == 0)
    def _():
        v = v_ref[...]
        for h in range(DF_HEADS):
            v_scr[h] = _with_ones_lanes(v[:, DF_V * h:DF_V * (h + 1)], DF_V)

    q = q_ref[...]
    q_heads = [jnp.where(_lane_group_mask(q.shape, DF_DK * g, DF_DK * (g + 1)), q, jnp.zeros_like(q))
               for g in range(2 * DF_HEADS)]
    _causal_softmax_heads(
        q_heads, lambda g, start: k_ref[pl.ds(start, tq), :],
        lambda g, start: v_scr[g // 2, pl.ds(start, tq), :], qi, s_scr, m_scr, acc_scr)

    @pl.when(qi == pl.num_programs(1) - 1)
    def _():
        lam = (jnp.exp(jnp.sum(lq1_ref[...] * lk1_ref[...], axis=1, keepdims=True))
               - jnp.exp(jnp.sum(lq2_ref[...] * lk2_ref[...], axis=1, keepdims=True)) + lambda_init)
        gain = g_ref[...] * (1.0 - lambda_init)
        for r0 in range(0, o_ref.shape[0], EPILOGUE_ROWS):
            rows = pl.ds(r0, EPILOGUE_ROWS)
            ys = []
            for h in range(DF_HEADS):
                o = (_softmax_normalise(acc_scr[2 * h, rows, :], DF_V)
                     - lam * _softmax_normalise(acc_scr[2 * h + 1, rows, :], DF_V))
                ms = jnp.sum(o * o, axis=1, keepdims=True) * (1.0 / DF_V)
                ys.append(o * lax.rsqrt(ms + RMS_EPS) * gain)
            o_ref[rows, :] = jnp.concatenate(
                [_pair_lanes(ys[0], ys[1], DF_V), _pair_lanes(ys[2], ys[3], DF_V)], axis=1).astype(bf16)


def _diff_attention(dq, dk, dv, lq1, lk1, lq2, lk2, g_sub, batch, seq, lambda_init):
    tq = 256
    nq = seq // tq
    const = lambda a: pl.BlockSpec(a.shape, lambda b, i: (0,) * a.ndim)
    return pl.pallas_call(
        functools.partial(_diff_body, lambda_init=lambda_init),
        grid=(batch, nq),
        in_specs=[
            pl.BlockSpec((tq, 256), lambda b, i: (b * nq + i, 0)),
            pl.BlockSpec((seq, 256), lambda b, i: (b, 0)),
            pl.BlockSpec((seq, 256), lambda b, i: (b, 0)),
            const(lq1), const(lk1), const(lq2), const(lk2), const(g_sub),
        ],
        out_specs=pl.BlockSpec((seq, GROUP_W), lambda b, i: (b, 0)),
        out_shape=jax.ShapeDtypeStruct((batch * seq, GROUP_W), bf16),
        scratch_shapes=[pltpu.VMEM((DF_HEADS, seq, LANE), bf16)]
        + _softmax_scratch(2 * DF_HEADS, nq, tq),
        compiler_params=_cparams(("arbitrary", "arbitrary")),
        name="diff_attn",
    )(dq, dk, dv, lq1, lk1, lq2, lk2, g_sub)


def _split3(a):
    hi = a.astype(bf16)
    r1 = a - hi.astype(f32)
    mid = r1.astype(bf16)
    return hi, mid, (r1 - mid.astype(f32)).astype(bf16)


def _dot_exact_rhs01(a, b01):
    hi, mid, lo = _split3(a)
    return _dot(hi, b01) + _dot(mid, b01) + _dot(lo, b01)


def _dot_exact_lhs01(a01, b):
    hi, mid, lo = _split3(b)
    return _dot(a01, hi) + _dot(a01, mid) + _dot(a01, lo)


def _mlstm_gate_terms(g, eif):
    R, W, L = g.shape[0], eif.shape[1] // 2, ML_CHUNK
    lf = jnp.minimum(g, 0.0) - jnp.log(1.0 + jnp.exp(-jnp.abs(g)))
    glane = lax.broadcasted_iota(jnp.int32, g.shape, 1)
    wide = _dot_exact_rhs01(jnp.where(glane < ML_HEADS, g, lf), eif)
    iw, lfw = wide[:, :W], wide[:, W:]
    rr = lax.broadcasted_iota(jnp.int32, (R, R), 0)
    cc = lax.broadcasted_iota(jnp.int32, (R, R), 1)
    bw = _dot_exact_lhs01(((rr >= cc) & (rr // L == cc // L)).astype(bf16), lfw)
    rw = iw - bw
    pos_in_chunk = lax.broadcasted_iota(jnp.int32, (R, W), 0) % L
    cm = rw
    for sh in (1, 2, 4, 8, 16, 32):
        cm = jnp.maximum(cm, jnp.where(pos_in_chunk >= sh, pltpu.roll(cm, sh, axis=0), NEG))
    return bw, rw, cm


def _mlstm_chunk_local(q, k, kt_bd, v, rw, cm):
    L, W = q.shape
    row = lax.broadcasted_iota(jnp.int32, (L, W), 0)
    pos = lax.broadcasted_iota(jnp.int32, (L, W), 1) % L
    head_eq = (lax.broadcasted_iota(jnp.int32, (W, W), 0) // ML_DH
               == lax.broadcasted_iota(jnp.int32, (W, W), 1) // ML_DH)
    r_row = jnp.sum(jnp.where(pos == row, rw, 0.0), axis=0, keepdims=True)
    v4 = jnp.concatenate([v] * ML_HEADS, axis=0)
    vbd = jnp.where(head_eq, v4, jnp.zeros_like(v4))
    rmax = cm[L - 1:L, :]
    c_loc = _dot((kt_bd * jnp.exp(r_row - rmax)).astype(bf16), vbd)
    n_loc = jnp.sum(k.astype(f32) * jnp.exp(rw - rmax), axis=0, keepdims=True)
    return _dot(q, kt_bd.astype(bf16)), vbd, r_row, c_loc, n_loc


def _mlstm_chunk_step(q, o, bw, cm, qk, vbd, r_row, c_loc, n_loc, c_prev, n_prev, m_prev):
    L, W = q.shape
    row = lax.broadcasted_iota(jnp.int32, (L, W), 0)
    pos = lax.broadcasted_iota(jnp.int32, (L, W), 1) % L
    head_eq = (lax.broadcasted_iota(jnp.int32, (W, W), 0) // ML_DH
               == lax.broadcasted_iota(jnp.int32, (W, W), 1) // ML_DH)
    mw = jnp.maximum(m_prev, cm)
    a_inter = jnp.exp(m_prev - mw)
    pw = qk * jnp.exp(jnp.where(pos <= row, r_row - mw, NEG))
    sums = _dot_exact_rhs01(jnp.concatenate([q.astype(f32) * n_prev, pw], axis=0), head_eq.astype(bf16))
    den = a_inter * sums[:L] + sums[L:]
    hden = jnp.maximum(jnp.abs(den), jnp.exp(-(bw + mw)))
    num = a_inter * _dot(q, c_prev.astype(bf16)) + _dot(pw.astype(bf16), vbd)
    y = num / hden * _sigmoid(o.astype(f32))
    rmax, m_last = cm[L - 1:L, :], mw[L - 1:L, :]
    a_w = jnp.exp(m_prev - m_last)
    b_w = jnp.exp(rmax - m_last)
    return y, c_prev * a_w + c_loc * b_w, a_w * n_prev + b_w * n_loc, bw[L - 1:L, :] + m_last


def _mlstm_body(q_ref, k_ref, kt_ref, v_ref, o_ref, g_ref, eif_ref, sel_ref, y_ref, c_scr, nm_scr):
    @pl.when(pl.program_id(1) == 0)
    def _():
        c_scr[...] = jnp.zeros_like(c_scr)
        nm_scr[...] = jnp.zeros_like(nm_scr)

    c, n, m = c_scr[...], nm_scr[0:1, :], nm_scr[1:2, :]
    kt = kt_ref[...]
    eif = eif_ref[...]
    w = kt.shape[0]
    head_eq = (lax.broadcasted_iota(jnp.int32, (w, w), 0) // ML_DH
               == lax.broadcasted_iota(jnp.int32, (w, w), 1) // ML_DH)
    bw, rw, cm = _mlstm_gate_terms(g_ref[...], eif)
    n_chunks = q_ref.shape[0] // ML_CHUNK
    local = []
    for i in range(n_chunks):
        rows = pl.ds(i * ML_CHUNK, ML_CHUNK)
        lo, hi = i * ML_CHUNK, (i + 1) * ML_CHUNK
        kt_bd = jnp.where(head_eq, _dot(kt, sel_ref[i]), 0.0)
        local.append(_mlstm_chunk_local(q_ref[rows, :], k_ref[rows, :], kt_bd, v_ref[rows, :],
                                        rw[lo:hi], cm[lo:hi]))
    for i in range(n_chunks):
        rows = pl.ds(i * ML_CHUNK, ML_CHUNK)
        lo, hi = i * ML_CHUNK, (i + 1) * ML_CHUNK
        y, c, n, m = _mlstm_chunk_step(q_ref[rows, :], o_ref[rows, :], bw[lo:hi], cm[lo:hi],
                                       *local[i], c, n, m)
        y_ref[rows, :] = y.astype(y_ref.dtype)
    c_scr[...] = c
    nm_scr[0:1, :] = n
    nm_scr[1:2, :] = m


MLSTM_CHUNKS_PER_STEP = 4


def _mlstm(mq, mk, mkt, mv, mo, gcol, batch, seq):
    cps = MLSTM_CHUNKS_PER_STEP
    L = ML_CHUNK * cps
    nc = seq // L
    blk = lambda w: pl.BlockSpec((L, w), lambda b, c: (b * nc + c, 0))
    const = lambda a: pl.BlockSpec(a.shape, lambda b, c: (0,) * a.ndim)
    gl = jnp.arange(LANE)[:, None]
    col = jnp.arange(2 * GROUP_W)[None, :]
    eif = (gl == (col // GROUP_W) * ML_HEADS + (col % GROUP_W) // ML_DH).astype(bf16)
    p = jnp.arange(L)[None, :, None]
    cc = jnp.arange(GROUP_W)[None, None, :]
    sel = (p == jnp.arange(cps)[:, None, None] * ML_CHUNK + cc % ML_CHUNK).astype(bf16)
    return pl.pallas_call(
        _mlstm_body,
        grid=(batch, nc),
        in_specs=[blk(256), blk(256), pl.BlockSpec((GROUP_W, L), lambda b, c: (0, b * nc + c)),
                  blk(256), blk(256), blk(128), const(eif), const(sel)],
        out_specs=blk(GROUP_W),
        out_shape=jax.ShapeDtypeStruct((batch * seq, GROUP_W), bf16),
        scratch_shapes=[pltpu.VMEM((GROUP_W, GROUP_W), f32), pltpu.VMEM((8, GROUP_W), f32)],
        compiler_params=_cparams(("arbitrary", "arbitrary")),
        name="mlstm",
    )(mq, mk, mkt, mv, mo, gcol, eif, sel)


def _route(scores, sel):
    ne, tm = sel.shape
    eid = lax.broadcasted_iota(jnp.int32, (ne, tm), 0)
    grp = eid // E_PER_GROUP
    big = ne + 1

    def top2(vals):
        m1 = jnp.max(vals, axis=0, keepdims=True)
        i1 = jnp.min(jnp.where(vals == m1, eid, big), axis=0, keepdims=True)
        rest = jnp.where(eid == i1, NEG, vals)
        m2 = jnp.max(rest, axis=0, keepdims=True)
        i2 = jnp.min(jnp.where(rest == m2, eid, big), axis=0, keepdims=True)
        return m1, i1, m2, i2

    best_score, best_grp = None, None
    for gi in range(N_GROUPS):
        m1, _, m2, _ = top2(jnp.where(grp == gi, sel, NEG))
        sc = m1 + m2
        if best_score is None:
            best_score, best_grp = sc, jnp.zeros_like(sc, dtype=jnp.int32)
        else:
            better = sc > best_score
            best_grp = jnp.where(better, gi, best_grp)
            best_score = jnp.where(better, sc, best_score)
    _, i1, _, i2 = top2(jnp.where(grp == best_grp, sel, NEG))
    picked = jnp.where((eid == i1) | (eid == i2), scores, 0.0)
    return picked / jnp.sum(picked, axis=0, keepdims=True), best_grp


MOE_ROW_TILE = 256


def _outproj_route_body(ya_ref, yb_ref, yc_ref, yd_ref, x_ref, mod_ref, w_ref, g_ref, b_ref,
                        wr_ref, br_ref, x1_ref, u2e_ref, grp_ref):
    d = x_ref.shape[1]
    mix = None
    for j, y_ref in enumerate((ya_ref, yb_ref, yc_ref, yd_ref)):
        part = _dot(y_ref[...], w_ref[GROUP_W * j:GROUP_W * (j + 1), :])
        mix = part if mix is None else mix + part
    x1 = _layer_norm(DN_ALPHA * x_ref[...] + mod_ref[0, 2:3, :] * mix) * g_ref[...] + b_ref[...]
    x1_ref[...] = x1
    u = _layer_norm(x1) * (1.0 + mod_ref[0, 4:5, :]) + mod_ref[0, 3:4, :]
    u_hi = u.astype(bf16)
    u_lo = (u - u_hi.astype(f32)).astype(bf16)
    ne = br_ref.shape[0]
    part = _dot_nt(wr_ref[...], u_hi)
    logits = part[:ne] + part[ne:] + _dot_nt(wr_ref[:ne, :], u_lo)
    scores = _sigmoid(logits)
    gates_t, best_grp = _route(scores, scores + br_ref[...])
    u2e_ref[:, :d] = u_hi.astype(f32)
    u2e_ref[:, d:] = jnp.zeros((x1.shape[0], LANE), f32)
    u2e_ref[:, d:d + ne] = gates_t.T
    grp_ref[...] = best_grp


def _outproj_route(ya, yb, yc, yd, x2d, mod, w_out, ln_g, ln_b, wr_split, b_router, seq):
    t, d = x2d.shape
    tm = 1024
    tps = seq // tm
    row = lambda w: pl.BlockSpec((tm, w), lambda i: (i, 0))
    const = lambda a: pl.BlockSpec(a.shape, lambda i: (0,) * a.ndim)
    return pl.pallas_call(
        _outproj_route_body,
        grid=(t // tm,),
        in_specs=[row(256), row(256), row(256), row(256), row(d),
                  pl.BlockSpec((1, 6, d), lambda i: (i // tps, 0, 0)),
                  const(w_out), const(ln_g), const(ln_b), const(wr_split), const(b_router)],
        out_specs=[row(d), row(d + LANE), pl.BlockSpec((1, tm), lambda i: (0, i))],
        out_shape=[jax.ShapeDtypeStruct((t, d), f32), jax.ShapeDtypeStruct((t, d + LANE), f32),
                   jax.ShapeDtypeStruct((1, t), jnp.int32)],
        compiler_params=pltpu.CompilerParams(dimension_semantics=("arbitrary",), vmem_limit_bytes=VMEM_LIMIT,
                                             allow_input_fusion=[i == 6 for i in range(11)]),
        name="outproj_route",
    )(ya, yb, yc, yd, x2d, mod, w_out, ln_g, ln_b, wr_split, b_router)


def _dispatch_plan(grp, tr):
    t = grp.shape[0]
    rows = t + N_GROUPS * tr
    order = jnp.sort(grp * t + jnp.arange(t, dtype=jnp.int32)) % t
    order = jnp.concatenate([order, jnp.zeros((rows - t,), order.dtype)])
    onehot = (grp[:, None] == jnp.arange(N_GROUPS)[None, :]).astype(jnp.int32)
    counts = jnp.sum(onehot, axis=0)
    padded = ((counts + tr - 1) // tr) * tr
    row_end = jnp.cumsum(padded)
    rank = jnp.sum((jnp.cumsum(onehot, axis=0) - 1) * onehot, axis=1)
    row_of_tok = (row_end - padded)[grp] + rank
    r = jnp.arange(rows, dtype=jnp.int32)
    tok_of_row = jnp.zeros((rows,), jnp.int32)
    for g in range(N_GROUPS):
        row_start = row_end[g] - padded[g]
        tok_start = jnp.sum(counts[:g])
        shifted = jnp.roll(order, row_start - tok_start)
        in_grp = (r >= row_start) & (r < row_end[g])
        valid = in_grp & (r < row_start + counts[g])
        tok_of_row = jnp.where(valid, shifted,
                               jnp.where(in_grp, order[jnp.minimum(tok_start, t - 1)], tok_of_row))
    tile_start = jnp.arange(rows // tr, dtype=jnp.int32) * tr
    tile_grp = jnp.minimum(jnp.sum((tile_start[:, None] >= row_end[None, :]).astype(jnp.int32), axis=1),
                           N_GROUPS - 1)
    i32 = lambda a: a.astype(jnp.int32)
    return i32(tok_of_row), i32(row_of_tok), i32(tile_grp), i32(row_end[-1:] // tr)


def _wait_row_gather(n_rows, hbm, vmem, sem):
    pltpu.make_async_copy(hbm.at[pl.ds(0, n_rows), :], vmem, sem).wait()


def _start_row_gather(idx_ref, first, n_rows, hbm, vmem, sem):
    for i in range(n_rows):
        pltpu.make_async_copy(hbm.at[pl.ds(idx_ref[first + i], 1), :], vmem.at[pl.ds(i, 1), :], sem).start()


def _experts_body(tok_ref, tgrp_ref, ntile_ref, u2e_hbm, w1_ref, w3_ref, w2_ref, o_ref,
                  xbuf0, xbuf1, w1b, w3b, w2b, gsem):
    j = pl.program_id(0)
    n_tiles = ntile_ref[0]
    tr, d = o_ref.shape
    xbuf = (xbuf0, xbuf1)

    def compute(s):
        xs = xbuf[s][...]
        xb = xs[:, :d].astype(bf16)
        gslab = xs[:, d:]
        lane = lax.broadcasted_iota(jnp.int32, gslab.shape, 1)
        first_expert = tgrp_ref[j] * E_PER_GROUP
        acc = None
        for e in range(E_PER_GROUP):
            ge = jnp.sum(jnp.where(lane == first_expert + e, gslab, 0.0), axis=1, keepdims=True)
            h1 = _dot(xb, w1b[e])
            h = h1 * _sigmoid(h1) * _dot(xb, w3b[e]) * ge
            part = _dot(h.astype(bf16), w2b[e])
            acc = part if acc is None else acc + part
        o_ref[...] = acc

    @pl.when((j < n_tiles) & ((j == 0) | (tgrp_ref[j] != tgrp_ref[jnp.maximum(j - 1, 0)])))
    def _():
        w1b[...] = w1_ref[0].astype(bf16)
        w3b[...] = w3_ref[0].astype(bf16)
        w2b[...] = w2_ref[0].astype(bf16)

    @pl.when(j == 0)
    def _():
        _start_row_gather(tok_ref, 0, tr, u2e_hbm, xbuf[0], gsem.at[0])

    for s in (0, 1):
        @pl.when((j + 1 < n_tiles) & (j % 2 == s))
        def _(s=s):
            _start_row_gather(tok_ref, (j + 1) * tr, tr, u2e_hbm, xbuf[1 - s], gsem.at[1 - s])

        @pl.when((j < n_tiles) & (j % 2 == s))
        def _(s=s):
            _wait_row_gather(tr, u2e_hbm, xbuf[s], gsem.at[s])
            compute(s)

    @pl.when(j >= n_tiles)
    def _():
        o_ref[...] = jnp.zeros_like(o_ref)


def _experts(tok_of_row, tile_grp, n_tiles, u2e, w1, w3, w2, layer):
    rows = tok_of_row.shape[0]
    tr = MOE_ROW_TILE
    _, ne, d, fe = w1.shape
    grouped = lambda w: w.reshape(w.shape[0], N_GROUPS, E_PER_GROUP, *w.shape[2:])
    wspec = lambda a, b: pl.BlockSpec((None, 1, E_PER_GROUP, a, b), lambda j, tok, tg, nt: (layer, tg[j], 0, 0, 0))
    return pl.pallas_call(
        _experts_body,
        grid_spec=pltpu.PrefetchScalarGridSpec(
            num_scalar_prefetch=3, grid=(rows // tr,),
            in_specs=[pl.BlockSpec(memory_space=pl.ANY), wspec(d, fe), wspec(d, fe), wspec(fe, d)],
            out_specs=pl.BlockSpec((tr, d), lambda j, tok, tg, nt: (j, 0)),
            scratch_shapes=[pltpu.VMEM((tr, u2e.shape[1]), f32), pltpu.VMEM((tr, u2e.shape[1]), f32),
                            pltpu.VMEM((E_PER_GROUP, d, fe), bf16), pltpu.VMEM((E_PER_GROUP, d, fe), bf16),
                            pltpu.VMEM((E_PER_GROUP, fe, d), bf16),
                            pltpu.SemaphoreType.DMA((2,))]),
        out_shape=jax.ShapeDtypeStruct((rows, d), f32),
        compiler_params=_cparams(("arbitrary",)),
        name="experts",
    )(tok_of_row, tile_grp, n_tiles, u2e, grouped(w1), grouped(w3), grouped(w2))


def _moe_combine_body(row_ref, ys_hbm, x_ref, mod_ref, g_ref, b_ref, o_ref, ybuf0, ybuf1, sem):
    i = pl.program_id(0)
    tm = x_ref.shape[0]
    ybuf = (ybuf0, ybuf1)

    @pl.when(i == 0)
    def _():
        _start_row_gather(row_ref, 0, tm, ys_hbm, ybuf[0], sem.at[0])

    for s in (0, 1):
        @pl.when((i + 1 < pl.num_programs(0)) & (i % 2 == s))
        def _(s=s):
            _start_row_gather(row_ref, (i + 1) * tm, tm, ys_hbm, ybuf[1 - s], sem.at[1 - s])

        @pl.when(i % 2 == s)
        def _(s=s):
            _wait_row_gather(tm, ys_hbm, ybuf[s], sem.at[s])
            r = DN_ALPHA * x_ref[...] + mod_ref[0, 5:6, :] * ybuf[s][...]
            o_ref[...] = _layer_norm(r) * g_ref[...] + b_ref[...]


def _moe_combine(row_of_tok, ys, x1, mod, ln_g, ln_b, seq):
    t, d = x1.shape
    tm = 256
    tps = seq // tm
    const = lambda a: pl.BlockSpec(a.shape, lambda i, rows: (0,) * a.ndim)
    return pl.pallas_call(
        _moe_combine_body,
        grid_spec=pltpu.PrefetchScalarGridSpec(
            num_scalar_prefetch=1, grid=(t // tm,),
            in_specs=[pl.BlockSpec(memory_space=pl.ANY),
                      pl.BlockSpec((tm, d), lambda i, rows: (i, 0)),
                      pl.BlockSpec((1, 6, d), lambda i, rows: (i // tps, 0, 0)),
                      const(ln_g), const(ln_b)],
            out_specs=pl.BlockSpec((tm, d), lambda i, rows: (i, 0)),
            scratch_shapes=[pltpu.VMEM((tm, d), f32), pltpu.VMEM((tm, d), f32),
                            pltpu.SemaphoreType.DMA((2,))]),
        out_shape=jax.ShapeDtypeStruct((t, d), f32),
        compiler_params=_cparams(("arbitrary",)),
        name="moe_combine",
    )(row_of_tok, ys, x1, mod, ln_g, ln_b)


def _pad_cols(a, width):
    return jnp.pad(a, ((0, 0), (0, width - a.shape[1])))


def _pack_w_in(w):
    parts, acc = [], 0
    for sz in IN_SIZES:
        parts.append(w[:, acc:acc + sz])
        acc += sz
    (a_q, a_kv, a_kr, m_q, m_k, m_v, m_i, m_f, m_o, c_b, c_c, c_h, d_q, d_k, d_v) = parts
    gates = _pad_cols(jnp.concatenate([m_i, m_f], axis=1), LANE)
    packed = jnp.concatenate([
        _pad_cols(a_q, 256), a_kv, _pad_cols(a_kr, LANE),
        m_q, m_k, m_v, gates, m_o, c_b, c_c, c_h, d_q, d_k, d_v], axis=1)
    return packed.astype(bf16)


def _pack_mla(w_uq, w_ukv):
    half = MLA_ROPE // 2
    wq3 = w_uq.reshape(MLA_Q_LORA, MLA_HEADS, MLA_NOPE + MLA_ROPE).transpose(1, 0, 2)
    wkv3 = w_ukv.reshape(MLA_KV_LORA, MLA_HEADS, MLA_NOPE + MLA_V).transpose(1, 0, 2)
    nope, pe = wq3[:, :, :MLA_NOPE], wq3[:, :, MLA_NOPE:]
    rot = jnp.concatenate([-pe[:, :, half:], pe[:, :, :half]], axis=2)
    pad = lambda a: jnp.pad(a, ((0, 0), (0, 256 - a.shape[1]), (0, LANE - a.shape[2]))).astype(bf16)
    wq = pad(wq3)
    wqr = pad(jnp.concatenate([jnp.zeros_like(nope), rot], axis=2))
    wk = jnp.pad(wkv3[:, :, :MLA_NOPE], ((0, 0), (0, 0), (0, LANE - MLA_NOPE))).astype(bf16)
    return wq, wqr, wk, wkv3[:, :, MLA_NOPE:].astype(bf16)


def _rope_tables(seq):
    half = DF_DK // 2
    inv = 1.0 / (ROPE_THETA ** (jnp.arange(0, DF_DK, 2, dtype=f32) / DF_DK))
    ang = jnp.arange(seq, dtype=f32)[:, None] * inv[None, :]
    cos, sin = jnp.cos(ang), jnp.sin(ang)
    cos_t = jnp.tile(jnp.concatenate([cos, cos], axis=1), (1, 256 // DF_DK))
    sin_t = jnp.tile(jnp.concatenate([-sin, sin], axis=1), (1, 256 // DF_DK))
    scale = (MLA_NOPE + MLA_ROPE) ** -0.5 * LOG2E
    ones = jnp.ones((seq, MLA_NOPE), f32)
    zeros = jnp.zeros((seq, MLA_NOPE), f32)
    cq = _pad_cols(jnp.concatenate([ones, cos, cos], axis=1) * scale, LANE)
    sq = _pad_cols(jnp.concatenate([zeros, sin, sin], axis=1) * scale, LANE)
    return cos_t, sin_t, cq, sq


def kernel(x, c, w_ada, b_ada, w_in, mla_g_q, mla_g_kv, mla_w_uq, mla_w_ukv, ml_b_i, ml_b_f, sc_w,
           df_lq1, df_lk1, df_lq2, df_lk2, df_g, w_out, ln1_g, ln1_b, w_router, b_router, w1, w3, w2,
           ln2_g, ln2_b):
    batch, seq, d = x.shape
    depth = w_in.shape[0]
    assert MLA_ROPE == DF_DK, "both rotary blocks share one table"
    cos_t, sin_t, cq, sq = _rope_tables(seq)
    place = jnp.zeros((LANE, LANE), f32).at[jnp.arange(MLA_ROPE), MLA_NOPE + jnp.arange(MLA_ROPE)].set(1.0).astype(bf16)
    wr_hi = w_router.astype(bf16)
    wr_split = jnp.concatenate([wr_hi, (w_router - wr_hi.astype(f32)).astype(bf16)], axis=1).T
    mod_all = _ada_mod(c, w_ada, b_ada).reshape(depth, batch, 6, d)
    xf = x.reshape(batch * seq, d)
    for l in range(depth):
        mod = mod_all[l]
        w_packed = _pack_w_in(w_in[l])
        gq = _pad_cols(mla_g_q[l][None, :], 256)
        gkv = mla_g_kv[l][None, :]
        gbias = _pad_cols(jnp.concatenate([ml_b_i[l], ml_b_f[l]])[None, :], LANE)
        wkt = w_packed[:, OFF_MK:OFF_MK + GROUP_W].T
        (qn, kvn, kpe, mq, mk, mv, mo, gcol, yc, dq, dk, dv, mkt) = _inproj(
            xf, mod, w_packed, wkt, cos_t, sin_t, gq, gkv, gbias, sc_w[l], seq)
        wq, wqr, wk, wv = _pack_mla(mla_w_uq[l], mla_w_ukv[l])
        ya = _mla_attention(qn, kvn, kpe, wq, wqr, wk, wv, place, cq, sq, batch, seq)
        yb = _mlstm(mq, mk, mkt, mv, mo, gcol, batch, seq)
        lambda_init = 0.8 - 0.6 * math.exp(-0.3 * l)
        yd = _diff_attention(dq, dk, dv, df_lq1[l][None, :], df_lk1[l][None, :], df_lq2[l][None, :],
                             df_lk2[l][None, :], _pad_cols(df_g[l][None, :], LANE),
                             batch, seq, lambda_init)
        x1, u2e, grp = _outproj_route(ya, yb, yc, yd, xf, mod, w_out[l].astype(bf16), ln1_g[l][None, :],
                                      ln1_b[l][None, :], wr_split, b_router[:, None], seq)
        tok_of_row, row_of_tok, tile_grp, n_tiles = _dispatch_plan(grp[0], MOE_ROW_TILE)
        ys = _experts(tok_of_row, tile_grp, n_tiles, u2e, w1, w3, w2, l)
        xf = _moe_combine(row_of_tok, ys, x1, mod, ln2_g[l][None, :], ln2_b[l][None, :], seq)
    return xf.reshape(batch, seq, d)
```
